```python
import functools
import jax, jax.numpy as jnp
from jax import lax
import numpy as np

D_MODEL = 1024
BATCH = 16
SEQ = 256
DEPTH = 1
DEC_BATCH = 2
DEC_SEQ = 2048
PAST_LEN = 512

GRID_W = 64
HEAD_DIM = 64
N_ATT_HEADS = 8
N_RWKV_HEADS = 8
ATT_WIDTH = N_ATT_HEADS * HEAD_DIM
RWKV_WIDTH = N_RWKV_HEADS * HEAD_DIM
MIX_WIDTH = ATT_WIDTH + RWKV_WIDTH
NA_ROWS = 8
NA_COLS = 16
N_DIRS = 2
DECAY_LORA = 64
AAA_LORA = 64
GATE_LORA = 128
SHIFT_WIDTH = 3
FFN_CONV_WIDTH = 3
D_FF = 2816
EPS = 1e-6
GN_EPS = 64e-5
ATT_SCALE = HEAD_DIM ** -0.5
RWKV_COLS = 3 * RWKV_WIDTH + N_DIRS * (DECAY_LORA + AAA_LORA) + GATE_LORA
IN_WIDTH = 3 * ATT_WIDTH + RWKV_COLS

kernel_name = 'hymba_natten_rwkv7_dit_step'


def _rmsnorm(x, g):
    xf = x.astype(jnp.float32)
    y = xf * lax.rsqrt(jnp.mean(xf * xf, axis=-1, keepdims=True) + EPS)
    return (y * g.astype(jnp.float32)).astype(x.dtype)


def _dwconv(x, w):
    K = w.shape[0]
    pad = K // 2
    T = x.shape[1]
    xp = jnp.pad(x, ((0, 0), (pad, pad), (0, 0)))
    return sum(xp[:, j:j + T] * w[j] for j in range(K))


def _modulation(cond, w_ada, b_ada):
    m = jax.nn.silu(cond) @ w_ada + b_ada
    return jnp.split(m[..., None, :], 6, axis=-1)


def _heads(x, n):
    return x.reshape(x.shape[:-1] + (n, HEAD_DIM))


def _context_attention(q, k, v):
    B, L = q.shape[:2]
    s = jnp.einsum('bqhd,bkhd->bhqk', q, k).astype(jnp.float32) * ATT_SCALE
    p = jax.nn.softmax(s, axis=-1).astype(v.dtype)
    return jnp.einsum('bhqk,bkhd->bqhd', p, v).reshape(B, L, ATT_WIDTH)


def _neighbourhood_attention(q, k, v, k_ctx, v_ctx, rpb):
    B, T, H, Dh = q.shape
    R = T // GRID_W
    KH = min(NA_ROWS, R)
    KW = NA_COLS
    qg = q.reshape(B, R, GRID_W, H, Dh)
    kg = k.reshape(B, R, GRID_W, H, Dh)
    vg = v.reshape(B, R, GRID_W, H, Dh)
    k_ctx = k_ctx.astype(q.dtype)
    v_ctx = v_ctx.astype(v.dtype)
    cols = jnp.arange(GRID_W)
    col_start = jnp.clip(cols - KW // 2, 0, GRID_W - KW)
    col_idx = col_start[:, None] + jnp.arange(KW)[None, :]
    col_off = col_idx - cols[:, None] + (NA_COLS - 1)
    rpb_cols = rpb[:, :, col_off].astype(jnp.float32)

    def row_block(i):
        si = jnp.clip(i - KH // 2, 0, R - KH)
        q_i = lax.dynamic_index_in_dim(qg, i, axis=1, keepdims=False)
        k_win = lax.dynamic_slice_in_dim(kg, si, KH, axis=1)[:, :, col_idx]
        v_win = lax.dynamic_slice_in_dim(vg, si, KH, axis=1)[:, :, col_idx]
        row_off = si + jnp.arange(KH) - i + (NA_ROWS - 1)
        bias = jnp.take(rpb_cols, row_off, axis=1)
        s_loc = (jnp.einsum('bwhd,bawkhd->bhwak', q_i, k_win).astype(jnp.float32) * ATT_SCALE
                 + jnp.transpose(bias, (0, 2, 1, 3))[None])
        s_loc = s_loc.reshape(B, H, GRID_W, KH * KW)
        s_ctx = jnp.einsum('bwhd,blhd->bhwl', q_i, k_ctx).astype(jnp.float32) * ATT_SCALE
        p = jax.nn.softmax(jnp.concatenate([s_loc, s_ctx], axis=-1), axis=-1).astype(v.dtype)
        p_loc = p[..., :KH * KW].reshape(B, H, GRID_W, KH, KW)
        p_ctx = p[..., KH * KW:]
        return (jnp.einsum('bhwak,bawkhd->bwhd', p_loc, v_win)
                + jnp.einsum('bhwl,blhd->bwhd', p_ctx, v_ctx))

    o = lax.map(row_block, jnp.arange(R))
    return jnp.moveaxis(o, 0, 1).reshape(B, T, H * Dh)


def _wkv_scan(S0, r, decay, kk, kka, v, k, reverse):
    def step(S, inp):
        r_t, w_t, kk_t, b_t, v_t, k_t = inp
        sa = jnp.einsum('bhvk,bhk->bhv', S, kk_t)
        S = S * w_t[:, :, None, :] - sa[..., None] * b_t[:, :, None, :] + v_t[..., None] * k_t[:, :, None, :]
        return S, jnp.einsum('bhvk,bhk->bhv', S, r_t)
    return lax.scan(step, S0, (r, decay, kk, kka, v, k), reverse=reverse)


def _rwkv_mix(u, S0, w0, w2, a0, a2, g2, k_k, k_a, r_k, ln_g, ln_b):
    f32 = jnp.float32
    u = u.astype(f32)
    B, T, _ = u.shape
    H = N_RWKV_HEADS
    r = u[..., :RWKV_WIDTH]
    kr = u[..., RWKV_WIDTH:2 * RWKV_WIDTH]
    v = u[..., 2 * RWKV_WIDTH:3 * RWKV_WIDTH]
    o = 3 * RWKV_WIDTH
    xw = u[..., o:o + N_DIRS * DECAY_LORA].reshape(B, T, N_DIRS, DECAY_LORA)
    o += N_DIRS * DECAY_LORA
    xa = u[..., o:o + N_DIRS * AAA_LORA].reshape(B, T, N_DIRS, AAA_LORA)
    o += N_DIRS * AAA_LORA
    xg = u[..., o:o + GATE_LORA]
    w_log = -jax.nn.softplus(-(w0.astype(f32) + jnp.einsum('btnl,nlc->btnc', jnp.tanh(xw), w2.astype(f32)))) - 0.5
    decay = jnp.exp(-jnp.exp(w_log))
    a = jax.nn.sigmoid(a0.astype(f32) + jnp.einsum('btnl,nlc->btnc', xa, a2.astype(f32)))
    g = jax.nn.sigmoid(xg) @ g2.astype(f32)
    kk = _heads(kr * k_k.astype(f32), H)
    kk = kk * lax.rsqrt(jnp.sum(kk * kk, axis=-1, keepdims=True) + 1e-12)
    kd = kr[:, :, None] * (1.0 + (a - 1.0) * k_a.astype(f32))
    r_h = _heads(r, H)
    v_h = _heads(v, H)
    decay_h = _heads(decay, H)
    a_h = _heads(a, H)
    kd_h = _heads(kd, H)
    tm = lambda t: jnp.moveaxis(t, 1, 0)
    S_f, y_f = _wkv_scan(S0[:, 0].astype(f32), tm(r_h), tm(decay_h[:, :, 0]), tm(kk),
                         tm(kk * a_h[:, :, 0]), tm(v_h), tm(kd_h[:, :, 0]), reverse=False)
    S_b, y_b = _wkv_scan(S0[:, 1].astype(f32), tm(r_h), tm(decay_h[:, :, 1]), tm(kk),
                         tm(kk * a_h[:, :, 1]), tm(v_h), tm(kd_h[:, :, 1]), reverse=True)
    y = jnp.moveaxis(y_f + y_b, 0, 1)
    mu = jnp.mean(y, axis=-1, keepdims=True)
    var = jnp.mean(jnp.square(y - mu), axis=-1, keepdims=True)
    y = ((y - mu) * lax.rsqrt(var + GN_EPS)).reshape(B, T, RWKV_WIDTH) * ln_g.astype(f32) + ln_b.astype(f32)
    bonus = jnp.sum(jnp.sum(r_h[:, :, None] * kd_h * r_k.astype(f32), axis=-1, keepdims=True) * v_h[:, :, None], axis=2)
    out = (y + bonus.reshape(B, T, RWKV_WIDTH)) * g
    return out, jnp.stack([S_f, S_b], axis=1)


def _conv_ffn(h, w1, w3, wc, w2):
    return (jax.nn.silu(_dwconv(h @ w1, wc)) * (h @ w3)) @ w2


def _layer(x, mod, attend, S0, w_in, w_ts, rp, w_out, norm1, norm2, fp):
    sh1, sc1, g1, sh2, sc2, g2 = mod
    h = _rmsnorm(x, norm1) * (1 + sc1) + sh1
    z = h @ w_in
    q = _heads(z[..., :ATT_WIDTH], N_ATT_HEADS)
    k = _heads(z[..., ATT_WIDTH:2 * ATT_WIDTH], N_ATT_HEADS)
    v = _heads(z[..., 2 * ATT_WIDTH:3 * ATT_WIDTH], N_ATT_HEADS)
    a_out = attend(q, k, v)
    u = _dwconv(z[..., 3 * ATT_WIDTH:], w_ts)
    r_out, S = _rwkv_mix(u, S0, *rp)
    x = x + g1 * (jnp.concatenate([a_out, r_out.astype(x.dtype)], axis=-1) @ w_out)
    h2 = _rmsnorm(x, norm2) * (1 + sc2) + sh2
    x = x + g2 * _conv_ffn(h2, *fp)
    return x, k, v, S


def setup_inputs(seed: int = 0) -> dict:
    key = jax.random.key(seed)
    ks = jax.random.split(key, 32)
    n = lambda i, shape, s: jax.random.normal(ks[i], shape, jnp.float32) * s
    D = D_MODEL
    return {
        'x_prompt': n(0, (BATCH, SEQ, D), 1.0),
        'x_sample': n(1, (DEC_BATCH, DEC_SEQ, D), 1.0),
        'cache_k': n(2, (DEC_BATCH, DEPTH, PAST_LEN, N_ATT_HEADS, HEAD_DIM), 1.0),
        'cache_v': n(3, (DEC_BATCH, DEPTH, PAST_LEN, N_ATT_HEADS, HEAD_DIM), 1.0),
        'state_rwkv': n(4, (DEC_BATCH, DEPTH, N_DIRS, N_RWKV_HEADS, HEAD_DIM, HEAD_DIM), 0.3),
        'c': n(5, (DEC_BATCH, D), 1.0),
        'c_ctx': n(6, (D,), 1.0),
        'w_ada': n(7, (DEPTH, D, 6 * D), 0.5 * D ** -0.5),
        'b_ada': n(8, (DEPTH, 6 * D), 0.02),
        'norm1': 1.0 + n(9, (DEPTH, D), 0.05),
        'norm2': 1.0 + n(10, (DEPTH, D), 0.05),
        'w_in': n(11, (DEPTH, D, IN_WIDTH), D ** -0.5),
        'w_ts': n(12, (DEPTH, SHIFT_WIDTH, RWKV_COLS), SHIFT_WIDTH ** -0.5),
        'w0': n(13, (DEPTH, N_DIRS, RWKV_WIDTH), 0.5),
        'w2': n(14, (DEPTH, N_DIRS, DECAY_LORA, RWKV_WIDTH), 0.5 * DECAY_LORA ** -0.5),
        'a0': n(15, (DEPTH, N_DIRS, RWKV_WIDTH), 0.5),
        'a2': n(16, (DEPTH, N_DIRS, AAA_LORA, RWKV_WIDTH), AAA_LORA ** -0.5),
        'g2': n(17, (DEPTH, GATE_LORA, RWKV_WIDTH), GATE_LORA ** -0.5),
        'k_k': 0.85 + n(18, (DEPTH, RWKV_WIDTH), 0.05),
        'k_a': 1.0 + n(19, (DEPTH, RWKV_WIDTH), 0.05),
        'r_k': n(20, (DEPTH, N_RWKV_HEADS, HEAD_DIM), 0.1),
        'ln_x_g': 1.0 + n(21, (DEPTH, RWKV_WIDTH), 0.05),
        'ln_x_b': n(22, (DEPTH, RWKV_WIDTH), 0.02),
        'rpb': n(23, (DEPTH, N_ATT_HEADS, 2 * NA_ROWS - 1, 2 * NA_COLS - 1), 0.1),
        'w_out': n(24, (DEPTH, MIX_WIDTH, D), MIX_WIDTH ** -0.5),
        'w_ffn1': n(25, (DEPTH, D, D_FF), D ** -0.5),
        'w_ffn3': n(26, (DEPTH, D, D_FF), D ** -0.5),
        'w_ffn_conv': n(27, (DEPTH, FFN_CONV_WIDTH, D_FF), FFN_CONV_WIDTH ** -0.5),
        'w_ffn2': n(28, (DEPTH, D_FF, D), D_FF ** -0.5),
        'norm_f': 1.0 + n(29, (D,), 0.05),
    }


def reference(x_prompt, x_sample, cache_k, cache_v, state_rwkv, c, c_ctx, w_ada, b_ada, norm1, norm2,
              w_in, w_ts, w0, w2, a0, a2, g2, k_k, k_a, r_k, ln_x_g, ln_x_b, rpb, w_out,
              w_ffn1, w_ffn3, w_ffn_conv, w_ffn2, norm_f):
    xp = x_prompt
    xs = x_sample
    new_k, new_v, new_s = [], [], []
    zero_state = jnp.zeros((xp.shape[0], N_DIRS, N_RWKV_HEADS, HEAD_DIM, HEAD_DIM), jnp.float32)
    for l in range(DEPTH):
        rp = (w0[l], w2[l], a0[l], a2[l], g2[l], k_k[l], k_a[l], r_k[l], ln_x_g[l], ln_x_b[l])
        fp = (w_ffn1[l], w_ffn3[l], w_ffn_conv[l], w_ffn2[l])
        mod_ctx = _modulation(c_ctx[None, :], w_ada[l], b_ada[l])
        xp, kc, vc, sc = _layer(xp, mod_ctx, _context_attention, zero_state, w_in[l], w_ts[l], rp,
                                w_out[l], norm1[l], norm2[l], fp)
        new_k.append(kc)
        new_v.append(vc)
        new_s.append(sc)
        mod_lat = _modulation(c, w_ada[l], b_ada[l])
        attend = functools.partial(_neighbourhood_attention, k_ctx=cache_k[:, l], v_ctx=cache_v[:, l], rpb=rpb[l])
        xs, _, _, _ = _layer(xs, mod_lat, attend, state_rwkv[:, l], w_in[l], w_ts[l], rp,
                             w_out[l], norm1[l], norm2[l], fp)
    y_prompt = _rmsnorm(xp, norm_f)
    y_sample = _rmsnorm(xs, norm_f)
    new_cache_k = jnp.stack(new_k, axis=1)
    new_cache_v = jnp.stack(new_v, axis=1)
    new_state_rwkv = jnp.stack(new_s, axis=1)
    return (y_prompt, y_sample, new_cache_k, new_cache_v, new_state_rwkv)
```

```python
import functools

import jax
import jax.numpy as jnp
from jax import lax
from jax.experimental import pallas as pl
from jax.experimental.pallas import tpu as pltpu

F32 = jnp.float32
BF16 = jnp.bfloat16
HIGHEST = lax.Precision.HIGHEST

D_MODEL = 1024
BATCH = 16
SEQ = 256
DEC_BATCH = 2
DEC_SEQ = 2048
PAST_LEN = 512
GRID_W = 64
HEAD_DIM = 64
N_HEADS = 8
WIDTH = N_HEADS * HEAD_DIM
NA_ROWS = 8
NA_COLS = 16
N_DIRS = 2
LORA = 64
GATE_LORA = 128
D_FF = 2816
EPS = 1e-6
GN_EPS = 64e-5
ATT_SCALE = HEAD_DIM ** -0.5
RWKV_COLS = 3 * WIDTH + N_DIRS * 2 * LORA + GATE_LORA
MASK_VALUE = -1e30

N_CTX_TOK = BATCH * SEQ
N_LAT_TOK = DEC_BATCH * DEC_SEQ
N_TOK = N_CTX_TOK + N_LAT_TOK
TOK_BLOCK = 256
N_BLOCKS = N_TOK // TOK_BLOCK
CTX_BLOCKS = N_CTX_TOK // TOK_BLOCK
LAT_BLOCKS_PER_SEQ = DEC_SEQ // TOK_BLOCK
N_SEQS = BATCH + DEC_BATCH
CHUNK = 64
N_CHUNKS = N_TOK // CHUNK
CTX_CHUNKS = N_CTX_TOK // CHUNK
CHUNKS_PER_CTX_SEQ = SEQ // CHUNK
CHUNKS_PER_LAT_SEQ = DEC_SEQ // CHUNK
GRID_ROWS = DEC_SEQ // GRID_W
FF_TILE = D_FF // 2
VMEM_LIMIT = 56 * 1024 * 1024


def _params(n_axes, limit=VMEM_LIMIT):
    return pltpu.CompilerParams(dimension_semantics=("arbitrary",) * n_axes, vmem_limit_bytes=limit)


def _const_spec(shape):
    zeros = (0,) * len(shape)
    return pl.BlockSpec(shape, lambda *_: zeros, pipeline_mode=pl.Buffered(1))


def _mod_row(i):
    return jnp.where(i < CTX_BLOCKS, 0, 1 + (i - CTX_BLOCKS) // LAT_BLOCKS_PER_SEQ)


def _seq_neighbours(i):
    j = (i - CTX_BLOCKS) % LAT_BLOCKS_PER_SEQ
    lat = i >= CTX_BLOCKS
    return lat & (j != 0), lat & (j != LAT_BLOCKS_PER_SEQ - 1)


def _silu(x):
    return x * jax.nn.sigmoid(x)


def _softplus(x):
    return jnp.maximum(x, 0.0) + jnp.log1p(jnp.exp(-jnp.abs(x)))


def _rmsnorm(x, g):
    return x * lax.rsqrt(jnp.mean(x * x, axis=-1, keepdims=True) + EPS) * g


def _dot(a, b, precision=None):
    return jnp.dot(a, b, precision=precision, preferred_element_type=F32)


def _dot_nt(a, b, precision=None):
    return lax.dot_general(a, b, (((1,), (1,)), ((), ())), precision=precision, preferred_element_type=F32)


def _dot_tn(a, b, precision=None):
    return lax.dot_general(a, b, (((0,), (0,)), ((), ())), precision=precision, preferred_element_type=F32)


def _shifted(a, prev_row, next_row):
    t = a.shape[0]
    rid = lax.broadcasted_iota(jnp.int32, (t, 1), 0)
    prev = jnp.where(rid == 0, prev_row, pltpu.roll(a, 1, axis=0))
    nxt = jnp.where(rid == t - 1, next_row, pltpu.roll(a, t - 1, axis=0))
    return prev, nxt


def _mod_kernel(cond_ref, w_ref, b_ref, o_ref):
    o_ref[...] = _dot(_silu(cond_ref[...]), w_ref[...], HIGHEST) + b_ref[...]


def _modulation(cond, w_ada, b_ada):
    n = 6 * D_MODEL
    return pl.pallas_call(
        _mod_kernel,
        grid=(6,),
        in_specs=[pl.BlockSpec((8, D_MODEL), lambda j: (0, 0)),
                  pl.BlockSpec((D_MODEL, D_MODEL), lambda j: (0, j)),
                  pl.BlockSpec((1, D_MODEL), lambda j: (0, j))],
        out_specs=pl.BlockSpec((8, D_MODEL), lambda j: (0, j)),
        out_shape=jax.ShapeDtypeStruct((8, n), F32),
        compiler_params=_params(1),
        name="mod",
    )(cond, w_ada, b_ada.reshape(1, n))


def _inproj_kernel(x_ref, mod_ref, n1_ref, w_ref, q_ref, k_ref, v_ref, zr_ref):
    h = _rmsnorm(x_ref[...], n1_ref[...]) * (1.0 + mod_ref[0, 1:2, :]) + mod_ref[0, 0:1, :]
    h = h.astype(BF16)
    q_ref[...] = _dot(h, w_ref[:, 0:WIDTH])
    k_ref[...] = _dot(h, w_ref[:, WIDTH:2 * WIDTH])
    v_ref[...] = _dot(h, w_ref[:, 2 * WIDTH:3 * WIDTH])
    zr_ref[...] = _dot(h, w_ref[:, 3 * WIDTH:])


def _inproj(x_all, mod, norm1, w_in_bf):
    tok = lambda c: pl.BlockSpec((TOK_BLOCK, c), lambda i: (i, 0))
    return pl.pallas_call(
        _inproj_kernel,
        grid=(N_BLOCKS,),
        in_specs=[tok(D_MODEL),
                  pl.BlockSpec((1, 6, D_MODEL), lambda i: (_mod_row(i), 0, 0)),
                  _const_spec((1, D_MODEL)),
                  _const_spec((D_MODEL, 3 * WIDTH + RWKV_COLS))],
        out_specs=[tok(WIDTH), tok(WIDTH), tok(WIDTH), tok(RWKV_COLS)],
        out_shape=[jax.ShapeDtypeStruct((N_TOK, WIDTH), F32)] * 3
        + [jax.ShapeDtypeStruct((N_TOK, RWKV_COLS), F32)],
        compiler_params=_params(1),
        name="inproj",
    )(x_all, mod, norm1, w_in_bf)


def _ctx_attn_kernel(q_ref, k_ref, v_ref, o_ref):
    outs = []
    for h in range(N_HEADS):
        sl = slice(h * HEAD_DIM, (h + 1) * HEAD_DIM)
        qh = q_ref[:, sl].astype(BF16)
        kh = k_ref[:, sl].astype(BF16)
        vh = v_ref[:, sl].astype(BF16)
        s = _dot_nt(qh, kh) * ATT_SCALE
        e = jnp.exp(s - jnp.max(s, axis=-1, keepdims=True))
        l = jnp.sum(e, axis=-1, keepdims=True)
        outs.append(_dot(e.astype(BF16), vh) / l)
    o_ref[...] = jnp.concatenate(outs, axis=-1).astype(BF16)


def _ctx_attention(q, k, v):
    spec = pl.BlockSpec((SEQ, WIDTH), lambda b: (b, 0))
    return pl.pallas_call(
        _ctx_attn_kernel,
        grid=(BATCH,),
        in_specs=[spec, spec, spec],
        out_specs=spec,
        out_shape=jax.ShapeDtypeStruct((N_CTX_TOK, WIDTH), BF16),
        compiler_params=_params(1),
        name="ctxattn",
    )(q, k, v)


def _lat_attn_kernel(q_ref, k_ref, v_ref, ck_ref, cv_ref, bias_ref, o_ref, kbf, vbf, ckbf, cvbf):
    i = pl.program_id(1)

    @pl.when(i == 0)
    def _():
        kbf[...] = k_ref[...].astype(BF16)
        vbf[...] = v_ref[...].astype(BF16)
        ckbf[...] = ck_ref[0].astype(BF16)
        cvbf[...] = cv_ref[0].astype(BF16)

    win = NA_ROWS * GRID_W
    start = pl.multiple_of(jnp.clip(i - NA_ROWS // 2, 0, GRID_ROWS - NA_ROWS) * GRID_W, GRID_W)
    outs = []
    for h in range(N_HEADS):
        sl = slice(h * HEAD_DIM, (h + 1) * HEAD_DIM)
        qh = q_ref[:, sl].astype(BF16)
        s_loc = _dot_nt(qh, kbf[pl.ds(start, win), sl]) * ATT_SCALE + bias_ref[0, h]
        s_ctx = _dot_nt(qh, ckbf[:, sl]) * ATT_SCALE
        m = jnp.maximum(jnp.max(s_loc, axis=-1, keepdims=True), jnp.max(s_ctx, axis=-1, keepdims=True))
        e_loc = jnp.exp(s_loc - m)
        e_ctx = jnp.exp(s_ctx - m)
        l = jnp.sum(e_loc, axis=-1, keepdims=True) + jnp.sum(e_ctx, axis=-1, keepdims=True)
        o = _dot(e_loc.astype(BF16), vbf[pl.ds(start, win), sl]) + _dot(e_ctx.astype(BF16), cvbf[:, sl])
        outs.append(o / l)
    o_ref[...] = jnp.concatenate(outs, axis=-1).astype(BF16)


def _na_bias_table(rpb):
    cq = jnp.arange(GRID_W)[:, None]
    ck = jnp.arange(GRID_W)[None, :]
    cs = jnp.clip(cq - NA_COLS // 2, 0, GRID_W - NA_COLS)
    inwin = (ck >= cs) & (ck < cs + NA_COLS)
    coff = jnp.clip(ck - cq + NA_COLS - 1, 0, 2 * NA_COLS - 2)
    t = jnp.where(inwin, rpb[:, :, coff].astype(F32), MASK_VALUE)
    d = jnp.arange(NA_ROWS)[:, None] + jnp.arange(NA_ROWS)[None, :]
    t = t[:, d]
    return jnp.transpose(t, (1, 0, 3, 2, 4)).reshape(NA_ROWS, N_HEADS, GRID_W, NA_ROWS * GRID_W)


def _lat_attention(q, k, v, cache_k, cache_v, bias):
    lat0 = N_CTX_TOK // GRID_W
    seq0 = N_CTX_TOK // DEC_SEQ

    def bias_idx(b, i):
        si = jnp.clip(i - NA_ROWS // 2, 0, GRID_ROWS - NA_ROWS)
        return (si - i + NA_ROWS - 1, 0, 0, 0)

    seq = pl.BlockSpec((DEC_SEQ, WIDTH), lambda b, i: (seq0 + b, 0))
    ctx = pl.BlockSpec((1, PAST_LEN, WIDTH), lambda b, i: (b, 0, 0))
    return pl.pallas_call(
        _lat_attn_kernel,
        grid=(DEC_BATCH, GRID_ROWS),
        in_specs=[pl.BlockSpec((GRID_W, WIDTH), lambda b, i: (lat0 + b * GRID_ROWS + i, 0)),
                  seq, seq, ctx, ctx,
                  pl.BlockSpec((1, N_HEADS, GRID_W, NA_ROWS * GRID_W), bias_idx)],
        out_specs=pl.BlockSpec((GRID_W, WIDTH), lambda b, i: (b * GRID_ROWS + i, 0)),
        out_shape=jax.ShapeDtypeStruct((N_LAT_TOK, WIDTH), BF16),
        scratch_shapes=[pltpu.VMEM((DEC_SEQ, WIDTH), BF16), pltpu.VMEM((DEC_SEQ, WIDTH), BF16),
                        pltpu.VMEM((PAST_LEN, WIDTH), BF16), pltpu.VMEM((PAST_LEN, WIDTH), BF16)],
        compiler_params=_params(2),
        name="latattn",
    )(q, k, v, cache_k, cache_v, bias)


def _rwkv_pre_kernel(z_ref, zp_ref, zn_ref, wts_ref, w0_ref, w2_ref, a0_ref, a2_ref, g2_ref,
                     kk_ref, ka_ref, rk_ref, ones_ref,
                     r_out, kk_out, v_out, g_out, bonus_out, lw_out, b_out, kd_out):
    has_prev, has_next = _seq_neighbours(pl.program_id(0))

    def conv(c0, c1):
        z = z_ref[:, c0:c1]
        prow = jnp.where(has_prev, zp_ref[7:8, c0:c1], 0.0)
        nrow = jnp.where(has_next, zn_ref[0:1, c0:c1], 0.0)
        zprev, znext = _shifted(z, prow, nrow)
        return zprev * wts_ref[0:1, c0:c1] + z * wts_ref[1:2, c0:c1] + znext * wts_ref[2:3, c0:c1]

    r = conv(0, WIDTH)
    kr = conv(WIDTH, 2 * WIDTH)
    v = conv(2 * WIDTH, 3 * WIDTH)
    o = 3 * WIDTH
    tw = jnp.tanh(conv(o, o + N_DIRS * LORA))
    o += N_DIRS * LORA
    xa = conv(o, o + N_DIRS * LORA)
    o += N_DIRS * LORA
    xg = conv(o, o + GATE_LORA)
    ones = ones_ref[...]

    kkraw = kr * kk_ref[...]
    kk = kkraw * lax.rsqrt(_dot(kkraw * kkraw, ones, HIGHEST) + 1e-12)
    kd_sum = jnp.zeros_like(kr)
    for d in range(N_DIRS):
        ls = slice(d * LORA, (d + 1) * LORA)
        w_log = -_softplus(-(w0_ref[d:d + 1, :] + _dot(tw[:, ls], w2_ref[d], HIGHEST))) - 0.5
        a = jax.nn.sigmoid(a0_ref[d:d + 1, :] + _dot(xa[:, ls], a2_ref[d], HIGHEST))
        kd = kr * (1.0 + (a - 1.0) * ka_ref[...])
        lw_out[d] = -jnp.exp(w_log)
        b_out[d] = kk * a
        kd_out[d] = kd
        kd_sum = kd_sum + kd
    r_out[...] = r
    kk_out[...] = kk
    v_out[...] = v
    g_out[...] = _dot(jax.nn.sigmoid(xg), g2_ref[...], HIGHEST)
    bonus_out[...] = _dot(r * kd_sum * rk_ref[...], ones, HIGHEST) * v


def _rwkv_pre(zr, w_ts, w0, w2, a0, a2, g2, k_k, k_a, r_k, ones):
    tok = pl.BlockSpec((TOK_BLOCK, WIDTH), lambda i: (i, 0))
    dtok = pl.BlockSpec((N_DIRS, TOK_BLOCK, WIDTH), lambda i: (0, i, 0))
    rows8 = TOK_BLOCK // 8
    return pl.pallas_call(
        _rwkv_pre_kernel,
        grid=(N_BLOCKS,),
        in_specs=[pl.BlockSpec((TOK_BLOCK, RWKV_COLS), lambda i: (i, 0)),
                  pl.BlockSpec((8, RWKV_COLS), lambda i: (jnp.maximum(i * rows8 - 1, 0), 0)),
                  pl.BlockSpec((8, RWKV_COLS), lambda i: (jnp.minimum((i + 1) * rows8, N_TOK // 8 - 1), 0)),
                  _const_spec((3, RWKV_COLS)),
                  _const_spec((N_DIRS, WIDTH)), _const_spec((N_DIRS, LORA, WIDTH)),
                  _const_spec((N_DIRS, WIDTH)), _const_spec((N_DIRS, LORA, WIDTH)),
                  _const_spec((GATE_LORA, WIDTH)),
                  _const_spec((1, WIDTH)), _const_spec((1, WIDTH)), _const_spec((1, WIDTH)),
                  _const_spec((WIDTH, WIDTH))],
        out_specs=[tok] * 5 + [dtok] * 3,
        out_shape=[jax.ShapeDtypeStruct((N_TOK, WIDTH), F32)] * 5
        + [jax.ShapeDtypeStruct((N_DIRS, N_TOK, WIDTH), F32)] * 3,
        compiler_params=_params(1),
        name="rwkvpre",
    )(zr, zr, zr, w_ts, w0, w2, a0, a2, g2, k_k, k_a, r_k, ones)


def _scan_kernel(r_ref, kk_ref, v_ref, lw_ref, b_ref, kd_ref, s0_ref, y_ref, s_ref):
    d = pl.program_id(0)
    c = pl.program_id(1)
    g = jnp.where(d == 0, c, N_CHUNKS - 1 - c)
    is_ctx = g < CTX_CHUNKS
    local = jnp.where(is_ctx, g % CHUNKS_PER_CTX_SEQ, (g - CTX_CHUNKS) % CHUNKS_PER_LAT_SEQ)
    last = jnp.where(is_ctx, CHUNKS_PER_CTX_SEQ - 1, CHUNKS_PER_LAT_SEQ - 1)
    first_in_scan = jnp.where(d == 0, local == 0, local == last)

    @pl.when(first_in_scan)
    def _():
        s_ref[...] = s0_ref[...]

    row = lax.broadcasted_iota(jnp.int32, (CHUNK, CHUNK), 0)
    col = lax.broadcasted_iota(jnp.int32, (CHUNK, CHUNK), 1)
    ahead = (col - row) * (2 * d - 1)
    strict = ahead > 0
    incl = ahead >= 0
    eye = (col == row).astype(F32)

    lw = lw_ref[0]
    cum = _dot(incl.astype(F32), lw, HIGHEST)
    tot = jnp.sum(lw, axis=0, keepdims=True)
    p_in = jnp.exp(cum)
    p_neg = jnp.exp(-cum)
    p_rem = jnp.exp(tot - cum)
    kt = kk_ref[...] * jnp.exp(cum - lw)
    rt = r_ref[...] * p_in
    b = b_ref[0]
    kd = kd_ref[0]
    bt = b * p_neg
    kdt = kd * p_neg
    bh = b * p_rem
    kh = kd * p_rem
    p_end = jnp.exp(tot)
    v = v_ref[...]

    ys = []
    for h in range(N_HEADS):
        sl = slice(h * HEAD_DIM, (h + 1) * HEAD_DIM)
        kt_h, rt_h, bt_h, kdt_h, v_h = kt[:, sl], rt[:, sl], bt[:, sl], kdt[:, sl], v[:, sl]
        a_b = jnp.where(strict, _dot_nt(kt_h, bt_h, HIGHEST), 0.0)
        a_k = jnp.where(strict, _dot_nt(kt_h, kdt_h, HIGHEST), 0.0)
        a_rb = jnp.where(incl, _dot_nt(rt_h, bt_h, HIGHEST), 0.0)
        a_rk = jnp.where(incl, _dot_nt(rt_h, kdt_h, HIGHEST), 0.0)
        pw = -a_b
        inv = eye + pw
        for _ in range(5):
            pw = _dot(pw, pw, HIGHEST)
            inv = inv + _dot(pw, inv, HIGHEST)
        kx = _dot(inv, kt_h, HIGHEST)
        u0 = _dot(inv, _dot(a_k, v_h, HIGHEST), HIGHEST)
        rx = rt_h - _dot(a_rb, kx, HIGHEST)
        y0 = _dot(a_rk, v_h, HIGHEST) - _dot(a_rb, u0, HIGHEST)
        s = s_ref[0, 0, h]
        ys.append(_dot_nt(rx, s, HIGHEST) + y0)
        m = eye * p_end[:, sl] - _dot_tn(bh[:, sl], kx, HIGHEST)
        s_ref[0, 0, h] = (_dot_nt(s, m, HIGHEST) + _dot_tn(v_h, kh[:, sl], HIGHEST)
                          - _dot_tn(u0, bh[:, sl], HIGHEST))
    y_ref[0] = jnp.concatenate(ys, axis=-1)


def _scan(r, kk, v, lw, b, kd, s0):
    def chunk(d, c):
        return jnp.where(d == 0, c, N_CHUNKS - 1 - c)

    def seq(d, c):
        g = chunk(d, c)
        return jnp.where(g < CTX_CHUNKS, g // CHUNKS_PER_CTX_SEQ,
                         BATCH + (g - CTX_CHUNKS) // CHUNKS_PER_LAT_SEQ)

    tok = pl.BlockSpec((CHUNK, WIDTH), lambda d, c: (chunk(d, c), 0))
    dtok = pl.BlockSpec((1, CHUNK, WIDTH), lambda d, c: (d, chunk(d, c), 0))
    st = pl.BlockSpec((1, 1, N_HEADS, HEAD_DIM, HEAD_DIM), lambda d, c: (seq(d, c), d, 0, 0, 0))
    return pl.pallas_call(
        _scan_kernel,
        grid=(N_DIRS, N_CHUNKS),
        in_specs=[tok, tok, tok, dtok, dtok, dtok, st],
        out_specs=[dtok, st],
        out_shape=[jax.ShapeDtypeStruct((N_DIRS, N_TOK, WIDTH), F32),
                   jax.ShapeDtypeStruct((N_SEQS, N_DIRS, N_HEADS, HEAD_DIM, HEAD_DIM), F32)],
        compiler_params=_params(2),
        name="scan",
    )(r, kk, v, lw, b, kd, s0)


def _post_kernel(y_ref, bonus_ref, g_ref, att_ref, x_ref, mod_ref, lng_ref, lnb_ref, ones_ref,
                 wout_ref, n2_ref, x1_ref, h2_ref):
    ones = ones_ref[...]
    inv_n = 1.0 / HEAD_DIM
    y = y_ref[0] + y_ref[1]
    yc = y - _dot(y, ones, HIGHEST) * inv_n
    var = _dot(yc * yc, ones, HIGHEST) * inv_n
    yn = yc * lax.rsqrt(var + GN_EPS) * lng_ref[...] + lnb_ref[...]
    r_out = ((yn + bonus_ref[...]) * g_ref[...]).astype(BF16)
    o = _dot(att_ref[...], wout_ref[0:WIDTH, :]) + _dot(r_out, wout_ref[WIDTH:, :])
    x1 = x_ref[...] + mod_ref[0, 2:3, :] * o
    x1_ref[...] = x1
    h2 = _rmsnorm(x1, n2_ref[...]) * (1.0 + mod_ref[0, 4:5, :]) + mod_ref[0, 3:4, :]
    h2_ref[...] = h2.astype(BF16)


def _post(y, bonus, g, att, x_all, mod, ln_g, ln_b, ones, w_out_bf, norm2):
    tok = lambda c: pl.BlockSpec((TOK_BLOCK, c), lambda i: (i, 0))
    return pl.pallas_call(
        _post_kernel,
        grid=(N_BLOCKS,),
        in_specs=[pl.BlockSpec((N_DIRS, TOK_BLOCK, WIDTH), lambda i: (0, i, 0)),
                  tok(WIDTH), tok(WIDTH), tok(WIDTH), tok(D_MODEL),
                  pl.BlockSpec((1, 6, D_MODEL), lambda i: (_mod_row(i), 0, 0)),
                  _const_spec((1, WIDTH)), _const_spec((1, WIDTH)), _const_spec((WIDTH, WIDTH)),
                  _const_spec((2 * WIDTH, D_MODEL)), _const_spec((1, D_MODEL))],
        out_specs=[tok(D_MODEL), tok(D_MODEL)],
        out_shape=[jax.ShapeDtypeStruct((N_TOK, D_MODEL), F32),
                   jax.ShapeDtypeStruct((N_TOK, D_MODEL), BF16)],
        compiler_params=_params(1),
        name="post",
    )(y, bonus, g, att, x_all, mod, ln_g, ln_b, ones, w_out_bf, norm2)


def _ffn_kernel(x1_ref, h2_ref, hp_ref, hn_ref, mod_ref, w1_ref, w3_ref, wc_ref, w2_ref, nf_ref, o_ref):
    has_prev, has_next = _seq_neighbours(pl.program_id(0))
    h2 = h2_ref[...]
    halo = jnp.concatenate([hp_ref[...], hn_ref[...]], axis=0)
    acc = jnp.zeros((TOK_BLOCK, D_MODEL), F32)
    for f in range(D_FF // FF_TILE):
        fs = slice(f * FF_TILE, (f + 1) * FF_TILE)
        a = _dot(h2, w1_ref[:, fs])
        ah = _dot(halo, w1_ref[:, fs])
        prow = jnp.where(has_prev, ah[15:16, :], 0.0)
        nrow = jnp.where(has_next, ah[16:17, :], 0.0)
        aprev, anext = _shifted(a, prow, nrow)
        cv = aprev * wc_ref[0:1, fs] + a * wc_ref[1:2, fs] + anext * wc_ref[2:3, fs]
        act = _silu(cv) * _dot(h2, w3_ref[:, fs])
        acc = acc + _dot(act.astype(BF16), w2_ref[fs, :])
    x2 = x1_ref[...] + mod_ref[0, 5:6, :] * acc
    o_ref[...] = _rmsnorm(x2, nf_ref[...])


def _ffn(x1, h2, mod, w1_bf, w3_bf, wc, w2_bf, norm_f):
    tok = pl.BlockSpec((TOK_BLOCK, D_MODEL), lambda i: (i, 0))
    rows16 = TOK_BLOCK // 16
    return pl.pallas_call(
        _ffn_kernel,
        grid=(N_BLOCKS,),
        in_specs=[tok, tok,
                  pl.BlockSpec((16, D_MODEL), lambda i: (jnp.maximum(i * rows16 - 1, 0), 0)),
                  pl.BlockSpec((16, D_MODEL), lambda i: (jnp.minimum((i + 1) * rows16, N_TOK // 16 - 1), 0)),
                  pl.BlockSpec((1, 6, D_MODEL), lambda i: (_mod_row(i), 0, 0)),
                  _const_spec((D_MODEL, D_FF)), _const_spec((D_MODEL, D_FF)), _const_spec((3, D_FF)),
                  _const_spec((D_FF, D_MODEL)), _const_spec((1, D_MODEL))],
        out_specs=tok,
        out_shape=jax.ShapeDtypeStruct((N_TOK, D_MODEL), F32),
        compiler_params=_params(1),
        name="ffn",
    )(x1, h2, h2, h2, mod, w1_bf, w3_bf, wc, w2_bf, norm_f)


def kernel(x_prompt, x_sample, cache_k, cache_v, state_rwkv, c, c_ctx, w_ada, b_ada, norm1, norm2,
           w_in, w_ts, w0, w2, a0, a2, g2, k_k, k_a, r_k, ln_x_g, ln_x_b, rpb, w_out,
           w_ffn1, w_ffn3, w_ffn_conv, w_ffn2, norm_f):
    x_all = jnp.concatenate([x_prompt.reshape(N_CTX_TOK, D_MODEL), x_sample.reshape(N_LAT_TOK, D_MODEL)], axis=0)
    cond = jnp.concatenate([c_ctx[None, :], c, jnp.zeros((8 - 1 - DEC_BATCH, D_MODEL), F32)], axis=0)
    mod = _modulation(cond, w_ada[0], b_ada[0]).reshape(8, 6, D_MODEL)

    q, k, v, zr = _inproj(x_all, mod, norm1[0].reshape(1, D_MODEL), w_in[0].astype(BF16))

    att_ctx = _ctx_attention(q, k, v)
    att_lat = _lat_attention(q, k, v, cache_k[:, 0].reshape(DEC_BATCH, PAST_LEN, WIDTH),
                             cache_v[:, 0].reshape(DEC_BATCH, PAST_LEN, WIDTH), _na_bias_table(rpb[0]))
    att = jnp.concatenate([att_ctx, att_lat], axis=0)

    head_id = jnp.arange(WIDTH) // HEAD_DIM
    ones = (head_id[:, None] == head_id[None, :]).astype(F32)
    row = lambda t: t.reshape(1, -1)
    r, kk, vv, g, bonus, lw, b, kd = _rwkv_pre(zr, w_ts[0], w0[0], w2[0], a0[0], a2[0], g2[0],
                                               row(k_k[0]), row(k_a[0]), row(r_k[0]), ones)
    s0 = jnp.concatenate([jnp.zeros((BATCH,) + state_rwkv.shape[2:], F32), state_rwkv[:, 0]], axis=0)
    y, s_fin = _scan(r, kk, vv, lw, b, kd, s0)

    x1, h2 = _post(y, bonus, g, att, x_all, mod, row(ln_x_g[0]), row(ln_x_b[0]), ones,
                   w_out[0].astype(BF16), row(norm2[0]))
    out = _ffn(x1, h2, mod, w_ffn1[0].astype(BF16), w_ffn3[0].astype(BF16), w_ffn_conv[0],
               w_ffn2[0].astype(BF16), row(norm_f))

    y_prompt = out[:N_CTX_TOK].reshape(BATCH, SEQ, D_MODEL)
    y_sample = out[N_CTX_TOK:].reshape(DEC_BATCH, DEC_SEQ, D_MODEL)
    new_k = k[:N_CTX_TOK].reshape(BATCH, 1, SEQ, N_HEADS, HEAD_DIM)
    new_v = v[:N_CTX_TOK].reshape(BATCH, 1, SEQ, N_HEADS, HEAD_DIM)
    new_s = s_fin[:BATCH].reshape(BATCH, 1, N_DIRS, N_HEADS, HEAD_DIM, HEAD_DIM)
    return (y_prompt, y_sample, new_k, new_v, new_s)
```

```python
import functools

import jax
import jax.numpy as jnp
from jax import lax
from jax.experimental import pallas as pl
from jax.experimental.pallas import tpu as pltpu

F32 = jnp.float32
BF16 = jnp.bfloat16
HIGHEST = lax.Precision.HIGHEST

D_MODEL = 1024
BATCH = 16
SEQ = 256
DEC_BATCH = 2
DEC_SEQ = 2048
PAST_LEN = 512
GRID_W = 64
HEAD_DIM = 64
N_HEADS = 8
WIDTH = N_HEADS * HEAD_DIM
NA_ROWS = 8
NA_COLS = 16
N_DIRS = 2
LORA = 64
GATE_LORA = 128
D_FF = 2816
EPS = 1e-6
GN_EPS = 64e-5
ATT_SCALE = HEAD_DIM ** -0.5
RWKV_COLS = 3 * WIDTH + N_DIRS * 2 * LORA + GATE_LORA
MASK_VALUE = -1e30

N_CTX_TOK = BATCH * SEQ
N_LAT_TOK = DEC_BATCH * DEC_SEQ
N_TOK = N_CTX_TOK + N_LAT_TOK
TOK_BLOCK = 256
N_BLOCKS = N_TOK // TOK_BLOCK
CTX_BLOCKS = N_CTX_TOK // TOK_BLOCK
LAT_BLOCKS_PER_SEQ = DEC_SEQ // TOK_BLOCK
N_SEQS = BATCH + DEC_BATCH
CHUNK = 64
N_CHUNKS = N_TOK // CHUNK
CTX_CHUNKS = N_CTX_TOK // CHUNK
CHUNKS_PER_CTX_SEQ = SEQ // CHUNK
CHUNKS_PER_LAT_SEQ = DEC_SEQ // CHUNK
GRID_ROWS = DEC_SEQ // GRID_W
FF_TILE = D_FF // 2
VMEM_LIMIT = 56 * 1024 * 1024


def _params(n_axes, limit=VMEM_LIMIT):
    return pltpu.CompilerParams(dimension_semantics=("arbitrary",) * n_axes, vmem_limit_bytes=limit)


def _const_spec(shape):
    zeros = (0,) * len(shape)
    return pl.BlockSpec(shape, lambda *_: zeros, pipeline_mode=pl.Buffered(1))


def _mod_row(i):
    return jnp.where(i < CTX_BLOCKS, 0, 1 + (i - CTX_BLOCKS) // LAT_BLOCKS_PER_SEQ)


def _seq_neighbours(i):
    j = (i - CTX_BLOCKS) % LAT_BLOCKS_PER_SEQ
    lat = i >= CTX_BLOCKS
    return lat & (j != 0), lat & (j != LAT_BLOCKS_PER_SEQ - 1)


def _silu(x):
    return x * jax.nn.sigmoid(x)


def _softplus(x):
    return jnp.maximum(x, 0.0) + jnp.log1p(jnp.exp(-jnp.abs(x)))


def _rmsnorm(x, g):
    return x * lax.rsqrt(jnp.mean(x * x, axis=-1, keepdims=True) + EPS) * g


def _dot(a, b, precision=None):
    return jnp.dot(a, b, precision=precision, preferred_element_type=F32)


def _dot_nt(a, b, precision=None):
    return lax.dot_general(a, b, (((1,), (1,)), ((), ())), precision=precision, preferred_element_type=F32)


def _dot_tn(a, b, precision=None):
    return lax.dot_general(a, b, (((0,), (0,)), ((), ())), precision=precision, preferred_element_type=F32)


def _shifted(a, prev_row, next_row):
    t = a.shape[0]
    rid = lax.broadcasted_iota(jnp.int32, (t, 1), 0)
    prev = jnp.where(rid == 0, prev_row, pltpu.roll(a, 1, axis=0))
    nxt = jnp.where(rid == t - 1, next_row, pltpu.roll(a, t - 1, axis=0))
    return prev, nxt


def _mod_kernel(cond_ref, w_ref, b_ref, o_ref):
    o_ref[...] = _dot(_silu(cond_ref[...]), w_ref[...], HIGHEST) + b_ref[...]


def _modulation(cond, w_ada, b_ada):
    n = 6 * D_MODEL
    return pl.pallas_call(
        _mod_kernel,
        grid=(6,),
        in_specs=[pl.BlockSpec((8, D_MODEL), lambda j: (0, 0)),
                  pl.BlockSpec((D_MODEL, D_MODEL), lambda j: (0, j)),
                  pl.BlockSpec((1, D_MODEL), lambda j: (0, j))],
        out_specs=pl.BlockSpec((8, D_MODEL), lambda j: (0, j)),
        out_shape=jax.ShapeDtypeStruct((8, n), F32),
        compiler_params=_params(1),
        name="mod",
    )(cond, w_ada, b_ada.reshape(1, n))


def _inproj_kernel(x_ref, mod_ref, n1_ref, w_ref, q_ref, k_ref, v_ref, zr_ref):
    h = _rmsnorm(x_ref[...], n1_ref[...]) * (1.0 + mod_ref[0, 1:2, :]) + mod_ref[0, 0:1, :]
    h = h.astype(BF16)
    q_ref[...] = _dot(h, w_ref[:, 0:WIDTH])
    k_ref[...] = _dot(h, w_ref[:, WIDTH:2 * WIDTH])
    v_ref[...] = _dot(h, w_ref[:, 2 * WIDTH:3 * WIDTH])
    zr_ref[...] = _dot(h, w_ref[:, 3 * WIDTH:])


def _inproj(x_all, mod, norm1, w_in_bf):
    tok = lambda c: pl.BlockSpec((TOK_BLOCK, c), lambda i: (i, 0))
    return pl.pallas_call(
        _inproj_kernel,
        grid=(N_BLOCKS,),
        in_specs=[tok(D_MODEL),
                  pl.BlockSpec((1, 6, D_MODEL), lambda i: (_mod_row(i), 0, 0)),
                  _const_spec((1, D_MODEL)),
                  _const_spec((D_MODEL, 3 * WIDTH + RWKV_COLS))],
        out_specs=[tok(WIDTH), tok(WIDTH), tok(WIDTH), tok(RWKV_COLS)],
        out_shape=[jax.ShapeDtypeStruct((N_TOK, WIDTH), F32)] * 3
        + [jax.ShapeDtypeStruct((N_TOK, RWKV_COLS), F32)],
        compiler_params=_params(1),
        name="inproj",
    )(x_all, mod, norm1, w_in_bf)


def _ctx_attn_kernel(q_ref, k_ref, v_ref, o_ref):
    outs = []
    for h in range(N_HEADS):
        sl = slice(h * HEAD_DIM, (h + 1) * HEAD_DIM)
        qh = q_ref[:, sl].astype(BF16)
        kh = k_ref[:, sl].astype(BF16)
        vh = v_ref[:, sl].astype(BF16)
        s = _dot_nt(qh, kh) * ATT_SCALE
        e = jnp.exp(s - jnp.max(s, axis=-1, keepdims=True))
        l = jnp.sum(e, axis=-1, keepdims=True)
        outs.append(_dot(e.astype(BF16), vh) / l)
    o_ref[...] = jnp.concatenate(outs, axis=-1).astype(BF16)


def _ctx_attention(q, k, v):
    spec = pl.BlockSpec((SEQ, WIDTH), lambda b: (b, 0))
    return pl.pallas_call(
        _ctx_attn_kernel,
        grid=(BATCH,),
        in_specs=[spec, spec, spec],
        out_specs=spec,
        out_shape=jax.ShapeDtypeStruct((N_CTX_TOK, WIDTH), BF16),
        compiler_params=_params(1),
        name="ctxattn",
    )(q, k, v)


def _lat_attn_kernel(q_ref, k_ref, v_ref, ck_ref, cv_ref, bias_ref, o_ref, kbf, vbf, ckbf, cvbf):
    i = pl.program_id(1)

    @pl.when(i == 0)
    def _():
        kbf[...] = k_ref[...].astype(BF16)
        vbf[...] = v_ref[...].astype(BF16)
        ckbf[...] = ck_ref[0].astype(BF16)
        cvbf[...] = cv_ref[0].astype(BF16)

    win = NA_ROWS * GRID_W
    start = pl.multiple_of(jnp.clip(i - NA_ROWS // 2, 0, GRID_ROWS - NA_ROWS) * GRID_W, GRID_W)
    outs = []
    for h in range(N_HEADS):
        sl = slice(h * HEAD_DIM, (h + 1) * HEAD_DIM)
        qh = q_ref[:, sl].astype(BF16)
        s_loc = _dot_nt(qh, kbf[pl.ds(start, win), sl]) * ATT_SCALE + bias_ref[0, h]
        s_ctx = _dot_nt(qh, ckbf[:, sl]) * ATT_SCALE
        m = jnp.maximum(jnp.max(s_loc, axis=-1, keepdims=True), jnp.max(s_ctx, axis=-1, keepdims=True))
        e_loc = jnp.exp(s_loc - m)
        e_ctx = jnp.exp(s_ctx - m)
        l = jnp.sum(e_loc, axis=-1, keepdims=True) + jnp.sum(e_ctx, axis=-1, keepdims=True)
        o = _dot(e_loc.astype(BF16), vbf[pl.ds(start, win), sl]) + _dot(e_ctx.astype(BF16), cvbf[:, sl])
        outs.append(o / l)
    o_ref[...] = jnp.concatenate(outs, axis=-1).astype(BF16)


def _na_bias_table(rpb):
    cq = jnp.arange(GRID_W)[:, None]
    ck = jnp.arange(GRID_W)[None, :]
    cs = jnp.clip(cq - NA_COLS // 2, 0, GRID_W - NA_COLS)
    inwin = (ck >= cs) & (ck < cs + NA_COLS)
    coff = jnp.clip(ck - cq + NA_COLS - 1, 0, 2 * NA_COLS - 2)
    t = jnp.where(inwin, rpb[:, :, coff].astype(F32), MASK_VALUE)
    d = jnp.arange(NA_ROWS)[:, None] + jnp.arange(NA_ROWS)[None, :]
    t = t[:, d]
    return jnp.transpose(t, (1, 0, 3, 2, 4)).reshape(NA_ROWS, N_HEADS, GRID_W, NA_ROWS * GRID_W)


def _lat_attention(q, k, v, cache_k, cache_v, bias):
    lat0 = N_CTX_TOK // GRID_W
    seq0 = N_CTX_TOK // DEC_SEQ

    def bias_idx(b, i):
        si = jnp.clip(i - NA_ROWS // 2, 0, GRID_ROWS - NA_ROWS)
        return (si - i + NA_ROWS - 1, 0, 0, 0)

    seq = pl.BlockSpec((DEC_SEQ, WIDTH), lambda b, i: (seq0 + b, 0))
    ctx = pl.BlockSpec((1, PAST_LEN, WIDTH), lambda b, i: (b, 0, 0))
    return pl.pallas_call(
        _lat_attn_kernel,
        grid=(DEC_BATCH, GRID_ROWS),
        in_specs=[pl.BlockSpec((GRID_W, WIDTH), lambda b, i: (lat0 + b * GRID_ROWS + i, 0)),
                  seq, seq, ctx, ctx,
                  pl.BlockSpec((1, N_HEADS, GRID_W, NA_ROWS * GRID_W), bias_idx)],
        out_specs=pl.BlockSpec((GRID_W, WIDTH), lambda b, i: (b * GRID_ROWS + i, 0)),
        out_shape=jax.ShapeDtypeStruct((N_LAT_TOK, WIDTH), BF16),
        scratch_shapes=[pltpu.VMEM((DEC_SEQ, WIDTH), BF16), pltpu.VMEM((DEC_SEQ, WIDTH), BF16),
                        pltpu.VMEM((PAST_LEN, WIDTH), BF16), pltpu.VMEM((PAST_LEN, WIDTH), BF16)],
        compiler_params=_params(2),
        name="latattn",
    )(q, k, v, cache_k, cache_v, bias)


def _rwkv_pre_kernel(z_ref, zp_ref, zn_ref, wts_ref, w0_ref, w2_ref, a0_ref, a2_ref, g2_ref,
                     kk_ref, ka_ref, rk_ref, ones_ref,
                     r_out, kk_out, v_out, g_out, bonus_out, lw_out, b_out, kd_out):
    has_prev, has_next = _seq_neighbours(pl.program_id(0))

    def conv(c0, c1):
        z = z_ref[:, c0:c1]
        prow = jnp.where(has_prev, zp_ref[7:8, c0:c1], 0.0)
        nrow = jnp.where(has_next, zn_ref[0:1, c0:c1], 0.0)
        zprev, znext = _shifted(z, prow, nrow)
        return zprev * wts_ref[0:1, c0:c1] + z * wts_ref[1:2, c0:c1] + znext * wts_ref[2:3, c0:c1]

    r = conv(0, WIDTH)
    kr = conv(WIDTH, 2 * WIDTH)
    v = conv(2 * WIDTH, 3 * WIDTH)
    o = 3 * WIDTH
    tw = jnp.tanh(conv(o, o + N_DIRS * LORA))
    o += N_DIRS * LORA
    xa = conv(o, o + N_DIRS * LORA)
    o += N_DIRS * LORA
    xg = conv(o, o + GATE_LORA)
    ones = ones_ref[...]

    kkraw = kr * kk_ref[...]
    kk = kkraw * lax.rsqrt(_dot(kkraw * kkraw, ones, HIGHEST) + 1e-12)
    kd_sum = jnp.zeros_like(kr)
    for d in range(N_DIRS):
        ls = slice(d * LORA, (d + 1) * LORA)
        w_log = -_softplus(-(w0_ref[d:d + 1, :] + _dot(tw[:, ls], w2_ref[d], HIGHEST))) - 0.5
        a = jax.nn.sigmoid(a0_ref[d:d + 1, :] + _dot(xa[:, ls], a2_ref[d], HIGHEST))
        kd = kr * (1.0 + (a - 1.0) * ka_ref[...])
        lw_out[d] = -jnp.exp(w_log)
        b_out[d] = kk * a
        kd_out[d] = kd
        kd_sum = kd_sum + kd
    r_out[...] = r
    kk_out[...] = kk
    v_out[...] = v
    g_out[...] = _dot(jax.nn.sigmoid(xg), g2_ref[...], HIGHEST)
    bonus_out[...] = _dot(r * kd_sum * rk_ref[...], ones, HIGHEST) * v


def _rwkv_pre(zr, w_ts, w0, w2, a0, a2, g2, k_k, k_a, r_k, ones):
    tok = pl.BlockSpec((TOK_BLOCK, WIDTH), lambda i: (i, 0))
    dtok = pl.BlockSpec((N_DIRS, TOK_BLOCK, WIDTH), lambda i: (0, i, 0))
    rows8 = TOK_BLOCK // 8
    return pl.pallas_call(
        _rwkv_pre_kernel,
        grid=(N_BLOCKS,),
        in_specs=[pl.BlockSpec((TOK_BLOCK, RWKV_COLS), lambda i: (i, 0)),
                  pl.BlockSpec((8, RWKV_COLS), lambda i: (jnp.maximum(i * rows8 - 1, 0), 0)),
                  pl.BlockSpec((8, RWKV_COLS), lambda i: (jnp.minimum((i + 1) * rows8, N_TOK // 8 - 1), 0)),
                  _const_spec((3, RWKV_COLS)),
                  _const_spec((N_DIRS, WIDTH)), _const_spec((N_DIRS, LORA, WIDTH)),
                  _const_spec((N_DIRS, WIDTH)), _const_spec((N_DIRS, LORA, WIDTH)),
                  _const_spec((GATE_LORA, WIDTH)),
                  _const_spec((1, WIDTH)), _const_spec((1, WIDTH)), _const_spec((1, WIDTH)),
                  _const_spec((WIDTH, WIDTH))],
        out_specs=[tok] * 5 + [dtok] * 3,
        out_shape=[jax.ShapeDtypeStruct((N_TOK, WIDTH), F32)] * 5
        + [jax.ShapeDtypeStruct((N_DIRS, N_TOK, WIDTH), F32)] * 3,
        compiler_params=_params(1),
        name="rwkvpre",
    )(zr, zr, zr, w_ts, w0, w2, a0, a2, g2, k_k, k_a, r_k, ones)


GROUP = 4
GROUP_W = GROUP * HEAD_DIM


def _split(x):
    hi = x.astype(BF16)
    return hi, (x - hi.astype(F32)).astype(BF16)


def _blockdiag(x):
    def one(p):
        head = lax.broadcasted_iota(jnp.int32, p.shape, 1) // HEAD_DIM
        return jnp.concatenate([jnp.where(head == h, p, jnp.zeros_like(p)) for h in range(GROUP)], axis=0)
    return one(x[0]), one(x[1])


def _mm3(dot, a, b):
    m = a[0].shape[0]
    hi = dot(jnp.concatenate([a[0], a[1]], axis=0), b[0])
    return hi[:m] + hi[m:] + dot(a[0], b[1])


def _mm(a, b):
    return _mm3(_dot, a, b)


def _mm_nt(a, b):
    return _mm3(_dot_nt, a, b)


def _scan_kernel(r_ref, kk_ref, v_ref, lw_ref, b_ref, kd_ref, s0_ref, y_ref, s_ref):
    d = pl.program_id(0)
    c = pl.program_id(1)
    g = jnp.where(d == 0, c, N_CHUNKS - 1 - c)
    is_ctx = g < CTX_CHUNKS
    local = jnp.where(is_ctx, g % CHUNKS_PER_CTX_SEQ, (g - CTX_CHUNKS) % CHUNKS_PER_LAT_SEQ)
    last = jnp.where(is_ctx, CHUNKS_PER_CTX_SEQ - 1, CHUNKS_PER_LAT_SEQ - 1)
    first_in_scan = jnp.where(d == 0, local == 0, local == last)

    @pl.when(first_in_scan)
    def _():
        s_ref[...] = s0_ref[...]

    row = lax.broadcasted_iota(jnp.int32, (CHUNK, GROUP_W), 0)
    lane = lax.broadcasted_iota(jnp.int32, (CHUNK, GROUP_W), 1)
    lane_head = lane // HEAD_DIM
    col = lane % CHUNK
    ahead = (col - row) * (2 * d - 1)
    strict = ahead > 0
    incl = ahead >= 0
    eye = (col == row).astype(F32)

    lw = lw_ref[0]
    cum = _dot(incl[:, :CHUNK].astype(F32), lw, HIGHEST)
    tot = jnp.sum(lw, axis=0, keepdims=True)
    p_in = jnp.exp(cum)
    p_neg = jnp.exp(-cum)
    p_rem = jnp.exp(tot - cum)
    kt = kk_ref[...] * jnp.exp(cum - lw)
    rt = r_ref[...] * p_in
    b = b_ref[0]
    kd = kd_ref[0]
    bt = b * p_neg
    kdt = kd * p_neg
    bh = b * p_rem
    kh = kd * p_rem
    p_end = jnp.exp(tot)
    v = v_ref[...]

    for gi in range(N_HEADS // GROUP):
        sl = slice(gi * GROUP_W, (gi + 1) * GROUP_W)
        kt_g, rt_g, v_g = kt[:, sl], rt[:, sl], v[:, sl]
        lhs = _split(jnp.concatenate([kt_g, rt_g], axis=0))
        g_b = _mm_nt(lhs, _blockdiag(_split(bt[:, sl])))
        g_k = _mm_nt(lhs, _blockdiag(_split(kdt[:, sl])))
        a_k = jnp.where(strict, g_k[:CHUNK], 0.0)
        a_rb = jnp.where(incl, g_b[CHUNK:], 0.0)
        a_rk = jnp.where(incl, g_k[CHUNK:], 0.0)
        pw = jnp.where(strict, -g_b[:CHUNK], 0.0)
        inv = eye + pw
        for _ in range(5):
            pw = _mm(_split(pw), _blockdiag(_split(pw)))
            inv = inv + _mm(_split(pw), _blockdiag(_split(inv)))
        v_bd = _blockdiag(_split(v_g))
        av = _mm(_split(jnp.concatenate([a_k, a_rk], axis=0)), v_bd)
        inv_s = _split(inv)
        kx = _mm(inv_s, _blockdiag(_split(kt_g)))
        u0 = _mm(inv_s, _blockdiag(_split(av[:CHUNK])))
        s = s_ref[0, 0, :, sl]
        z = _mm_nt(_split(jnp.concatenate([rt_g, kx], axis=0)), _blockdiag(_split(s)))
        u = z[CHUNK:] + u0
        y_ref[0, :, sl] = z[:CHUNK] + av[CHUNK:] - _mm(_split(a_rb), _blockdiag(_split(u)))
        vu_t = _split(jnp.transpose(jnp.concatenate([v_g, -u], axis=0)))
        upd = _mm(vu_t, _split(jnp.concatenate([kh[:, sl], bh[:, sl]], axis=0)))
        s_new = s * p_end[:, sl]
        for h in range(GROUP):
            s_new = s_new + jnp.where(lane_head == h, upd[h * HEAD_DIM:(h + 1) * HEAD_DIM], 0.0)
        s_ref[0, 0, :, sl] = s_new


def _scan(r, kk, v, lw, b, kd, s0):
    def chunk(d, c):
        return jnp.where(d == 0, c, N_CHUNKS - 1 - c)

    def seq(d, c):
        g = chunk(d, c)
        return jnp.where(g < CTX_CHUNKS, g // CHUNKS_PER_CTX_SEQ,
                         BATCH + (g - CTX_CHUNKS) // CHUNKS_PER_LAT_SEQ)

    tok = pl.BlockSpec((CHUNK, WIDTH), lambda d, c: (chunk(d, c), 0))
    dtok = pl.BlockSpec((1, CHUNK, WIDTH), lambda d, c: (d, chunk(d, c), 0))
    st = pl.BlockSpec((1, 1, HEAD_DIM, WIDTH), lambda d, c: (seq(d, c), d, 0, 0))
    return pl.pallas_call(
        _scan_kernel,
        grid=(N_DIRS, N_CHUNKS),
        in_specs=[tok, tok, tok, dtok, dtok, dtok, st],
        out_specs=[dtok, st],
        out_shape=[jax.ShapeDtypeStruct((N_DIRS, N_TOK, WIDTH), F32),
                   jax.ShapeDtypeStruct((N_SEQS, N_DIRS, HEAD_DIM, WIDTH), F32)],
        compiler_params=_params(2),
        name="scan",
    )(r, kk, v, lw, b, kd, s0)


def _post_kernel(y_ref, bonus_ref, g_ref, att_ref, x_ref, mod_ref, lng_ref, lnb_ref, ones_ref,
                 wout_ref, n2_ref, x1_ref, h2_ref):
    ones = ones_ref[...]
    inv_n = 1.0 / HEAD_DIM
    y = y_ref[0] + y_ref[1]
    yc = y - _dot(y, ones, HIGHEST) * inv_n
    var = _dot(yc * yc, ones, HIGHEST) * inv_n
    yn = yc * lax.rsqrt(var + GN_EPS) * lng_ref[...] + lnb_ref[...]
    r_out = ((yn + bonus_ref[...]) * g_ref[...]).astype(BF16)
    o = _dot(att_ref[...], wout_ref[0:WIDTH, :]) + _dot(r_out, wout_ref[WIDTH:, :])
    x1 = x_ref[...] + mod_ref[0, 2:3, :] * o
    x1_ref[...] = x1
    h2 = _rmsnorm(x1, n2_ref[...]) * (1.0 + mod_ref[0, 4:5, :]) + mod_ref[0, 3:4, :]
    h2_ref[...] = h2.astype(BF16)


def _post(y, bonus, g, att, x_all, mod, ln_g, ln_b, ones, w_out_bf, norm2):
    tok = lambda c: pl.BlockSpec((TOK_BLOCK, c), lambda i: (i, 0))
    return pl.pallas_call(
        _post_kernel,
        grid=(N_BLOCKS,),
        in_specs=[pl.BlockSpec((N_DIRS, TOK_BLOCK, WIDTH), lambda i: (0, i, 0)),
                  tok(WIDTH), tok(WIDTH), tok(WIDTH), tok(D_MODEL),
                  pl.BlockSpec((1, 6, D_MODEL), lambda i: (_mod_row(i), 0, 0)),
                  _const_spec((1, WIDTH)), _const_spec((1, WIDTH)), _const_spec((WIDTH, WIDTH)),
                  _const_spec((2 * WIDTH, D_MODEL)), _const_spec((1, D_MODEL))],
        out_specs=[tok(D_MODEL), tok(D_MODEL)],
        out_shape=[jax.ShapeDtypeStruct((N_TOK, D_MODEL), F32),
                   jax.ShapeDtypeStruct((N_TOK, D_MODEL), BF16)],
        compiler_params=_params(1),
        name="post",
    )(y, bonus, g, att, x_all, mod, ln_g, ln_b, ones, w_out_bf, norm2)


def _ffn_kernel(x1_ref, h2_ref, hp_ref, hn_ref, mod_ref, w1_ref, w3_ref, wc_ref, w2_ref, nf_ref, o_ref):
    has_prev, has_next = _seq_neighbours(pl.program_id(0))
    h2 = h2_ref[...]
    halo = jnp.concatenate([hp_ref[...], hn_ref[...]], axis=0)
    acc = jnp.zeros((TOK_BLOCK, D_MODEL), F32)
    for f in range(D_FF // FF_TILE):
        fs = slice(f * FF_TILE, (f + 1) * FF_TILE)
        a = _dot(h2, w1_ref[:, fs])
        ah = _dot(halo, w1_ref[:, fs])
        prow = jnp.where(has_prev, ah[15:16, :], 0.0)
        nrow = jnp.where(has_next, ah[16:17, :], 0.0)
        aprev, anext = _shifted(a, prow, nrow)
        cv = aprev * wc_ref[0:1, fs] + a * wc_ref[1:2, fs] + anext * wc_ref[2:3, fs]
        act = _silu(cv) * _dot(h2, w3_ref[:, fs])
        acc = acc + _dot(act.astype(BF16), w2_ref[fs, :])
    x2 = x1_ref[...] + mod_ref[0, 5:6, :] * acc
    o_ref[...] = _rmsnorm(x2, nf_ref[...])


def _ffn(x1, h2, mod, w1_bf, w3_bf, wc, w2_bf, norm_f):
    tok = pl.BlockSpec((TOK_BLOCK, D_MODEL), lambda i: (i, 0))
    rows16 = TOK_BLOCK // 16
    return pl.pallas_call(
        _ffn_kernel,
        grid=(N_BLOCKS,),
        in_specs=[tok, tok,
                  pl.BlockSpec((16, D_MODEL), lambda i: (jnp.maximum(i * rows16 - 1, 0), 0)),
                  pl.BlockSpec((16, D_MODEL), lambda i: (jnp.minimum((i + 1) * rows16, N_TOK // 16 - 1), 0)),
                  pl.BlockSpec((1, 6, D_MODEL), lambda i: (_mod_row(i), 0, 0)),
                  _const_spec((D_MODEL, D_FF)), _const_spec((D_MODEL, D_FF)), _const_spec((3, D_FF)),
                  _const_spec((D_FF, D_MODEL)), _const_spec((1, D_MODEL))],
        out_specs=tok,
        out_shape=jax.ShapeDtypeStruct((N_TOK, D_MODEL), F32),
        compiler_params=_params(1),
        name="ffn",
    )(x1, h2, h2, h2, mod, w1_bf, w3_bf, wc, w2_bf, norm_f)


def kernel(x_prompt, x_sample, cache_k, cache_v, state_rwkv, c, c_ctx, w_ada, b_ada, norm1, norm2,
           w_in, w_ts, w0, w2, a0, a2, g2, k_k, k_a, r_k, ln_x_g, ln_x_b, rpb, w_out,
           w_ffn1, w_ffn3, w_ffn_conv, w_ffn2, norm_f):
    x_all = jnp.concatenate([x_prompt.reshape(N_CTX_TOK, D_MODEL), x_sample.reshape(N_LAT_TOK, D_MODEL)], axis=0)
    cond = jnp.concatenate([c_ctx[None, :], c, jnp.zeros((8 - 1 - DEC_BATCH, D_MODEL), F32)], axis=0)
    mod = _modulation(cond, w_ada[0], b_ada[0]).reshape(8, 6, D_MODEL)

    q, k, v, zr = _inproj(x_all, mod, norm1[0].reshape(1, D_MODEL), w_in[0].astype(BF16))

    att_ctx = _ctx_attention(q, k, v)
    att_lat = _lat_attention(q, k, v, cache_k[:, 0].reshape(DEC_BATCH, PAST_LEN, WIDTH),
                             cache_v[:, 0].reshape(DEC_BATCH, PAST_LEN, WIDTH), _na_bias_table(rpb[0]))
    att = jnp.concatenate([att_ctx, att_lat], axis=0)

    head_id = jnp.arange(WIDTH) // HEAD_DIM
    ones = (head_id[:, None] == head_id[None, :]).astype(F32)
    row = lambda t: t.reshape(1, -1)
    r, kk, vv, g, bonus, lw, b, kd = _rwkv_pre(zr, w_ts[0], w0[0], w2[0], a0[0], a2[0], g2[0],
                                               row(k_k[0]), row(k_a[0]), row(r_k[0]), ones)
    s_lat = jnp.transpose(state_rwkv[:, 0], (0, 1, 3, 2, 4)).reshape(DEC_BATCH, N_DIRS, HEAD_DIM, WIDTH)
    s0 = jnp.concatenate([jnp.zeros((BATCH, N_DIRS, HEAD_DIM, WIDTH), F32), s_lat], axis=0)
    y, s_fin = _scan(r, kk, vv, lw, b, kd, s0)

    x1, h2 = _post(y, bonus, g, att, x_all, mod, row(ln_x_g[0]), row(ln_x_b[0]), ones,
                   w_out[0].astype(BF16), row(norm2[0]))
    out = _ffn(x1, h2, mod, w_ffn1[0].astype(BF16), w_ffn3[0].astype(BF16), w_ffn_conv[0],
               w_ffn2[0].astype(BF16), row(norm_f))

    y_prompt = out[:N_CTX_TOK].reshape(BATCH, SEQ, D_MODEL)
    y_sample = out[N_CTX_TOK:].reshape(DEC_BATCH, DEC_SEQ, D_MODEL)
    new_k = k[:N_CTX_TOK].reshape(BATCH, 1, SEQ, N_HEADS, HEAD_DIM)
    new_v = v[:N_CTX_TOK].reshape(BATCH, 1, SEQ, N_HEADS, HEAD_DIM)
    new_s = jnp.transpose(s_fin[:BATCH].reshape(BATCH, N_DIRS, HEAD_DIM, N_HEADS, HEAD_DIM),
                          (0, 1, 3, 2, 4)).reshape(BATCH, 1, N_DIRS, N_HEADS, HEAD_DIM, HEAD_DIM)
    return (y_prompt, y_sample, new_k, new_v, new_s)
```

```python
import functools

import jax
import jax.numpy as jnp
from jax import lax
from jax.experimental import pallas as pl
from jax.experimental.pallas import tpu as pltpu

F32 = jnp.float32
BF16 = jnp.bfloat16
HIGHEST = lax.Precision.HIGHEST

D_MODEL = 1024
BATCH = 16
SEQ = 256
DEC_BATCH = 2
DEC_SEQ = 2048
PAST_LEN = 512
GRID_W = 64
HEAD_DIM = 64
N_HEADS = 8
WIDTH = N_HEADS * HEAD_DIM
NA_ROWS = 8
NA_COLS = 16
N_DIRS = 2
LORA = 64
GATE_LORA = 128
D_FF = 2816
EPS = 1e-6
GN_EPS = 64e-5
ATT_SCALE = HEAD_DIM ** -0.5
RWKV_COLS = 3 * WIDTH + N_DIRS * 2 * LORA + GATE_LORA
MASK_VALUE = -1e30

N_CTX_TOK = BATCH * SEQ
N_LAT_TOK = DEC_BATCH * DEC_SEQ
N_TOK = N_CTX_TOK + N_LAT_TOK
TOK_BLOCK = 256
N_BLOCKS = N_TOK // TOK_BLOCK
CTX_BLOCKS = N_CTX_TOK // TOK_BLOCK
LAT_BLOCKS_PER_SEQ = DEC_SEQ // TOK_BLOCK
N_SEQS = BATCH + DEC_BATCH
CHUNK = 64
N_CHUNKS = N_TOK // CHUNK
CTX_CHUNKS = N_CTX_TOK // CHUNK
CHUNKS_PER_CTX_SEQ = SEQ // CHUNK
CHUNKS_PER_LAT_SEQ = DEC_SEQ // CHUNK
GRID_ROWS = DEC_SEQ // GRID_W
FF_TILE = D_FF // 2
VMEM_LIMIT = 56 * 1024 * 1024


def _params(n_axes, limit=VMEM_LIMIT):
    return pltpu.CompilerParams(dimension_semantics=("arbitrary",) * n_axes, vmem_limit_bytes=limit)


def _const_spec(shape):
    zeros = (0,) * len(shape)
    return pl.BlockSpec(shape, lambda *_: zeros, pipeline_mode=pl.Buffered(1))


def _mod_row(i):
    return jnp.where(i < CTX_BLOCKS, 0, 1 + (i - CTX_BLOCKS) // LAT_BLOCKS_PER_SEQ)


def _seq_neighbours(i):
    j = (i - CTX_BLOCKS) % LAT_BLOCKS_PER_SEQ
    lat = i >= CTX_BLOCKS
    return lat & (j != 0), lat & (j != LAT_BLOCKS_PER_SEQ - 1)


def _silu(x):
    return x * jax.nn.sigmoid(x)


def _softplus(x):
    return jnp.maximum(x, 0.0) + jnp.log1p(jnp.exp(-jnp.abs(x)))


def _rmsnorm(x, g):
    return x * lax.rsqrt(jnp.mean(x * x, axis=-1, keepdims=True) + EPS) * g


def _dot(a, b, precision=None):
    return jnp.dot(a, b, precision=precision, preferred_element_type=F32)


def _dot_nt(a, b, precision=None):
    return lax.dot_general(a, b, (((1,), (1,)), ((), ())), precision=precision, preferred_element_type=F32)


def _dot_tn(a, b, precision=None):
    return lax.dot_general(a, b, (((0,), (0,)), ((), ())), precision=precision, preferred_element_type=F32)


def _shifted(a, prev_row, next_row):
    t = a.shape[0]
    rid = lax.broadcasted_iota(jnp.int32, (t, 1), 0)
    prev = jnp.where(rid == 0, prev_row, pltpu.roll(a, 1, axis=0))
    nxt = jnp.where(rid == t - 1, next_row, pltpu.roll(a, t - 1, axis=0))
    return prev, nxt


def _mod_kernel(cond_ref, w_ref, b_ref, o_ref):
    o_ref[...] = _dot(_silu(cond_ref[...]), w_ref[...], HIGHEST) + b_ref[...]


def _modulation(cond, w_ada, b_ada):
    n = 6 * D_MODEL
    return pl.pallas_call(
        _mod_kernel,
        grid=(6,),
        in_specs=[pl.BlockSpec((8, D_MODEL), lambda j: (0, 0)),
                  pl.BlockSpec((D_MODEL, D_MODEL), lambda j: (0, j)),
                  pl.BlockSpec((1, D_MODEL), lambda j: (0, j))],
        out_specs=pl.BlockSpec((8, D_MODEL), lambda j: (0, j)),
        out_shape=jax.ShapeDtypeStruct((8, n), F32),
        compiler_params=_params(1),
        name="mod",
    )(cond, w_ada, b_ada.reshape(1, n))


def _inproj_kernel(x_ref, mod_ref, n1_ref, w_ref, q_ref, k_ref, v_ref, zr_ref):
    h = _rmsnorm(x_ref[...], n1_ref[...]) * (1.0 + mod_ref[0, 1:2, :]) + mod_ref[0, 0:1, :]
    h = h.astype(BF16)
    q_ref[...] = _dot(h, w_ref[:, 0:WIDTH])
    k_ref[...] = _dot(h, w_ref[:, WIDTH:2 * WIDTH])
    v_ref[...] = _dot(h, w_ref[:, 2 * WIDTH:3 * WIDTH])
    zr_ref[...] = _dot(h, w_ref[:, 3 * WIDTH:])


def _inproj(x_all, mod, norm1, w_in_bf):
    tok = lambda c: pl.BlockSpec((TOK_BLOCK, c), lambda i: (i, 0))
    return pl.pallas_call(
        _inproj_kernel,
        grid=(N_BLOCKS,),
        in_specs=[tok(D_MODEL),
                  pl.BlockSpec((1, 6, D_MODEL), lambda i: (_mod_row(i), 0, 0)),
                  _const_spec((1, D_MODEL)),
                  _const_spec((D_MODEL, 3 * WIDTH + RWKV_COLS))],
        out_specs=[tok(WIDTH), tok(WIDTH), tok(WIDTH), tok(RWKV_COLS)],
        out_shape=[jax.ShapeDtypeStruct((N_TOK, WIDTH), F32)] * 3
        + [jax.ShapeDtypeStruct((N_TOK, RWKV_COLS), F32)],
        compiler_params=_params(1),
        name="inproj",
    )(x_all, mod, norm1, w_in_bf)


def _ctx_attn_kernel(q_ref, k_ref, v_ref, o_ref):
    outs = []
    for h in range(N_HEADS):
        sl = slice(h * HEAD_DIM, (h + 1) * HEAD_DIM)
        qh = q_ref[:, sl].astype(BF16)
        kh = k_ref[:, sl].astype(BF16)
        vh = v_ref[:, sl].astype(BF16)
        s = _dot_nt(qh, kh) * ATT_SCALE
        e = jnp.exp(s - jnp.max(s, axis=-1, keepdims=True))
        l = jnp.sum(e, axis=-1, keepdims=True)
        outs.append(_dot(e.astype(BF16), vh) / l)
    o_ref[...] = jnp.concatenate(outs, axis=-1).astype(BF16)


def _ctx_attention(q, k, v):
    spec = pl.BlockSpec((SEQ, WIDTH), lambda b: (b, 0))
    return pl.pallas_call(
        _ctx_attn_kernel,
        grid=(BATCH,),
        in_specs=[spec, spec, spec],
        out_specs=spec,
        out_shape=jax.ShapeDtypeStruct((N_CTX_TOK, WIDTH), BF16),
        compiler_params=_params(1),
        name="ctxattn",
    )(q, k, v)


def _lat_attn_kernel(q_ref, k_ref, v_ref, ck_ref, cv_ref, bias_ref, o_ref, kbf, vbf, ckbf, cvbf):
    i = pl.program_id(1)

    @pl.when(i == 0)
    def _():
        kbf[...] = k_ref[...].astype(BF16)
        vbf[...] = v_ref[...].astype(BF16)
        ckbf[...] = ck_ref[0].astype(BF16)
        cvbf[...] = cv_ref[0].astype(BF16)

    win = NA_ROWS * GRID_W
    start = pl.multiple_of(jnp.clip(i - NA_ROWS // 2, 0, GRID_ROWS - NA_ROWS) * GRID_W, GRID_W)
    outs = []
    for h in range(N_HEADS):
        sl = slice(h * HEAD_DIM, (h + 1) * HEAD_DIM)
        qh = q_ref[:, sl].astype(BF16)
        s_loc = _dot_nt(qh, kbf[pl.ds(start, win), sl]) * ATT_SCALE + bias_ref[0, h]
        s_ctx = _dot_nt(qh, ckbf[:, sl]) * ATT_SCALE
        m = jnp.maximum(jnp.max(s_loc, axis=-1, keepdims=True), jnp.max(s_ctx, axis=-1, keepdims=True))
        e_loc = jnp.exp(s_loc - m)
        e_ctx = jnp.exp(s_ctx - m)
        l = jnp.sum(e_loc, axis=-1, keepdims=True) + jnp.sum(e_ctx, axis=-1, keepdims=True)
        o = _dot(e_loc.astype(BF16), vbf[pl.ds(start, win), sl]) + _dot(e_ctx.astype(BF16), cvbf[:, sl])
        outs.append(o / l)
    o_ref[...] = jnp.concatenate(outs, axis=-1).astype(BF16)


def _na_bias_table(rpb):
    cq = jnp.arange(GRID_W)[:, None]
    ck = jnp.arange(GRID_W)[None, :]
    cs = jnp.clip(cq - NA_COLS // 2, 0, GRID_W - NA_COLS)
    inwin = (ck >= cs) & (ck < cs + NA_COLS)
    coff = jnp.clip(ck - cq + NA_COLS - 1, 0, 2 * NA_COLS - 2)
    t = jnp.where(inwin, rpb[:, :, coff].astype(F32), MASK_VALUE)
    d = jnp.arange(NA_ROWS)[:, None] + jnp.arange(NA_ROWS)[None, :]
    t = t[:, d]
    return jnp.transpose(t, (1, 0, 3, 2, 4)).reshape(NA_ROWS, N_HEADS, GRID_W, NA_ROWS * GRID_W)


def _lat_attention(q, k, v, cache_k, cache_v, bias):
    lat0 = N_CTX_TOK // GRID_W
    seq0 = N_CTX_TOK // DEC_SEQ

    def bias_idx(b, i):
        si = jnp.clip(i - NA_ROWS // 2, 0, GRID_ROWS - NA_ROWS)
        return (si - i + NA_ROWS - 1, 0, 0, 0)

    seq = pl.BlockSpec((DEC_SEQ, WIDTH), lambda b, i: (seq0 + b, 0))
    ctx = pl.BlockSpec((1, PAST_LEN, WIDTH), lambda b, i: (b, 0, 0))
    return pl.pallas_call(
        _lat_attn_kernel,
        grid=(DEC_BATCH, GRID_ROWS),
        in_specs=[pl.BlockSpec((GRID_W, WIDTH), lambda b, i: (lat0 + b * GRID_ROWS + i, 0)),
                  seq, seq, ctx, ctx,
                  pl.BlockSpec((1, N_HEADS, GRID_W, NA_ROWS * GRID_W), bias_idx)],
        out_specs=pl.BlockSpec((GRID_W, WIDTH), lambda b, i: (b * GRID_ROWS + i, 0)),
        out_shape=jax.ShapeDtypeStruct((N_LAT_TOK, WIDTH), BF16),
        scratch_shapes=[pltpu.VMEM((DEC_SEQ, WIDTH), BF16), pltpu.VMEM((DEC_SEQ, WIDTH), BF16),
                        pltpu.VMEM((PAST_LEN, WIDTH), BF16), pltpu.VMEM((PAST_LEN, WIDTH), BF16)],
        compiler_params=_params(2),
        name="latattn",
    )(q, k, v, cache_k, cache_v, bias)


def _rwkv_pre_kernel(z_ref, zp_ref, zn_ref, wts_ref, w0_ref, w2_ref, a0_ref, a2_ref, g2_ref,
                     kk_ref, ka_ref, rk_ref, ones_ref,
                     r_out, kk_out, v_out, g_out, bonus_out, lw_out, b_out, kd_out):
    has_prev, has_next = _seq_neighbours(pl.program_id(0))

    def conv(c0, c1):
        z = z_ref[:, c0:c1]
        prow = jnp.where(has_prev, zp_ref[7:8, c0:c1], 0.0)
        nrow = jnp.where(has_next, zn_ref[0:1, c0:c1], 0.0)
        zprev, znext = _shifted(z, prow, nrow)
        return zprev * wts_ref[0:1, c0:c1] + z * wts_ref[1:2, c0:c1] + znext * wts_ref[2:3, c0:c1]

    r = conv(0, WIDTH)
    kr = conv(WIDTH, 2 * WIDTH)
    v = conv(2 * WIDTH, 3 * WIDTH)
    o = 3 * WIDTH
    tw = jnp.tanh(conv(o, o + N_DIRS * LORA))
    o += N_DIRS * LORA
    xa = conv(o, o + N_DIRS * LORA)
    o += N_DIRS * LORA
    xg = conv(o, o + GATE_LORA)
    ones = ones_ref[...]

    kkraw = kr * kk_ref[...]
    kk = kkraw * lax.rsqrt(_dot(kkraw * kkraw, ones, HIGHEST) + 1e-12)
    kd_sum = jnp.zeros_like(kr)
    for d in range(N_DIRS):
        ls = slice(d * LORA, (d + 1) * LORA)
        w_log = -_softplus(-(w0_ref[d:d + 1, :] + _dot(tw[:, ls], w2_ref[d], HIGHEST))) - 0.5
        a = jax.nn.sigmoid(a0_ref[d:d + 1, :] + _dot(xa[:, ls], a2_ref[d], HIGHEST))
        kd = kr * (1.0 + (a - 1.0) * ka_ref[...])
        lw_out[d] = -jnp.exp(w_log)
        b_out[d] = kk * a
        kd_out[d] = kd
        kd_sum = kd_sum + kd
    r_out[...] = r
    kk_out[...] = kk
    v_out[...] = v
    g_out[...] = _dot(jax.nn.sigmoid(xg), g2_ref[...], HIGHEST)
    bonus_out[...] = _dot(r * kd_sum * rk_ref[...], ones, HIGHEST) * v


def _rwkv_pre(zr, w_ts, w0, w2, a0, a2, g2, k_k, k_a, r_k, ones):
    tok = pl.BlockSpec((TOK_BLOCK, WIDTH), lambda i: (i, 0))
    dtok = pl.BlockSpec((N_DIRS, TOK_BLOCK, WIDTH), lambda i: (0, i, 0))
    rows8 = TOK_BLOCK // 8
    return pl.pallas_call(
        _rwkv_pre_kernel,
        grid=(N_BLOCKS,),
        in_specs=[pl.BlockSpec((TOK_BLOCK, RWKV_COLS), lambda i: (i, 0)),
                  pl.BlockSpec((8, RWKV_COLS), lambda i: (jnp.maximum(i * rows8 - 1, 0), 0)),
                  pl.BlockSpec((8, RWKV_COLS), lambda i: (jnp.minimum((i + 1) * rows8, N_TOK // 8 - 1), 0)),
                  _const_spec((3, RWKV_COLS)),
                  _const_spec((N_DIRS, WIDTH)), _const_spec((N_DIRS, LORA, WIDTH)),
                  _const_spec((N_DIRS, WIDTH)), _const_spec((N_DIRS, LORA, WIDTH)),
                  _const_spec((GATE_LORA, WIDTH)),
                  _const_spec((1, WIDTH)), _const_spec((1, WIDTH)), _const_spec((1, WIDTH)),
                  _const_spec((WIDTH, WIDTH))],
        out_specs=[tok] * 5 + [dtok] * 3,
        out_shape=[jax.ShapeDtypeStruct((N_TOK, WIDTH), F32)] * 5
        + [jax.ShapeDtypeStruct((N_DIRS, N_TOK, WIDTH), F32)] * 3,
        compiler_params=_params(1),
        name="rwkvpre",
    )(zr, zr, zr, w_ts, w0, w2, a0, a2, g2, k_k, k_a, r_k, ones)


GROUP = 4
GROUP_W = GROUP * HEAD_DIM


def _split(x):
    hi = x.astype(BF16)
    return hi, (x - hi.astype(F32)).astype(BF16)


def _blockdiag(x):
    def one(p):
        head = lax.broadcasted_iota(jnp.int32, p.shape, 1) // HEAD_DIM
        return jnp.concatenate([jnp.where(head == h, p, jnp.zeros_like(p)) for h in range(GROUP)], axis=0)
    return one(x[0]), one(x[1])


def _mm3(dot, a, b):
    m = a[0].shape[0]
    hi = dot(jnp.concatenate([a[0], a[1]], axis=0), b[0])
    return hi[:m] + hi[m:] + dot(a[0], b[1])


def _mm(a, b):
    return _mm3(_dot, a, b)


def _mm_nt(a, b):
    return _mm3(_dot_nt, a, b)


def _scan_prepare(d, r_ref, kk_ref, v_ref, lw_ref, b_ref, kd_ref, y_ref, s_ref):
    row = lax.broadcasted_iota(jnp.int32, (CHUNK, GROUP_W), 0)
    lane = lax.broadcasted_iota(jnp.int32, (CHUNK, GROUP_W), 1)
    lane_head = lane // HEAD_DIM
    col = lane % CHUNK
    strict = col > row if d else col < row
    incl = col >= row if d else col <= row
    eye = (col == row).astype(F32)

    lw = lw_ref[0]
    cum = _dot(incl[:, :CHUNK].astype(F32), lw, HIGHEST)
    tot = jnp.sum(lw, axis=0, keepdims=True)
    p_in = jnp.exp(cum)
    p_neg = jnp.exp(-cum)
    p_rem = jnp.exp(tot - cum)
    kt = kk_ref[...] * jnp.exp(cum - lw)
    rt = r_ref[...] * p_in
    b = b_ref[0]
    kd = kd_ref[0]
    bt = b * p_neg
    kdt = kd * p_neg
    bh = b * p_rem
    kh = kd * p_rem
    p_end = jnp.exp(tot)
    v = v_ref[...]

    chains = []
    for gi in range(N_HEADS // GROUP):
        sl = slice(gi * GROUP_W, (gi + 1) * GROUP_W)
        chains.append(dict(sl=sl, strict=strict, incl=incl, eye=eye, lane_head=lane_head, y_ref=y_ref, s_ref=s_ref,
                           kt=kt[:, sl], rt=rt[:, sl], v=v[:, sl], bt=bt[:, sl], kdt=kdt[:, sl],
                           kh=kh[:, sl], bh=bh[:, sl], p_end=p_end[:, sl]))
    return chains


def _scan_chains(chains):
    for c in chains:
        lhs = _split(jnp.concatenate([c["kt"], c["rt"]], axis=0))
        g_b = _mm_nt(lhs, _blockdiag(_split(c["bt"])))
        g_k = _mm_nt(lhs, _blockdiag(_split(c["kdt"])))
        c["a_k"] = jnp.where(c["strict"], g_k[:CHUNK], 0.0)
        c["a_rb"] = jnp.where(c["incl"], g_b[CHUNK:], 0.0)
        c["a_rk"] = jnp.where(c["incl"], g_k[CHUNK:], 0.0)
        c["pw"] = jnp.where(c["strict"], -g_b[:CHUNK], 0.0)
        c["inv"] = c["eye"] + c["pw"]
    for _ in range(5):
        for c in chains:
            c["pw"] = _mm(_split(c["pw"]), _blockdiag(_split(c["pw"])))
        for c in chains:
            c["inv"] = c["inv"] + _mm(_split(c["pw"]), _blockdiag(_split(c["inv"])))
    for c in chains:
        c["av"] = _mm(_split(jnp.concatenate([c["a_k"], c["a_rk"]], axis=0)), _blockdiag(_split(c["v"])))
    for c in chains:
        inv_s = _split(c["inv"])
        c["kx"] = _mm(inv_s, _blockdiag(_split(c["kt"])))
        c["u0"] = _mm(inv_s, _blockdiag(_split(c["av"][:CHUNK])))
    for c in chains:
        c["s"] = c["s_ref"][0, :, c["sl"]]
        c["z"] = _mm_nt(_split(jnp.concatenate([c["rt"], c["kx"]], axis=0)), _blockdiag(_split(c["s"])))
        c["u"] = c["z"][CHUNK:] + c["u0"]
    for c in chains:
        c["y_ref"][:, c["sl"]] = (c["z"][:CHUNK] + c["av"][CHUNK:]
                                  - _mm(_split(c["a_rb"]), _blockdiag(_split(c["u"]))))
    for c in chains:
        vu_t = _split(jnp.transpose(jnp.concatenate([c["v"], -c["u"]], axis=0)))
        upd = _mm(vu_t, _split(jnp.concatenate([c["kh"], c["bh"]], axis=0)))
        s_new = c["s"] * c["p_end"]
        for h in range(GROUP):
            s_new = s_new + jnp.where(c["lane_head"] == h, upd[h * HEAD_DIM:(h + 1) * HEAD_DIM], 0.0)
        c["s_ref"][0, :, c["sl"]] = s_new


def _scan_chunk_id(d, c):
    return N_CHUNKS - 1 - c if d else c


def _scan_kernel(rf_ref, kkf_ref, vf_ref, rb_ref, kkb_ref, vb_ref, lwf_ref, bf_ref, kdf_ref,
                 lwb_ref, bb_ref, kdb_ref, s0f_ref, s0b_ref, yf_ref, yb_ref, sf_ref, sb_ref):
    c = pl.program_id(0)
    for d, s0_ref, s_ref in ((0, s0f_ref, sf_ref), (1, s0b_ref, sb_ref)):
        g = _scan_chunk_id(d, c)
        is_ctx = g < CTX_CHUNKS
        local = jnp.where(is_ctx, g % CHUNKS_PER_CTX_SEQ, (g - CTX_CHUNKS) % CHUNKS_PER_LAT_SEQ)
        last = jnp.where(is_ctx, CHUNKS_PER_CTX_SEQ - 1, CHUNKS_PER_LAT_SEQ - 1)

        @pl.when(local == (last if d else 0))
        def _():
            s_ref[...] = s0_ref[0]

    _scan_chains(_scan_prepare(0, rf_ref, kkf_ref, vf_ref, lwf_ref, bf_ref, kdf_ref, yf_ref, sf_ref)
                 + _scan_prepare(1, rb_ref, kkb_ref, vb_ref, lwb_ref, bb_ref, kdb_ref, yb_ref, sb_ref))


def _scan(r, kk, v, lw, b, kd, s0):
    def seq(d, c):
        g = _scan_chunk_id(d, c)
        return jnp.where(g < CTX_CHUNKS, g // CHUNKS_PER_CTX_SEQ,
                         BATCH + (g - CTX_CHUNKS) // CHUNKS_PER_LAT_SEQ)

    dirs = range(N_DIRS)
    tok = [pl.BlockSpec((CHUNK, WIDTH), lambda c, d=d: (_scan_chunk_id(d, c), 0)) for d in dirs]
    dtok = [pl.BlockSpec((1, CHUNK, WIDTH), lambda c, d=d: (d, _scan_chunk_id(d, c), 0)) for d in dirs]
    st_in = [pl.BlockSpec((1, 1, HEAD_DIM, WIDTH), lambda c, d=d: (seq(d, c), d, 0, 0)) for d in dirs]
    st_out = [pl.BlockSpec((1, HEAD_DIM, WIDTH), lambda c, d=d: (seq(d, c), 0, 0)) for d in dirs]
    return pl.pallas_call(
        _scan_kernel,
        grid=(N_CHUNKS,),
        in_specs=[tok[0]] * 3 + [tok[1]] * 3 + [dtok[0]] * 3 + [dtok[1]] * 3 + st_in,
        out_specs=tok + st_out,
        out_shape=[jax.ShapeDtypeStruct((N_TOK, WIDTH), F32)] * 2
        + [jax.ShapeDtypeStruct((N_SEQS, HEAD_DIM, WIDTH), F32)] * 2,
        compiler_params=_params(1),
        name="scan",
    )(r, kk, v, r, kk, v, lw, b, kd, lw, b, kd, s0, s0)


def _post_kernel(yf_ref, yb_ref, bonus_ref, g_ref, att_ref, x_ref, mod_ref, lng_ref, lnb_ref, ones_ref,
                 wout_ref, n2_ref, x1_ref, h2_ref):
    ones = ones_ref[...]
    inv_n = 1.0 / HEAD_DIM
    y = yf_ref[...] + yb_ref[...]
    yc = y - _dot(y, ones, HIGHEST) * inv_n
    var = _dot(yc * yc, ones, HIGHEST) * inv_n
    yn = yc * lax.rsqrt(var + GN_EPS) * lng_ref[...] + lnb_ref[...]
    r_out = ((yn + bonus_ref[...]) * g_ref[...]).astype(BF16)
    o = _dot(att_ref[...], wout_ref[0:WIDTH, :]) + _dot(r_out, wout_ref[WIDTH:, :])
    x1 = x_ref[...] + mod_ref[0, 2:3, :] * o
    x1_ref[...] = x1
    h2 = _rmsnorm(x1, n2_ref[...]) * (1.0 + mod_ref[0, 4:5, :]) + mod_ref[0, 3:4, :]
    h2_ref[...] = h2.astype(BF16)


def _post(y_f, y_b, bonus, g, att, x_all, mod, ln_g, ln_b, ones, w_out_bf, norm2):
    tok = lambda c: pl.BlockSpec((TOK_BLOCK, c), lambda i: (i, 0))
    return pl.pallas_call(
        _post_kernel,
        grid=(N_BLOCKS,),
        in_specs=[tok(WIDTH), tok(WIDTH), tok(WIDTH), tok(WIDTH), tok(WIDTH), tok(D_MODEL),
                  pl.BlockSpec((1, 6, D_MODEL), lambda i: (_mod_row(i), 0, 0)),
                  _const_spec((1, WIDTH)), _const_spec((1, WIDTH)), _const_spec((WIDTH, WIDTH)),
                  _const_spec((2 * WIDTH, D_MODEL)), _const_spec((1, D_MODEL))],
        out_specs=[tok(D_MODEL), tok(D_MODEL)],
        out_shape=[jax.ShapeDtypeStruct((N_TOK, D_MODEL), F32),
                   jax.ShapeDtypeStruct((N_TOK, D_MODEL), BF16)],
        compiler_params=_params(1),
        name="post",
    )(y_f, y_b, bonus, g, att, x_all, mod, ln_g, ln_b, ones, w_out_bf, norm2)


def _ffn_kernel(x1_ref, h2_ref, hp_ref, hn_ref, mod_ref, w1_ref, w3_ref, wc_ref, w2_ref, nf_ref, o_ref):
    has_prev, has_next = _seq_neighbours(pl.program_id(0))
    h2 = h2_ref[...]
    halo = jnp.concatenate([hp_ref[...], hn_ref[...]], axis=0)
    acc = jnp.zeros((TOK_BLOCK, D_MODEL), F32)
    for f in range(D_FF // FF_TILE):
        fs = slice(f * FF_TILE, (f + 1) * FF_TILE)
        a = _dot(h2, w1_ref[:, fs])
        ah = _dot(halo, w1_ref[:, fs])
        prow = jnp.where(has_prev, ah[15:16, :], 0.0)
        nrow = jnp.where(has_next, ah[16:17, :], 0.0)
        aprev, anext = _shifted(a, prow, nrow)
        cv = aprev * wc_ref[0:1, fs] + a * wc_ref[1:2, fs] + anext * wc_ref[2:3, fs]
        act = _silu(cv) * _dot(h2, w3_ref[:, fs])
        acc = acc + _dot(act.astype(BF16), w2_ref[fs, :])
    x2 = x1_ref[...] + mod_ref[0, 5:6, :] * acc
    o_ref[...] = _rmsnorm(x2, nf_ref[...])


def _ffn(x1, h2, mod, w1_bf, w3_bf, wc, w2_bf, norm_f):
    tok = pl.BlockSpec((TOK_BLOCK, D_MODEL), lambda i: (i, 0))
    rows16 = TOK_BLOCK // 16
    return pl.pallas_call(
        _ffn_kernel,
        grid=(N_BLOCKS,),
        in_specs=[tok, tok,
                  pl.BlockSpec((16, D_MODEL), lambda i: (jnp.maximum(i * rows16 - 1, 0), 0)),
                  pl.BlockSpec((16, D_MODEL), lambda i: (jnp.minimum((i + 1) * rows16, N_TOK // 16 - 1), 0)),
                  pl.BlockSpec((1, 6, D_MODEL), lambda i: (_mod_row(i), 0, 0)),
                  _const_spec((D_MODEL, D_FF)), _const_spec((D_MODEL, D_FF)), _const_spec((3, D_FF)),
                  _const_spec((D_FF, D_MODEL)), _const_spec((1, D_MODEL))],
        out_specs=tok,
        out_shape=jax.ShapeDtypeStruct((N_TOK, D_MODEL), F32),
        compiler_params=_params(1),
        name="ffn",
    )(x1, h2, h2, h2, mod, w1_bf, w3_bf, wc, w2_bf, norm_f)


def kernel(x_prompt, x_sample, cache_k, cache_v, state_rwkv, c, c_ctx, w_ada, b_ada, norm1, norm2,
           w_in, w_ts, w0, w2, a0, a2, g2, k_k, k_a, r_k, ln_x_g, ln_x_b, rpb, w_out,
           w_ffn1, w_ffn3, w_ffn_conv, w_ffn2, norm_f):
    x_all = jnp.concatenate([x_prompt.reshape(N_CTX_TOK, D_MODEL), x_sample.reshape(N_LAT_TOK, D_MODEL)], axis=0)
    cond = jnp.concatenate([c_ctx[None, :], c, jnp.zeros((8 - 1 - DEC_BATCH, D_MODEL), F32)], axis=0)
    mod = _modulation(cond, w_ada[0], b_ada[0]).reshape(8, 6, D_MODEL)

    q, k, v, zr = _inproj(x_all, mod, norm1[0].reshape(1, D_MODEL), w_in[0].astype(BF16))

    att_ctx = _ctx_attention(q, k, v)
    att_lat = _lat_attention(q, k, v, cache_k[:, 0].reshape(DEC_BATCH, PAST_LEN, WIDTH),
                             cache_v[:, 0].reshape(DEC_BATCH, PAST_LEN, WIDTH), _na_bias_table(rpb[0]))
    att = jnp.concatenate([att_ctx, att_lat], axis=0)

    head_id = jnp.arange(WIDTH) // HEAD_DIM
    ones = (head_id[:, None] == head_id[None, :]).astype(F32)
    row = lambda t: t.reshape(1, -1)
    r, kk, vv, g, bonus, lw, b, kd = _rwkv_pre(zr, w_ts[0], w0[0], w2[0], a0[0], a2[0], g2[0],
                                               row(k_k[0]), row(k_a[0]), row(r_k[0]), ones)
    s_lat = jnp.transpose(state_rwkv[:, 0], (0, 1, 3, 2, 4)).reshape(DEC_BATCH, N_DIRS, HEAD_DIM, WIDTH)
    s0 = jnp.concatenate([jnp.zeros((BATCH, N_DIRS, HEAD_DIM, WIDTH), F32), s_lat], axis=0)
    y_f, y_b, s_f, s_b = _scan(r, kk, vv, lw, b, kd, s0)
    s_fin = jnp.stack([s_f, s_b], axis=1)

    x1, h2 = _post(y_f, y_b, bonus, g, att, x_all, mod, row(ln_x_g[0]), row(ln_x_b[0]), ones,
                   w_out[0].astype(BF16), row(norm2[0]))
    out = _ffn(x1, h2, mod, w_ffn1[0].astype(BF16), w_ffn3[0].astype(BF16), w_ffn_conv[0],
               w_ffn2[0].astype(BF16), row(norm_f))

    y_prompt = out[:N_CTX_TOK].reshape(BATCH, SEQ, D_MODEL)
    y_sample = out[N_CTX_TOK:].reshape(DEC_BATCH, DEC_SEQ, D_MODEL)
    new_k = k[:N_CTX_TOK].reshape(BATCH, 1, SEQ, N_HEADS, HEAD_DIM)
    new_v = v[:N_CTX_TOK].reshape(BATCH, 1, SEQ, N_HEADS, HEAD_DIM)
    new_s = jnp.transpose(s_fin[:BATCH].reshape(BATCH, N_DIRS, HEAD_DIM, N_HEADS, HEAD_DIM),
                          (0, 1, 3, 2, 4)).reshape(BATCH, 1, N_DIRS, N_HEADS, HEAD_DIM, HEAD_DIM)
    return (y_prompt, y_sample, new_k, new_v, new_s)
```

```python
import jax
import jax.numpy as jnp
from jax import lax
from jax.experimental import pallas as pl
from jax.experimental.pallas import tpu as pltpu

F32 = jnp.float32
BF16 = jnp.bfloat16
HIGHEST = lax.Precision.HIGHEST

D_MODEL = 1024
BATCH = 16
SEQ = 256
DEC_BATCH = 2
DEC_SEQ = 2048
PAST_LEN = 512
GRID_W = 64
HEAD_DIM = 64
N_HEADS = 8
WIDTH = N_HEADS * HEAD_DIM
PAIR_W = 2 * HEAD_DIM
NA_ROWS = 8
NA_COLS = 16
N_DIRS = 2
LORA = 64
GATE_LORA = 128
D_FF = 2816
EPS = 1e-6
GN_EPS = 64e-5
ATT_SCALE = HEAD_DIM ** -0.5
RWKV_COLS = 3 * WIDTH + N_DIRS * 2 * LORA + GATE_LORA
MASK_VALUE = -1e30

N_CTX_TOK = BATCH * SEQ
N_LAT_TOK = DEC_BATCH * DEC_SEQ
N_TOK = N_CTX_TOK + N_LAT_TOK
TOK_BLOCK = 256
N_BLOCKS = N_TOK // TOK_BLOCK
CTX_BLOCKS = N_CTX_TOK // TOK_BLOCK
LAT_BLOCKS_PER_SEQ = DEC_SEQ // TOK_BLOCK
N_SEQS = BATCH + DEC_BATCH
CHUNK = 64
N_CHUNKS = N_TOK // CHUNK
CTX_CHUNKS = N_CTX_TOK // CHUNK
CHUNKS_PER_CTX_SEQ = SEQ // CHUNK
CHUNKS_PER_LAT_SEQ = DEC_SEQ // CHUNK
GRID_ROWS = DEC_SEQ // GRID_W
FF_TILE = D_FF // 2
GROUP = 4
GROUP_W = GROUP * HEAD_DIM
VMEM_LIMIT = 56 * 1024 * 1024


def _params(n_axes, limit=VMEM_LIMIT):
    return pltpu.CompilerParams(dimension_semantics=("arbitrary",) * n_axes, vmem_limit_bytes=limit)


def _const_spec(shape):
    zeros = (0,) * len(shape)
    return pl.BlockSpec(shape, lambda *_: zeros, pipeline_mode=pl.Buffered(1))


def _tok_spec(cols):
    return pl.BlockSpec((TOK_BLOCK, cols), lambda i: (i, 0))


def _ctx_tok_spec(cols):
    return pl.BlockSpec((TOK_BLOCK, cols), lambda i: (jnp.minimum(i, CTX_BLOCKS - 1), 0))


def _lat_tok_spec(cols):
    return pl.BlockSpec((TOK_BLOCK, cols), lambda i: (jnp.maximum(i - CTX_BLOCKS, 0), 0))


def _ctx_or_lat(i, ctx_ref, lat_ref):
    return jnp.where(i < CTX_BLOCKS, ctx_ref[...], lat_ref[...])


def _mod_spec():
    def row(i):
        return jnp.where(i < CTX_BLOCKS, 0, 1 + (i - CTX_BLOCKS) // LAT_BLOCKS_PER_SEQ)
    return pl.BlockSpec((1, 6, D_MODEL), lambda i: (row(i), 0, 0))


def _seq_neighbours(i):
    j = (i - CTX_BLOCKS) % LAT_BLOCKS_PER_SEQ
    lat = i >= CTX_BLOCKS
    return lat & (j != 0), lat & (j != LAT_BLOCKS_PER_SEQ - 1)


def _silu(x):
    return x * jax.nn.sigmoid(x)


def _softplus(x):
    return jnp.maximum(x, 0.0) + jnp.log1p(jnp.exp(-jnp.abs(x)))


def _rmsnorm(x, g):
    return x * lax.rsqrt(jnp.mean(x * x, axis=-1, keepdims=True) + EPS) * g


def _dot(a, b, precision=None):
    return jnp.dot(a, b, precision=precision, preferred_element_type=F32)


def _dot_nt(a, b, precision=None):
    return lax.dot_general(a, b, (((1,), (1,)), ((), ())), precision=precision, preferred_element_type=F32)


def _split(x):
    hi = x.astype(BF16)
    return hi, (x - hi.astype(F32)).astype(BF16)


def _mm3(dot, a, b):
    m = a[0].shape[0]
    hi = dot(jnp.concatenate([a[0], a[1]], axis=0), b[0])
    return hi[:m] + hi[m:] + dot(a[0], b[1])


def _mm(a, b):
    return _mm3(_dot, a, b)


def _mm_nt(a, b):
    return _mm3(_dot_nt, a, b)


def _segsum(x, ones_bf):
    m = x.shape[0]
    hi, lo = _split(x)
    s = _dot(jnp.concatenate([hi, lo], axis=0), ones_bf)
    return s[:m] + s[m:]


def _shifted(a, prev_row, next_row):
    t = a.shape[0]
    rid = lax.broadcasted_iota(jnp.int32, (t, 1), 0)
    prev = jnp.where(rid == 0, prev_row, pltpu.roll(a, 1, axis=0))
    nxt = jnp.where(rid == t - 1, next_row, pltpu.roll(a, t - 1, axis=0))
    return prev, nxt


def _mod_kernel(cond_ref, w_ref, b_ref, o_ref):
    o_ref[...] = _dot(_silu(cond_ref[...]), w_ref[...], HIGHEST) + b_ref[...]


def _modulation(cond, w_ada, b_ada):
    n = 6 * D_MODEL
    return pl.pallas_call(
        _mod_kernel,
        grid=(6,),
        in_specs=[pl.BlockSpec((8, D_MODEL), lambda j: (0, 0)),
                  pl.BlockSpec((D_MODEL, D_MODEL), lambda j: (0, j)),
                  pl.BlockSpec((1, D_MODEL), lambda j: (0, j))],
        out_specs=pl.BlockSpec((8, D_MODEL), lambda j: (0, j)),
        out_shape=jax.ShapeDtypeStruct((8, n), F32),
        compiler_params=_params(1),
        name="mod",
    )(cond, w_ada, b_ada.reshape(1, n))


def _inproj_kernel(xp_ref, xs_ref, mod_ref, n1_ref, w_ref, q_ref, k_ref, v_ref, zr_ref):
    x = _ctx_or_lat(pl.program_id(0), xp_ref, xs_ref)
    h = _rmsnorm(x, n1_ref[...]) * (1.0 + mod_ref[0, 1:2, :]) + mod_ref[0, 0:1, :]
    h = h.astype(BF16)
    q_ref[...] = _dot(h, w_ref[:, 0:WIDTH])
    k_ref[...] = _dot(h, w_ref[:, WIDTH:2 * WIDTH])
    v_ref[...] = _dot(h, w_ref[:, 2 * WIDTH:3 * WIDTH])
    zr_ref[...] = _dot(h, w_ref[:, 3 * WIDTH:])


def _inproj(xp, xs, mod, norm1, w_in_bf):
    return pl.pallas_call(
        _inproj_kernel,
        grid=(N_BLOCKS,),
        in_specs=[_ctx_tok_spec(D_MODEL), _lat_tok_spec(D_MODEL), _mod_spec(),
                  _const_spec((1, D_MODEL)),
                  _const_spec((D_MODEL, 3 * WIDTH + RWKV_COLS))],
        out_specs=[_tok_spec(WIDTH), _tok_spec(WIDTH), _tok_spec(WIDTH), _tok_spec(RWKV_COLS)],
        out_shape=[jax.ShapeDtypeStruct((N_TOK, WIDTH), F32)] * 3
        + [jax.ShapeDtypeStruct((N_TOK, RWKV_COLS), F32)],
        compiler_params=_params(1),
        name="inproj",
    )(xp, xs, mod, norm1, w_in_bf)


def _pair_queries(q):
    lo_half = lax.broadcasted_iota(jnp.int32, q.shape, 1) < HEAD_DIM
    return jnp.concatenate([jnp.where(lo_half, q, 0.0), jnp.where(lo_half, 0.0, q)], axis=0).astype(BF16)


def _pair_outputs(o):
    t = o.shape[0] // 2
    lo_half = lax.broadcasted_iota(jnp.int32, (t, PAIR_W), 1) < HEAD_DIM
    return jnp.where(lo_half, o[:t], o[t:])


def _ctx_attn_kernel(q_ref, k_ref, v_ref, o_ref):
    pairs = [slice(p * PAIR_W, (p + 1) * PAIR_W) for p in range(N_HEADS // 2)]
    scores = [_dot_nt(_pair_queries(q_ref[:, ps]), k_ref[:, ps].astype(BF16)) * ATT_SCALE for ps in pairs]
    probs = []
    for s in scores:
        e = jnp.exp(s - jnp.max(s, axis=-1, keepdims=True))
        probs.append((e.astype(BF16), jnp.sum(e, axis=-1, keepdims=True)))
    for ps, (e, l) in zip(pairs, probs):
        o_ref[:, ps] = _pair_outputs(_dot(e, v_ref[:, ps].astype(BF16)) / l).astype(BF16)


def _ctx_attention(q, k, v):
    spec = pl.BlockSpec((SEQ, WIDTH), lambda b: (b, 0))
    return pl.pallas_call(
        _ctx_attn_kernel,
        grid=(BATCH,),
        in_specs=[spec, spec, spec],
        out_specs=spec,
        out_shape=jax.ShapeDtypeStruct((N_CTX_TOK, WIDTH), BF16),
        compiler_params=_params(1),
        name="ctxattn",
    )(q, k, v)


def _lat_attn_kernel(q_ref, k_ref, v_ref, ck_ref, cv_ref, bias_ref, o_ref, kbf, vbf, ckbf, cvbf):
    i = pl.program_id(1)

    @pl.when(i == 0)
    def _():
        kbf[...] = k_ref[...].astype(BF16)
        vbf[...] = v_ref[...].astype(BF16)
        ckbf[...] = ck_ref[0].astype(BF16)
        cvbf[...] = cv_ref[0].astype(BF16)

    win = NA_ROWS * GRID_W
    first_row = jnp.clip(i - NA_ROWS // 2, 0, GRID_ROWS - NA_ROWS)
    start = pl.multiple_of(first_row * GRID_W, GRID_W)
    off0 = first_row - i + NA_ROWS - 1
    pairs = [slice(p * PAIR_W, (p + 1) * PAIR_W) for p in range(N_HEADS // 2)]
    scores = []
    for p, ps in enumerate(pairs):
        lhs = _pair_queries(q_ref[:, ps])
        bias = jnp.concatenate(
            [jnp.concatenate([bias_ref[2 * p + hh, off0 + 2 * j] for j in range(NA_ROWS // 2)], axis=-1)
             for hh in range(2)], axis=0)
        s_loc = _dot_nt(lhs, kbf[pl.ds(start, win), ps]) * ATT_SCALE + bias
        s_ctx = _dot_nt(lhs, ckbf[:, ps]) * ATT_SCALE
        scores.append((s_loc, s_ctx))
    probs = []
    for s_loc, s_ctx in scores:
        m = jnp.maximum(jnp.max(s_loc, axis=-1, keepdims=True), jnp.max(s_ctx, axis=-1, keepdims=True))
        e_loc = jnp.exp(s_loc - m)
        e_ctx = jnp.exp(s_ctx - m)
        l = jnp.sum(e_loc, axis=-1, keepdims=True) + jnp.sum(e_ctx, axis=-1, keepdims=True)
        probs.append((e_loc.astype(BF16), e_ctx.astype(BF16), l))
    for ps, (e_loc, e_ctx, l) in zip(pairs, probs):
        o = _dot(e_loc, vbf[pl.ds(start, win), ps]) + _dot(e_ctx, cvbf[:, ps])
        o_ref[:, ps] = _pair_outputs(o / l).astype(BF16)


def _na_bias_table(rpb):
    cq = jnp.arange(GRID_W)[:, None]
    ck = jnp.arange(GRID_W)[None, :]
    cs = jnp.clip(cq - NA_COLS // 2, 0, GRID_W - NA_COLS)
    inwin = (ck >= cs) & (ck < cs + NA_COLS)
    coff = jnp.clip(ck - cq + NA_COLS - 1, 0, 2 * NA_COLS - 2)
    t = jnp.where(inwin, rpb[:, :, coff].astype(F32), MASK_VALUE)
    return jnp.concatenate([t[:, :-1], t[:, 1:]], axis=-1)


def _lat_attention(q, k, v, cache_k, cache_v, bias):
    lat0 = N_CTX_TOK // GRID_W
    seq0 = N_CTX_TOK // DEC_SEQ
    seq = pl.BlockSpec((DEC_SEQ, WIDTH), lambda b, i: (seq0 + b, 0))
    ctx = pl.BlockSpec((1, PAST_LEN, WIDTH), lambda b, i: (b, 0, 0))
    return pl.pallas_call(
        _lat_attn_kernel,
        grid=(DEC_BATCH, GRID_ROWS),
        in_specs=[pl.BlockSpec((GRID_W, WIDTH), lambda b, i: (lat0 + b * GRID_ROWS + i, 0)),
                  seq, seq, ctx, ctx,
                  _const_spec((N_HEADS, 2 * NA_ROWS - 2, GRID_W, PAIR_W))],
        out_specs=pl.BlockSpec((GRID_W, WIDTH), lambda b, i: (b * GRID_ROWS + i, 0)),
        out_shape=jax.ShapeDtypeStruct((N_LAT_TOK, WIDTH), BF16),
        scratch_shapes=[pltpu.VMEM((DEC_SEQ, WIDTH), BF16), pltpu.VMEM((DEC_SEQ, WIDTH), BF16),
                        pltpu.VMEM((PAST_LEN, WIDTH), BF16), pltpu.VMEM((PAST_LEN, WIDTH), BF16)],
        compiler_params=_params(2),
        name="latattn",
    )(q, k, v, cache_k, cache_v, bias)


def _rwkv_pre_kernel(z_ref, zp_ref, zn_ref, wts_ref, w0_ref, w2_ref, a0_ref, a2_ref, g2_ref,
                     kk_ref, ka_ref, rk_ref, ones_ref,
                     r_out, kk_out, v_out, g_out, bonus_out, lw_out, b_out, kd_out):
    has_prev, has_next = _seq_neighbours(pl.program_id(0))

    def conv(c0, c1):
        z = z_ref[:, c0:c1]
        prow = jnp.where(has_prev, zp_ref[7:8, c0:c1], 0.0)
        nrow = jnp.where(has_next, zn_ref[0:1, c0:c1], 0.0)
        zprev, znext = _shifted(z, prow, nrow)
        return zprev * wts_ref[0:1, c0:c1] + z * wts_ref[1:2, c0:c1] + znext * wts_ref[2:3, c0:c1]

    r = conv(0, WIDTH)
    kr = conv(WIDTH, 2 * WIDTH)
    v = conv(2 * WIDTH, 3 * WIDTH)
    o = 3 * WIDTH
    tw = jnp.tanh(conv(o, o + N_DIRS * LORA))
    o += N_DIRS * LORA
    xa = conv(o, o + N_DIRS * LORA)
    o += N_DIRS * LORA
    xg = conv(o, o + GATE_LORA)
    ones = ones_ref[...]

    kkraw = kr * kk_ref[...]
    kk = kkraw * lax.rsqrt(_segsum(kkraw * kkraw, ones) + 1e-12)
    kd_sum = jnp.zeros_like(kr)
    for d in range(N_DIRS):
        ls = slice(d * LORA, (d + 1) * LORA)
        w_log = -_softplus(-(w0_ref[d:d + 1, :] + _mm(_split(tw[:, ls]), _split(w2_ref[d])))) - 0.5
        a = jax.nn.sigmoid(a0_ref[d:d + 1, :] + _mm(_split(xa[:, ls]), _split(a2_ref[d])))
        kd = kr * (1.0 + (a - 1.0) * ka_ref[...])
        lw_out[d] = -jnp.exp(w_log)
        b_out[d] = kk * a
        kd_out[d] = kd
        kd_sum = kd_sum + kd
    r_out[...] = r
    kk_out[...] = kk
    v_out[...] = v
    g_out[...] = _mm(_split(jax.nn.sigmoid(xg)), _split(g2_ref[...]))
    bonus_out[...] = _segsum(r * kd_sum * rk_ref[...], ones) * v


def _rwkv_pre(zr, w_ts, w0, w2, a0, a2, g2, k_k, k_a, r_k, ones):
    tok = _tok_spec(WIDTH)
    dtok = pl.BlockSpec((N_DIRS, TOK_BLOCK, WIDTH), lambda i: (0, i, 0))
    rows8 = TOK_BLOCK // 8
    return pl.pallas_call(
        _rwkv_pre_kernel,
        grid=(N_BLOCKS,),
        in_specs=[_tok_spec(RWKV_COLS),
                  pl.BlockSpec((8, RWKV_COLS), lambda i: (jnp.maximum(i * rows8 - 1, 0), 0)),
                  pl.BlockSpec((8, RWKV_COLS), lambda i: (jnp.minimum((i + 1) * rows8, N_TOK // 8 - 1), 0)),
                  _const_spec((3, RWKV_COLS)),
                  _const_spec((N_DIRS, WIDTH)), _const_spec((N_DIRS, LORA, WIDTH)),
                  _const_spec((N_DIRS, WIDTH)), _const_spec((N_DIRS, LORA, WIDTH)),
                  _const_spec((GATE_LORA, WIDTH)),
                  _const_spec((1, WIDTH)), _const_spec((1, WIDTH)), _const_spec((1, WIDTH)),
                  _const_spec((WIDTH, WIDTH))],
        out_specs=[tok] * 5 + [dtok] * 3,
        out_shape=[jax.ShapeDtypeStruct((N_TOK, WIDTH), F32)] * 5
        + [jax.ShapeDtypeStruct((N_DIRS, N_TOK, WIDTH), F32)] * 3,
        compiler_params=_params(1),
        name="rwkvpre",
    )(zr, zr, zr, w_ts, w0, w2, a0, a2, g2, k_k, k_a, r_k, ones)


def _blockdiag(x):
    def one(p):
        head = lax.broadcasted_iota(jnp.int32, p.shape, 1) // HEAD_DIM
        return jnp.concatenate([jnp.where(head == h, p, jnp.zeros_like(p)) for h in range(GROUP)], axis=0)
    return one(x[0]), one(x[1])


def _scan_prepare(d, r_ref, kk_ref, v_ref, lw_ref, b_ref, kd_ref, y_ref, s_ref):
    row = lax.broadcasted_iota(jnp.int32, (CHUNK, GROUP_W), 0)
    lane = lax.broadcasted_iota(jnp.int32, (CHUNK, GROUP_W), 1)
    lane_head = lane // HEAD_DIM
    col = lane % CHUNK
    strict = col > row if d else col < row
    incl = col >= row if d else col <= row
    eye = (col == row).astype(F32)

    lw = lw_ref[0]
    cum = _dot(incl[:, :CHUNK].astype(F32), lw, HIGHEST)
    tot = jnp.sum(lw, axis=0, keepdims=True)
    p_in = jnp.exp(cum)
    p_neg = jnp.exp(-cum)
    p_rem = jnp.exp(tot - cum)
    kt = kk_ref[...] * jnp.exp(cum - lw)
    rt = r_ref[...] * p_in
    b = b_ref[0]
    kd = kd_ref[0]
    bt = b * p_neg
    kdt = kd * p_neg
    bh = b * p_rem
    kh = kd * p_rem
    p_end = jnp.exp(tot)
    v = v_ref[...]

    chains = []
    for gi in range(N_HEADS // GROUP):
        sl = slice(gi * GROUP_W, (gi + 1) * GROUP_W)
        chains.append(dict(sl=sl, strict=strict, incl=incl, eye=eye, lane_head=lane_head, y_ref=y_ref, s_ref=s_ref,
                           kt=kt[:, sl], rt=rt[:, sl], v=v[:, sl], bt=bt[:, sl], kdt=kdt[:, sl],
                           kh=kh[:, sl], bh=bh[:, sl], p_end=p_end[:, sl]))
    return chains


def _scan_chains(chains):
    for c in chains:
        lhs = _split(jnp.concatenate([c["kt"], c["rt"]], axis=0))
        g_b = _mm_nt(lhs, _blockdiag(_split(c["bt"])))
        g_k = _mm_nt(lhs, _blockdiag(_split(c["kdt"])))
        c["a_k"] = jnp.where(c["strict"], g_k[:CHUNK], 0.0)
        c["a_rb"] = jnp.where(c["incl"], g_b[CHUNK:], 0.0)
        c["a_rk"] = jnp.where(c["incl"], g_k[CHUNK:], 0.0)
        c["pw"] = jnp.where(c["strict"], -g_b[:CHUNK], 0.0)
        c["inv"] = c["eye"] + c["pw"]
    for _ in range(5):
        for c in chains:
            c["pw"] = _mm(_split(c["pw"]), _blockdiag(_split(c["pw"])))
        for c in chains:
            c["inv"] = c["inv"] + _mm(_split(c["pw"]), _blockdiag(_split(c["inv"])))
    for c in chains:
        c["av"] = _mm(_split(jnp.concatenate([c["a_k"], c["a_rk"]], axis=0)), _blockdiag(_split(c["v"])))
    for c in chains:
        inv_s = _split(c["inv"])
        c["kx"] = _mm(inv_s, _blockdiag(_split(c["kt"])))
        c["u0"] = _mm(inv_s, _blockdiag(_split(c["av"][:CHUNK])))
    for c in chains:
        c["s"] = c["s_ref"][0, :, c["sl"]]
        c["z"] = _mm_nt(_split(jnp.concatenate([c["rt"], c["kx"]], axis=0)), _blockdiag(_split(c["s"])))
        c["u"] = c["z"][CHUNK:] + c["u0"]
    for c in chains:
        c["y_ref"][:, c["sl"]] = (c["z"][:CHUNK] + c["av"][CHUNK:]
                                  - _mm(_split(c["a_rb"]), _blockdiag(_split(c["u"]))))
    for c in chains:
        vu_t = _split(jnp.transpose(jnp.concatenate([c["v"], -c["u"]], axis=0)))
        upd = _mm(vu_t, _split(jnp.concatenate([c["kh"], c["bh"]], axis=0)))
        s_new = c["s"] * c["p_end"]
        for h in range(GROUP):
            s_new = s_new + jnp.where(c["lane_head"] == h, upd[h * HEAD_DIM:(h + 1) * HEAD_DIM], 0.0)
        c["s_ref"][0, :, c["sl"]] = s_new


def _scan_chunk_id(d, c):
    return N_CHUNKS - 1 - c if d else c


def _scan_kernel(rf_ref, kkf_ref, vf_ref, rb_ref, kkb_ref, vb_ref, lwf_ref, bf_ref, kdf_ref,
                 lwb_ref, bb_ref, kdb_ref, s0f_ref, s0b_ref, yf_ref, yb_ref, sf_ref, sb_ref):
    c = pl.program_id(0)
    for d, s0_ref, s_ref in ((0, s0f_ref, sf_ref), (1, s0b_ref, sb_ref)):
        g = _scan_chunk_id(d, c)
        is_ctx = g < CTX_CHUNKS
        local = jnp.where(is_ctx, g % CHUNKS_PER_CTX_SEQ, (g - CTX_CHUNKS) % CHUNKS_PER_LAT_SEQ)
        last = jnp.where(is_ctx, CHUNKS_PER_CTX_SEQ - 1, CHUNKS_PER_LAT_SEQ - 1)

        @pl.when(local == (last if d else 0))
        def _():
            s_ref[...] = s0_ref[0]

    _scan_chains(_scan_prepare(0, rf_ref, kkf_ref, vf_ref, lwf_ref, bf_ref, kdf_ref, yf_ref, sf_ref)
                 + _scan_prepare(1, rb_ref, kkb_ref, vb_ref, lwb_ref, bb_ref, kdb_ref, yb_ref, sb_ref))


def _scan(r, kk, v, lw, b, kd, s0):
    def seq(d, c):
        g = _scan_chunk_id(d, c)
        return jnp.where(g < CTX_CHUNKS, g // CHUNKS_PER_CTX_SEQ,
                         BATCH + (g - CTX_CHUNKS) // CHUNKS_PER_LAT_SEQ)

    dirs = range(N_DIRS)
    tok = [pl.BlockSpec((CHUNK, WIDTH), lambda c, d=d: (_scan_chunk_id(d, c), 0)) for d in dirs]
    dtok = [pl.BlockSpec((1, CHUNK, WIDTH), lambda c, d=d: (d, _scan_chunk_id(d, c), 0)) for d in dirs]
    st_in = [pl.BlockSpec((1, 1, HEAD_DIM, WIDTH), lambda c, d=d: (seq(d, c), d, 0, 0)) for d in dirs]
    st_out = [pl.BlockSpec((1, HEAD_DIM, WIDTH), lambda c, d=d: (seq(d, c), 0, 0)) for d in dirs]
    return pl.pallas_call(
        _scan_kernel,
        grid=(N_CHUNKS,),
        in_specs=[tok[0]] * 3 + [tok[1]] * 3 + [dtok[0]] * 3 + [dtok[1]] * 3 + st_in,
        out_specs=tok + st_out,
        out_shape=[jax.ShapeDtypeStruct((N_TOK, WIDTH), F32)] * 2
        + [jax.ShapeDtypeStruct((N_SEQS, HEAD_DIM, WIDTH), F32)] * 2,
        compiler_params=_params(1),
        name="scan",
    )(r, kk, v, r, kk, v, lw, b, kd, lw, b, kd, s0, s0)


def _post_kernel(yf_ref, yb_ref, bonus_ref, g_ref, attc_ref, attl_ref, xp_ref, xs_ref, mod_ref, lng_ref, lnb_ref,
                 ones_ref, wout_ref, n2_ref, x1_ref, h2_ref):
    i = pl.program_id(0)
    ones = ones_ref[...]
    inv_n = 1.0 / HEAD_DIM
    y = yf_ref[...] + yb_ref[...]
    yc = y - _segsum(y, ones) * inv_n
    var = _segsum(yc * yc, ones) * inv_n
    yn = yc * lax.rsqrt(var + GN_EPS) * lng_ref[...] + lnb_ref[...]
    r_out = ((yn + bonus_ref[...]) * g_ref[...]).astype(BF16)
    att = _ctx_or_lat(i, attc_ref, attl_ref)
    o = _dot(att, wout_ref[0:WIDTH, :]) + _dot(r_out, wout_ref[WIDTH:, :])
    x1 = _ctx_or_lat(i, xp_ref, xs_ref) + mod_ref[0, 2:3, :] * o
    x1_ref[...] = x1
    h2 = _rmsnorm(x1, n2_ref[...]) * (1.0 + mod_ref[0, 4:5, :]) + mod_ref[0, 3:4, :]
    h2_ref[...] = h2.astype(BF16)


def _post(y_f, y_b, bonus, g, att_ctx, att_lat, xp, xs, mod, ln_g, ln_b, ones, w_out_bf, norm2):
    tok = _tok_spec
    return pl.pallas_call(
        _post_kernel,
        grid=(N_BLOCKS,),
        in_specs=[tok(WIDTH), tok(WIDTH), tok(WIDTH), tok(WIDTH),
                  _ctx_tok_spec(WIDTH), _lat_tok_spec(WIDTH), _ctx_tok_spec(D_MODEL), _lat_tok_spec(D_MODEL),
                  _mod_spec(),
                  _const_spec((1, WIDTH)), _const_spec((1, WIDTH)), _const_spec((WIDTH, WIDTH)),
                  _const_spec((2 * WIDTH, D_MODEL)), _const_spec((1, D_MODEL))],
        out_specs=[tok(D_MODEL), tok(D_MODEL)],
        out_shape=[jax.ShapeDtypeStruct((N_TOK, D_MODEL), F32),
                   jax.ShapeDtypeStruct((N_TOK, D_MODEL), BF16)],
        compiler_params=_params(1),
        name="post",
    )(y_f, y_b, bonus, g, att_ctx, att_lat, xp, xs, mod, ln_g, ln_b, ones, w_out_bf, norm2)


def _ffn_kernel(x1_ref, h2_ref, hp_ref, hn_ref, mod_ref, w1_ref, w3_ref, wc_ref, w2_ref, nf_ref, yp_ref, ys_ref):
    i = pl.program_id(0)
    has_prev, has_next = _seq_neighbours(i)
    h2 = h2_ref[...]
    halo = jnp.concatenate([hp_ref[...], hn_ref[...]], axis=0)
    acc = jnp.zeros((TOK_BLOCK, D_MODEL), F32)
    for f in range(D_FF // FF_TILE):
        fs = slice(f * FF_TILE, (f + 1) * FF_TILE)
        a = _dot(h2, w1_ref[:, fs])
        ah = _dot(halo, w1_ref[:, fs])
        prow = jnp.where(has_prev, ah[15:16, :], 0.0)
        nrow = jnp.where(has_next, ah[16:17, :], 0.0)
        aprev, anext = _shifted(a, prow, nrow)
        cv = aprev * wc_ref[0:1, fs] + a * wc_ref[1:2, fs] + anext * wc_ref[2:3, fs]
        act = _silu(cv) * _dot(h2, w3_ref[:, fs])
        acc = acc + _dot(act.astype(BF16), w2_ref[fs, :])
    x2 = x1_ref[...] + mod_ref[0, 5:6, :] * acc
    y = _rmsnorm(x2, nf_ref[...])

    @pl.when(i < CTX_BLOCKS)
    def _():
        yp_ref[...] = y

    @pl.when(i >= CTX_BLOCKS)
    def _():
        ys_ref[...] = y


def _ffn(x1, h2, mod, w1_bf, w3_bf, wc, w2_bf, norm_f):
    rows16 = TOK_BLOCK // 16
    return pl.pallas_call(
        _ffn_kernel,
        grid=(N_BLOCKS,),
        in_specs=[_tok_spec(D_MODEL), _tok_spec(D_MODEL),
                  pl.BlockSpec((16, D_MODEL), lambda i: (jnp.maximum(i * rows16 - 1, 0), 0)),
                  pl.BlockSpec((16, D_MODEL), lambda i: (jnp.minimum((i + 1) * rows16, N_TOK // 16 - 1), 0)),
                  _mod_spec(),
                  _const_spec((D_MODEL, D_FF)), _const_spec((D_MODEL, D_FF)), _const_spec((3, D_FF)),
                  _const_spec((D_FF, D_MODEL)), _const_spec((1, D_MODEL))],
        out_specs=[_ctx_tok_spec(D_MODEL), _lat_tok_spec(D_MODEL)],
        out_shape=[jax.ShapeDtypeStruct((N_CTX_TOK, D_MODEL), F32),
                   jax.ShapeDtypeStruct((N_LAT_TOK, D_MODEL), F32)],
        compiler_params=_params(1),
        name="ffn",
    )(x1, h2, h2, h2, mod, w1_bf, w3_bf, wc, w2_bf, norm_f)


def kernel(x_prompt, x_sample, cache_k, cache_v, state_rwkv, c, c_ctx, w_ada, b_ada, norm1, norm2,
           w_in, w_ts, w0, w2, a0, a2, g2, k_k, k_a, r_k, ln_x_g, ln_x_b, rpb, w_out,
           w_ffn1, w_ffn3, w_ffn_conv, w_ffn2, norm_f):
    xp = x_prompt.reshape(N_CTX_TOK, D_MODEL)
    xs = x_sample.reshape(N_LAT_TOK, D_MODEL)
    row = lambda t: t.reshape(1, -1)
    cond = jnp.concatenate([c_ctx[None, :], c, jnp.zeros((8 - 1 - DEC_BATCH, D_MODEL), F32)], axis=0)
    mod = _modulation(cond, w_ada[0], b_ada[0]).reshape(8, 6, D_MODEL)

    q, k, v, zr = _inproj(xp, xs, mod, row(norm1[0]), w_in[0].astype(BF16))

    att_ctx = _ctx_attention(q, k, v)
    att_lat = _lat_attention(q, k, v, cache_k[:, 0].reshape(DEC_BATCH, PAST_LEN, WIDTH),
                             cache_v[:, 0].reshape(DEC_BATCH, PAST_LEN, WIDTH), _na_bias_table(rpb[0]))

    head_id = jnp.arange(WIDTH) // HEAD_DIM
    ones = (head_id[:, None] == head_id[None, :]).astype(BF16)
    r, kk, vv, g, bonus, lw, b, kd = _rwkv_pre(zr, w_ts[0], w0[0], w2[0], a0[0], a2[0], g2[0],
                                               row(k_k[0]), row(k_a[0]), row(r_k[0]), ones)
    s_lat = jnp.transpose(state_rwkv[:, 0], (0, 1, 3, 2, 4)).reshape(DEC_BATCH, N_DIRS, HEAD_DIM, WIDTH)
    s0 = jnp.concatenate([jnp.zeros((BATCH, N_DIRS, HEAD_DIM, WIDTH), F32), s_lat], axis=0)
    y_f, y_b, s_f, s_b = _scan(r, kk, vv, lw, b, kd, s0)
    s_fin = jnp.stack([s_f, s_b], axis=1)

    x1, h2 = _post(y_f, y_b, bonus, g, att_ctx, att_lat, xp, xs, mod, row(ln_x_g[0]), row(ln_x_b[0]), ones,
                   w_out[0].astype(BF16), row(norm2[0]))
    yp, ys = _ffn(x1, h2, mod, w_ffn1[0].astype(BF16), w_ffn3[0].astype(BF16), w_ffn_conv[0],
                  w_ffn2[0].astype(BF16), row(norm_f))

    new_k = k[:N_CTX_TOK].reshape(BATCH, 1, SEQ, N_HEADS, HEAD_DIM)
    new_v = v[:N_CTX_TOK].reshape(BATCH, 1, SEQ, N_HEADS, HEAD_DIM)
    new_s = jnp.transpose(s_fin[:BATCH].reshape(BATCH, N_DIRS, HEAD_DIM, N_HEADS, HEAD_DIM),
                          (0, 1, 3, 2, 4)).reshape(BATCH, 1, N_DIRS, N_HEADS, HEAD_DIM, HEAD_DIM)
    return (yp.reshape(BATCH, SEQ, D_MODEL), ys.reshape(DEC_BATCH, DEC_SEQ, D_MODEL), new_k, new_v, new_s)
```

```python
import jax
import jax.numpy as jnp
from jax import lax
from jax.experimental import pallas as pl
from jax.experimental.pallas import tpu as pltpu

F32 = jnp.float32
BF16 = jnp.bfloat16
HIGHEST = lax.Precision.HIGHEST

D_MODEL = 1024
BATCH = 16
SEQ = 256
DEC_BATCH = 2
DEC_SEQ = 2048
PAST_LEN = 512
GRID_W = 64
HEAD_DIM = 64
N_HEADS = 8
WIDTH = N_HEADS * HEAD_DIM
PAIR_W = 2 * HEAD_DIM
NA_ROWS = 8
NA_COLS = 16
N_DIRS = 2
LORA = 64
GATE_LORA = 128
D_FF = 2816
EPS = 1e-6
GN_EPS = 64e-5
ATT_SCALE = HEAD_DIM ** -0.5
RWKV_COLS = 3 * WIDTH + N_DIRS * 2 * LORA + GATE_LORA
MASK_VALUE = -1e30

N_CTX_TOK = BATCH * SEQ
N_LAT_TOK = DEC_BATCH * DEC_SEQ
N_TOK = N_CTX_TOK + N_LAT_TOK
TOK_BLOCK = 256
N_BLOCKS = N_TOK // TOK_BLOCK
CTX_BLOCKS = N_CTX_TOK // TOK_BLOCK
LAT_BLOCKS_PER_SEQ = DEC_SEQ // TOK_BLOCK
N_SEQS = BATCH + DEC_BATCH
CHUNK = 64
N_CHUNKS = N_TOK // CHUNK
CTX_CHUNKS = N_CTX_TOK // CHUNK
CHUNKS_PER_CTX_SEQ = SEQ // CHUNK
CHUNKS_PER_LAT_SEQ = DEC_SEQ // CHUNK
GRID_ROWS = DEC_SEQ // GRID_W
FF_TILE = D_FF // 2
GROUP = 4
GROUP_W = GROUP * HEAD_DIM
VMEM_LIMIT = 56 * 1024 * 1024


def _params(n_axes, limit=VMEM_LIMIT):
    return pltpu.CompilerParams(dimension_semantics=("arbitrary",) * n_axes, vmem_limit_bytes=limit)


def _const_spec(shape):
    zeros = (0,) * len(shape)
    return pl.BlockSpec(shape, lambda *_: zeros, pipeline_mode=pl.Buffered(1))


def _tok_spec(cols):
    return pl.BlockSpec((TOK_BLOCK, cols), lambda i: (i, 0))


def _ctx_tok_spec(cols):
    return pl.BlockSpec((TOK_BLOCK, cols), lambda i: (jnp.minimum(i, CTX_BLOCKS - 1), 0))


def _lat_tok_spec(cols):
    return pl.BlockSpec((TOK_BLOCK, cols), lambda i: (jnp.maximum(i - CTX_BLOCKS, 0), 0))


def _ctx_or_lat(i, ctx_ref, lat_ref):
    return jnp.where(i < CTX_BLOCKS, ctx_ref[...], lat_ref[...])


def _mod_spec():
    def row(i):
        return jnp.where(i < CTX_BLOCKS, 0, 1 + (i - CTX_BLOCKS) // LAT_BLOCKS_PER_SEQ)
    return pl.BlockSpec((1, 6, D_MODEL), lambda i: (row(i), 0, 0))


def _seq_neighbours(i):
    j = (i - CTX_BLOCKS) % LAT_BLOCKS_PER_SEQ
    lat = i >= CTX_BLOCKS
    return lat & (j != 0), lat & (j != LAT_BLOCKS_PER_SEQ - 1)


def _silu(x):
    return x * jax.nn.sigmoid(x)


def _softplus(x):
    return jnp.maximum(x, 0.0) + jnp.log1p(jnp.exp(-jnp.abs(x)))


def _rmsnorm(x, g):
    return x * lax.rsqrt(jnp.mean(x * x, axis=-1, keepdims=True) + EPS) * g


def _dot(a, b, precision=None):
    return jnp.dot(a, b, precision=precision, preferred_element_type=F32)


def _dot_nt(a, b, precision=None):
    return lax.dot_general(a, b, (((1,), (1,)), ((), ())), precision=precision, preferred_element_type=F32)


def _split(x):
    hi = x.astype(BF16)
    return hi, (x - hi.astype(F32)).astype(BF16)


def _mm3(dot, a, b):
    m = a[0].shape[0]
    hi = dot(jnp.concatenate([a[0], a[1]], axis=0), b[0])
    return hi[:m] + hi[m:] + dot(a[0], b[1])


def _mm(a, b):
    return _mm3(_dot, a, b)


def _mm_nt(a, b):
    return _mm3(_dot_nt, a, b)


def _segsum(x, ones_bf):
    m = x.shape[0]
    hi, lo = _split(x)
    s = _dot(jnp.concatenate([hi, lo], axis=0), ones_bf)
    return s[:m] + s[m:]


def _shifted(a, prev_row, next_row):
    t = a.shape[0]
    rid = lax.broadcasted_iota(jnp.int32, (t, 1), 0)
    prev = jnp.where(rid == 0, prev_row, pltpu.roll(a, 1, axis=0))
    nxt = jnp.where(rid == t - 1, next_row, pltpu.roll(a, t - 1, axis=0))
    return prev, nxt


def _mod_kernel(cond_ref, w_ref, b_ref, o_ref):
    o_ref[...] = _dot(_silu(cond_ref[...]), w_ref[...], HIGHEST) + b_ref[...]


def _modulation(cond, w_ada, b_ada):
    n = 6 * D_MODEL
    return pl.pallas_call(
        _mod_kernel,
        grid=(6,),
        in_specs=[pl.BlockSpec((8, D_MODEL), lambda j: (0, 0)),
                  pl.BlockSpec((D_MODEL, D_MODEL), lambda j: (0, j)),
                  pl.BlockSpec((1, D_MODEL), lambda j: (0, j))],
        out_specs=pl.BlockSpec((8, D_MODEL), lambda j: (0, j)),
        out_shape=jax.ShapeDtypeStruct((8, n), F32),
        compiler_params=_params(1),
        name="mod",
    )(cond, w_ada, b_ada.reshape(1, n))


def _inproj_kernel(xp_ref, xs_ref, mod_ref, n1_ref, w_ref, q_ref, kc_ref, vc_ref, kl_ref, vl_ref, zr_ref):
    i = pl.program_id(0)
    x = _ctx_or_lat(i, xp_ref, xs_ref)
    h = _rmsnorm(x, n1_ref[...]) * (1.0 + mod_ref[0, 1:2, :]) + mod_ref[0, 0:1, :]
    h = h.astype(BF16)
    q_ref[...] = _dot(h, w_ref[:, 0:WIDTH])
    k = _dot(h, w_ref[:, WIDTH:2 * WIDTH])
    v = _dot(h, w_ref[:, 2 * WIDTH:3 * WIDTH])

    @pl.when(i < CTX_BLOCKS)
    def _():
        kc_ref[...] = k
        vc_ref[...] = v

    @pl.when(i >= CTX_BLOCKS)
    def _():
        kl_ref[...] = k
        vl_ref[...] = v

    zr_ref[...] = _dot(h, w_ref[:, 3 * WIDTH:])


def _inproj(xp, xs, mod, norm1, w_in_bf):
    half = jax.ShapeDtypeStruct((N_CTX_TOK, WIDTH), F32)
    return pl.pallas_call(
        _inproj_kernel,
        grid=(N_BLOCKS,),
        in_specs=[_ctx_tok_spec(D_MODEL), _lat_tok_spec(D_MODEL), _mod_spec(),
                  _const_spec((1, D_MODEL)),
                  _const_spec((D_MODEL, 3 * WIDTH + RWKV_COLS))],
        out_specs=[_tok_spec(WIDTH), _ctx_tok_spec(WIDTH), _ctx_tok_spec(WIDTH),
                   _lat_tok_spec(WIDTH), _lat_tok_spec(WIDTH), _tok_spec(RWKV_COLS)],
        out_shape=[jax.ShapeDtypeStruct((N_TOK, WIDTH), F32), half, half, half, half,
                   jax.ShapeDtypeStruct((N_TOK, RWKV_COLS), F32)],
        compiler_params=_params(1),
        name="inproj",
    )(xp, xs, mod, norm1, w_in_bf)


def _pair_queries(q):
    lo_half = lax.broadcasted_iota(jnp.int32, q.shape, 1) < HEAD_DIM
    return jnp.concatenate([jnp.where(lo_half, q, 0.0), jnp.where(lo_half, 0.0, q)], axis=0).astype(BF16)


def _pair_outputs(o):
    t = o.shape[0] // 2
    lo_half = lax.broadcasted_iota(jnp.int32, (t, PAIR_W), 1) < HEAD_DIM
    return jnp.where(lo_half, o[:t], o[t:])


def _ctx_attn_kernel(q_ref, k_ref, v_ref, o_ref):
    pairs = [slice(p * PAIR_W, (p + 1) * PAIR_W) for p in range(N_HEADS // 2)]
    scores = [_dot_nt(_pair_queries(q_ref[:, ps]), k_ref[:, ps].astype(BF16)) * ATT_SCALE for ps in pairs]
    probs = []
    for s in scores:
        e = jnp.exp(s - jnp.max(s, axis=-1, keepdims=True))
        probs.append((e.astype(BF16), jnp.sum(e, axis=-1, keepdims=True)))
    for ps, (e, l) in zip(pairs, probs):
        o_ref[:, ps] = _pair_outputs(_dot(e, v_ref[:, ps].astype(BF16)) / l).astype(BF16)


def _ctx_attention(q, k, v):
    spec = pl.BlockSpec((SEQ, WIDTH), lambda b: (b, 0))
    return pl.pallas_call(
        _ctx_attn_kernel,
        grid=(BATCH,),
        in_specs=[spec, spec, spec],
        out_specs=spec,
        out_shape=jax.ShapeDtypeStruct((N_CTX_TOK, WIDTH), BF16),
        compiler_params=_params(1),
        name="ctxattn",
    )(q, k, v)


def _lat_attn_kernel(q_ref, k_ref, v_ref, ck_ref, cv_ref, bias_ref, o_ref, kbf, vbf, ckbf, cvbf):
    i = pl.program_id(1)

    @pl.when(i == 0)
    def _():
        kbf[...] = k_ref[...].astype(BF16)
        vbf[...] = v_ref[...].astype(BF16)
        ckbf[...] = ck_ref[0].astype(BF16)
        cvbf[...] = cv_ref[0].astype(BF16)

    win = NA_ROWS * GRID_W
    first_row = jnp.clip(i - NA_ROWS // 2, 0, GRID_ROWS - NA_ROWS)
    start = pl.multiple_of(first_row * GRID_W, GRID_W)
    off0 = first_row - i + NA_ROWS - 1
    pairs = [slice(p * PAIR_W, (p + 1) * PAIR_W) for p in range(N_HEADS // 2)]
    scores = []
    for p, ps in enumerate(pairs):
        lhs = _pair_queries(q_ref[:, ps])
        bias = jnp.concatenate(
            [jnp.concatenate([bias_ref[2 * p + hh, off0 + 2 * j] for j in range(NA_ROWS // 2)], axis=-1)
             for hh in range(2)], axis=0)
        s_loc = _dot_nt(lhs, kbf[pl.ds(start, win), ps]) * ATT_SCALE + bias
        s_ctx = _dot_nt(lhs, ckbf[:, ps]) * ATT_SCALE
        scores.append((s_loc, s_ctx))
    probs = []
    for s_loc, s_ctx in scores:
        m = jnp.maximum(jnp.max(s_loc, axis=-1, keepdims=True), jnp.max(s_ctx, axis=-1, keepdims=True))
        e_loc = jnp.exp(s_loc - m)
        e_ctx = jnp.exp(s_ctx - m)
        l = jnp.sum(e_loc, axis=-1, keepdims=True) + jnp.sum(e_ctx, axis=-1, keepdims=True)
        probs.append((e_loc.astype(BF16), e_ctx.astype(BF16), l))
    for ps, (e_loc, e_ctx, l) in zip(pairs, probs):
        o = _dot(e_loc, vbf[pl.ds(start, win), ps]) + _dot(e_ctx, cvbf[:, ps])
        o_ref[:, ps] = _pair_outputs(o / l).astype(BF16)


def _na_bias_table(rpb):
    h, d, n = rpb.shape
    w = GRID_W
    lead = w - NA_COLS
    p = jnp.pad(rpb.astype(F32), ((0, 0), (0, 0), (lead, 2 * w - lead - n)), constant_values=MASK_VALUE)
    flat = jnp.broadcast_to(p[:, :, None, :], (h, d, w, 2 * w)).reshape(h, d, 2 * w * w)
    t = flat[:, :, :w * (2 * w - 1)].reshape(h, d, w, 2 * w - 1)[:, :, :, w - 1:]
    cq = jnp.arange(w)[:, None]
    ck = jnp.arange(w)[None, :]
    cs = jnp.clip(cq - NA_COLS // 2, 0, w - NA_COLS)
    t = jnp.where((ck >= cs) & (ck < cs + NA_COLS), t, MASK_VALUE)
    return jnp.concatenate([t[:, :-1], t[:, 1:]], axis=-1)


def _lat_attention(q, k, v, cache_k, cache_v, bias):
    lat0 = N_CTX_TOK // GRID_W
    seq = pl.BlockSpec((DEC_SEQ, WIDTH), lambda b, i: (b, 0))
    ctx = pl.BlockSpec((1, PAST_LEN, WIDTH), lambda b, i: (b, 0, 0))
    return pl.pallas_call(
        _lat_attn_kernel,
        grid=(DEC_BATCH, GRID_ROWS),
        in_specs=[pl.BlockSpec((GRID_W, WIDTH), lambda b, i: (lat0 + b * GRID_ROWS + i, 0)),
                  seq, seq, ctx, ctx,
                  _const_spec((N_HEADS, 2 * NA_ROWS - 2, GRID_W, PAIR_W))],
        out_specs=pl.BlockSpec((GRID_W, WIDTH), lambda b, i: (b * GRID_ROWS + i, 0)),
        out_shape=jax.ShapeDtypeStruct((N_LAT_TOK, WIDTH), BF16),
        scratch_shapes=[pltpu.VMEM((DEC_SEQ, WIDTH), BF16), pltpu.VMEM((DEC_SEQ, WIDTH), BF16),
                        pltpu.VMEM((PAST_LEN, WIDTH), BF16), pltpu.VMEM((PAST_LEN, WIDTH), BF16)],
        compiler_params=_params(2),
        name="latattn",
    )(q, k, v, cache_k, cache_v, bias)


def _rwkv_pre_kernel(z_ref, zp_ref, zn_ref, wts_ref, w0_ref, w2_ref, a0_ref, a2_ref, g2_ref,
                     kk_ref, ka_ref, rk_ref, ones_ref,
                     r_out, kk_out, v_out, g_out, bonus_out, lw_out, b_out, kd_out):
    has_prev, has_next = _seq_neighbours(pl.program_id(0))

    def conv(c0, c1):
        z = z_ref[:, c0:c1]
        prow = jnp.where(has_prev, zp_ref[7:8, c0:c1], 0.0)
        nrow = jnp.where(has_next, zn_ref[0:1, c0:c1], 0.0)
        zprev, znext = _shifted(z, prow, nrow)
        return zprev * wts_ref[0:1, c0:c1] + z * wts_ref[1:2, c0:c1] + znext * wts_ref[2:3, c0:c1]

    r = conv(0, WIDTH)
    kr = conv(WIDTH, 2 * WIDTH)
    v = conv(2 * WIDTH, 3 * WIDTH)
    o = 3 * WIDTH
    tw = jnp.tanh(conv(o, o + N_DIRS * LORA))
    o += N_DIRS * LORA
    xa = conv(o, o + N_DIRS * LORA)
    o += N_DIRS * LORA
    xg = conv(o, o + GATE_LORA)
    ones = ones_ref[...]

    kkraw = kr * kk_ref[...]
    kk = kkraw * lax.rsqrt(_segsum(kkraw * kkraw, ones) + 1e-12)
    kd_sum = jnp.zeros_like(kr)
    for d in range(N_DIRS):
        ls = slice(d * LORA, (d + 1) * LORA)
        w_log = -_softplus(-(w0_ref[d:d + 1, :] + _mm(_split(tw[:, ls]), _split(w2_ref[d])))) - 0.5
        a = jax.nn.sigmoid(a0_ref[d:d + 1, :] + _mm(_split(xa[:, ls]), _split(a2_ref[d])))
        kd = kr * (1.0 + (a - 1.0) * ka_ref[...])
        lw_out[d] = -jnp.exp(w_log)
        b_out[d] = kk * a
        kd_out[d] = kd
        kd_sum = kd_sum + kd
    r_out[...] = r
    kk_out[...] = kk
    v_out[...] = v
    g_out[...] = _mm(_split(jax.nn.sigmoid(xg)), _split(g2_ref[...]))
    bonus_out[...] = _segsum(r * kd_sum * rk_ref[...], ones) * v


def _rwkv_pre(zr, w_ts, w0, w2, a0, a2, g2, k_k, k_a, r_k, ones):
    tok = _tok_spec(WIDTH)
    dtok = pl.BlockSpec((N_DIRS, TOK_BLOCK, WIDTH), lambda i: (0, i, 0))
    rows8 = TOK_BLOCK // 8
    return pl.pallas_call(
        _rwkv_pre_kernel,
        grid=(N_BLOCKS,),
        in_specs=[_tok_spec(RWKV_COLS),
                  pl.BlockSpec((8, RWKV_COLS), lambda i: (jnp.maximum(i * rows8 - 1, 0), 0)),
                  pl.BlockSpec((8, RWKV_COLS), lambda i: (jnp.minimum((i + 1) * rows8, N_TOK // 8 - 1), 0)),
                  _const_spec((3, RWKV_COLS)),
                  _const_spec((N_DIRS, WIDTH)), _const_spec((N_DIRS, LORA, WIDTH)),
                  _const_spec((N_DIRS, WIDTH)), _const_spec((N_DIRS, LORA, WIDTH)),
                  _const_spec((GATE_LORA, WIDTH)),
                  _const_spec((1, WIDTH)), _const_spec((1, WIDTH)), _const_spec((1, WIDTH)),
                  _const_spec((WIDTH, WIDTH))],
        out_specs=[tok] * 5 + [dtok] * 3,
        out_shape=[jax.ShapeDtypeStruct((N_TOK, WIDTH), F32)] * 5
        + [jax.ShapeDtypeStruct((N_DIRS, N_TOK, WIDTH), F32)] * 3,
        compiler_params=_params(1),
        name="rwkvpre",
    )(zr, zr, zr, w_ts, w0, w2, a0, a2, g2, k_k, k_a, r_k, ones)


HALF = CHUNK // 2


def _blockdiag(x, block=HEAD_DIM):
    def one(p):
        blk = lax.broadcasted_iota(jnp.int32, p.shape, 1) // block
        return jnp.concatenate([jnp.where(blk == q, p, jnp.zeros_like(p)) for q in range(GROUP_W // block)], axis=0)
    return one(x[0]), one(x[1])


def _head_transpose(x):
    xt = jnp.transpose(x)
    return jnp.concatenate([xt[h * HEAD_DIM:(h + 1) * HEAD_DIM] for h in range(GROUP)], axis=1)


def _scan_prepare(d, r_ref, kk_ref, v_ref, lw_ref, b_ref, kd_ref, y_ref, s_ref):
    row = lax.broadcasted_iota(jnp.int32, (CHUNK, GROUP_W), 0)
    lane = lax.broadcasted_iota(jnp.int32, (CHUNK, GROUP_W), 1)
    col = lane % CHUNK
    strict = col > row if d else col < row
    incl = col >= row if d else col <= row
    row_h = lax.broadcasted_iota(jnp.int32, (HALF, GROUP_W), 0)
    lane_h = lax.broadcasted_iota(jnp.int32, (HALF, GROUP_W), 1)
    lo_s = lane_h % CHUNK < HALF
    eye_half = (lane_h % HALF == row_h).astype(F32)

    lw = lw_ref[0]
    cum = _dot(incl[:, :CHUNK].astype(F32), lw, HIGHEST)
    tot = jnp.sum(lw, axis=0, keepdims=True)
    p_in = jnp.exp(cum)
    p_neg = jnp.exp(-cum)
    p_rem = jnp.exp(tot - cum)
    kt = kk_ref[...] * jnp.exp(cum - lw)
    rt = r_ref[...] * p_in
    b = b_ref[0]
    kd = kd_ref[0]
    bt = b * p_neg
    kdt = kd * p_neg
    bh = b * p_rem
    kh = kd * p_rem
    p_end = jnp.exp(tot)
    v = v_ref[...]

    chains = []
    for gi in range(N_HEADS // GROUP):
        sl = slice(gi * GROUP_W, (gi + 1) * GROUP_W)
        chains.append(dict(d=d, sl=sl, strict=strict, incl=incl, lo_s=lo_s, eye_half=eye_half, y_ref=y_ref, s_ref=s_ref,
                           kt=kt[:, sl], rt=rt[:, sl], v=v[:, sl], bt=bt[:, sl], kdt=kdt[:, sl],
                           kh=kh[:, sl], bh=bh[:, sl], p_end=p_end[:, sl]))
    return chains


def _scan_chains(chains):
    zeros = jnp.zeros((HALF, GROUP_W), F32)
    for c in chains:
        lhs = _split(jnp.concatenate([c["kt"], c["rt"]], axis=0))
        g_b = _mm_nt(lhs, _blockdiag(_split(c["bt"])))
        g_k = _mm_nt(lhs, _blockdiag(_split(c["kdt"])))
        c["a_k"] = jnp.where(c["strict"], g_k[:CHUNK], 0.0)
        c["a_rb"] = jnp.where(c["incl"], g_b[CHUNK:], 0.0)
        c["a_rk"] = jnp.where(c["incl"], g_k[CHUNK:], 0.0)
        n = jnp.where(c["strict"], -g_b[:CHUNK], 0.0)
        c["n"] = n
        c["pw"] = jnp.where(c["lo_s"], n[:HALF], n[HALF:])
        c["tp"] = c["eye_half"] + c["pw"]
    for c in chains:
        c["pw"] = _mm(_split(c["pw"]), _blockdiag(_split(c["pw"]), HALF))
    for _ in range(3):
        for c in chains:
            both = _mm(_split(jnp.concatenate([c["tp"], c["pw"]], axis=0)), _blockdiag(_split(c["pw"]), HALF))
            c["tp"] = c["tp"] + both[:HALF]
            c["pw"] = both[HALF:]
    for c in chains:
        c["tp"] = c["tp"] + _mm(_split(c["tp"]), _blockdiag(_split(c["pw"]), HALF))
    for c in chains:
        c["t1"] = jnp.where(c["lo_s"], c["tp"], 0.0)
        c["t2"] = jnp.where(c["lo_s"], 0.0, c["tp"])
        if c["d"] == 0:
            off, inner = jnp.where(c["lo_s"], c["n"][HALF:], 0.0), jnp.concatenate([c["t1"], zeros], axis=0)
        else:
            off, inner = jnp.where(c["lo_s"], 0.0, c["n"][:HALF]), jnp.concatenate([zeros, c["t2"]], axis=0)
        c["off"] = _mm(_split(off), _blockdiag(_split(inner)))
    for c in chains:
        if c["d"] == 0:
            x = _mm(_split(c["t2"]), _blockdiag(_split(jnp.concatenate([zeros, c["off"]], axis=0))))
            c["inv"] = jnp.concatenate([c["t1"], jnp.where(c["lo_s"], x, c["tp"])], axis=0)
        else:
            x = _mm(_split(c["t1"]), _blockdiag(_split(jnp.concatenate([c["off"], zeros], axis=0))))
            c["inv"] = jnp.concatenate([jnp.where(c["lo_s"], c["tp"], x), c["t2"]], axis=0)
    for c in chains:
        c["av"] = _mm(_split(jnp.concatenate([c["a_k"], c["a_rk"]], axis=0)), _blockdiag(_split(c["v"])))
        c["vt"] = _head_transpose(c["v"])
    for c in chains:
        c["s"] = c["s_ref"][0, :, c["sl"]]
        c["z"] = _mm_nt(_split(jnp.concatenate([c["rt"], c["kt"]], axis=0)), _blockdiag(_split(c["s"])))
    for c in chains:
        c["u"] = _mm(_split(c["inv"]), _blockdiag(_split(c["z"][CHUNK:] + c["av"][:CHUNK])))
    for c in chains:
        c["y_ref"][:, c["sl"]] = (c["z"][:CHUNK] + c["av"][CHUNK:]
                                  - _mm(_split(c["a_rb"]), _blockdiag(_split(c["u"]))))
    for c in chains:
        upd = (_mm(_split(c["vt"]), _blockdiag(_split(c["kh"])))
               - _mm(_split(_head_transpose(c["u"])), _blockdiag(_split(c["bh"]))))
        c["s_ref"][0, :, c["sl"]] = c["s"] * c["p_end"] + upd


REGION_CHUNKS = CTX_CHUNKS
SCAN_STREAMS = tuple((d, base, per_seq) for d in range(N_DIRS)
                     for base, per_seq in ((0, CHUNKS_PER_CTX_SEQ), (CTX_CHUNKS, CHUNKS_PER_LAT_SEQ)))
STREAM_INS = 7


def _stream_local_chunk(stream, j):
    return REGION_CHUNKS - 1 - j if stream[0] else j


def _scan_kernel(*refs):
    n_in = STREAM_INS * len(SCAN_STREAMS)
    j = pl.program_id(0)
    per_stream = []
    for s, stream in enumerate(SCAN_STREAMS):
        d, _, per_seq = stream
        ins = refs[STREAM_INS * s:STREAM_INS * (s + 1)]
        y_ref, s_ref = refs[n_in + 2 * s:n_in + 2 * s + 2]
        s0_ref = ins[-1]
        local = _stream_local_chunk(stream, j) % per_seq

        @pl.when(local == (per_seq - 1 if d else 0))
        def _():
            s_ref[...] = s0_ref[0]

        per_stream.append((d,) + ins[:-1] + (y_ref, s_ref))

    chains = []
    for args in per_stream:
        chains += _scan_prepare(*args)
    _scan_chains(chains)


def _scan(r, kk, v, lw, b, kd, s0):
    in_specs, out_specs, out_shape, operands = [], [], [], []
    for stream in SCAN_STREAMS:
        d, base, per_seq = stream
        seq0 = 0 if base == 0 else BATCH
        n_seq = REGION_CHUNKS // per_seq
        loc = lambda j, stream=stream: _stream_local_chunk(stream, j)
        tok = pl.BlockSpec((CHUNK, WIDTH), lambda j, loc=loc, base=base: (base + loc(j), 0))
        dtok = pl.BlockSpec((1, CHUNK, WIDTH), lambda j, loc=loc, base=base, d=d: (d, base + loc(j), 0))
        st_in = pl.BlockSpec((1, 1, HEAD_DIM, WIDTH),
                             lambda j, loc=loc, seq0=seq0, per_seq=per_seq, d=d: (seq0 + loc(j) // per_seq, d, 0, 0))
        in_specs += [tok, tok, tok, dtok, dtok, dtok, st_in]
        operands += [r, kk, v, lw, b, kd, s0]
        out_specs += [pl.BlockSpec((CHUNK, WIDTH), lambda j, loc=loc: (loc(j), 0)),
                      pl.BlockSpec((1, HEAD_DIM, WIDTH), lambda j, loc=loc, per_seq=per_seq: (loc(j) // per_seq, 0, 0))]
        out_shape += [jax.ShapeDtypeStruct((REGION_CHUNKS * CHUNK, WIDTH), F32),
                      jax.ShapeDtypeStruct((n_seq, HEAD_DIM, WIDTH), F32)]
    return pl.pallas_call(
        _scan_kernel,
        grid=(REGION_CHUNKS,),
        in_specs=in_specs,
        out_specs=out_specs,
        out_shape=out_shape,
        compiler_params=_params(1),
        name="scan",
    )(*operands)


def _post_kernel(yfc_ref, yfl_ref, ybc_ref, ybl_ref, bonus_ref, g_ref, attc_ref, attl_ref, xp_ref, xs_ref, mod_ref,
                 lng_ref, lnb_ref, ones_ref, wout_ref, n2_ref, x1_ref, h2_ref):
    i = pl.program_id(0)
    ones = ones_ref[...]
    inv_n = 1.0 / HEAD_DIM
    y = _ctx_or_lat(i, yfc_ref, yfl_ref) + _ctx_or_lat(i, ybc_ref, ybl_ref)
    yc = y - _segsum(y, ones) * inv_n
    var = _segsum(yc * yc, ones) * inv_n
    yn = yc * lax.rsqrt(var + GN_EPS) * lng_ref[...] + lnb_ref[...]
    r_out = ((yn + bonus_ref[...]) * g_ref[...]).astype(BF16)
    att = _ctx_or_lat(i, attc_ref, attl_ref)
    o = _dot(att, wout_ref[0:WIDTH, :]) + _dot(r_out, wout_ref[WIDTH:, :])
    x1 = _ctx_or_lat(i, xp_ref, xs_ref) + mod_ref[0, 2:3, :] * o
    x1_ref[...] = x1
    h2 = _rmsnorm(x1, n2_ref[...]) * (1.0 + mod_ref[0, 4:5, :]) + mod_ref[0, 3:4, :]
    h2_ref[...] = h2.astype(BF16)


def _post(ys, bonus, g, att_ctx, att_lat, xp, xs, mod, ln_g, ln_b, ones, w_out_bf, norm2):
    tok = _tok_spec
    ctx_lat = [_ctx_tok_spec(WIDTH), _lat_tok_spec(WIDTH)]
    return pl.pallas_call(
        _post_kernel,
        grid=(N_BLOCKS,),
        in_specs=ctx_lat + ctx_lat + [tok(WIDTH), tok(WIDTH)]
        + ctx_lat + [_ctx_tok_spec(D_MODEL), _lat_tok_spec(D_MODEL),
                  _mod_spec(),
                  _const_spec((1, WIDTH)), _const_spec((1, WIDTH)), _const_spec((WIDTH, WIDTH)),
                  _const_spec((2 * WIDTH, D_MODEL)), _const_spec((1, D_MODEL))],
        out_specs=[tok(D_MODEL), tok(D_MODEL)],
        out_shape=[jax.ShapeDtypeStruct((N_TOK, D_MODEL), F32),
                   jax.ShapeDtypeStruct((N_TOK, D_MODEL), BF16)],
        compiler_params=_params(1),
        name="post",
    )(*ys, bonus, g, att_ctx, att_lat, xp, xs, mod, ln_g, ln_b, ones, w_out_bf, norm2)


def _ffn_kernel(x1_ref, h2_ref, hp_ref, hn_ref, mod_ref, w1_ref, w3_ref, wc_ref, w2_ref, nf_ref, yp_ref, ys_ref):
    i = pl.program_id(0)
    has_prev, has_next = _seq_neighbours(i)
    h2 = h2_ref[...]
    halo = jnp.concatenate([hp_ref[...], hn_ref[...]], axis=0)
    acc = jnp.zeros((TOK_BLOCK, D_MODEL), F32)
    for f in range(D_FF // FF_TILE):
        fs = slice(f * FF_TILE, (f + 1) * FF_TILE)
        a = _dot(h2, w1_ref[:, fs])
        ah = _dot(halo, w1_ref[:, fs])
        prow = jnp.where(has_prev, ah[15:16, :], 0.0)
        nrow = jnp.where(has_next, ah[16:17, :], 0.0)
        aprev, anext = _shifted(a, prow, nrow)
        cv = aprev * wc_ref[0:1, fs] + a * wc_ref[1:2, fs] + anext * wc_ref[2:3, fs]
        act = _silu(cv) * _dot(h2, w3_ref[:, fs])
        acc = acc + _dot(act.astype(BF16), w2_ref[fs, :])
    x2 = x1_ref[...] + mod_ref[0, 5:6, :] * acc
    y = _rmsnorm(x2, nf_ref[...])

    @pl.when(i < CTX_BLOCKS)
    def _():
        yp_ref[...] = y

    @pl.when(i >= CTX_BLOCKS)
    def _():
        ys_ref[...] = y


def _ffn(x1, h2, mod, w1_bf, w3_bf, wc, w2_bf, norm_f):
    rows16 = TOK_BLOCK // 16
    return pl.pallas_call(
        _ffn_kernel,
        grid=(N_BLOCKS,),
        in_specs=[_tok_spec(D_MODEL), _tok_spec(D_MODEL),
                  pl.BlockSpec((16, D_MODEL), lambda i: (jnp.maximum(i * rows16 - 1, 0), 0)),
                  pl.BlockSpec((16, D_MODEL), lambda i: (jnp.minimum((i + 1) * rows16, N_TOK // 16 - 1), 0)),
                  _mod_spec(),
                  _const_spec((D_MODEL, D_FF)), _const_spec((D_MODEL, D_FF)), _const_spec((3, D_FF)),
                  _const_spec((D_FF, D_MODEL)), _const_spec((1, D_MODEL))],
        out_specs=[_ctx_tok_spec(D_MODEL), _lat_tok_spec(D_MODEL)],
        out_shape=[jax.ShapeDtypeStruct((N_CTX_TOK, D_MODEL), F32),
                   jax.ShapeDtypeStruct((N_LAT_TOK, D_MODEL), F32)],
        compiler_params=_params(1),
        name="ffn",
    )(x1, h2, h2, h2, mod, w1_bf, w3_bf, wc, w2_bf, norm_f)


def kernel(x_prompt, x_sample, cache_k, cache_v, state_rwkv, c, c_ctx, w_ada, b_ada, norm1, norm2,
           w_in, w_ts, w0, w2, a0, a2, g2, k_k, k_a, r_k, ln_x_g, ln_x_b, rpb, w_out,
           w_ffn1, w_ffn3, w_ffn_conv, w_ffn2, norm_f):
    xp = x_prompt.reshape(N_CTX_TOK, D_MODEL)
    xs = x_sample.reshape(N_LAT_TOK, D_MODEL)
    row = lambda t: t.reshape(1, -1)
    cond = jnp.concatenate([c_ctx[None, :], c, jnp.zeros((8 - 1 - DEC_BATCH, D_MODEL), F32)], axis=0)
    mod = _modulation(cond, w_ada[0], b_ada[0]).reshape(8, 6, D_MODEL)

    q, k_ctx, v_ctx, k_lat, v_lat, zr = _inproj(xp, xs, mod, row(norm1[0]), w_in[0].astype(BF16))

    att_ctx = _ctx_attention(q, k_ctx, v_ctx)
    att_lat = _lat_attention(q, k_lat, v_lat, cache_k[:, 0].reshape(DEC_BATCH, PAST_LEN, WIDTH),
                             cache_v[:, 0].reshape(DEC_BATCH, PAST_LEN, WIDTH), _na_bias_table(rpb[0]))

    head_id = jnp.arange(WIDTH) // HEAD_DIM
    ones = (head_id[:, None] == head_id[None, :]).astype(BF16)
    r, kk, vv, g, bonus, lw, b, kd = _rwkv_pre(zr, w_ts[0], w0[0], w2[0], a0[0], a2[0], g2[0],
                                               row(k_k[0]), row(k_a[0]), row(r_k[0]), ones)
    s_lat = jnp.transpose(state_rwkv[:, 0], (0, 1, 3, 2, 4)).reshape(DEC_BATCH, N_DIRS, HEAD_DIM, WIDTH)
    s0 = jnp.concatenate([jnp.zeros((BATCH, N_DIRS, HEAD_DIM, WIDTH), F32), s_lat], axis=0)
    y_fc, s_fc, y_fl, _, y_bc, s_bc, y_bl, _ = _scan(r, kk, vv, lw, b, kd, s0)
    s_fin = jnp.stack([s_fc, s_bc], axis=1)

    x1, h2 = _post((y_fc, y_fl, y_bc, y_bl), bonus, g, att_ctx, att_lat, xp, xs, mod, row(ln_x_g[0]), row(ln_x_b[0]), ones,
                   w_out[0].astype(BF16), row(norm2[0]))
    yp, ys = _ffn(x1, h2, mod, w_ffn1[0].astype(BF16), w_ffn3[0].astype(BF16), w_ffn_conv[0],
                  w_ffn2[0].astype(BF16), row(norm_f))

    new_k = k_ctx.reshape(BATCH, 1, SEQ, N_HEADS, HEAD_DIM)
    new_v = v_ctx.reshape(BATCH, 1, SEQ, N_HEADS, HEAD_DIM)
    new_s = jnp.transpose(s_fin.reshape(BATCH, N_DIRS, HEAD_DIM, N_HEADS, HEAD_DIM),
                          (0, 1, 3, 2, 4)).reshape(BATCH, 1, N_DIRS, N_HEADS, HEAD_DIM, HEAD_DIM)
    return (yp.reshape(BATCH, SEQ, D_MODEL), ys.reshape(DEC_BATCH, DEC_SEQ, D_MODEL), new_k, new_v, new_s)
```

```python
import jax
import jax.numpy as jnp
from jax import lax
from jax.experimental import pallas as pl
from jax.experimental.pallas import tpu as pltpu

F32 = jnp.float32
BF16 = jnp.bfloat16
HIGHEST = lax.Precision.HIGHEST

D_MODEL = 1024
BATCH = 16
SEQ = 256
DEC_BATCH = 2
DEC_SEQ = 2048
PAST_LEN = 512
GRID_W = 64
HEAD_DIM = 64
N_HEADS = 8
WIDTH = N_HEADS * HEAD_DIM
PAIR_W = 2 * HEAD_DIM
NA_ROWS = 8
NA_COLS = 16
N_DIRS = 2
LORA = 64
GATE_LORA = 128
D_FF = 2816
EPS = 1e-6
GN_EPS = 64e-5
ATT_SCALE = HEAD_DIM ** -0.5
RWKV_COLS = 3 * WIDTH + N_DIRS * 2 * LORA + GATE_LORA
MASK_VALUE = -1e30

N_CTX_TOK = BATCH * SEQ
N_LAT_TOK = DEC_BATCH * DEC_SEQ
N_TOK = N_CTX_TOK + N_LAT_TOK
TOK_BLOCK = 256
N_BLOCKS = N_TOK // TOK_BLOCK
BIG_BLOCK = 512
N_SEQS = BATCH + DEC_BATCH
CHUNK = 64
N_CHUNKS = N_TOK // CHUNK
CTX_CHUNKS = N_CTX_TOK // CHUNK
CHUNKS_PER_CTX_SEQ = SEQ // CHUNK
CHUNKS_PER_LAT_SEQ = DEC_SEQ // CHUNK
GRID_ROWS = DEC_SEQ // GRID_W
FF_TILE = D_FF // 2
GROUP = 4
GROUP_W = GROUP * HEAD_DIM
VMEM_LIMIT = 56 * 1024 * 1024


def _params(n_axes, limit=VMEM_LIMIT):
    return pltpu.CompilerParams(dimension_semantics=("arbitrary",) * n_axes, vmem_limit_bytes=limit)


def _const_spec(shape):
    zeros = (0,) * len(shape)
    return pl.BlockSpec(shape, lambda *_: zeros, pipeline_mode=pl.Buffered(1))


def _tok_spec(cols, blk=TOK_BLOCK):
    return pl.BlockSpec((blk, cols), lambda i: (i, 0))


def _ctx_tok_spec(cols, blk=TOK_BLOCK):
    return pl.BlockSpec((blk, cols), lambda i: (jnp.minimum(i, N_CTX_TOK // blk - 1), 0))


def _lat_tok_spec(cols, blk=TOK_BLOCK):
    return pl.BlockSpec((blk, cols), lambda i: (jnp.maximum(i - N_CTX_TOK // blk, 0), 0))


def _ctx_or_lat(i, ctx_ref, lat_ref, blk=TOK_BLOCK):
    return jnp.where(i < N_CTX_TOK // blk, ctx_ref[...], lat_ref[...])


def _mod_spec(blk=TOK_BLOCK):
    def row(i):
        return jnp.where(i < N_CTX_TOK // blk, 0, 1 + (i - N_CTX_TOK // blk) // (DEC_SEQ // blk))
    return pl.BlockSpec((1, 6, D_MODEL), lambda i: (row(i), 0, 0))


def _seq_neighbours(i, blk=TOK_BLOCK):
    per_seq = DEC_SEQ // blk
    j = (i - N_CTX_TOK // blk) % per_seq
    lat = i >= N_CTX_TOK // blk
    return lat & (j != 0), lat & (j != per_seq - 1)


def _silu(x):
    return x * jax.nn.sigmoid(x)


def _softplus(x):
    return jnp.maximum(x, 0.0) + jnp.log1p(jnp.exp(-jnp.abs(x)))


def _rmsnorm(x, g):
    return x * lax.rsqrt(jnp.mean(x * x, axis=-1, keepdims=True) + EPS) * g


def _dot(a, b, precision=None):
    return jnp.dot(a, b, precision=precision, preferred_element_type=F32)


def _dot_nt(a, b, precision=None):
    return lax.dot_general(a, b, (((1,), (1,)), ((), ())), precision=precision, preferred_element_type=F32)


def _split(x):
    hi = x.astype(BF16)
    return hi, (x - hi.astype(F32)).astype(BF16)


def _mm3(dot, a, b):
    m = a[0].shape[0]
    hi = dot(jnp.concatenate([a[0], a[1]], axis=0), b[0])
    return hi[:m] + hi[m:] + dot(a[0], b[1])


def _mm(a, b):
    return _mm3(_dot, a, b)


def _mm_nt(a, b):
    return _mm3(_dot_nt, a, b)


def _segsum(x, ones_bf):
    m = x.shape[0]
    hi, lo = _split(x)
    s = _dot(jnp.concatenate([hi, lo], axis=0), ones_bf)
    return s[:m] + s[m:]


def _shifted(a, prev_row, next_row):
    t = a.shape[0]
    rid = lax.broadcasted_iota(jnp.int32, (t, 1), 0)
    prev = jnp.where(rid == 0, prev_row, pltpu.roll(a, 1, axis=0))
    nxt = jnp.where(rid == t - 1, next_row, pltpu.roll(a, t - 1, axis=0))
    return prev, nxt


def _mod_kernel(cond_ref, w_ref, b_ref, o_ref):
    o_ref[...] = _dot(_silu(cond_ref[...]), w_ref[...], HIGHEST) + b_ref[...]


def _modulation(cond, w_ada, b_ada):
    n = 6 * D_MODEL
    return pl.pallas_call(
        _mod_kernel,
        grid=(6,),
        in_specs=[pl.BlockSpec((8, D_MODEL), lambda j: (0, 0)),
                  pl.BlockSpec((D_MODEL, D_MODEL), lambda j: (0, j)),
                  pl.BlockSpec((1, D_MODEL), lambda j: (0, j))],
        out_specs=pl.BlockSpec((8, D_MODEL), lambda j: (0, j)),
        out_shape=jax.ShapeDtypeStruct((8, n), F32),
        compiler_params=_params(1),
        name="mod",
    )(cond, w_ada, b_ada.reshape(1, n))


def _inproj_kernel(xp_ref, xs_ref, mod_ref, n1_ref, w_ref, q_ref, kc_ref, vc_ref, kl_ref, vl_ref, zr_ref):
    i = pl.program_id(0)
    x = _ctx_or_lat(i, xp_ref, xs_ref, BIG_BLOCK)
    h = _rmsnorm(x, n1_ref[...]) * (1.0 + mod_ref[0, 1:2, :]) + mod_ref[0, 0:1, :]
    h = h.astype(BF16)
    q_ref[...] = _dot(h, w_ref[:, 0:WIDTH])
    k = _dot(h, w_ref[:, WIDTH:2 * WIDTH])
    v = _dot(h, w_ref[:, 2 * WIDTH:3 * WIDTH])

    @pl.when(i < N_CTX_TOK // BIG_BLOCK)
    def _():
        kc_ref[...] = k
        vc_ref[...] = v

    @pl.when(i >= N_CTX_TOK // BIG_BLOCK)
    def _():
        kl_ref[...] = k
        vl_ref[...] = v

    zr_ref[...] = _dot(h, w_ref[:, 3 * WIDTH:])


def _inproj(xp, xs, mod, norm1, w_in_bf):
    half = jax.ShapeDtypeStruct((N_CTX_TOK, WIDTH), F32)
    blk = BIG_BLOCK
    return pl.pallas_call(
        _inproj_kernel,
        grid=(N_TOK // blk,),
        in_specs=[_ctx_tok_spec(D_MODEL, blk), _lat_tok_spec(D_MODEL, blk), _mod_spec(blk),
                  _const_spec((1, D_MODEL)),
                  _const_spec((D_MODEL, 3 * WIDTH + RWKV_COLS))],
        out_specs=[_tok_spec(WIDTH, blk), _ctx_tok_spec(WIDTH, blk), _ctx_tok_spec(WIDTH, blk),
                   _lat_tok_spec(WIDTH, blk), _lat_tok_spec(WIDTH, blk), _tok_spec(RWKV_COLS, blk)],
        out_shape=[jax.ShapeDtypeStruct((N_TOK, WIDTH), F32), half, half, half, half,
                   jax.ShapeDtypeStruct((N_TOK, RWKV_COLS), F32)],
        compiler_params=_params(1),
        name="inproj",
    )(xp, xs, mod, norm1, w_in_bf)


def _pair_queries(q):
    lo_half = lax.broadcasted_iota(jnp.int32, q.shape, 1) < HEAD_DIM
    return jnp.concatenate([jnp.where(lo_half, q, 0.0), jnp.where(lo_half, 0.0, q)], axis=0).astype(BF16)


def _pair_outputs(o):
    t = o.shape[0] // 2
    lo_half = lax.broadcasted_iota(jnp.int32, (t, PAIR_W), 1) < HEAD_DIM
    return jnp.where(lo_half, o[:t], o[t:])


def _ctx_attn_kernel(q_ref, k_ref, v_ref, o_ref):
    pairs = [slice(p * PAIR_W, (p + 1) * PAIR_W) for p in range(N_HEADS // 2)]
    scores = [_dot_nt(_pair_queries(q_ref[:, ps]), k_ref[:, ps].astype(BF16)) * ATT_SCALE for ps in pairs]
    probs = []
    for s in scores:
        e = jnp.exp(s - jnp.max(s, axis=-1, keepdims=True))
        probs.append((e.astype(BF16), jnp.sum(e, axis=-1, keepdims=True)))
    for ps, (e, l) in zip(pairs, probs):
        o_ref[:, ps] = _pair_outputs(_dot(e, v_ref[:, ps].astype(BF16)) / l).astype(BF16)


def _ctx_attention(q, k, v):
    spec = pl.BlockSpec((SEQ, WIDTH), lambda b: (b, 0))
    return pl.pallas_call(
        _ctx_attn_kernel,
        grid=(BATCH,),
        in_specs=[spec, spec, spec],
        out_specs=spec,
        out_shape=jax.ShapeDtypeStruct((N_CTX_TOK, WIDTH), BF16),
        compiler_params=_params(1),
        name="ctxattn",
    )(q, k, v)


def _lat_attn_kernel(q_ref, k_ref, v_ref, ck_ref, cv_ref, bias_ref, o_ref, kbf, vbf, ckbf, cvbf):
    i = pl.program_id(1)

    @pl.when(i == 0)
    def _():
        kbf[...] = k_ref[...].astype(BF16)
        vbf[...] = v_ref[...].astype(BF16)
        ckbf[...] = ck_ref[0].astype(BF16)
        cvbf[...] = cv_ref[0].astype(BF16)

    win = NA_ROWS * GRID_W
    first_row = jnp.clip(i - NA_ROWS // 2, 0, GRID_ROWS - NA_ROWS)
    start = pl.multiple_of(first_row * GRID_W, GRID_W)
    off0 = first_row - i + NA_ROWS - 1
    pairs = [slice(p * PAIR_W, (p + 1) * PAIR_W) for p in range(N_HEADS // 2)]
    scores = []
    for p, ps in enumerate(pairs):
        lhs = _pair_queries(q_ref[:, ps])
        bias = jnp.concatenate(
            [jnp.concatenate([bias_ref[2 * p + hh, off0 + 2 * j] for j in range(NA_ROWS // 2)], axis=-1)
             for hh in range(2)], axis=0)
        s_loc = _dot_nt(lhs, kbf[pl.ds(start, win), ps]) * ATT_SCALE + bias
        s_ctx = _dot_nt(lhs, ckbf[:, ps]) * ATT_SCALE
        scores.append((s_loc, s_ctx))
    probs = []
    for s_loc, s_ctx in scores:
        m = jnp.maximum(jnp.max(s_loc, axis=-1, keepdims=True), jnp.max(s_ctx, axis=-1, keepdims=True))
        e_loc = jnp.exp(s_loc - m)
        e_ctx = jnp.exp(s_ctx - m)
        l = jnp.sum(e_loc, axis=-1, keepdims=True) + jnp.sum(e_ctx, axis=-1, keepdims=True)
        probs.append((e_loc.astype(BF16), e_ctx.astype(BF16), l))
    for ps, (e_loc, e_ctx, l) in zip(pairs, probs):
        o = _dot(e_loc, vbf[pl.ds(start, win), ps]) + _dot(e_ctx, cvbf[:, ps])
        o_ref[:, ps] = _pair_outputs(o / l).astype(BF16)


def _na_bias_table(rpb):
    h, d, n = rpb.shape
    w = GRID_W
    lead = w - NA_COLS
    p = jnp.pad(rpb.astype(F32), ((0, 0), (0, 0), (lead, 2 * w - lead - n)), constant_values=MASK_VALUE)
    flat = jnp.broadcast_to(p[:, :, None, :], (h, d, w, 2 * w)).reshape(h, d, 2 * w * w)
    t = flat[:, :, :w * (2 * w - 1)].reshape(h, d, w, 2 * w - 1)[:, :, :, w - 1:]
    cq = jnp.arange(w)[:, None]
    ck = jnp.arange(w)[None, :]
    cs = jnp.clip(cq - NA_COLS // 2, 0, w - NA_COLS)
    t = jnp.where((ck >= cs) & (ck < cs + NA_COLS), t, MASK_VALUE)
    return jnp.concatenate([t[:, :-1], t[:, 1:]], axis=-1)


def _lat_attention(q, k, v, cache_k, cache_v, bias):
    lat0 = N_CTX_TOK // GRID_W
    seq = pl.BlockSpec((DEC_SEQ, WIDTH), lambda b, i: (b, 0))
    ctx = pl.BlockSpec((1, PAST_LEN, WIDTH), lambda b, i: (b, 0, 0))
    return pl.pallas_call(
        _lat_attn_kernel,
        grid=(DEC_BATCH, GRID_ROWS),
        in_specs=[pl.BlockSpec((GRID_W, WIDTH), lambda b, i: (lat0 + b * GRID_ROWS + i, 0)),
                  seq, seq, ctx, ctx,
                  _const_spec((N_HEADS, 2 * NA_ROWS - 2, GRID_W, PAIR_W))],
        out_specs=pl.BlockSpec((GRID_W, WIDTH), lambda b, i: (b * GRID_ROWS + i, 0)),
        out_shape=jax.ShapeDtypeStruct((N_LAT_TOK, WIDTH), BF16),
        scratch_shapes=[pltpu.VMEM((DEC_SEQ, WIDTH), BF16), pltpu.VMEM((DEC_SEQ, WIDTH), BF16),
                        pltpu.VMEM((PAST_LEN, WIDTH), BF16), pltpu.VMEM((PAST_LEN, WIDTH), BF16)],
        compiler_params=_params(2),
        name="latattn",
    )(q, k, v, cache_k, cache_v, bias)


def _rwkv_pre_kernel(z_ref, zp_ref, zn_ref, wts_ref, w0_ref, w2_ref, a0_ref, a2_ref, g2_ref,
                     kk_ref, ka_ref, rk_ref, ones_ref,
                     r_out, kk_out, v_out, g_out, bonus_out, lw_out, b_out, kd_out):
    has_prev, has_next = _seq_neighbours(pl.program_id(0))

    def conv(c0, c1):
        z = z_ref[:, c0:c1]
        prow = jnp.where(has_prev, zp_ref[7:8, c0:c1], 0.0)
        nrow = jnp.where(has_next, zn_ref[0:1, c0:c1], 0.0)
        zprev, znext = _shifted(z, prow, nrow)
        return zprev * wts_ref[0:1, c0:c1] + z * wts_ref[1:2, c0:c1] + znext * wts_ref[2:3, c0:c1]

    r = conv(0, WIDTH)
    kr = conv(WIDTH, 2 * WIDTH)
    v = conv(2 * WIDTH, 3 * WIDTH)
    o = 3 * WIDTH
    tw = jnp.tanh(conv(o, o + N_DIRS * LORA))
    o += N_DIRS * LORA
    xa = conv(o, o + N_DIRS * LORA)
    o += N_DIRS * LORA
    xg = conv(o, o + GATE_LORA)
    ones = ones_ref[...]

    kkraw = kr * kk_ref[...]
    kk = kkraw * lax.rsqrt(_segsum(kkraw * kkraw, ones) + 1e-12)
    kd_sum = jnp.zeros_like(kr)
    for d in range(N_DIRS):
        ls = slice(d * LORA, (d + 1) * LORA)
        w_log = -_softplus(-(w0_ref[d:d + 1, :] + _mm(_split(tw[:, ls]), _split(w2_ref[d])))) - 0.5
        a = jax.nn.sigmoid(a0_ref[d:d + 1, :] + _mm(_split(xa[:, ls]), _split(a2_ref[d])))
        kd = kr * (1.0 + (a - 1.0) * ka_ref[...])
        lw_out[d] = -jnp.exp(w_log)
        b_out[d] = kk * a
        kd_out[d] = kd
        kd_sum = kd_sum + kd
    r_out[...] = r
    kk_out[...] = kk
    v_out[...] = v
    g_out[...] = _mm(_split(jax.nn.sigmoid(xg)), _split(g2_ref[...]))
    bonus_out[...] = _segsum(r * kd_sum * rk_ref[...], ones) * v


def _rwkv_pre(zr, w_ts, w0, w2, a0, a2, g2, k_k, k_a, r_k, ones):
    tok = _tok_spec(WIDTH)
    dtok = pl.BlockSpec((N_DIRS, TOK_BLOCK, WIDTH), lambda i: (0, i, 0))
    rows8 = TOK_BLOCK // 8
    return pl.pallas_call(
        _rwkv_pre_kernel,
        grid=(N_BLOCKS,),
        in_specs=[_tok_spec(RWKV_COLS),
                  pl.BlockSpec((8, RWKV_COLS), lambda i: (jnp.maximum(i * rows8 - 1, 0), 0)),
                  pl.BlockSpec((8, RWKV_COLS), lambda i: (jnp.minimum((i + 1) * rows8, N_TOK // 8 - 1), 0)),
                  _const_spec((3, RWKV_COLS)),
                  _const_spec((N_DIRS, WIDTH)), _const_spec((N_DIRS, LORA, WIDTH)),
                  _const_spec((N_DIRS, WIDTH)), _const_spec((N_DIRS, LORA, WIDTH)),
                  _const_spec((GATE_LORA, WIDTH)),
                  _const_spec((1, WIDTH)), _const_spec((1, WIDTH)), _const_spec((1, WIDTH)),
                  _const_spec((WIDTH, WIDTH))],
        out_specs=[tok] * 5 + [dtok] * 3,
        out_shape=[jax.ShapeDtypeStruct((N_TOK, WIDTH), F32)] * 5
        + [jax.ShapeDtypeStruct((N_DIRS, N_TOK, WIDTH), F32)] * 3,
        compiler_params=_params(1),
        name="rwkvpre",
    )(zr, zr, zr, w_ts, w0, w2, a0, a2, g2, k_k, k_a, r_k, ones)


HALF = CHUNK // 2


def _blockdiag(x, block=HEAD_DIM):
    def one(p):
        blk = lax.broadcasted_iota(jnp.int32, p.shape, 1) // block
        return jnp.concatenate([jnp.where(blk == q, p, jnp.zeros_like(p)) for q in range(GROUP_W // block)], axis=0)
    return one(x[0]), one(x[1])


def _head_transpose(x):
    xt = jnp.transpose(x)
    return jnp.concatenate([xt[h * HEAD_DIM:(h + 1) * HEAD_DIM] for h in range(GROUP)], axis=1)


def _scan_prepare(d, r_ref, kk_ref, v_ref, lw_ref, b_ref, kd_ref, y_ref, s_ref):
    row = lax.broadcasted_iota(jnp.int32, (CHUNK, GROUP_W), 0)
    lane = lax.broadcasted_iota(jnp.int32, (CHUNK, GROUP_W), 1)
    col = lane % CHUNK
    strict = col > row if d else col < row
    incl = col >= row if d else col <= row
    row_h = lax.broadcasted_iota(jnp.int32, (HALF, GROUP_W), 0)
    lane_h = lax.broadcasted_iota(jnp.int32, (HALF, GROUP_W), 1)
    lo_s = lane_h % CHUNK < HALF
    eye_half = (lane_h % HALF == row_h).astype(F32)

    lw = lw_ref[0]
    mask = jnp.where(incl[:, :CHUNK], 1.0, 0.0).astype(BF16)
    lw_hi, lw_mid = _split(lw)
    lw_lo = (lw - lw_hi.astype(F32) - lw_mid.astype(F32)).astype(BF16)
    cum = _dot(mask, lw_hi) + _dot(mask, lw_mid) + _dot(mask, lw_lo)
    tot = jnp.sum(lw, axis=0, keepdims=True)
    p_in = jnp.exp(cum)
    p_neg = jnp.exp(-cum)
    p_rem = jnp.exp(tot - cum)
    kt = kk_ref[...] * jnp.exp(cum - lw)
    rt = r_ref[...] * p_in
    b = b_ref[0]
    kd = kd_ref[0]
    bt = b * p_neg
    kdt = kd * p_neg
    bh = b * p_rem
    kh = kd * p_rem
    p_end = jnp.exp(tot)
    v = v_ref[...]

    chains = []
    for gi in range(N_HEADS // GROUP):
        sl = slice(gi * GROUP_W, (gi + 1) * GROUP_W)
        chains.append(dict(d=d, sl=sl, strict=strict, incl=incl, lo_s=lo_s, eye_half=eye_half, y_ref=y_ref, s_ref=s_ref,
                           kt=kt[:, sl], rt=rt[:, sl], v=v[:, sl], bt=bt[:, sl], kdt=kdt[:, sl],
                           kh=kh[:, sl], bh=bh[:, sl], p_end=p_end[:, sl]))
    return chains


def _scan_chains(chains):
    zeros = jnp.zeros((HALF, GROUP_W), F32)
    for c in chains:
        lhs = _split(jnp.concatenate([c["kt"], c["rt"]], axis=0))
        g_b = _mm_nt(lhs, _blockdiag(_split(c["bt"])))
        g_k = _mm_nt(lhs, _blockdiag(_split(c["kdt"])))
        c["a_k"] = jnp.where(c["strict"], g_k[:CHUNK], 0.0)
        c["a_rb"] = jnp.where(c["incl"], g_b[CHUNK:], 0.0)
        c["a_rk"] = jnp.where(c["incl"], g_k[CHUNK:], 0.0)
        n = jnp.where(c["strict"], -g_b[:CHUNK], 0.0)
        c["n"] = n
        c["pw"] = jnp.where(c["lo_s"], n[:HALF], n[HALF:])
        c["tp"] = c["eye_half"] + c["pw"]
    for c in chains:
        c["pw"] = _mm(_split(c["pw"]), _blockdiag(_split(c["pw"]), HALF))
    for _ in range(3):
        for c in chains:
            both = _mm(_split(jnp.concatenate([c["tp"], c["pw"]], axis=0)), _blockdiag(_split(c["pw"]), HALF))
            c["tp"] = c["tp"] + both[:HALF]
            c["pw"] = both[HALF:]
    for c in chains:
        c["tp"] = c["tp"] + _mm(_split(c["tp"]), _blockdiag(_split(c["pw"]), HALF))
    for c in chains:
        c["t1"] = jnp.where(c["lo_s"], c["tp"], 0.0)
        c["t2"] = jnp.where(c["lo_s"], 0.0, c["tp"])
        if c["d"] == 0:
            off, inner = jnp.where(c["lo_s"], c["n"][HALF:], 0.0), jnp.concatenate([c["t1"], zeros], axis=0)
        else:
            off, inner = jnp.where(c["lo_s"], 0.0, c["n"][:HALF]), jnp.concatenate([zeros, c["t2"]], axis=0)
        c["off"] = _mm(_split(off), _blockdiag(_split(inner)))
    for c in chains:
        if c["d"] == 0:
            x = _mm(_split(c["t2"]), _blockdiag(_split(jnp.concatenate([zeros, c["off"]], axis=0))))
            c["inv"] = jnp.concatenate([c["t1"], jnp.where(c["lo_s"], x, c["tp"])], axis=0)
        else:
            x = _mm(_split(c["t1"]), _blockdiag(_split(jnp.concatenate([c["off"], zeros], axis=0))))
            c["inv"] = jnp.concatenate([jnp.where(c["lo_s"], c["tp"], x), c["t2"]], axis=0)
    for c in chains:
        c["av"] = _mm(_split(jnp.concatenate([c["a_k"], c["a_rk"]], axis=0)), _blockdiag(_split(c["v"])))
        c["vt"] = _head_transpose(c["v"])
    for c in chains:
        c["s"] = c["s_ref"][0, :, c["sl"]]
        c["z"] = _mm_nt(_split(jnp.concatenate([c["rt"], c["kt"]], axis=0)), _blockdiag(_split(c["s"])))
    for c in chains:
        c["u"] = _mm(_split(c["inv"]), _blockdiag(_split(c["z"][CHUNK:] + c["av"][:CHUNK])))
    for c in chains:
        c["y_ref"][:, c["sl"]] = (c["z"][:CHUNK] + c["av"][CHUNK:]
                                  - _mm(_split(c["a_rb"]), _blockdiag(_split(c["u"]))))
    for c in chains:
        upd = (_mm(_split(c["vt"]), _blockdiag(_split(c["kh"])))
               - _mm(_split(_head_transpose(c["u"])), _blockdiag(_split(c["bh"]))))
        c["s_ref"][0, :, c["sl"]] = c["s"] * c["p_end"] + upd


REGION_CHUNKS = CTX_CHUNKS
SCAN_STREAMS = tuple((d, base, per_seq) for d in range(N_DIRS)
                     for base, per_seq in ((0, CHUNKS_PER_CTX_SEQ), (CTX_CHUNKS, CHUNKS_PER_LAT_SEQ)))
STREAM_INS = 7


def _stream_local_chunk(stream, j):
    return REGION_CHUNKS - 1 - j if stream[0] else j


def _scan_kernel(*refs):
    n_in = STREAM_INS * len(SCAN_STREAMS)
    j = pl.program_id(0)
    per_stream = []
    for s, stream in enumerate(SCAN_STREAMS):
        d, _, per_seq = stream
        ins = refs[STREAM_INS * s:STREAM_INS * (s + 1)]
        y_ref, s_ref = refs[n_in + 2 * s:n_in + 2 * s + 2]
        s0_ref = ins[-1]
        local = _stream_local_chunk(stream, j) % per_seq

        @pl.when(local == (per_seq - 1 if d else 0))
        def _():
            s_ref[...] = s0_ref[0]

        per_stream.append((d,) + ins[:-1] + (y_ref, s_ref))

    chains = []
    for args in per_stream:
        chains += _scan_prepare(*args)
    _scan_chains(chains)


def _scan(r, kk, v, lw, b, kd, s0):
    in_specs, out_specs, out_shape, operands = [], [], [], []
    for stream in SCAN_STREAMS:
        d, base, per_seq = stream
        seq0 = 0 if base == 0 else BATCH
        n_seq = REGION_CHUNKS // per_seq
        loc = lambda j, stream=stream: _stream_local_chunk(stream, j)
        tok = pl.BlockSpec((CHUNK, WIDTH), lambda j, loc=loc, base=base: (base + loc(j), 0))
        dtok = pl.BlockSpec((1, CHUNK, WIDTH), lambda j, loc=loc, base=base, d=d: (d, base + loc(j), 0))
        st_in = pl.BlockSpec((1, 1, HEAD_DIM, WIDTH),
                             lambda j, loc=loc, seq0=seq0, per_seq=per_seq, d=d: (seq0 + loc(j) // per_seq, d, 0, 0))
        in_specs += [tok, tok, tok, dtok, dtok, dtok, st_in]
        operands += [r, kk, v, lw, b, kd, s0]
        out_specs += [pl.BlockSpec((CHUNK, WIDTH), lambda j, loc=loc: (loc(j), 0)),
                      pl.BlockSpec((1, HEAD_DIM, WIDTH), lambda j, loc=loc, per_seq=per_seq: (loc(j) // per_seq, 0, 0))]
        out_shape += [jax.ShapeDtypeStruct((REGION_CHUNKS * CHUNK, WIDTH), F32),
                      jax.ShapeDtypeStruct((n_seq, HEAD_DIM, WIDTH), F32)]
    return pl.pallas_call(
        _scan_kernel,
        grid=(REGION_CHUNKS,),
        in_specs=in_specs,
        out_specs=out_specs,
        out_shape=out_shape,
        compiler_params=_params(1),
        name="scan",
    )(*operands)


def _post_kernel(yfc_ref, yfl_ref, ybc_ref, ybl_ref, bonus_ref, g_ref, attc_ref, attl_ref, xp_ref, xs_ref, mod_ref,
                 lng_ref, lnb_ref, ones_ref, wout_ref, n2_ref, x1_ref, h2_ref):
    i = pl.program_id(0)
    ones = ones_ref[...]
    inv_n = 1.0 / HEAD_DIM
    y = _ctx_or_lat(i, yfc_ref, yfl_ref) + _ctx_or_lat(i, ybc_ref, ybl_ref)
    yc = y - _segsum(y, ones) * inv_n
    var = _segsum(yc * yc, ones) * inv_n
    yn = yc * lax.rsqrt(var + GN_EPS) * lng_ref[...] + lnb_ref[...]
    r_out = ((yn + bonus_ref[...]) * g_ref[...]).astype(BF16)
    att = _ctx_or_lat(i, attc_ref, attl_ref)
    o = _dot(att, wout_ref[0:WIDTH, :]) + _dot(r_out, wout_ref[WIDTH:, :])
    x1 = _ctx_or_lat(i, xp_ref, xs_ref) + mod_ref[0, 2:3, :] * o
    x1_ref[...] = x1
    h2 = _rmsnorm(x1, n2_ref[...]) * (1.0 + mod_ref[0, 4:5, :]) + mod_ref[0, 3:4, :]
    h2_ref[...] = h2.astype(BF16)


def _post(ys, bonus, g, att_ctx, att_lat, xp, xs, mod, ln_g, ln_b, ones, w_out_bf, norm2):
    tok = _tok_spec
    ctx_lat = [_ctx_tok_spec(WIDTH), _lat_tok_spec(WIDTH)]
    return pl.pallas_call(
        _post_kernel,
        grid=(N_BLOCKS,),
        in_specs=ctx_lat + ctx_lat + [tok(WIDTH), tok(WIDTH)]
        + ctx_lat + [_ctx_tok_spec(D_MODEL), _lat_tok_spec(D_MODEL),
                  _mod_spec(),
                  _const_spec((1, WIDTH)), _const_spec((1, WIDTH)), _const_spec((WIDTH, WIDTH)),
                  _const_spec((2 * WIDTH, D_MODEL)), _const_spec((1, D_MODEL))],
        out_specs=[tok(D_MODEL), tok(D_MODEL)],
        out_shape=[jax.ShapeDtypeStruct((N_TOK, D_MODEL), F32),
                   jax.ShapeDtypeStruct((N_TOK, D_MODEL), BF16)],
        compiler_params=_params(1),
        name="post",
    )(*ys, bonus, g, att_ctx, att_lat, xp, xs, mod, ln_g, ln_b, ones, w_out_bf, norm2)


def _ffn_kernel(x1_ref, h2_ref, hp_ref, hn_ref, mod_ref, w1_ref, w3_ref, wc_ref, w2_ref, nf_ref, yp_ref, ys_ref):
    i = pl.program_id(0)
    blk = BIG_BLOCK
    is_ctx = i < N_CTX_TOK // blk
    has_prev, has_next = _seq_neighbours(i, blk)
    lhs = jnp.concatenate([h2_ref[...], hp_ref[...], hn_ref[...]], axis=0)
    h2 = lhs[:blk]
    rid = lax.broadcasted_iota(jnp.int32, (blk, 1), 0)
    inner = jnp.where(is_ctx, SEQ, -1)
    seq_start = rid == inner
    seq_end = rid == inner - 1
    acc = jnp.zeros((blk, D_MODEL), F32)
    for f in range(D_FF // FF_TILE):
        fs = slice(f * FF_TILE, (f + 1) * FF_TILE)
        a_all = _dot(lhs, w1_ref[:, fs])
        a = a_all[:blk]
        prow = jnp.where(has_prev, a_all[blk + 15:blk + 16, :], 0.0)
        nrow = jnp.where(has_next, a_all[blk + 16:blk + 17, :], 0.0)
        aprev, anext = _shifted(a, prow, nrow)
        aprev = jnp.where(seq_start, 0.0, aprev)
        anext = jnp.where(seq_end, 0.0, anext)
        cv = aprev * wc_ref[0:1, fs] + a * wc_ref[1:2, fs] + anext * wc_ref[2:3, fs]
        act = _silu(cv) * _dot(h2, w3_ref[:, fs])
        acc = acc + _dot(act.astype(BF16), w2_ref[fs, :])
    x2 = x1_ref[...] + mod_ref[0, 5:6, :] * acc
    y = _rmsnorm(x2, nf_ref[...])

    @pl.when(is_ctx)
    def _():
        yp_ref[...] = y

    @pl.when(jnp.logical_not(is_ctx))
    def _():
        ys_ref[...] = y


def _ffn(x1, h2, mod, w1_bf, w3_bf, wc, w2_bf, norm_f):
    blk = BIG_BLOCK
    rows16 = blk // 16
    return pl.pallas_call(
        _ffn_kernel,
        grid=(N_TOK // blk,),
        in_specs=[_tok_spec(D_MODEL, blk), _tok_spec(D_MODEL, blk),
                  pl.BlockSpec((16, D_MODEL), lambda i: (jnp.maximum(i * rows16 - 1, 0), 0)),
                  pl.BlockSpec((16, D_MODEL), lambda i: (jnp.minimum((i + 1) * rows16, N_TOK // 16 - 1), 0)),
                  _mod_spec(blk),
                  _const_spec((D_MODEL, D_FF)), _const_spec((D_MODEL, D_FF)), _const_spec((3, D_FF)),
                  _const_spec((D_FF, D_MODEL)), _const_spec((1, D_MODEL))],
        out_specs=[_ctx_tok_spec(D_MODEL, blk), _lat_tok_spec(D_MODEL, blk)],
        out_shape=[jax.ShapeDtypeStruct((N_CTX_TOK, D_MODEL), F32),
                   jax.ShapeDtypeStruct((N_LAT_TOK, D_MODEL), F32)],
        compiler_params=_params(1),
        name="ffn",
    )(x1, h2, h2, h2, mod, w1_bf, w3_bf, wc, w2_bf, norm_f)


def kernel(x_prompt, x_sample, cache_k, cache_v, state_rwkv, c, c_ctx, w_ada, b_ada, norm1, norm2,
           w_in, w_ts, w0, w2, a0, a2, g2, k_k, k_a, r_k, ln_x_g, ln_x_b, rpb, w_out,
           w_ffn1, w_ffn3, w_ffn_conv, w_ffn2, norm_f):
    xp = x_prompt.reshape(N_CTX_TOK, D_MODEL)
    xs = x_sample.reshape(N_LAT_TOK, D_MODEL)
    row = lambda t: t.reshape(1, -1)
    cond = jnp.concatenate([c_ctx[None, :], c, jnp.zeros((8 - 1 - DEC_BATCH, D_MODEL), F32)], axis=0)
    mod = _modulation(cond, w_ada[0], b_ada[0]).reshape(8, 6, D_MODEL)

    q, k_ctx, v_ctx, k_lat, v_lat, zr = _inproj(xp, xs, mod, row(norm1[0]), w_in[0].astype(BF16))

    att_ctx = _ctx_attention(q, k_ctx, v_ctx)
    att_lat = _lat_attention(q, k_lat, v_lat, cache_k[:, 0].reshape(DEC_BATCH, PAST_LEN, WIDTH),
                             cache_v[:, 0].reshape(DEC_BATCH, PAST_LEN, WIDTH), _na_bias_table(rpb[0]))

    head_id = jnp.arange(WIDTH) // HEAD_DIM
    ones = (head_id[:, None] == head_id[None, :]).astype(BF16)
    r, kk, vv, g, bonus, lw, b, kd = _rwkv_pre(zr, w_ts[0], w0[0], w2[0], a0[0], a2[0], g2[0],
                                               row(k_k[0]), row(k_a[0]), row(r_k[0]), ones)
    s_lat = jnp.transpose(state_rwkv[:, 0], (0, 1, 3, 2, 4)).reshape(DEC_BATCH, N_DIRS, HEAD_DIM, WIDTH)
    s0 = jnp.concatenate([jnp.zeros((BATCH, N_DIRS, HEAD_DIM, WIDTH), F32), s_lat], axis=0)
    y_fc, s_fc, y_fl, _, y_bc, s_bc, y_bl, _ = _scan(r, kk, vv, lw, b, kd, s0)
    s_fin = jnp.stack([s_fc, s_bc], axis=1)

    x1, h2 = _post((y_fc, y_fl, y_bc, y_bl), bonus, g, att_ctx, att_lat, xp, xs, mod, row(ln_x_g[0]), row(ln_x_b[0]), ones,
                   w_out[0].astype(BF16), row(norm2[0]))
    yp, ys = _ffn(x1, h2, mod, w_ffn1[0].astype(BF16), w_ffn3[0].astype(BF16), w_ffn_conv[0],
                  w_ffn2[0].astype(BF16), row(norm_f))

    new_k = k_ctx.reshape(BATCH, 1, SEQ, N_HEADS, HEAD_DIM)
    new_v = v_ctx.reshape(BATCH, 1, SEQ, N_HEADS, HEAD_DIM)
    new_s = jnp.transpose(s_fin.reshape(BATCH, N_DIRS, HEAD_DIM, N_HEADS, HEAD_DIM),
                          (0, 1, 3, 2, 4)).reshape(BATCH, 1, N_DIRS, N_HEADS, HEAD_DIM, HEAD_DIM)
    return (yp.reshape(BATCH, SEQ, D_MODEL), ys.reshape(DEC_BATCH, DEC_SEQ, D_MODEL), new_k, new_v, new_s)
```

```python
import jax
import jax.numpy as jnp
from jax import lax
from jax.experimental import pallas as pl
from jax.experimental.pallas import tpu as pltpu

F32 = jnp.float32
BF16 = jnp.bfloat16
HIGHEST = lax.Precision.HIGHEST

D_MODEL = 1024
BATCH = 16
SEQ = 256
DEC_BATCH = 2
DEC_SEQ = 2048
PAST_LEN = 512
GRID_W = 64
HEAD_DIM = 64
N_HEADS = 8
WIDTH = N_HEADS * HEAD_DIM
PAIR_W = 2 * HEAD_DIM
NA_ROWS = 8
NA_COLS = 16
N_DIRS = 2
LORA = 64
GATE_LORA = 128
D_FF = 2816
EPS = 1e-6
GN_EPS = 64e-5
ATT_SCALE = HEAD_DIM ** -0.5
RWKV_COLS = 3 * WIDTH + N_DIRS * 2 * LORA + GATE_LORA
MASK_VALUE = -1e30

N_CTX_TOK = BATCH * SEQ
N_LAT_TOK = DEC_BATCH * DEC_SEQ
N_TOK = N_CTX_TOK + N_LAT_TOK
TOK_BLOCK = 256
N_BLOCKS = N_TOK // TOK_BLOCK
BIG_BLOCK = 512
N_SEQS = BATCH + DEC_BATCH
CHUNK = 64
N_CHUNKS = N_TOK // CHUNK
CTX_CHUNKS = N_CTX_TOK // CHUNK
CHUNKS_PER_CTX_SEQ = SEQ // CHUNK
CHUNKS_PER_LAT_SEQ = DEC_SEQ // CHUNK
GRID_ROWS = DEC_SEQ // GRID_W
FF_TILE = D_FF // 2
GROUP = 4
GROUP_W = GROUP * HEAD_DIM
VMEM_LIMIT = 56 * 1024 * 1024


def _params(n_axes, limit=VMEM_LIMIT):
    return pltpu.CompilerParams(dimension_semantics=("arbitrary",) * n_axes, vmem_limit_bytes=limit)


def _const_spec(shape):
    zeros = (0,) * len(shape)
    return pl.BlockSpec(shape, lambda *_: zeros, pipeline_mode=pl.Buffered(1))


def _tok_spec(cols, blk=TOK_BLOCK):
    return pl.BlockSpec((blk, cols), lambda i: (i, 0))


def _ctx_tok_spec(cols, blk=TOK_BLOCK):
    return pl.BlockSpec((blk, cols), lambda i: (jnp.minimum(i, N_CTX_TOK // blk - 1), 0))


def _lat_tok_spec(cols, blk=TOK_BLOCK):
    return pl.BlockSpec((blk, cols), lambda i: (jnp.maximum(i - N_CTX_TOK // blk, 0), 0))


def _ctx_or_lat(i, ctx_ref, lat_ref, blk=TOK_BLOCK):
    return jnp.where(i < N_CTX_TOK // blk, ctx_ref[...], lat_ref[...])


def _mod_spec(blk=TOK_BLOCK):
    def row(i):
        return jnp.where(i < N_CTX_TOK // blk, 0, 1 + (i - N_CTX_TOK // blk) // (DEC_SEQ // blk))
    return pl.BlockSpec((1, 6, D_MODEL), lambda i: (row(i), 0, 0))


def _seq_neighbours(i, blk=TOK_BLOCK):
    per_seq = DEC_SEQ // blk
    j = (i - N_CTX_TOK // blk) % per_seq
    lat = i >= N_CTX_TOK // blk
    return lat & (j != 0), lat & (j != per_seq - 1)


def _silu(x):
    return x * jax.nn.sigmoid(x)


def _softplus(x):
    return jnp.maximum(x, 0.0) + jnp.log1p(jnp.exp(-jnp.abs(x)))


def _rmsnorm(x, g):
    return x * lax.rsqrt(jnp.mean(x * x, axis=-1, keepdims=True) + EPS) * g


def _dot(a, b, precision=None):
    return jnp.dot(a, b, precision=precision, preferred_element_type=F32)


def _dot_nt(a, b, precision=None):
    return lax.dot_general(a, b, (((1,), (1,)), ((), ())), precision=precision, preferred_element_type=F32)


def _split(x):
    hi = x.astype(BF16)
    return hi, (x - hi.astype(F32)).astype(BF16)


def _mm3(dot, a, b):
    m = a[0].shape[0]
    hi = dot(jnp.concatenate([a[0], a[1]], axis=0), b[0])
    return hi[:m] + hi[m:] + dot(a[0], b[1])


def _mm(a, b):
    return _mm3(_dot, a, b)


def _mm_nt(a, b):
    return _mm3(_dot_nt, a, b)


def _segsum(x, ones_bf):
    m = x.shape[0]
    hi, lo = _split(x)
    s = _dot(jnp.concatenate([hi, lo], axis=0), ones_bf)
    return s[:m] + s[m:]


def _shifted(a, prev_row, next_row):
    t = a.shape[0]
    rid = lax.broadcasted_iota(jnp.int32, (t, 1), 0)
    prev = jnp.where(rid == 0, prev_row, pltpu.roll(a, 1, axis=0))
    nxt = jnp.where(rid == t - 1, next_row, pltpu.roll(a, t - 1, axis=0))
    return prev, nxt


def _mod_kernel(cond_ref, w_ref, b_ref, o_ref):
    o_ref[...] = _dot(_silu(cond_ref[...]), w_ref[...], HIGHEST) + b_ref[...]


def _modulation(cond, w_ada, b_ada):
    n = 6 * D_MODEL
    return pl.pallas_call(
        _mod_kernel,
        grid=(6,),
        in_specs=[pl.BlockSpec((8, D_MODEL), lambda j: (0, 0)),
                  pl.BlockSpec((D_MODEL, D_MODEL), lambda j: (0, j)),
                  pl.BlockSpec((1, D_MODEL), lambda j: (0, j))],
        out_specs=pl.BlockSpec((8, D_MODEL), lambda j: (0, j)),
        out_shape=jax.ShapeDtypeStruct((8, n), F32),
        compiler_params=_params(1),
        name="mod",
    )(cond, w_ada, b_ada.reshape(1, n))


def _inproj_kernel(xp_ref, xs_ref, mod_ref, n1_ref, w_ref, q_ref, kc_ref, vc_ref, kl_ref, vl_ref, zr_ref):
    i = pl.program_id(0)
    x = _ctx_or_lat(i, xp_ref, xs_ref, BIG_BLOCK)
    h = _rmsnorm(x, n1_ref[...]) * (1.0 + mod_ref[0, 1:2, :]) + mod_ref[0, 0:1, :]
    h = h.astype(BF16)
    q_ref[...] = _dot(h, w_ref[:, 0:WIDTH])
    k = _dot(h, w_ref[:, WIDTH:2 * WIDTH])
    v = _dot(h, w_ref[:, 2 * WIDTH:3 * WIDTH])

    @pl.when(i < N_CTX_TOK // BIG_BLOCK)
    def _():
        kc_ref[...] = k
        vc_ref[...] = v

    @pl.when(i >= N_CTX_TOK // BIG_BLOCK)
    def _():
        kl_ref[...] = k
        vl_ref[...] = v

    zr_ref[...] = _dot(h, w_ref[:, 3 * WIDTH:])


def _inproj(xp, xs, mod, norm1, w_in_bf):
    half = jax.ShapeDtypeStruct((N_CTX_TOK, WIDTH), F32)
    blk = BIG_BLOCK
    return pl.pallas_call(
        _inproj_kernel,
        grid=(N_TOK // blk,),
        in_specs=[_ctx_tok_spec(D_MODEL, blk), _lat_tok_spec(D_MODEL, blk), _mod_spec(blk),
                  _const_spec((1, D_MODEL)),
                  _const_spec((D_MODEL, 3 * WIDTH + RWKV_COLS))],
        out_specs=[_tok_spec(WIDTH, blk), _ctx_tok_spec(WIDTH, blk), _ctx_tok_spec(WIDTH, blk),
                   _lat_tok_spec(WIDTH, blk), _lat_tok_spec(WIDTH, blk), _tok_spec(RWKV_COLS, blk)],
        out_shape=[jax.ShapeDtypeStruct((N_TOK, WIDTH), F32), half, half, half, half,
                   jax.ShapeDtypeStruct((N_TOK, RWKV_COLS), F32)],
        compiler_params=_params(1),
        name="inproj",
    )(xp, xs, mod, norm1, w_in_bf)


def _pair_queries(q):
    lo_half = lax.broadcasted_iota(jnp.int32, q.shape, 1) < HEAD_DIM
    return jnp.concatenate([jnp.where(lo_half, q, 0.0), jnp.where(lo_half, 0.0, q)], axis=0).astype(BF16)


def _pair_outputs(o):
    t = o.shape[0] // 2
    lo_half = lax.broadcasted_iota(jnp.int32, (t, PAIR_W), 1) < HEAD_DIM
    return jnp.where(lo_half, o[:t], o[t:])


def _ctx_attn_kernel(q_ref, k_ref, v_ref, o_ref):
    pairs = [slice(p * PAIR_W, (p + 1) * PAIR_W) for p in range(N_HEADS // 2)]
    scores = [_dot_nt(_pair_queries(q_ref[:, ps]), k_ref[:, ps].astype(BF16)) * ATT_SCALE for ps in pairs]
    probs = []
    for s in scores:
        e = jnp.exp(s - jnp.max(s, axis=-1, keepdims=True))
        probs.append((e.astype(BF16), jnp.sum(e, axis=-1, keepdims=True)))
    for ps, (e, l) in zip(pairs, probs):
        o_ref[:, ps] = _pair_outputs(_dot(e, v_ref[:, ps].astype(BF16)) / l).astype(BF16)


def _ctx_attention(q, k, v):
    spec = pl.BlockSpec((SEQ, WIDTH), lambda b: (b, 0))
    return pl.pallas_call(
        _ctx_attn_kernel,
        grid=(BATCH,),
        in_specs=[spec, spec, spec],
        out_specs=spec,
        out_shape=jax.ShapeDtypeStruct((N_CTX_TOK, WIDTH), BF16),
        compiler_params=_params(1),
        name="ctxattn",
    )(q, k, v)


def _lat_attn_kernel(q_ref, k_ref, v_ref, ck_ref, cv_ref, bias_ref, o_ref, kbf, vbf, ckbf, cvbf):
    i = pl.program_id(1)

    @pl.when(i == 0)
    def _():
        kbf[...] = k_ref[...].astype(BF16)
        vbf[...] = v_ref[...].astype(BF16)
        ckbf[...] = ck_ref[0].astype(BF16)
        cvbf[...] = cv_ref[0].astype(BF16)

    win = NA_ROWS * GRID_W
    first_row = jnp.clip(i - NA_ROWS // 2, 0, GRID_ROWS - NA_ROWS)
    start = pl.multiple_of(first_row * GRID_W, GRID_W)
    off0 = first_row - i + NA_ROWS - 1
    pairs = [slice(p * PAIR_W, (p + 1) * PAIR_W) for p in range(N_HEADS // 2)]
    scores = []
    for p, ps in enumerate(pairs):
        lhs = _pair_queries(q_ref[:, ps])
        bias = jnp.concatenate(
            [jnp.concatenate([bias_ref[2 * p + hh, off0 + 2 * j] for j in range(NA_ROWS // 2)], axis=-1)
             for hh in range(2)], axis=0)
        s_loc = _dot_nt(lhs, kbf[pl.ds(start, win), ps]) * ATT_SCALE + bias
        s_ctx = _dot_nt(lhs, ckbf[:, ps]) * ATT_SCALE
        scores.append((s_loc, s_ctx))
    probs = []
    for s_loc, s_ctx in scores:
        m = jnp.maximum(jnp.max(s_loc, axis=-1, keepdims=True), jnp.max(s_ctx, axis=-1, keepdims=True))
        e_loc = jnp.exp(s_loc - m)
        e_ctx = jnp.exp(s_ctx - m)
        l = jnp.sum(e_loc, axis=-1, keepdims=True) + jnp.sum(e_ctx, axis=-1, keepdims=True)
        probs.append((e_loc.astype(BF16), e_ctx.astype(BF16), l))
    for ps, (e_loc, e_ctx, l) in zip(pairs, probs):
        o = _dot(e_loc, vbf[pl.ds(start, win), ps]) + _dot(e_ctx, cvbf[:, ps])
        o_ref[:, ps] = _pair_outputs(o / l).astype(BF16)


def _na_bias_table(rpb):
    h, d, n = rpb.shape
    w = GRID_W
    lead = w - NA_COLS
    p = jnp.pad(rpb.astype(F32), ((0, 0), (0, 0), (lead, 2 * w - lead - n)), constant_values=MASK_VALUE)
    flat = jnp.broadcast_to(p[:, :, None, :], (h, d, w, 2 * w)).reshape(h, d, 2 * w * w)
    t = flat[:, :, :w * (2 * w - 1)].reshape(h, d, w, 2 * w - 1)[:, :, :, w - 1:]
    cq = jnp.arange(w)[:, None]
    ck = jnp.arange(w)[None, :]
    cs = jnp.clip(cq - NA_COLS // 2, 0, w - NA_COLS)
    t = jnp.where((ck >= cs) & (ck < cs + NA_COLS), t, MASK_VALUE)
    return jnp.concatenate([t[:, :-1], t[:, 1:]], axis=-1)


def _lat_attention(q, k, v, cache_k, cache_v, bias):
    lat0 = N_CTX_TOK // GRID_W
    seq = pl.BlockSpec((DEC_SEQ, WIDTH), lambda b, i: (b, 0))
    ctx = pl.BlockSpec((1, PAST_LEN, WIDTH), lambda b, i: (b, 0, 0))
    return pl.pallas_call(
        _lat_attn_kernel,
        grid=(DEC_BATCH, GRID_ROWS),
        in_specs=[pl.BlockSpec((GRID_W, WIDTH), lambda b, i: (lat0 + b * GRID_ROWS + i, 0)),
                  seq, seq, ctx, ctx,
                  _const_spec((N_HEADS, 2 * NA_ROWS - 2, GRID_W, PAIR_W))],
        out_specs=pl.BlockSpec((GRID_W, WIDTH), lambda b, i: (b * GRID_ROWS + i, 0)),
        out_shape=jax.ShapeDtypeStruct((N_LAT_TOK, WIDTH), BF16),
        scratch_shapes=[pltpu.VMEM((DEC_SEQ, WIDTH), BF16), pltpu.VMEM((DEC_SEQ, WIDTH), BF16),
                        pltpu.VMEM((PAST_LEN, WIDTH), BF16), pltpu.VMEM((PAST_LEN, WIDTH), BF16)],
        compiler_params=_params(2),
        name="latattn",
    )(q, k, v, cache_k, cache_v, bias)


def _rwkv_pre_kernel(z_ref, zp_ref, zn_ref, wts_ref, w0_ref, w2_ref, a0_ref, a2_ref, g2_ref,
                     kk_ref, ka_ref, rk_ref, ones_ref,
                     shared_out, g_out, bonus_out, dir_out):
    has_prev, has_next = _seq_neighbours(pl.program_id(0))

    def conv(c0, c1):
        z = z_ref[:, c0:c1]
        prow = jnp.where(has_prev, zp_ref[7:8, c0:c1], 0.0)
        nrow = jnp.where(has_next, zn_ref[0:1, c0:c1], 0.0)
        zprev, znext = _shifted(z, prow, nrow)
        return zprev * wts_ref[0:1, c0:c1] + z * wts_ref[1:2, c0:c1] + znext * wts_ref[2:3, c0:c1]

    r = conv(0, WIDTH)
    kr = conv(WIDTH, 2 * WIDTH)
    v = conv(2 * WIDTH, 3 * WIDTH)
    o = 3 * WIDTH
    tw = jnp.tanh(conv(o, o + N_DIRS * LORA))
    o += N_DIRS * LORA
    xa = conv(o, o + N_DIRS * LORA)
    o += N_DIRS * LORA
    xg = conv(o, o + GATE_LORA)
    ones = ones_ref[...]

    kkraw = kr * kk_ref[...]
    kk = kkraw * lax.rsqrt(_segsum(kkraw * kkraw, ones) + 1e-12)
    kd_sum = jnp.zeros_like(kr)
    for d in range(N_DIRS):
        ls = slice(d * LORA, (d + 1) * LORA)
        w_log = -_softplus(-(w0_ref[d:d + 1, :] + _mm(_split(tw[:, ls]), _split(w2_ref[d])))) - 0.5
        a = jax.nn.sigmoid(a0_ref[d:d + 1, :] + _mm(_split(xa[:, ls]), _split(a2_ref[d])))
        kd = kr * (1.0 + (a - 1.0) * ka_ref[...])
        dir_out[d, :, 0:WIDTH] = -jnp.exp(w_log)
        dir_out[d, :, WIDTH:2 * WIDTH] = kk * a
        dir_out[d, :, 2 * WIDTH:] = kd
        kd_sum = kd_sum + kd
    shared_out[:, 0:WIDTH] = r
    shared_out[:, WIDTH:2 * WIDTH] = kk
    shared_out[:, 2 * WIDTH:] = v
    g_out[...] = _mm(_split(jax.nn.sigmoid(xg)), _split(g2_ref[...]))
    bonus_out[...] = _segsum(r * kd_sum * rk_ref[...], ones) * v


def _rwkv_pre(zr, w_ts, w0, w2, a0, a2, g2, k_k, k_a, r_k, ones):
    tok = _tok_spec(WIDTH)
    rows8 = TOK_BLOCK // 8
    return pl.pallas_call(
        _rwkv_pre_kernel,
        grid=(N_BLOCKS,),
        in_specs=[_tok_spec(RWKV_COLS),
                  pl.BlockSpec((8, RWKV_COLS), lambda i: (jnp.maximum(i * rows8 - 1, 0), 0)),
                  pl.BlockSpec((8, RWKV_COLS), lambda i: (jnp.minimum((i + 1) * rows8, N_TOK // 8 - 1), 0)),
                  _const_spec((3, RWKV_COLS)),
                  _const_spec((N_DIRS, WIDTH)), _const_spec((N_DIRS, LORA, WIDTH)),
                  _const_spec((N_DIRS, WIDTH)), _const_spec((N_DIRS, LORA, WIDTH)),
                  _const_spec((GATE_LORA, WIDTH)),
                  _const_spec((1, WIDTH)), _const_spec((1, WIDTH)), _const_spec((1, WIDTH)),
                  _const_spec((WIDTH, WIDTH))],
        out_specs=[_tok_spec(3 * WIDTH), tok, tok,
                   pl.BlockSpec((N_DIRS, TOK_BLOCK, 3 * WIDTH), lambda i: (0, i, 0))],
        out_shape=[jax.ShapeDtypeStruct((N_TOK, 3 * WIDTH), F32),
                   jax.ShapeDtypeStruct((N_TOK, WIDTH), F32), jax.ShapeDtypeStruct((N_TOK, WIDTH), F32),
                   jax.ShapeDtypeStruct((N_DIRS, N_TOK, 3 * WIDTH), F32)],
        compiler_params=_params(1),
        name="rwkvpre",
    )(zr, zr, zr, w_ts, w0, w2, a0, a2, g2, k_k, k_a, r_k, ones)


HALF = CHUNK // 2


def _blockdiag(x, block=HEAD_DIM):
    def one(p):
        blk = lax.broadcasted_iota(jnp.int32, p.shape, 1) // block
        return jnp.concatenate([jnp.where(blk == q, p, jnp.zeros_like(p)) for q in range(GROUP_W // block)], axis=0)
    return one(x[0]), one(x[1])


def _head_transpose(x):
    xt = jnp.transpose(x)
    return jnp.concatenate([xt[h * HEAD_DIM:(h + 1) * HEAD_DIM] for h in range(GROUP)], axis=1)


def _scan_prepare(d, shared_ref, dir_ref, y_ref, s_ref):
    row = lax.broadcasted_iota(jnp.int32, (CHUNK, GROUP_W), 0)
    lane = lax.broadcasted_iota(jnp.int32, (CHUNK, GROUP_W), 1)
    col = lane % CHUNK
    strict = col > row if d else col < row
    incl = col >= row if d else col <= row
    row_h = lax.broadcasted_iota(jnp.int32, (HALF, GROUP_W), 0)
    lane_h = lax.broadcasted_iota(jnp.int32, (HALF, GROUP_W), 1)
    lo_s = lane_h % CHUNK < HALF
    eye_half = (lane_h % HALF == row_h).astype(F32)

    lw = dir_ref[0, :, 0:WIDTH]
    mask = jnp.where(incl[:, :CHUNK], 1.0, 0.0).astype(BF16)
    lw_hi, lw_mid = _split(lw)
    lw_lo = (lw - lw_hi.astype(F32) - lw_mid.astype(F32)).astype(BF16)
    cum = _dot(mask, lw_hi) + _dot(mask, lw_mid) + _dot(mask, lw_lo)
    tot = jnp.sum(lw, axis=0, keepdims=True)
    p_in = jnp.exp(cum)
    p_neg = jnp.exp(-cum)
    p_rem = jnp.exp(tot - cum)
    kt = shared_ref[:, WIDTH:2 * WIDTH] * jnp.exp(cum - lw)
    rt = shared_ref[:, 0:WIDTH] * p_in
    b = dir_ref[0, :, WIDTH:2 * WIDTH]
    kd = dir_ref[0, :, 2 * WIDTH:]
    bt = b * p_neg
    kdt = kd * p_neg
    bh = b * p_rem
    kh = kd * p_rem
    p_end = jnp.exp(tot)
    v = shared_ref[:, 2 * WIDTH:]

    chains = []
    for gi in range(N_HEADS // GROUP):
        sl = slice(gi * GROUP_W, (gi + 1) * GROUP_W)
        chains.append(dict(d=d, sl=sl, strict=strict, incl=incl, lo_s=lo_s, eye_half=eye_half, y_ref=y_ref, s_ref=s_ref,
                           kt=kt[:, sl], rt=rt[:, sl], v=v[:, sl], bt=bt[:, sl], kdt=kdt[:, sl],
                           kh=kh[:, sl], bh=bh[:, sl], p_end=p_end[:, sl]))
    return chains


def _scan_chains(chains):
    zeros = jnp.zeros((HALF, GROUP_W), F32)
    for c in chains:
        lhs = _split(jnp.concatenate([c["kt"], c["rt"]], axis=0))
        g_b = _mm_nt(lhs, _blockdiag(_split(c["bt"])))
        g_k = _mm_nt(lhs, _blockdiag(_split(c["kdt"])))
        c["a_k"] = jnp.where(c["strict"], g_k[:CHUNK], 0.0)
        c["a_rb"] = jnp.where(c["incl"], g_b[CHUNK:], 0.0)
        c["a_rk"] = jnp.where(c["incl"], g_k[CHUNK:], 0.0)
        n = jnp.where(c["strict"], -g_b[:CHUNK], 0.0)
        c["n"] = n
        c["pw"] = jnp.where(c["lo_s"], n[:HALF], n[HALF:])
        c["tp"] = c["eye_half"] + c["pw"]
    for c in chains:
        c["pw"] = _mm(_split(c["pw"]), _blockdiag(_split(c["pw"]), HALF))
    for _ in range(3):
        for c in chains:
            both = _mm(_split(jnp.concatenate([c["tp"], c["pw"]], axis=0)), _blockdiag(_split(c["pw"]), HALF))
            c["tp"] = c["tp"] + both[:HALF]
            c["pw"] = both[HALF:]
    for c in chains:
        c["tp"] = c["tp"] + _mm(_split(c["tp"]), _blockdiag(_split(c["pw"]), HALF))
    for c in chains:
        c["t1"] = jnp.where(c["lo_s"], c["tp"], 0.0)
        c["t2"] = jnp.where(c["lo_s"], 0.0, c["tp"])
        if c["d"] == 0:
            off, inner = jnp.where(c["lo_s"], c["n"][HALF:], 0.0), jnp.concatenate([c["t1"], zeros], axis=0)
        else:
            off, inner = jnp.where(c["lo_s"], 0.0, c["n"][:HALF]), jnp.concatenate([zeros, c["t2"]], axis=0)
        c["off"] = _mm(_split(off), _blockdiag(_split(inner)))
    for c in chains:
        if c["d"] == 0:
            x = _mm(_split(c["t2"]), _blockdiag(_split(jnp.concatenate([zeros, c["off"]], axis=0))))
            c["inv"] = jnp.concatenate([c["t1"], jnp.where(c["lo_s"], x, c["tp"])], axis=0)
        else:
            x = _mm(_split(c["t1"]), _blockdiag(_split(jnp.concatenate([c["off"], zeros], axis=0))))
            c["inv"] = jnp.concatenate([jnp.where(c["lo_s"], c["tp"], x), c["t2"]], axis=0)
    for c in chains:
        c["av"] = _mm(_split(jnp.concatenate([c["a_k"], c["a_rk"]], axis=0)), _blockdiag(_split(c["v"])))
        c["vt"] = _head_transpose(c["v"])
    for c in chains:
        c["s"] = c["s_ref"][0, :, c["sl"]]
        c["z"] = _mm_nt(_split(jnp.concatenate([c["rt"], c["kt"]], axis=0)), _blockdiag(_split(c["s"])))
    for c in chains:
        c["u"] = _mm(_split(c["inv"]), _blockdiag(_split(c["z"][CHUNK:] + c["av"][:CHUNK])))
    for c in chains:
        c["y_ref"][:, c["sl"]] = (c["z"][:CHUNK] + c["av"][CHUNK:]
                                  - _mm(_split(c["a_rb"]), _blockdiag(_split(c["u"]))))
    for c in chains:
        upd = (_mm(_split(c["vt"]), _blockdiag(_split(c["kh"])))
               - _mm(_split(_head_transpose(c["u"])), _blockdiag(_split(c["bh"]))))
        c["s_ref"][0, :, c["sl"]] = c["s"] * c["p_end"] + upd


REGION_CHUNKS = CTX_CHUNKS
SCAN_STREAMS = tuple((d, base, per_seq) for d in range(N_DIRS)
                     for base, per_seq in ((0, CHUNKS_PER_CTX_SEQ), (CTX_CHUNKS, CHUNKS_PER_LAT_SEQ)))
STREAM_INS = 3


def _stream_local_chunk(stream, j):
    return REGION_CHUNKS - 1 - j if stream[0] else j


def _scan_kernel(*refs):
    n_in = STREAM_INS * len(SCAN_STREAMS)
    j = pl.program_id(0)
    per_stream = []
    for s, stream in enumerate(SCAN_STREAMS):
        d, _, per_seq = stream
        ins = refs[STREAM_INS * s:STREAM_INS * (s + 1)]
        y_ref, s_ref = refs[n_in + 2 * s:n_in + 2 * s + 2]
        s0_ref = ins[-1]
        local = _stream_local_chunk(stream, j) % per_seq

        @pl.when(local == (per_seq - 1 if d else 0))
        def _():
            s_ref[...] = s0_ref[0]

        per_stream.append((d,) + ins[:-1] + (y_ref, s_ref))

    chains = []
    for args in per_stream:
        chains += _scan_prepare(*args)
    _scan_chains(chains)


def _scan(shared, perdir, s0):
    in_specs, out_specs, out_shape, operands = [], [], [], []
    for stream in SCAN_STREAMS:
        d, base, per_seq = stream
        seq0 = 0 if base == 0 else BATCH
        n_seq = REGION_CHUNKS // per_seq
        loc = lambda j, stream=stream: _stream_local_chunk(stream, j)
        tok = pl.BlockSpec((CHUNK, 3 * WIDTH), lambda j, loc=loc, base=base: (base + loc(j), 0))
        dtok = pl.BlockSpec((1, CHUNK, 3 * WIDTH), lambda j, loc=loc, base=base, d=d: (d, base + loc(j), 0))
        st_in = pl.BlockSpec((1, 1, HEAD_DIM, WIDTH),
                             lambda j, loc=loc, seq0=seq0, per_seq=per_seq, d=d: (seq0 + loc(j) // per_seq, d, 0, 0))
        in_specs += [tok, dtok, st_in]
        operands += [shared, perdir, s0]
        out_specs += [pl.BlockSpec((CHUNK, WIDTH), lambda j, loc=loc: (loc(j), 0)),
                      pl.BlockSpec((1, HEAD_DIM, WIDTH), lambda j, loc=loc, per_seq=per_seq: (loc(j) // per_seq, 0, 0))]
        out_shape += [jax.ShapeDtypeStruct((REGION_CHUNKS * CHUNK, WIDTH), F32),
                      jax.ShapeDtypeStruct((n_seq, HEAD_DIM, WIDTH), F32)]
    return pl.pallas_call(
        _scan_kernel,
        grid=(REGION_CHUNKS,),
        in_specs=in_specs,
        out_specs=out_specs,
        out_shape=out_shape,
        compiler_params=_params(1),
        name="scan",
    )(*operands)


def _post_kernel(yfc_ref, yfl_ref, ybc_ref, ybl_ref, bonus_ref, g_ref, attc_ref, attl_ref, xp_ref, xs_ref, mod_ref,
                 lng_ref, lnb_ref, ones_ref, wout_ref, n2_ref, x1_ref, h2_ref):
    i = pl.program_id(0)
    ones = ones_ref[...]
    inv_n = 1.0 / HEAD_DIM
    y = _ctx_or_lat(i, yfc_ref, yfl_ref) + _ctx_or_lat(i, ybc_ref, ybl_ref)
    yc = y - _segsum(y, ones) * inv_n
    var = _segsum(yc * yc, ones) * inv_n
    yn = yc * lax.rsqrt(var + GN_EPS) * lng_ref[...] + lnb_ref[...]
    r_out = ((yn + bonus_ref[...]) * g_ref[...]).astype(BF16)
    att = _ctx_or_lat(i, attc_ref, attl_ref)
    o = _dot(att, wout_ref[0:WIDTH, :]) + _dot(r_out, wout_ref[WIDTH:, :])
    x1 = _ctx_or_lat(i, xp_ref, xs_ref) + mod_ref[0, 2:3, :] * o
    x1_ref[...] = x1
    h2 = _rmsnorm(x1, n2_ref[...]) * (1.0 + mod_ref[0, 4:5, :]) + mod_ref[0, 3:4, :]
    h2_ref[...] = h2.astype(BF16)


def _post(ys, bonus, g, att_ctx, att_lat, xp, xs, mod, ln_g, ln_b, ones, w_out_bf, norm2):
    tok = _tok_spec
    ctx_lat = [_ctx_tok_spec(WIDTH), _lat_tok_spec(WIDTH)]
    return pl.pallas_call(
        _post_kernel,
        grid=(N_BLOCKS,),
        in_specs=ctx_lat + ctx_lat + [tok(WIDTH), tok(WIDTH)]
        + ctx_lat + [_ctx_tok_spec(D_MODEL), _lat_tok_spec(D_MODEL),
                  _mod_spec(),
                  _const_spec((1, WIDTH)), _const_spec((1, WIDTH)), _const_spec((WIDTH, WIDTH)),
                  _const_spec((2 * WIDTH, D_MODEL)), _const_spec((1, D_MODEL))],
        out_specs=[tok(D_MODEL), tok(D_MODEL)],
        out_shape=[jax.ShapeDtypeStruct((N_TOK, D_MODEL), F32),
                   jax.ShapeDtypeStruct((N_TOK, D_MODEL), BF16)],
        compiler_params=_params(1),
        name="post",
    )(*ys, bonus, g, att_ctx, att_lat, xp, xs, mod, ln_g, ln_b, ones, w_out_bf, norm2)


def _ffn_kernel(x1_ref, h2_ref, hp_ref, hn_ref, mod_ref, w1_ref, w3_ref, wc_ref, w2_ref, nf_ref, yp_ref, ys_ref):
    i = pl.program_id(0)
    blk = BIG_BLOCK
    is_ctx = i < N_CTX_TOK // blk
    has_prev, has_next = _seq_neighbours(i, blk)
    lhs = jnp.concatenate([h2_ref[...], hp_ref[...], hn_ref[...]], axis=0)
    h2 = lhs[:blk]
    rid = lax.broadcasted_iota(jnp.int32, (blk, 1), 0)
    inner = jnp.where(is_ctx, SEQ, -1)
    seq_start = rid == inner
    seq_end = rid == inner - 1
    acc = jnp.zeros((blk, D_MODEL), F32)
    for f in range(D_FF // FF_TILE):
        fs = slice(f * FF_TILE, (f + 1) * FF_TILE)
        a_all = _dot(lhs, w1_ref[:, fs])
        a = a_all[:blk]
        prow = jnp.where(has_prev, a_all[blk + 15:blk + 16, :], 0.0)
        nrow = jnp.where(has_next, a_all[blk + 16:blk + 17, :], 0.0)
        aprev, anext = _shifted(a, prow, nrow)
        aprev = jnp.where(seq_start, 0.0, aprev)
        anext = jnp.where(seq_end, 0.0, anext)
        cv = aprev * wc_ref[0:1, fs] + a * wc_ref[1:2, fs] + anext * wc_ref[2:3, fs]
        act = _silu(cv) * _dot(h2, w3_ref[:, fs])
        acc = acc + _dot(act.astype(BF16), w2_ref[fs, :])
    x2 = x1_ref[...] + mod_ref[0, 5:6, :] * acc
    y = _rmsnorm(x2, nf_ref[...])

    @pl.when(is_ctx)
    def _():
        yp_ref[...] = y

    @pl.when(jnp.logical_not(is_ctx))
    def _():
        ys_ref[...] = y


def _ffn(x1, h2, mod, w1_bf, w3_bf, wc, w2_bf, norm_f):
    blk = BIG_BLOCK
    rows16 = blk // 16
    return pl.pallas_call(
        _ffn_kernel,
        grid=(N_TOK // blk,),
        in_specs=[_tok_spec(D_MODEL, blk), _tok_spec(D_MODEL, blk),
                  pl.BlockSpec((16, D_MODEL), lambda i: (jnp.maximum(i * rows16 - 1, 0), 0)),
                  pl.BlockSpec((16, D_MODEL), lambda i: (jnp.minimum((i + 1) * rows16, N_TOK // 16 - 1), 0)),
                  _mod_spec(blk),
                  _const_spec((D_MODEL, D_FF)), _const_spec((D_MODEL, D_FF)), _const_spec((3, D_FF)),
                  _const_spec((D_FF, D_MODEL)), _const_spec((1, D_MODEL))],
        out_specs=[_ctx_tok_spec(D_MODEL, blk), _lat_tok_spec(D_MODEL, blk)],
        out_shape=[jax.ShapeDtypeStruct((N_CTX_TOK, D_MODEL), F32),
                   jax.ShapeDtypeStruct((N_LAT_TOK, D_MODEL), F32)],
        compiler_params=_params(1),
        name="ffn",
    )(x1, h2, h2, h2, mod, w1_bf, w3_bf, wc, w2_bf, norm_f)


def kernel(x_prompt, x_sample, cache_k, cache_v, state_rwkv, c, c_ctx, w_ada, b_ada, norm1, norm2,
           w_in, w_ts, w0, w2, a0, a2, g2, k_k, k_a, r_k, ln_x_g, ln_x_b, rpb, w_out,
           w_ffn1, w_ffn3, w_ffn_conv, w_ffn2, norm_f):
    xp = x_prompt.reshape(N_CTX_TOK, D_MODEL)
    xs = x_sample.reshape(N_LAT_TOK, D_MODEL)
    row = lambda t: t.reshape(1, -1)
    cond = jnp.concatenate([c_ctx[None, :], c, jnp.zeros((8 - 1 - DEC_BATCH, D_MODEL), F32)], axis=0)
    mod = _modulation(cond, w_ada[0], b_ada[0]).reshape(8, 6, D_MODEL)

    q, k_ctx, v_ctx, k_lat, v_lat, zr = _inproj(xp, xs, mod, row(norm1[0]), w_in[0].astype(BF16))

    att_ctx = _ctx_attention(q, k_ctx, v_ctx)
    att_lat = _lat_attention(q, k_lat, v_lat, cache_k[:, 0].reshape(DEC_BATCH, PAST_LEN, WIDTH),
                             cache_v[:, 0].reshape(DEC_BATCH, PAST_LEN, WIDTH), _na_bias_table(rpb[0]))

    head_id = jnp.arange(WIDTH) // HEAD_DIM
    ones = (head_id[:, None] == head_id[None, :]).astype(BF16)
    shared, g, bonus, perdir = _rwkv_pre(zr, w_ts[0], w0[0], w2[0], a0[0], a2[0], g2[0],
                                         row(k_k[0]), row(k_a[0]), row(r_k[0]), ones)
    s_lat = jnp.transpose(state_rwkv[:, 0], (0, 1, 3, 2, 4)).reshape(DEC_BATCH, N_DIRS, HEAD_DIM, WIDTH)
    s0 = jnp.concatenate([jnp.zeros((BATCH, N_DIRS, HEAD_DIM, WIDTH), F32), s_lat], axis=0)
    y_fc, s_fc, y_fl, _, y_bc, s_bc, y_bl, _ = _scan(shared, perdir, s0)
    s_fin = jnp.stack([s_fc, s_bc], axis=1)

    x1, h2 = _post((y_fc, y_fl, y_bc, y_bl), bonus, g, att_ctx, att_lat, xp, xs, mod, row(ln_x_g[0]), row(ln_x_b[0]), ones,
                   w_out[0].astype(BF16), row(norm2[0]))
    yp, ys = _ffn(x1, h2, mod, w_ffn1[0].astype(BF16), w_ffn3[0].astype(BF16), w_ffn_conv[0],
                  w_ffn2[0].astype(BF16), row(norm_f))

    new_k = k_ctx.reshape(BATCH, 1, SEQ, N_HEADS, HEAD_DIM)
    new_v = v_ctx.reshape(BATCH, 1, SEQ, N_HEADS, HEAD_DIM)
    new_s = jnp.transpose(s_fin.reshape(BATCH, N_DIRS, HEAD_DIM, N_HEADS, HEAD_DIM),
                          (0, 1, 3, 2, 4)).reshape(BATCH, 1, N_DIRS, N_HEADS, HEAD_DIM, HEAD_DIM)
    return (yp.reshape(BATCH, SEQ, D_MODEL), ys.reshape(DEC_BATCH, DEC_SEQ, D_MODEL), new_k, new_v, new_s)
```

```python
import jax
import jax.numpy as jnp
from jax import lax
from jax.experimental import pallas as pl
from jax.experimental.pallas import tpu as pltpu

F32 = jnp.float32
BF16 = jnp.bfloat16
HIGHEST = lax.Precision.HIGHEST

D_MODEL = 1024
BATCH = 16
SEQ = 256
DEC_BATCH = 2
DEC_SEQ = 2048
PAST_LEN = 512
GRID_W = 64
HEAD_DIM = 64
N_HEADS = 8
WIDTH = N_HEADS * HEAD_DIM
PAIR_W = 2 * HEAD_DIM
NA_ROWS = 8
NA_COLS = 16
N_DIRS = 2
LORA = 64
GATE_LORA = 128
D_FF = 2816
EPS = 1e-6
GN_EPS = 64e-5
ATT_SCALE = HEAD_DIM ** -0.5
RWKV_COLS = 3 * WIDTH + N_DIRS * 2 * LORA + GATE_LORA
MASK_VALUE = -1e30

N_CTX_TOK = BATCH * SEQ
N_LAT_TOK = DEC_BATCH * DEC_SEQ
N_TOK = N_CTX_TOK + N_LAT_TOK
TOK_BLOCK = 256
N_BLOCKS = N_TOK // TOK_BLOCK
BIG_BLOCK = 512
N_SEQS = BATCH + DEC_BATCH
CHUNK = 64
N_CHUNKS = N_TOK // CHUNK
CTX_CHUNKS = N_CTX_TOK // CHUNK
CHUNKS_PER_CTX_SEQ = SEQ // CHUNK
CHUNKS_PER_LAT_SEQ = DEC_SEQ // CHUNK
GRID_ROWS = DEC_SEQ // GRID_W
FF_TILE = D_FF // 2
GROUP = 4
GROUP_W = GROUP * HEAD_DIM
VMEM_LIMIT = 56 * 1024 * 1024


def _params(n_axes, limit=VMEM_LIMIT):
    return pltpu.CompilerParams(dimension_semantics=("arbitrary",) * n_axes, vmem_limit_bytes=limit)


def _const_spec(shape):
    zeros = (0,) * len(shape)
    return pl.BlockSpec(shape, lambda *_: zeros, pipeline_mode=pl.Buffered(1))


def _tok_spec(cols, blk=TOK_BLOCK):
    return pl.BlockSpec((blk, cols), lambda i: (i, 0))


def _ctx_tok_spec(cols, blk=TOK_BLOCK):
    return pl.BlockSpec((blk, cols), lambda i: (jnp.minimum(i, N_CTX_TOK // blk - 1), 0))


def _lat_tok_spec(cols, blk=TOK_BLOCK):
    return pl.BlockSpec((blk, cols), lambda i: (jnp.maximum(i - N_CTX_TOK // blk, 0), 0))


def _ctx_or_lat(i, ctx_ref, lat_ref, blk=TOK_BLOCK):
    return jnp.where(i < N_CTX_TOK // blk, ctx_ref[...], lat_ref[...])


def _mod_spec(blk=TOK_BLOCK):
    def row(i):
        return jnp.where(i < N_CTX_TOK // blk, 0, 1 + (i - N_CTX_TOK // blk) // (DEC_SEQ // blk))
    return pl.BlockSpec((1, 6, D_MODEL), lambda i: (row(i), 0, 0))


def _seq_neighbours(i, blk=TOK_BLOCK):
    per_seq = DEC_SEQ // blk
    j = (i - N_CTX_TOK // blk) % per_seq
    lat = i >= N_CTX_TOK // blk
    return lat & (j != 0), lat & (j != per_seq - 1)


def _silu(x):
    return x * jax.nn.sigmoid(x)


def _softplus(x):
    return jnp.maximum(x, 0.0) + jnp.log1p(jnp.exp(-jnp.abs(x)))


def _rmsnorm(x, g):
    return x * lax.rsqrt(jnp.mean(x * x, axis=-1, keepdims=True) + EPS) * g


def _dot(a, b, precision=None):
    return jnp.dot(a, b, precision=precision, preferred_element_type=F32)


def _dot_nt(a, b, precision=None):
    return lax.dot_general(a, b, (((1,), (1,)), ((), ())), precision=precision, preferred_element_type=F32)


def _split(x):
    hi = x.astype(BF16)
    return hi, (x - hi.astype(F32)).astype(BF16)


def _mm3(dot, a, b):
    m = a[0].shape[0]
    hi = dot(jnp.concatenate([a[0], a[1]], axis=0), b[0])
    return hi[:m] + hi[m:] + dot(a[0], b[1])


def _mm(a, b):
    return _mm3(_dot, a, b)


def _mm_nt(a, b):
    return _mm3(_dot_nt, a, b)


def _segsum(x, ones_bf):
    m = x.shape[0]
    hi, lo = _split(x)
    s = _dot(jnp.concatenate([hi, lo], axis=0), ones_bf)
    return s[:m] + s[m:]


def _shifted(a, prev_row, next_row):
    t = a.shape[0]
    rid = lax.broadcasted_iota(jnp.int32, (t, 1), 0)
    prev = jnp.where(rid == 0, prev_row, pltpu.roll(a, 1, axis=0))
    nxt = jnp.where(rid == t - 1, next_row, pltpu.roll(a, t - 1, axis=0))
    return prev, nxt


def _mod_kernel(cond_ref, w_ref, b_ref, o_ref):
    o_ref[...] = _dot(_silu(cond_ref[...]), w_ref[...], HIGHEST) + b_ref[...]


def _modulation(cond, w_ada, b_ada):
    n = 6 * D_MODEL
    return pl.pallas_call(
        _mod_kernel,
        grid=(6,),
        in_specs=[pl.BlockSpec((8, D_MODEL), lambda j: (0, 0)),
                  pl.BlockSpec((D_MODEL, D_MODEL), lambda j: (0, j)),
                  pl.BlockSpec((1, D_MODEL), lambda j: (0, j))],
        out_specs=pl.BlockSpec((8, D_MODEL), lambda j: (0, j)),
        out_shape=jax.ShapeDtypeStruct((8, n), F32),
        compiler_params=_params(1),
        name="mod",
    )(cond, w_ada, b_ada.reshape(1, n))


def _inproj_kernel(xp_ref, xs_ref, xprev_ref, xnext_ref, mod_ref, n1_ref, w_ref, *rest):
    rwkv_refs, (q_ref, kc_ref, vc_ref, kl_ref, vl_ref), rwkv_outs = rest[:10], rest[10:15], rest[15:]
    i = pl.program_id(0)
    has_prev, has_next = _seq_neighbours(i)

    def modulated(x):
        h = _rmsnorm(x, n1_ref[...]) * (1.0 + mod_ref[0, 1:2, :]) + mod_ref[0, 0:1, :]
        return h.astype(BF16)

    h = modulated(_ctx_or_lat(i, xp_ref, xs_ref))
    edge = modulated(jnp.concatenate([xprev_ref[...], xnext_ref[...]], axis=0))
    lhs = jnp.concatenate([h, edge], axis=0)

    def zcols(c0, c1):
        z_all = _dot(lhs, w_ref[:, 3 * WIDTH + c0:3 * WIDTH + c1])
        return (z_all[:TOK_BLOCK],
                jnp.where(has_prev, z_all[TOK_BLOCK + 7:TOK_BLOCK + 8], 0.0),
                jnp.where(has_next, z_all[TOK_BLOCK + 8:TOK_BLOCK + 9], 0.0))

    kv = []

    def project_q():
        q_ref[...] = _dot(h, w_ref[:, 0:WIDTH])

    def project_k():
        kv.append(_dot(h, w_ref[:, WIDTH:2 * WIDTH]))

    def project_v():
        kv.append(_dot(h, w_ref[:, 2 * WIDTH:3 * WIDTH]))

    _rwkv_tokens(zcols, (project_q, project_k, project_v), *rwkv_refs, *rwkv_outs)
    k, v = kv

    @pl.when(i < N_CTX_TOK // TOK_BLOCK)
    def _():
        kc_ref[...] = k
        vc_ref[...] = v

    @pl.when(i >= N_CTX_TOK // TOK_BLOCK)
    def _():
        kl_ref[...] = k
        vl_ref[...] = v


def _inproj(xp, xs, mod, norm1, w_in_bf, w_ts, w0, w2, a0, a2, g2, k_k, k_a, r_k, ones):
    half = jax.ShapeDtypeStruct((N_CTX_TOK, WIDTH), F32)
    tok = jax.ShapeDtypeStruct((N_TOK, WIDTH), F32)
    lat0 = N_CTX_TOK // TOK_BLOCK
    rows8 = TOK_BLOCK // 8
    n8 = N_LAT_TOK // 8
    prev8 = pl.BlockSpec((8, D_MODEL), lambda i: (jnp.clip((i - lat0) * rows8 - 1, 0, n8 - 1), 0))
    next8 = pl.BlockSpec((8, D_MODEL), lambda i: (jnp.clip((i - lat0 + 1) * rows8, 0, n8 - 1), 0))
    return pl.pallas_call(
        _inproj_kernel,
        grid=(N_BLOCKS,),
        in_specs=[_ctx_tok_spec(D_MODEL), _lat_tok_spec(D_MODEL), prev8, next8, _mod_spec(),
                  _const_spec((1, D_MODEL)),
                  _const_spec((D_MODEL, 3 * WIDTH + RWKV_COLS)),
                  _const_spec((3, RWKV_COLS)),
                  _const_spec((N_DIRS, WIDTH)), _const_spec((N_DIRS, LORA, WIDTH)),
                  _const_spec((N_DIRS, WIDTH)), _const_spec((N_DIRS, LORA, WIDTH)),
                  _const_spec((GATE_LORA, WIDTH)),
                  _const_spec((1, WIDTH)), _const_spec((1, WIDTH)), _const_spec((1, WIDTH)),
                  _const_spec((WIDTH, WIDTH))],
        out_specs=[_tok_spec(WIDTH), _ctx_tok_spec(WIDTH), _ctx_tok_spec(WIDTH),
                   _lat_tok_spec(WIDTH), _lat_tok_spec(WIDTH),
                   _tok_spec(3 * WIDTH), _tok_spec(WIDTH), _tok_spec(WIDTH),
                   pl.BlockSpec((N_DIRS, TOK_BLOCK, 3 * WIDTH), lambda i: (0, i, 0))],
        out_shape=[tok, half, half, half, half,
                   jax.ShapeDtypeStruct((N_TOK, 3 * WIDTH), F32), tok, tok,
                   jax.ShapeDtypeStruct((N_DIRS, N_TOK, 3 * WIDTH), F32)],
        compiler_params=_params(1),
        name="inproj",
    )(xp, xs, xs, xs, mod, norm1, w_in_bf, w_ts, w0, w2, a0, a2, g2, k_k, k_a, r_k, ones)


def _pair_queries(q):
    lo_half = lax.broadcasted_iota(jnp.int32, q.shape, 1) < HEAD_DIM
    return jnp.concatenate([jnp.where(lo_half, q, 0.0), jnp.where(lo_half, 0.0, q)], axis=0).astype(BF16)


def _pair_outputs(o):
    t = o.shape[0] // 2
    lo_half = lax.broadcasted_iota(jnp.int32, (t, PAIR_W), 1) < HEAD_DIM
    return jnp.where(lo_half, o[:t], o[t:])


def _ctx_attn_kernel(q_ref, k_ref, v_ref, o_ref):
    pairs = [slice(p * PAIR_W, (p + 1) * PAIR_W) for p in range(N_HEADS // 2)]
    scores = [_dot_nt(_pair_queries(q_ref[:, ps]), k_ref[:, ps].astype(BF16)) * ATT_SCALE for ps in pairs]
    probs = []
    for s in scores:
        e = jnp.exp(s - jnp.max(s, axis=-1, keepdims=True))
        probs.append((e.astype(BF16), jnp.sum(e, axis=-1, keepdims=True)))
    for ps, (e, l) in zip(pairs, probs):
        o_ref[:, ps] = _pair_outputs(_dot(e, v_ref[:, ps].astype(BF16)) / l).astype(BF16)


def _ctx_attention(q, k, v):
    spec = pl.BlockSpec((SEQ, WIDTH), lambda b: (b, 0))
    return pl.pallas_call(
        _ctx_attn_kernel,
        grid=(BATCH,),
        in_specs=[spec, spec, spec],
        out_specs=spec,
        out_shape=jax.ShapeDtypeStruct((N_CTX_TOK, WIDTH), BF16),
        compiler_params=_params(1),
        name="ctxattn",
    )(q, k, v)


def _lat_attn_kernel(q_ref, k_ref, v_ref, ck_ref, cv_ref, bias_ref, o_ref, kbf, vbf, ckbf, cvbf):
    i = pl.program_id(1)

    @pl.when(i == 0)
    def _():
        kbf[...] = k_ref[...].astype(BF16)
        vbf[...] = v_ref[...].astype(BF16)
        ckbf[...] = ck_ref[0].astype(BF16)
        cvbf[...] = cv_ref[0].astype(BF16)

    win = NA_ROWS * GRID_W
    first_row = jnp.clip(i - NA_ROWS // 2, 0, GRID_ROWS - NA_ROWS)
    start = pl.multiple_of(first_row * GRID_W, GRID_W)
    off0 = first_row - i + NA_ROWS - 1
    pairs = [slice(p * PAIR_W, (p + 1) * PAIR_W) for p in range(N_HEADS // 2)]
    scores = []
    for p, ps in enumerate(pairs):
        lhs = _pair_queries(q_ref[:, ps])
        bias = jnp.concatenate(
            [jnp.concatenate([bias_ref[2 * p + hh, off0 + 2 * j] for j in range(NA_ROWS // 2)], axis=-1)
             for hh in range(2)], axis=0)
        s_loc = _dot_nt(lhs, kbf[pl.ds(start, win), ps]) * ATT_SCALE + bias
        s_ctx = _dot_nt(lhs, ckbf[:, ps]) * ATT_SCALE
        scores.append((s_loc, s_ctx))
    probs = []
    for s_loc, s_ctx in scores:
        m = jnp.maximum(jnp.max(s_loc, axis=-1, keepdims=True), jnp.max(s_ctx, axis=-1, keepdims=True))
        e_loc = jnp.exp(s_loc - m)
        e_ctx = jnp.exp(s_ctx - m)
        l = jnp.sum(e_loc, axis=-1, keepdims=True) + jnp.sum(e_ctx, axis=-1, keepdims=True)
        probs.append((e_loc.astype(BF16), e_ctx.astype(BF16), l))
    for ps, (e_loc, e_ctx, l) in zip(pairs, probs):
        o = _dot(e_loc, vbf[pl.ds(start, win), ps]) + _dot(e_ctx, cvbf[:, ps])
        o_ref[:, ps] = _pair_outputs(o / l).astype(BF16)


def _na_bias_table(rpb):
    h, d, n = rpb.shape
    w = GRID_W
    lead = w - NA_COLS
    p = jnp.pad(rpb.astype(F32), ((0, 0), (0, 0), (lead, 2 * w - lead - n)), constant_values=MASK_VALUE)
    flat = jnp.broadcast_to(p[:, :, None, :], (h, d, w, 2 * w)).reshape(h, d, 2 * w * w)
    t = flat[:, :, :w * (2 * w - 1)].reshape(h, d, w, 2 * w - 1)[:, :, :, w - 1:]
    cq = jnp.arange(w)[:, None]
    ck = jnp.arange(w)[None, :]
    cs = jnp.clip(cq - NA_COLS // 2, 0, w - NA_COLS)
    t = jnp.where((ck >= cs) & (ck < cs + NA_COLS), t, MASK_VALUE)
    return jnp.concatenate([t[:, :-1], t[:, 1:]], axis=-1)


def _lat_attention(q, k, v, cache_k, cache_v, bias):
    lat0 = N_CTX_TOK // GRID_W
    seq = pl.BlockSpec((DEC_SEQ, WIDTH), lambda b, i: (b, 0))
    ctx = pl.BlockSpec((1, PAST_LEN, WIDTH), lambda b, i: (b, 0, 0))
    return pl.pallas_call(
        _lat_attn_kernel,
        grid=(DEC_BATCH, GRID_ROWS),
        in_specs=[pl.BlockSpec((GRID_W, WIDTH), lambda b, i: (lat0 + b * GRID_ROWS + i, 0)),
                  seq, seq, ctx, ctx,
                  _const_spec((N_HEADS, 2 * NA_ROWS - 2, GRID_W, PAIR_W))],
        out_specs=pl.BlockSpec((GRID_W, WIDTH), lambda b, i: (b * GRID_ROWS + i, 0)),
        out_shape=jax.ShapeDtypeStruct((N_LAT_TOK, WIDTH), BF16),
        scratch_shapes=[pltpu.VMEM((DEC_SEQ, WIDTH), BF16), pltpu.VMEM((DEC_SEQ, WIDTH), BF16),
                        pltpu.VMEM((PAST_LEN, WIDTH), BF16), pltpu.VMEM((PAST_LEN, WIDTH), BF16)],
        compiler_params=_params(2),
        name="latattn",
    )(q, k, v, cache_k, cache_v, bias)


def _rwkv_tokens(zcols, fillers, wts_ref, w0_ref, w2_ref, a0_ref, a2_ref, g2_ref, kk_ref, ka_ref, rk_ref, ones_ref,
                 shared_out, g_out, bonus_out, dir_out):
    def conv(c0, c1):
        zc, prow, nrow = zcols(c0, c1)
        zprev, znext = _shifted(zc, prow, nrow)
        return zprev * wts_ref[0:1, c0:c1] + zc * wts_ref[1:2, c0:c1] + znext * wts_ref[2:3, c0:c1]

    fill = iter(fillers)
    dirs = [slice(d * LORA, (d + 1) * LORA) for d in range(N_DIRS)]
    ones = ones_ref[...]
    o = 3 * WIDTH
    lora_in = conv(o, RWKV_COLS)
    kr = conv(WIDTH, 2 * WIDTH)
    tw = jnp.tanh(lora_in[:, 0:N_DIRS * LORA])
    xa = lora_in[:, N_DIRS * LORA:2 * N_DIRS * LORA]
    xg = lora_in[:, 2 * N_DIRS * LORA:]
    kkraw = kr * kk_ref[...]
    next(fill)()
    kk = kkraw * lax.rsqrt(_segsum(kkraw * kkraw, ones) + 1e-12)
    lora_w = [_mm(_split(tw[:, ls]), _split(w2_ref[d])) for d, ls in enumerate(dirs)]
    r = conv(0, WIDTH)
    next(fill)()
    lora_a = [_mm(_split(xa[:, ls]), _split(a2_ref[d])) for d, ls in enumerate(dirs)]
    g_out[...] = _mm(_split(jax.nn.sigmoid(xg)), _split(g2_ref[...]))
    v = conv(2 * WIDTH, 3 * WIDTH)
    next(fill)()
    kd_sum = jnp.zeros_like(kr)
    for d in range(N_DIRS):
        w_log = -_softplus(-(w0_ref[d:d + 1, :] + lora_w[d])) - 0.5
        a = jax.nn.sigmoid(a0_ref[d:d + 1, :] + lora_a[d])
        kd = kr * (1.0 + (a - 1.0) * ka_ref[...])
        dir_out[d, :, 0:WIDTH] = -jnp.exp(w_log)
        dir_out[d, :, WIDTH:2 * WIDTH] = kk * a
        dir_out[d, :, 2 * WIDTH:] = kd
        kd_sum = kd_sum + kd
    shared_out[:, 0:WIDTH] = r
    shared_out[:, WIDTH:2 * WIDTH] = kk
    shared_out[:, 2 * WIDTH:] = v
    bonus_out[...] = _segsum(r * kd_sum * rk_ref[...], ones) * v


HALF = CHUNK // 2


def _blockdiag(x, block=HEAD_DIM):
    def one(p):
        blk = lax.broadcasted_iota(jnp.int32, p.shape, 1) // block
        return jnp.concatenate([jnp.where(blk == q, p, jnp.zeros_like(p)) for q in range(GROUP_W // block)], axis=0)
    return one(x[0]), one(x[1])


def _head_transpose(x):
    xt = jnp.transpose(x)
    return jnp.concatenate([xt[h * HEAD_DIM:(h + 1) * HEAD_DIM] for h in range(GROUP)], axis=1)


def _scan_prepare(d, shared_ref, dir_ref, y_ref, s_ref):
    row = lax.broadcasted_iota(jnp.int32, (CHUNK, GROUP_W), 0)
    lane = lax.broadcasted_iota(jnp.int32, (CHUNK, GROUP_W), 1)
    col = lane % CHUNK
    strict = col > row if d else col < row
    incl = col >= row if d else col <= row
    row_h = lax.broadcasted_iota(jnp.int32, (HALF, GROUP_W), 0)
    lane_h = lax.broadcasted_iota(jnp.int32, (HALF, GROUP_W), 1)
    lo_s = lane_h % CHUNK < HALF
    eye_half = (lane_h % HALF == row_h).astype(F32)

    lw = dir_ref[0, :, 0:WIDTH]
    mask = jnp.where(incl[:, :CHUNK], 1.0, 0.0).astype(BF16)
    lw_hi, lw_mid = _split(lw)
    lw_lo = (lw - lw_hi.astype(F32) - lw_mid.astype(F32)).astype(BF16)
    cum = _dot(mask, lw_hi) + _dot(mask, lw_mid) + _dot(mask, lw_lo)
    tot = jnp.sum(lw, axis=0, keepdims=True)
    p_in = jnp.exp(cum)
    p_neg = jnp.exp(-cum)
    p_rem = jnp.exp(tot - cum)
    kt = shared_ref[:, WIDTH:2 * WIDTH] * jnp.exp(cum - lw)
    rt = shared_ref[:, 0:WIDTH] * p_in
    b = dir_ref[0, :, WIDTH:2 * WIDTH]
    kd = dir_ref[0, :, 2 * WIDTH:]
    bt = b * p_neg
    kdt = kd * p_neg
    bh = b * p_rem
    kh = kd * p_rem
    p_end = jnp.exp(tot)
    v = shared_ref[:, 2 * WIDTH:]

    chains = []
    for gi in range(N_HEADS // GROUP):
        sl = slice(gi * GROUP_W, (gi + 1) * GROUP_W)
        chains.append(dict(d=d, sl=sl, strict=strict, incl=incl, lo_s=lo_s, eye_half=eye_half, y_ref=y_ref, s_ref=s_ref,
                           kt=kt[:, sl], rt=rt[:, sl], v=v[:, sl], bt=bt[:, sl], kdt=kdt[:, sl],
                           kh=kh[:, sl], bh=bh[:, sl], p_end=p_end[:, sl]))
    return chains


def _scan_chains(chains):
    zeros = jnp.zeros((HALF, GROUP_W), F32)
    for c in chains:
        lhs = _split(jnp.concatenate([c["kt"], c["rt"]], axis=0))
        g_b = _mm_nt(lhs, _blockdiag(_split(c["bt"])))
        g_k = _mm_nt(lhs, _blockdiag(_split(c["kdt"])))
        c["a_k"] = jnp.where(c["strict"], g_k[:CHUNK], 0.0)
        c["a_rb"] = jnp.where(c["incl"], g_b[CHUNK:], 0.0)
        c["a_rk"] = jnp.where(c["incl"], g_k[CHUNK:], 0.0)
        n = jnp.where(c["strict"], -g_b[:CHUNK], 0.0)
        c["n"] = n
        c["pw"] = jnp.where(c["lo_s"], n[:HALF], n[HALF:])
        c["tp"] = c["eye_half"] + c["pw"]
    for c in chains:
        c["pw"] = _mm(_split(c["pw"]), _blockdiag(_split(c["pw"]), HALF))
    for _ in range(3):
        for c in chains:
            both = _mm(_split(jnp.concatenate([c["tp"], c["pw"]], axis=0)), _blockdiag(_split(c["pw"]), HALF))
            c["tp"] = c["tp"] + both[:HALF]
            c["pw"] = both[HALF:]
    for c in chains:
        c["tp"] = c["tp"] + _mm(_split(c["tp"]), _blockdiag(_split(c["pw"]), HALF))
    for c in chains:
        c["t1"] = jnp.where(c["lo_s"], c["tp"], 0.0)
        c["t2"] = jnp.where(c["lo_s"], 0.0, c["tp"])
        if c["d"] == 0:
            off, inner = jnp.where(c["lo_s"], c["n"][HALF:], 0.0), jnp.concatenate([c["t1"], zeros], axis=0)
        else:
            off, inner = jnp.where(c["lo_s"], 0.0, c["n"][:HALF]), jnp.concatenate([zeros, c["t2"]], axis=0)
        c["off"] = _mm(_split(off), _blockdiag(_split(inner)))
    for c in chains:
        if c["d"] == 0:
            x = _mm(_split(c["t2"]), _blockdiag(_split(jnp.concatenate([zeros, c["off"]], axis=0))))
            c["inv"] = jnp.concatenate([c["t1"], jnp.where(c["lo_s"], x, c["tp"])], axis=0)
        else:
            x = _mm(_split(c["t1"]), _blockdiag(_split(jnp.concatenate([c["off"], zeros], axis=0))))
            c["inv"] = jnp.concatenate([jnp.where(c["lo_s"], c["tp"], x), c["t2"]], axis=0)
    for c in chains:
        c["av"] = _mm(_split(jnp.concatenate([c["a_k"], c["a_rk"]], axis=0)), _blockdiag(_split(c["v"])))
        c["vt"] = _head_transpose(c["v"])
    for c in chains:
        c["s"] = c["s_ref"][0, :, c["sl"]]
        c["z"] = _mm_nt(_split(jnp.concatenate([c["rt"], c["kt"]], axis=0)), _blockdiag(_split(c["s"])))
    for c in chains:
        c["u"] = _mm(_split(c["inv"]), _blockdiag(_split(c["z"][CHUNK:] + c["av"][:CHUNK])))
    for c in chains:
        c["y_ref"][:, c["sl"]] = (c["z"][:CHUNK] + c["av"][CHUNK:]
                                  - _mm(_split(c["a_rb"]), _blockdiag(_split(c["u"]))))
    for c in chains:
        upd = (_mm(_split(c["vt"]), _blockdiag(_split(c["kh"])))
               - _mm(_split(_head_transpose(c["u"])), _blockdiag(_split(c["bh"]))))
        c["s_ref"][0, :, c["sl"]] = c["s"] * c["p_end"] + upd


REGION_CHUNKS = CTX_CHUNKS
SCAN_STREAMS = tuple((d, base, per_seq) for d in range(N_DIRS)
                     for base, per_seq in ((0, CHUNKS_PER_CTX_SEQ), (CTX_CHUNKS, CHUNKS_PER_LAT_SEQ)))
STREAM_INS = 3


def _stream_local_chunk(stream, j):
    return REGION_CHUNKS - 1 - j if stream[0] else j


def _scan_kernel(*refs):
    n_in = STREAM_INS * len(SCAN_STREAMS)
    j = pl.program_id(0)
    per_stream = []
    for s, stream in enumerate(SCAN_STREAMS):
        d, _, per_seq = stream
        ins = refs[STREAM_INS * s:STREAM_INS * (s + 1)]
        y_ref, s_ref = refs[n_in + 2 * s:n_in + 2 * s + 2]
        s0_ref = ins[-1]
        local = _stream_local_chunk(stream, j) % per_seq

        @pl.when(local == (per_seq - 1 if d else 0))
        def _():
            s_ref[...] = s0_ref[0]

        per_stream.append((d,) + ins[:-1] + (y_ref, s_ref))

    chains = []
    for args in per_stream:
        chains += _scan_prepare(*args)
    _scan_chains(chains)


def _scan(shared, perdir, s0):
    in_specs, out_specs, out_shape, operands = [], [], [], []
    for stream in SCAN_STREAMS:
        d, base, per_seq = stream
        seq0 = 0 if base == 0 else BATCH
        n_seq = REGION_CHUNKS // per_seq
        loc = lambda j, stream=stream: _stream_local_chunk(stream, j)
        tok = pl.BlockSpec((CHUNK, 3 * WIDTH), lambda j, loc=loc, base=base: (base + loc(j), 0))
        dtok = pl.BlockSpec((1, CHUNK, 3 * WIDTH), lambda j, loc=loc, base=base, d=d: (d, base + loc(j), 0))
        st_in = pl.BlockSpec((1, 1, HEAD_DIM, WIDTH),
                             lambda j, loc=loc, seq0=seq0, per_seq=per_seq, d=d: (seq0 + loc(j) // per_seq, d, 0, 0))
        in_specs += [tok, dtok, st_in]
        operands += [shared, perdir, s0]
        out_specs += [pl.BlockSpec((CHUNK, WIDTH), lambda j, loc=loc: (loc(j), 0)),
                      pl.BlockSpec((1, HEAD_DIM, WIDTH), lambda j, loc=loc, per_seq=per_seq: (loc(j) // per_seq, 0, 0))]
        out_shape += [jax.ShapeDtypeStruct((REGION_CHUNKS * CHUNK, WIDTH), F32),
                      jax.ShapeDtypeStruct((n_seq, HEAD_DIM, WIDTH), F32)]
    return pl.pallas_call(
        _scan_kernel,
        grid=(REGION_CHUNKS,),
        in_specs=in_specs,
        out_specs=out_specs,
        out_shape=out_shape,
        compiler_params=_params(1),
        name="scan",
    )(*operands)


def _post_kernel(yfc_ref, yfl_ref, ybc_ref, ybl_ref, bonus_ref, g_ref, attc_ref, attl_ref, xp_ref, xs_ref, mod_ref,
                 lng_ref, lnb_ref, ones_ref, wout_ref, n2_ref, x1_ref, h2_ref):
    i = pl.program_id(0)
    ones = ones_ref[...]
    inv_n = 1.0 / HEAD_DIM
    y = _ctx_or_lat(i, yfc_ref, yfl_ref) + _ctx_or_lat(i, ybc_ref, ybl_ref)
    yc = y - _segsum(y, ones) * inv_n
    var = _segsum(yc * yc, ones) * inv_n
    yn = yc * lax.rsqrt(var + GN_EPS) * lng_ref[...] + lnb_ref[...]
    r_out = ((yn + bonus_ref[...]) * g_ref[...]).astype(BF16)
    att = _ctx_or_lat(i, attc_ref, attl_ref)
    o = _dot(att, wout_ref[0:WIDTH, :]) + _dot(r_out, wout_ref[WIDTH:, :])
    x1 = _ctx_or_lat(i, xp_ref, xs_ref) + mod_ref[0, 2:3, :] * o
    x1_ref[...] = x1
    h2 = _rmsnorm(x1, n2_ref[...]) * (1.0 + mod_ref[0, 4:5, :]) + mod_ref[0, 3:4, :]
    h2_ref[...] = h2.astype(BF16)


def _post(ys, bonus, g, att_ctx, att_lat, xp, xs, mod, ln_g, ln_b, ones, w_out_bf, norm2):
    tok = _tok_spec
    ctx_lat = [_ctx_tok_spec(WIDTH), _lat_tok_spec(WIDTH)]
    return pl.pallas_call(
        _post_kernel,
        grid=(N_BLOCKS,),
        in_specs=ctx_lat + ctx_lat + [tok(WIDTH), tok(WIDTH)]
        + ctx_lat + [_ctx_tok_spec(D_MODEL), _lat_tok_spec(D_MODEL),
                  _mod_spec(),
                  _const_spec((1, WIDTH)), _const_spec((1, WIDTH)), _const_spec((WIDTH, WIDTH)),
                  _const_spec((2 * WIDTH, D_MODEL)), _const_spec((1, D_MODEL))],
        out_specs=[tok(D_MODEL), tok(D_MODEL)],
        out_shape=[jax.ShapeDtypeStruct((N_TOK, D_MODEL), F32),
                   jax.ShapeDtypeStruct((N_TOK, D_MODEL), BF16)],
        compiler_params=_params(1),
        name="post",
    )(*ys, bonus, g, att_ctx, att_lat, xp, xs, mod, ln_g, ln_b, ones, w_out_bf, norm2)


def _ffn_kernel(x1_ref, h2_ref, hp_ref, hn_ref, mod_ref, w1_ref, w3_ref, wc_ref, w2_ref, nf_ref, yp_ref, ys_ref):
    i = pl.program_id(0)
    blk = BIG_BLOCK
    is_ctx = i < N_CTX_TOK // blk
    has_prev, has_next = _seq_neighbours(i, blk)
    lhs = jnp.concatenate([h2_ref[...], hp_ref[...], hn_ref[...]], axis=0)
    h2 = lhs[:blk]
    rid = lax.broadcasted_iota(jnp.int32, (blk, 1), 0)
    inner = jnp.where(is_ctx, SEQ, -1)
    seq_start = rid == inner
    seq_end = rid == inner - 1
    acc = jnp.zeros((blk, D_MODEL), F32)
    for f in range(D_FF // FF_TILE):
        fs = slice(f * FF_TILE, (f + 1) * FF_TILE)
        a_all = _dot(lhs, w1_ref[:, fs])
        a = a_all[:blk]
        prow = jnp.where(has_prev, a_all[blk + 15:blk + 16, :], 0.0)
        nrow = jnp.where(has_next, a_all[blk + 16:blk + 17, :], 0.0)
        aprev, anext = _shifted(a, prow, nrow)
        aprev = jnp.where(seq_start, 0.0, aprev)
        anext = jnp.where(seq_end, 0.0, anext)
        cv = aprev * wc_ref[0:1, fs] + a * wc_ref[1:2, fs] + anext * wc_ref[2:3, fs]
        act = _silu(cv) * _dot(h2, w3_ref[:, fs])
        acc = acc + _dot(act.astype(BF16), w2_ref[fs, :])
    x2 = x1_ref[...] + mod_ref[0, 5:6, :] * acc
    y = _rmsnorm(x2, nf_ref[...])

    @pl.when(is_ctx)
    def _():
        yp_ref[...] = y

    @pl.when(jnp.logical_not(is_ctx))
    def _():
        ys_ref[...] = y


def _ffn(x1, h2, mod, w1_bf, w3_bf, wc, w2_bf, norm_f):
    blk = BIG_BLOCK
    rows16 = blk // 16
    return pl.pallas_call(
        _ffn_kernel,
        grid=(N_TOK // blk,),
        in_specs=[_tok_spec(D_MODEL, blk), _tok_spec(D_MODEL, blk),
                  pl.BlockSpec((16, D_MODEL), lambda i: (jnp.maximum(i * rows16 - 1, 0), 0)),
                  pl.BlockSpec((16, D_MODEL), lambda i: (jnp.minimum((i + 1) * rows16, N_TOK // 16 - 1), 0)),
                  _mod_spec(blk),
                  _const_spec((D_MODEL, D_FF)), _const_spec((D_MODEL, D_FF)), _const_spec((3, D_FF)),
                  _const_spec((D_FF, D_MODEL)), _const_spec((1, D_MODEL))],
        out_specs=[_ctx_tok_spec(D_MODEL, blk), _lat_tok_spec(D_MODEL, blk)],
        out_shape=[jax.ShapeDtypeStruct((N_CTX_TOK, D_MODEL), F32),
                   jax.ShapeDtypeStruct((N_LAT_TOK, D_MODEL), F32)],
        compiler_params=_params(1),
        name="ffn",
    )(x1, h2, h2, h2, mod, w1_bf, w3_bf, wc, w2_bf, norm_f)


def kernel(x_prompt, x_sample, cache_k, cache_v, state_rwkv, c, c_ctx, w_ada, b_ada, norm1, norm2,
           w_in, w_ts, w0, w2, a0, a2, g2, k_k, k_a, r_k, ln_x_g, ln_x_b, rpb, w_out,
           w_ffn1, w_ffn3, w_ffn_conv, w_ffn2, norm_f):
    xp = x_prompt.reshape(N_CTX_TOK, D_MODEL)
    xs = x_sample.reshape(N_LAT_TOK, D_MODEL)
    row = lambda t: t.reshape(1, -1)
    cond = jnp.concatenate([c_ctx[None, :], c, jnp.zeros((8 - 1 - DEC_BATCH, D_MODEL), F32)], axis=0)
    mod = _modulation(cond, w_ada[0], b_ada[0]).reshape(8, 6, D_MODEL)

    head_id = jnp.arange(WIDTH) // HEAD_DIM
    ones = (head_id[:, None] == head_id[None, :]).astype(BF16)
    q, k_ctx, v_ctx, k_lat, v_lat, shared, g, bonus, perdir = _inproj(
        xp, xs, mod, row(norm1[0]), w_in[0].astype(BF16), w_ts[0], w0[0], w2[0], a0[0], a2[0], g2[0],
        row(k_k[0]), row(k_a[0]), row(r_k[0]), ones)

    att_ctx = _ctx_attention(q, k_ctx, v_ctx)
    att_lat = _lat_attention(q, k_lat, v_lat, cache_k[:, 0].reshape(DEC_BATCH, PAST_LEN, WIDTH),
                             cache_v[:, 0].reshape(DEC_BATCH, PAST_LEN, WIDTH), _na_bias_table(rpb[0]))
    s_lat = jnp.transpose(state_rwkv[:, 0], (0, 1, 3, 2, 4)).reshape(DEC_BATCH, N_DIRS, HEAD_DIM, WIDTH)
    s0 = jnp.concatenate([jnp.zeros((BATCH, N_DIRS, HEAD_DIM, WIDTH), F32), s_lat], axis=0)
    y_fc, s_fc, y_fl, _, y_bc, s_bc, y_bl, _ = _scan(shared, perdir, s0)
    s_fin = jnp.stack([s_fc, s_bc], axis=1)

    x1, h2 = _post((y_fc, y_fl, y_bc, y_bl), bonus, g, att_ctx, att_lat, xp, xs, mod, row(ln_x_g[0]), row(ln_x_b[0]), ones,
                   w_out[0].astype(BF16), row(norm2[0]))
    yp, ys = _ffn(x1, h2, mod, w_ffn1[0].astype(BF16), w_ffn3[0].astype(BF16), w_ffn_conv[0],
                  w_ffn2[0].astype(BF16), row(norm_f))

    new_k = k_ctx.reshape(BATCH, 1, SEQ, N_HEADS, HEAD_DIM)
    new_v = v_ctx.reshape(BATCH, 1, SEQ, N_HEADS, HEAD_DIM)
    new_s = jnp.transpose(s_fin.reshape(BATCH, N_DIRS, HEAD_DIM, N_HEADS, HEAD_DIM),
                          (0, 1, 3, 2, 4)).reshape(BATCH, 1, N_DIRS, N_HEADS, HEAD_DIM, HEAD_DIM)
    return (yp.reshape(BATCH, SEQ, D_MODEL), ys.reshape(DEC_BATCH, DEC_SEQ, D_MODEL), new_k, new_v, new_s)
```

```python
import jax
import jax.numpy as jnp
from jax import lax
from jax.experimental import pallas as pl
from jax.experimental.pallas import tpu as pltpu

F32 = jnp.float32
BF16 = jnp.bfloat16
HIGHEST = lax.Precision.HIGHEST

D_MODEL = 1024
BATCH = 16
SEQ = 256
DEC_BATCH = 2
DEC_SEQ = 2048
PAST_LEN = 512
GRID_W = 64
HEAD_DIM = 64
N_HEADS = 8
WIDTH = N_HEADS * HEAD_DIM
PAIR_W = 2 * HEAD_DIM
NA_ROWS = 8
NA_COLS = 16
N_DIRS = 2
LORA = 64
GATE_LORA = 128
D_FF = 2816
EPS = 1e-6
GN_EPS = 64e-5
ATT_SCALE = HEAD_DIM ** -0.5
RWKV_COLS = 3 * WIDTH + N_DIRS * 2 * LORA + GATE_LORA
MASK_VALUE = -1e30

N_CTX_TOK = BATCH * SEQ
N_LAT_TOK = DEC_BATCH * DEC_SEQ
N_TOK = N_CTX_TOK + N_LAT_TOK
TOK_BLOCK = 256
N_BLOCKS = N_TOK // TOK_BLOCK
BIG_BLOCK = 512
N_SEQS = BATCH + DEC_BATCH
CHUNK = 64
N_CHUNKS = N_TOK // CHUNK
CTX_CHUNKS = N_CTX_TOK // CHUNK
CHUNKS_PER_CTX_SEQ = SEQ // CHUNK
CHUNKS_PER_LAT_SEQ = DEC_SEQ // CHUNK
GRID_ROWS = DEC_SEQ // GRID_W
FF_TILE = D_FF // 2
GROUP = 4
GROUP_W = GROUP * HEAD_DIM
VMEM_LIMIT = 56 * 1024 * 1024


def _params(n_axes, limit=VMEM_LIMIT):
    return pltpu.CompilerParams(dimension_semantics=("arbitrary",) * n_axes, vmem_limit_bytes=limit)


def _const_spec(shape):
    zeros = (0,) * len(shape)
    return pl.BlockSpec(shape, lambda *_: zeros, pipeline_mode=pl.Buffered(1))


def _tok_spec(cols, blk=TOK_BLOCK):
    return pl.BlockSpec((blk, cols), lambda i: (i, 0))


def _ctx_tok_spec(cols, blk=TOK_BLOCK):
    return pl.BlockSpec((blk, cols), lambda i: (jnp.minimum(i, N_CTX_TOK // blk - 1), 0))


def _lat_tok_spec(cols, blk=TOK_BLOCK):
    return pl.BlockSpec((blk, cols), lambda i: (jnp.maximum(i - N_CTX_TOK // blk, 0), 0))


def _ctx_or_lat(i, ctx_ref, lat_ref, blk=TOK_BLOCK):
    return jnp.where(i < N_CTX_TOK // blk, ctx_ref[...], lat_ref[...])


def _mod_spec(blk=TOK_BLOCK):
    def row(i):
        return jnp.where(i < N_CTX_TOK // blk, 0, 1 + (i - N_CTX_TOK // blk) // (DEC_SEQ // blk))
    return pl.BlockSpec((1, 6, D_MODEL), lambda i: (row(i), 0, 0))


def _seq_neighbours(i, blk=TOK_BLOCK):
    per_seq = DEC_SEQ // blk
    j = (i - N_CTX_TOK // blk) % per_seq
    lat = i >= N_CTX_TOK // blk
    return lat & (j != 0), lat & (j != per_seq - 1)


def _silu(x):
    return x * jax.nn.sigmoid(x)


def _softplus(x):
    return jnp.maximum(x, 0.0) + jnp.log1p(jnp.exp(-jnp.abs(x)))


def _rmsnorm(x, g):
    return x * lax.rsqrt(jnp.mean(x * x, axis=-1, keepdims=True) + EPS) * g


def _dot(a, b, precision=None):
    return jnp.dot(a, b, precision=precision, preferred_element_type=F32)


def _dot_nt(a, b, precision=None):
    return lax.dot_general(a, b, (((1,), (1,)), ((), ())), precision=precision, preferred_element_type=F32)


def _split(x):
    hi = x.astype(BF16)
    return hi, (x - hi.astype(F32)).astype(BF16)


def _mm3(dot, a, b):
    m = a[0].shape[0]
    hi = dot(jnp.concatenate([a[0], a[1]], axis=0), b[0])
    return hi[:m] + hi[m:] + dot(a[0], b[1])


def _mm(a, b):
    return _mm3(_dot, a, b)


def _mm_nt(a, b):
    return _mm3(_dot_nt, a, b)


def _segsum(x, ones_bf):
    m = x.shape[0]
    hi, lo = _split(x)
    s = _dot(jnp.concatenate([hi, lo], axis=0), ones_bf)
    return s[:m] + s[m:]


def _shifted(a, prev_row, next_row):
    t = a.shape[0]
    rid = lax.broadcasted_iota(jnp.int32, (t, 1), 0)
    prev = jnp.where(rid == 0, prev_row, pltpu.roll(a, 1, axis=0))
    nxt = jnp.where(rid == t - 1, next_row, pltpu.roll(a, t - 1, axis=0))
    return prev, nxt


def _mod_kernel(cond_ref, w_ref, b_ref, o_ref):
    o_ref[...] = _dot(_silu(cond_ref[...]), w_ref[...], HIGHEST) + b_ref[...]


def _modulation(cond, w_ada, b_ada):
    n = 6 * D_MODEL
    return pl.pallas_call(
        _mod_kernel,
        grid=(6,),
        in_specs=[pl.BlockSpec((8, D_MODEL), lambda j: (0, 0)),
                  pl.BlockSpec((D_MODEL, D_MODEL), lambda j: (0, j)),
                  pl.BlockSpec((1, D_MODEL), lambda j: (0, j))],
        out_specs=pl.BlockSpec((8, D_MODEL), lambda j: (0, j)),
        out_shape=jax.ShapeDtypeStruct((8, n), F32),
        compiler_params=_params(1),
        name="mod",
    )(cond, w_ada, b_ada.reshape(1, n))


def _inproj_kernel(xp_ref, xs_ref, xprev_ref, xnext_ref, mod_ref, n1_ref, w_ref, *rest):
    rwkv_refs, (q_ref, kc_ref, vc_ref, kl_ref, vl_ref), rwkv_outs, (nk_ref, nv_ref) = (
        rest[:10], rest[10:15], rest[15:19], rest[19:])
    i = pl.program_id(0)
    has_prev, has_next = _seq_neighbours(i)

    def modulated(x):
        h = _rmsnorm(x, n1_ref[...]) * (1.0 + mod_ref[0, 1:2, :]) + mod_ref[0, 0:1, :]
        return h.astype(BF16)

    h = modulated(_ctx_or_lat(i, xp_ref, xs_ref))
    edge = modulated(jnp.concatenate([xprev_ref[...], xnext_ref[...]], axis=0))
    lhs = jnp.concatenate([h, edge], axis=0)

    def zcols(c0, c1):
        z_all = _dot(lhs, w_ref[:, 3 * WIDTH + c0:3 * WIDTH + c1])
        return (z_all[:TOK_BLOCK],
                jnp.where(has_prev, z_all[TOK_BLOCK + 7:TOK_BLOCK + 8], 0.0),
                jnp.where(has_next, z_all[TOK_BLOCK + 8:TOK_BLOCK + 9], 0.0))

    kv = []

    def project_q():
        q_ref[...] = _dot(h, w_ref[:, 0:WIDTH])

    def project_k():
        kv.append(_dot(h, w_ref[:, WIDTH:2 * WIDTH]))

    def project_v():
        kv.append(_dot(h, w_ref[:, 2 * WIDTH:3 * WIDTH]))

    _rwkv_tokens(zcols, (project_q, project_k, project_v), *rwkv_refs, *rwkv_outs)
    k, v = kv

    @pl.when(i < N_CTX_TOK // TOK_BLOCK)
    def _():
        kc_ref[...] = k
        vc_ref[...] = v
        nk_ref[0, 0] = k.reshape(SEQ, N_HEADS, HEAD_DIM)
        nv_ref[0, 0] = v.reshape(SEQ, N_HEADS, HEAD_DIM)

    @pl.when(i >= N_CTX_TOK // TOK_BLOCK)
    def _():
        kl_ref[...] = k
        vl_ref[...] = v


def _inproj(xp, xs, mod, norm1, w_in_bf, w_ts, w0, w2, a0, a2, g2, k_k, k_a, r_k, ones):
    half = jax.ShapeDtypeStruct((N_CTX_TOK, WIDTH), F32)
    tok = jax.ShapeDtypeStruct((N_TOK, WIDTH), F32)
    lat0 = N_CTX_TOK // TOK_BLOCK
    rows8 = TOK_BLOCK // 8
    n8 = N_LAT_TOK // 8
    prev8 = pl.BlockSpec((8, D_MODEL), lambda i: (jnp.clip((i - lat0) * rows8 - 1, 0, n8 - 1), 0))
    next8 = pl.BlockSpec((8, D_MODEL), lambda i: (jnp.clip((i - lat0 + 1) * rows8, 0, n8 - 1), 0))
    cache = pl.BlockSpec((1, 1, SEQ, N_HEADS, HEAD_DIM), lambda i: (jnp.minimum(i, BATCH - 1), 0, 0, 0, 0))
    return pl.pallas_call(
        _inproj_kernel,
        grid=(N_BLOCKS,),
        in_specs=[_ctx_tok_spec(D_MODEL), _lat_tok_spec(D_MODEL), prev8, next8, _mod_spec(),
                  _const_spec((1, D_MODEL)),
                  _const_spec((D_MODEL, 3 * WIDTH + RWKV_COLS)),
                  _const_spec((3, RWKV_COLS)),
                  _const_spec((N_DIRS, WIDTH)), _const_spec((N_DIRS, LORA, WIDTH)),
                  _const_spec((N_DIRS, WIDTH)), _const_spec((N_DIRS, LORA, WIDTH)),
                  _const_spec((GATE_LORA, WIDTH)),
                  _const_spec((1, WIDTH)), _const_spec((1, WIDTH)), _const_spec((1, WIDTH)),
                  _const_spec((WIDTH, WIDTH))],
        out_specs=[_tok_spec(WIDTH), _ctx_tok_spec(WIDTH), _ctx_tok_spec(WIDTH),
                   _lat_tok_spec(WIDTH), _lat_tok_spec(WIDTH),
                   _tok_spec(3 * WIDTH), _tok_spec(WIDTH), _tok_spec(WIDTH),
                   pl.BlockSpec((N_DIRS, TOK_BLOCK, 3 * WIDTH), lambda i: (0, i, 0)),
                   cache, cache],
        out_shape=[tok, half, half, half, half,
                   jax.ShapeDtypeStruct((N_TOK, 3 * WIDTH), F32), tok, tok,
                   jax.ShapeDtypeStruct((N_DIRS, N_TOK, 3 * WIDTH), F32),
                   jax.ShapeDtypeStruct((BATCH, 1, SEQ, N_HEADS, HEAD_DIM), F32),
                   jax.ShapeDtypeStruct((BATCH, 1, SEQ, N_HEADS, HEAD_DIM), F32)],
        compiler_params=_params(1),
        name="inproj",
    )(xp, xs, xs, xs, mod, norm1, w_in_bf, w_ts, w0, w2, a0, a2, g2, k_k, k_a, r_k, ones)


def _pair_queries(q):
    lo_half = lax.broadcasted_iota(jnp.int32, q.shape, 1) < HEAD_DIM
    return jnp.concatenate([jnp.where(lo_half, q, 0.0), jnp.where(lo_half, 0.0, q)], axis=0).astype(BF16)


def _pair_outputs(o):
    t = o.shape[0] // 2
    lo_half = lax.broadcasted_iota(jnp.int32, (t, PAIR_W), 1) < HEAD_DIM
    return jnp.where(lo_half, o[:t], o[t:])


def _ctx_attn_kernel(q_ref, k_ref, v_ref, o_ref):
    pairs = [slice(p * PAIR_W, (p + 1) * PAIR_W) for p in range(N_HEADS // 2)]
    scores = [_dot_nt(_pair_queries(q_ref[:, ps]), k_ref[:, ps].astype(BF16)) * ATT_SCALE for ps in pairs]
    probs = []
    for s in scores:
        e = jnp.exp(s - jnp.max(s, axis=-1, keepdims=True))
        probs.append((e.astype(BF16), jnp.sum(e, axis=-1, keepdims=True)))
    for ps, (e, l) in zip(pairs, probs):
        o_ref[:, ps] = _pair_outputs(_dot(e, v_ref[:, ps].astype(BF16)) / l).astype(BF16)


def _ctx_attention(q, k, v):
    spec = pl.BlockSpec((SEQ, WIDTH), lambda b: (b, 0))
    return pl.pallas_call(
        _ctx_attn_kernel,
        grid=(BATCH,),
        in_specs=[spec, spec, spec],
        out_specs=spec,
        out_shape=jax.ShapeDtypeStruct((N_CTX_TOK, WIDTH), BF16),
        compiler_params=_params(1),
        name="ctxattn",
    )(q, k, v)


def _lat_attn_kernel(q_ref, k_ref, v_ref, ck_ref, cv_ref, bias_ref, o_ref, kbf, vbf, ckbf, cvbf):
    i = pl.program_id(1)

    @pl.when(i == 0)
    def _():
        kbf[...] = k_ref[...].astype(BF16)
        vbf[...] = v_ref[...].astype(BF16)
        ckbf[...] = ck_ref[0].astype(BF16)
        cvbf[...] = cv_ref[0].astype(BF16)

    win = NA_ROWS * GRID_W
    first_row = jnp.clip(i - NA_ROWS // 2, 0, GRID_ROWS - NA_ROWS)
    start = pl.multiple_of(first_row * GRID_W, GRID_W)
    off0 = first_row - i + NA_ROWS - 1
    pairs = [slice(p * PAIR_W, (p + 1) * PAIR_W) for p in range(N_HEADS // 2)]
    scores = []
    for p, ps in enumerate(pairs):
        lhs = _pair_queries(q_ref[:, ps])
        bias = jnp.concatenate(
            [jnp.concatenate([bias_ref[2 * p + hh, off0 + 2 * j] for j in range(NA_ROWS // 2)], axis=-1)
             for hh in range(2)], axis=0)
        s_loc = _dot_nt(lhs, kbf[pl.ds(start, win), ps]) * ATT_SCALE + bias
        s_ctx = _dot_nt(lhs, ckbf[:, ps]) * ATT_SCALE
        scores.append((s_loc, s_ctx))
    probs = []
    for s_loc, s_ctx in scores:
        m = jnp.maximum(jnp.max(s_loc, axis=-1, keepdims=True), jnp.max(s_ctx, axis=-1, keepdims=True))
        e_loc = jnp.exp(s_loc - m)
        e_ctx = jnp.exp(s_ctx - m)
        l = jnp.sum(e_loc, axis=-1, keepdims=True) + jnp.sum(e_ctx, axis=-1, keepdims=True)
        probs.append((e_loc.astype(BF16), e_ctx.astype(BF16), l))
    for ps, (e_loc, e_ctx, l) in zip(pairs, probs):
        o = _dot(e_loc, vbf[pl.ds(start, win), ps]) + _dot(e_ctx, cvbf[:, ps])
        o_ref[:, ps] = _pair_outputs(o / l).astype(BF16)


def _na_bias_table(rpb):
    h, d, n = rpb.shape
    w = GRID_W
    lead = w - NA_COLS
    p = jnp.pad(rpb.astype(F32), ((0, 0), (0, 0), (lead, 2 * w - lead - n)), constant_values=MASK_VALUE)
    flat = jnp.broadcast_to(p[:, :, None, :], (h, d, w, 2 * w)).reshape(h, d, 2 * w * w)
    t = flat[:, :, :w * (2 * w - 1)].reshape(h, d, w, 2 * w - 1)[:, :, :, w - 1:]
    cq = jnp.arange(w)[:, None]
    ck = jnp.arange(w)[None, :]
    cs = jnp.clip(cq - NA_COLS // 2, 0, w - NA_COLS)
    t = jnp.where((ck >= cs) & (ck < cs + NA_COLS), t, MASK_VALUE)
    return jnp.concatenate([t[:, :-1], t[:, 1:]], axis=-1)


def _lat_attention(q, k, v, cache_k, cache_v, bias):
    lat0 = N_CTX_TOK // GRID_W
    seq = pl.BlockSpec((DEC_SEQ, WIDTH), lambda b, i: (b, 0))
    ctx = pl.BlockSpec((1, PAST_LEN, WIDTH), lambda b, i: (b, 0, 0))
    return pl.pallas_call(
        _lat_attn_kernel,
        grid=(DEC_BATCH, GRID_ROWS),
        in_specs=[pl.BlockSpec((GRID_W, WIDTH), lambda b, i: (lat0 + b * GRID_ROWS + i, 0)),
                  seq, seq, ctx, ctx,
                  _const_spec((N_HEADS, 2 * NA_ROWS - 2, GRID_W, PAIR_W))],
        out_specs=pl.BlockSpec((GRID_W, WIDTH), lambda b, i: (b * GRID_ROWS + i, 0)),
        out_shape=jax.ShapeDtypeStruct((N_LAT_TOK, WIDTH), BF16),
        scratch_shapes=[pltpu.VMEM((DEC_SEQ, WIDTH), BF16), pltpu.VMEM((DEC_SEQ, WIDTH), BF16),
                        pltpu.VMEM((PAST_LEN, WIDTH), BF16), pltpu.VMEM((PAST_LEN, WIDTH), BF16)],
        compiler_params=_params(2),
        name="latattn",
    )(q, k, v, cache_k, cache_v, bias)


def _rwkv_tokens(zcols, fillers, wts_ref, w0_ref, w2_ref, a0_ref, a2_ref, g2_ref, kk_ref, ka_ref, rk_ref, ones_ref,
                 shared_out, g_out, bonus_out, dir_out):
    def conv(c0, c1):
        zc, prow, nrow = zcols(c0, c1)
        zprev, znext = _shifted(zc, prow, nrow)
        return zprev * wts_ref[0:1, c0:c1] + zc * wts_ref[1:2, c0:c1] + znext * wts_ref[2:3, c0:c1]

    fill = iter(fillers)
    dirs = [slice(d * LORA, (d + 1) * LORA) for d in range(N_DIRS)]
    ones = ones_ref[...]
    o = 3 * WIDTH
    lora_in = conv(o, RWKV_COLS)
    kr = conv(WIDTH, 2 * WIDTH)
    tw = jnp.tanh(lora_in[:, 0:N_DIRS * LORA])
    xa = lora_in[:, N_DIRS * LORA:2 * N_DIRS * LORA]
    xg = lora_in[:, 2 * N_DIRS * LORA:]
    kkraw = kr * kk_ref[...]
    next(fill)()
    kk = kkraw * lax.rsqrt(_segsum(kkraw * kkraw, ones) + 1e-12)
    lora_w = [_mm(_split(tw[:, ls]), _split(w2_ref[d])) for d, ls in enumerate(dirs)]
    r = conv(0, WIDTH)
    next(fill)()
    lora_a = [_mm(_split(xa[:, ls]), _split(a2_ref[d])) for d, ls in enumerate(dirs)]
    g_out[...] = _mm(_split(jax.nn.sigmoid(xg)), _split(g2_ref[...]))
    v = conv(2 * WIDTH, 3 * WIDTH)
    next(fill)()
    kd_sum = jnp.zeros_like(kr)
    for d in range(N_DIRS):
        w_log = -_softplus(-(w0_ref[d:d + 1, :] + lora_w[d])) - 0.5
        a = jax.nn.sigmoid(a0_ref[d:d + 1, :] + lora_a[d])
        kd = kr * (1.0 + (a - 1.0) * ka_ref[...])
        dir_out[d, :, 0:WIDTH] = -jnp.exp(w_log)
        dir_out[d, :, WIDTH:2 * WIDTH] = kk * a
        dir_out[d, :, 2 * WIDTH:] = kd
        kd_sum = kd_sum + kd
    shared_out[:, 0:WIDTH] = r
    shared_out[:, WIDTH:2 * WIDTH] = kk
    shared_out[:, 2 * WIDTH:] = v
    bonus_out[...] = _segsum(r * kd_sum * rk_ref[...], ones) * v


HALF = CHUNK // 2


def _blockdiag(x, block=HEAD_DIM):
    def one(p):
        blk = lax.broadcasted_iota(jnp.int32, p.shape, 1) // block
        return jnp.concatenate([jnp.where(blk == q, p, jnp.zeros_like(p)) for q in range(GROUP_W // block)], axis=0)
    return one(x[0]), one(x[1])


def _head_transpose(x):
    xt = jnp.transpose(x)
    return jnp.concatenate([xt[h * HEAD_DIM:(h + 1) * HEAD_DIM] for h in range(GROUP)], axis=1)


def _scan_prepare(d, shared_ref, dir_ref, y_ref, s_ref):
    row = lax.broadcasted_iota(jnp.int32, (CHUNK, GROUP_W), 0)
    lane = lax.broadcasted_iota(jnp.int32, (CHUNK, GROUP_W), 1)
    col = lane % CHUNK
    strict = col > row if d else col < row
    incl = col >= row if d else col <= row
    row_h = lax.broadcasted_iota(jnp.int32, (HALF, GROUP_W), 0)
    lane_h = lax.broadcasted_iota(jnp.int32, (HALF, GROUP_W), 1)
    lo_s = lane_h % CHUNK < HALF
    eye_half = (lane_h % HALF == row_h).astype(F32)

    lw = dir_ref[0, :, 0:WIDTH]
    mask = jnp.where(incl[:, :CHUNK], 1.0, 0.0).astype(BF16)
    lw_hi, lw_mid = _split(lw)
    lw_lo = (lw - lw_hi.astype(F32) - lw_mid.astype(F32)).astype(BF16)
    cum = _dot(mask, lw_hi) + _dot(mask, lw_mid) + _dot(mask, lw_lo)
    tot = jnp.sum(lw, axis=0, keepdims=True)
    p_in = jnp.exp(cum)
    p_neg = jnp.exp(-cum)
    p_rem = jnp.exp(tot - cum)
    kt = shared_ref[:, WIDTH:2 * WIDTH] * jnp.exp(cum - lw)
    rt = shared_ref[:, 0:WIDTH] * p_in
    b = dir_ref[0, :, WIDTH:2 * WIDTH]
    kd = dir_ref[0, :, 2 * WIDTH:]
    bt = b * p_neg
    kdt = kd * p_neg
    bh = b * p_rem
    kh = kd * p_rem
    p_end = jnp.exp(tot)
    v = shared_ref[:, 2 * WIDTH:]

    chains = []
    for gi in range(N_HEADS // GROUP):
        sl = slice(gi * GROUP_W, (gi + 1) * GROUP_W)
        chains.append(dict(d=d, sl=sl, strict=strict, incl=incl, lo_s=lo_s, eye_half=eye_half, y_ref=y_ref, s_ref=s_ref,
                           kt=kt[:, sl], rt=rt[:, sl], v=v[:, sl], bt=bt[:, sl], kdt=kdt[:, sl],
                           kh=kh[:, sl], bh=bh[:, sl], p_end=p_end[:, sl]))
    return chains


def _scan_chains(chains):
    zeros = jnp.zeros((HALF, GROUP_W), F32)
    for c in chains:
        lhs = _split(jnp.concatenate([c["kt"], c["rt"]], axis=0))
        g_b = _mm_nt(lhs, _blockdiag(_split(c["bt"])))
        g_k = _mm_nt(lhs, _blockdiag(_split(c["kdt"])))
        c["a_k"] = jnp.where(c["strict"], g_k[:CHUNK], 0.0)
        c["a_rb"] = jnp.where(c["incl"], g_b[CHUNK:], 0.0)
        c["a_rk"] = jnp.where(c["incl"], g_k[CHUNK:], 0.0)
        n = jnp.where(c["strict"], -g_b[:CHUNK], 0.0)
        c["n"] = n
        c["pw"] = jnp.where(c["lo_s"], n[:HALF], n[HALF:])
        c["tp"] = c["eye_half"] + c["pw"]
    for c in chains:
        c["pw"] = _mm(_split(c["pw"]), _blockdiag(_split(c["pw"]), HALF))
    for _ in range(3):
        for c in chains:
            both = _mm(_split(jnp.concatenate([c["tp"], c["pw"]], axis=0)), _blockdiag(_split(c["pw"]), HALF))
            c["tp"] = c["tp"] + both[:HALF]
            c["pw"] = both[HALF:]
    for c in chains:
        c["tp"] = c["tp"] + _mm(_split(c["tp"]), _blockdiag(_split(c["pw"]), HALF))
    for c in chains:
        c["t1"] = jnp.where(c["lo_s"], c["tp"], 0.0)
        c["t2"] = jnp.where(c["lo_s"], 0.0, c["tp"])
        if c["d"] == 0:
            off, inner = jnp.where(c["lo_s"], c["n"][HALF:], 0.0), jnp.concatenate([c["t1"], zeros], axis=0)
        else:
            off, inner = jnp.where(c["lo_s"], 0.0, c["n"][:HALF]), jnp.concatenate([zeros, c["t2"]], axis=0)
        c["off"] = _mm(_split(off), _blockdiag(_split(inner)))
    for c in chains:
        if c["d"] == 0:
            x = _mm(_split(c["t2"]), _blockdiag(_split(jnp.concatenate([zeros, c["off"]], axis=0))))
            c["inv"] = jnp.concatenate([c["t1"], jnp.where(c["lo_s"], x, c["tp"])], axis=0)
        else:
            x = _mm(_split(c["t1"]), _blockdiag(_split(jnp.concatenate([c["off"], zeros], axis=0))))
            c["inv"] = jnp.concatenate([jnp.where(c["lo_s"], c["tp"], x), c["t2"]], axis=0)
    for c in chains:
        c["av"] = _mm(_split(jnp.concatenate([c["a_k"], c["a_rk"]], axis=0)), _blockdiag(_split(c["v"])))
        c["vt"] = _head_transpose(c["v"])
    for c in chains:
        c["s"] = c["s_ref"][0, :, c["sl"]]
        c["z"] = _mm_nt(_split(jnp.concatenate([c["rt"], c["kt"]], axis=0)), _blockdiag(_split(c["s"])))
    for c in chains:
        c["u"] = _mm(_split(c["inv"]), _blockdiag(_split(c["z"][CHUNK:] + c["av"][:CHUNK])))
    for c in chains:
        c["y_ref"][:, c["sl"]] = (c["z"][:CHUNK] + c["av"][CHUNK:]
                                  - _mm(_split(c["a_rb"]), _blockdiag(_split(c["u"]))))
    for c in chains:
        upd = (_mm(_split(c["vt"]), _blockdiag(_split(c["kh"])))
               - _mm(_split(_head_transpose(c["u"])), _blockdiag(_split(c["bh"]))))
        c["s_ref"][0, :, c["sl"]] = c["s"] * c["p_end"] + upd


REGION_CHUNKS = CTX_CHUNKS
SCAN_STREAMS = tuple((d, base, per_seq) for d in range(N_DIRS)
                     for base, per_seq in ((0, CHUNKS_PER_CTX_SEQ), (CTX_CHUNKS, CHUNKS_PER_LAT_SEQ)))
STREAM_INS = 3


def _stream_local_chunk(stream, j):
    return REGION_CHUNKS - 1 - j if stream[0] else j


def _scan_kernel(*refs):
    n_in = STREAM_INS * len(SCAN_STREAMS)
    j = pl.program_id(0)
    per_stream = []
    for s, stream in enumerate(SCAN_STREAMS):
        d, _, per_seq = stream
        ins = refs[STREAM_INS * s:STREAM_INS * (s + 1)]
        y_ref, s_ref = refs[n_in + 2 * s:n_in + 2 * s + 2]
        s0_ref = ins[-1]
        local = _stream_local_chunk(stream, j) % per_seq

        @pl.when(local == (per_seq - 1 if d else 0))
        def _():
            s_ref[...] = s0_ref[0]

        per_stream.append((d,) + ins[:-1] + (y_ref, s_ref))

    chains = []
    for args in per_stream:
        chains += _scan_prepare(*args)
    _scan_chains(chains)


def _scan(shared, perdir, s0):
    in_specs, out_specs, out_shape, operands = [], [], [], []
    for stream in SCAN_STREAMS:
        d, base, per_seq = stream
        seq0 = 0 if base == 0 else BATCH
        n_seq = REGION_CHUNKS // per_seq
        loc = lambda j, stream=stream: _stream_local_chunk(stream, j)
        tok = pl.BlockSpec((CHUNK, 3 * WIDTH), lambda j, loc=loc, base=base: (base + loc(j), 0))
        dtok = pl.BlockSpec((1, CHUNK, 3 * WIDTH), lambda j, loc=loc, base=base, d=d: (d, base + loc(j), 0))
        st_in = pl.BlockSpec((1, 1, HEAD_DIM, WIDTH),
                             lambda j, loc=loc, seq0=seq0, per_seq=per_seq, d=d: (seq0 + loc(j) // per_seq, d, 0, 0))
        in_specs += [tok, dtok, st_in]
        operands += [shared, perdir, s0]
        out_specs += [pl.BlockSpec((CHUNK, WIDTH), lambda j, loc=loc: (loc(j), 0)),
                      pl.BlockSpec((1, HEAD_DIM, WIDTH), lambda j, loc=loc, per_seq=per_seq: (loc(j) // per_seq, 0, 0))]
        out_shape += [jax.ShapeDtypeStruct((REGION_CHUNKS * CHUNK, WIDTH), F32),
                      jax.ShapeDtypeStruct((n_seq, HEAD_DIM, WIDTH), F32)]
    return pl.pallas_call(
        _scan_kernel,
        grid=(REGION_CHUNKS,),
        in_specs=in_specs,
        out_specs=out_specs,
        out_shape=out_shape,
        compiler_params=_params(1),
        name="scan",
    )(*operands)


def _post_kernel(yfc_ref, yfl_ref, ybc_ref, ybl_ref, bonus_ref, g_ref, attc_ref, attl_ref, xp_ref, xs_ref, mod_ref,
                 lng_ref, lnb_ref, ones_ref, wout_ref, n2_ref, x1_ref, h2_ref):
    i = pl.program_id(0)
    ones = ones_ref[...]
    inv_n = 1.0 / HEAD_DIM
    y = _ctx_or_lat(i, yfc_ref, yfl_ref) + _ctx_or_lat(i, ybc_ref, ybl_ref)
    yc = y - _segsum(y, ones) * inv_n
    var = _segsum(yc * yc, ones) * inv_n
    yn = yc * lax.rsqrt(var + GN_EPS) * lng_ref[...] + lnb_ref[...]
    r_out = ((yn + bonus_ref[...]) * g_ref[...]).astype(BF16)
    att = _ctx_or_lat(i, attc_ref, attl_ref)
    o = _dot(att, wout_ref[0:WIDTH, :]) + _dot(r_out, wout_ref[WIDTH:, :])
    x1 = _ctx_or_lat(i, xp_ref, xs_ref) + mod_ref[0, 2:3, :] * o
    x1_ref[...] = x1
    h2 = _rmsnorm(x1, n2_ref[...]) * (1.0 + mod_ref[0, 4:5, :]) + mod_ref[0, 3:4, :]
    h2_ref[...] = h2.astype(BF16)


def _post(ys, bonus, g, att_ctx, att_lat, xp, xs, mod, ln_g, ln_b, ones, w_out_bf, norm2):
    tok = _tok_spec
    ctx_lat = [_ctx_tok_spec(WIDTH), _lat_tok_spec(WIDTH)]
    return pl.pallas_call(
        _post_kernel,
        grid=(N_BLOCKS,),
        in_specs=ctx_lat + ctx_lat + [tok(WIDTH), tok(WIDTH)]
        + ctx_lat + [_ctx_tok_spec(D_MODEL), _lat_tok_spec(D_MODEL),
                  _mod_spec(),
                  _const_spec((1, WIDTH)), _const_spec((1, WIDTH)), _const_spec((WIDTH, WIDTH)),
                  _const_spec((2 * WIDTH, D_MODEL)), _const_spec((1, D_MODEL))],
        out_specs=[tok(D_MODEL), tok(D_MODEL)],
        out_shape=[jax.ShapeDtypeStruct((N_TOK, D_MODEL), F32),
                   jax.ShapeDtypeStruct((N_TOK, D_MODEL), BF16)],
        compiler_params=_params(1),
        name="post",
    )(*ys, bonus, g, att_ctx, att_lat, xp, xs, mod, ln_g, ln_b, ones, w_out_bf, norm2)


def _ffn_kernel(x1_ref, h2_ref, hp_ref, hn_ref, mod_ref, w1_ref, w3_ref, wc_ref, w2_ref, nf_ref, yp_ref, ys_ref):
    i = pl.program_id(0)
    blk = BIG_BLOCK
    is_ctx = i < N_CTX_TOK // blk
    has_prev, has_next = _seq_neighbours(i, blk)
    lhs = jnp.concatenate([h2_ref[...], hp_ref[...], hn_ref[...]], axis=0)
    h2 = lhs[:blk]
    rid = lax.broadcasted_iota(jnp.int32, (blk, 1), 0)
    inner = jnp.where(is_ctx, SEQ, -1)
    seq_start = rid == inner
    seq_end = rid == inner - 1
    acc = jnp.zeros((blk, D_MODEL), F32)
    for f in range(D_FF // FF_TILE):
        fs = slice(f * FF_TILE, (f + 1) * FF_TILE)
        a_all = _dot(lhs, w1_ref[:, fs])
        a = a_all[:blk]
        prow = jnp.where(has_prev, a_all[blk + 15:blk + 16, :], 0.0)
        nrow = jnp.where(has_next, a_all[blk + 16:blk + 17, :], 0.0)
        aprev, anext = _shifted(a, prow, nrow)
        aprev = jnp.where(seq_start, 0.0, aprev)
        anext = jnp.where(seq_end, 0.0, anext)
        cv = aprev * wc_ref[0:1, fs] + a * wc_ref[1:2, fs] + anext * wc_ref[2:3, fs]
        act = _silu(cv) * _dot(h2, w3_ref[:, fs])
        acc = acc + _dot(act.astype(BF16), w2_ref[fs, :])
    x2 = x1_ref[...] + mod_ref[0, 5:6, :] * acc
    y = _rmsnorm(x2, nf_ref[...])

    @pl.when(is_ctx)
    def _():
        yp_ref[...] = y

    @pl.when(jnp.logical_not(is_ctx))
    def _():
        ys_ref[...] = y


def _ffn(x1, h2, mod, w1_bf, w3_bf, wc, w2_bf, norm_f):
    blk = BIG_BLOCK
    rows16 = blk // 16
    return pl.pallas_call(
        _ffn_kernel,
        grid=(N_TOK // blk,),
        in_specs=[_tok_spec(D_MODEL, blk), _tok_spec(D_MODEL, blk),
                  pl.BlockSpec((16, D_MODEL), lambda i: (jnp.maximum(i * rows16 - 1, 0), 0)),
                  pl.BlockSpec((16, D_MODEL), lambda i: (jnp.minimum((i + 1) * rows16, N_TOK // 16 - 1), 0)),
                  _mod_spec(blk),
                  _const_spec((D_MODEL, D_FF)), _const_spec((D_MODEL, D_FF)), _const_spec((3, D_FF)),
                  _const_spec((D_FF, D_MODEL)), _const_spec((1, D_MODEL))],
        out_specs=[_ctx_tok_spec(D_MODEL, blk), _lat_tok_spec(D_MODEL, blk)],
        out_shape=[jax.ShapeDtypeStruct((N_CTX_TOK, D_MODEL), F32),
                   jax.ShapeDtypeStruct((N_LAT_TOK, D_MODEL), F32)],
        compiler_params=_params(1),
        name="ffn",
    )(x1, h2, h2, h2, mod, w1_bf, w3_bf, wc, w2_bf, norm_f)


def kernel(x_prompt, x_sample, cache_k, cache_v, state_rwkv, c, c_ctx, w_ada, b_ada, norm1, norm2,
           w_in, w_ts, w0, w2, a0, a2, g2, k_k, k_a, r_k, ln_x_g, ln_x_b, rpb, w_out,
           w_ffn1, w_ffn3, w_ffn_conv, w_ffn2, norm_f):
    xp = x_prompt.reshape(N_CTX_TOK, D_MODEL)
    xs = x_sample.reshape(N_LAT_TOK, D_MODEL)
    row = lambda t: t.reshape(1, -1)
    cond = jnp.concatenate([c_ctx[None, :], c, jnp.zeros((8 - 1 - DEC_BATCH, D_MODEL), F32)], axis=0)
    mod = _modulation(cond, w_ada[0], b_ada[0]).reshape(8, 6, D_MODEL)

    head_id = jnp.arange(WIDTH) // HEAD_DIM
    ones = (head_id[:, None] == head_id[None, :]).astype(BF16)
    q, k_ctx, v_ctx, k_lat, v_lat, shared, g, bonus, perdir, new_k, new_v = _inproj(
        xp, xs, mod, row(norm1[0]), w_in[0].astype(BF16), w_ts[0], w0[0], w2[0], a0[0], a2[0], g2[0],
        row(k_k[0]), row(k_a[0]), row(r_k[0]), ones)

    att_ctx = _ctx_attention(q, k_ctx, v_ctx)
    att_lat = _lat_attention(q, k_lat, v_lat, cache_k[:, 0].reshape(DEC_BATCH, PAST_LEN, WIDTH),
                             cache_v[:, 0].reshape(DEC_BATCH, PAST_LEN, WIDTH), _na_bias_table(rpb[0]))
    s_lat = jnp.transpose(state_rwkv[:, 0], (0, 1, 3, 2, 4)).reshape(DEC_BATCH, N_DIRS, HEAD_DIM, WIDTH)
    s0 = jnp.concatenate([jnp.zeros((BATCH, N_DIRS, HEAD_DIM, WIDTH), F32), s_lat], axis=0)
    y_fc, s_fc, y_fl, _, y_bc, s_bc, y_bl, _ = _scan(shared, perdir, s0)
    s_fin = jnp.stack([s_fc, s_bc], axis=1)

    x1, h2 = _post((y_fc, y_fl, y_bc, y_bl), bonus, g, att_ctx, att_lat, xp, xs, mod, row(ln_x_g[0]), row(ln_x_b[0]), ones,
                   w_out[0].astype(BF16), row(norm2[0]))
    yp, ys = _ffn(x1, h2, mod, w_ffn1[0].astype(BF16), w_ffn3[0].astype(BF16), w_ffn_conv[0],
                  w_ffn2[0].astype(BF16), row(norm_f))

    new_s = jnp.transpose(s_fin.reshape(BATCH, N_DIRS, HEAD_DIM, N_HEADS, HEAD_DIM),
                          (0, 1, 3, 2, 4)).reshape(BATCH, 1, N_DIRS, N_HEADS, HEAD_DIM, HEAD_DIM)
    return (yp.reshape(BATCH, SEQ, D_MODEL), ys.reshape(DEC_BATCH, DEC_SEQ, D_MODEL), new_k, new_v, new_s)
```

```python
import jax
import jax.numpy as jnp
from jax import lax
from jax.experimental import pallas as pl
from jax.experimental.pallas import tpu as pltpu

F32 = jnp.float32
BF16 = jnp.bfloat16
HIGHEST = lax.Precision.HIGHEST

D_MODEL = 1024
BATCH = 16
SEQ = 256
DEC_BATCH = 2
DEC_SEQ = 2048
PAST_LEN = 512
GRID_W = 64
HEAD_DIM = 64
N_HEADS = 8
WIDTH = N_HEADS * HEAD_DIM
PAIR_W = 2 * HEAD_DIM
NA_ROWS = 8
NA_COLS = 16
N_DIRS = 2
LORA = 64
GATE_LORA = 128
D_FF = 2816
EPS = 1e-6
GN_EPS = 64e-5
ATT_SCALE = HEAD_DIM ** -0.5
RWKV_COLS = 3 * WIDTH + N_DIRS * 2 * LORA + GATE_LORA
MASK_VALUE = -1e30

N_CTX_TOK = BATCH * SEQ
N_LAT_TOK = DEC_BATCH * DEC_SEQ
N_TOK = N_CTX_TOK + N_LAT_TOK
TOK_BLOCK = 256
N_BLOCKS = N_TOK // TOK_BLOCK
BIG_BLOCK = 512
N_SEQS = BATCH + DEC_BATCH
CHUNK = 64
N_CHUNKS = N_TOK // CHUNK
CTX_CHUNKS = N_CTX_TOK // CHUNK
CHUNKS_PER_CTX_SEQ = SEQ // CHUNK
CHUNKS_PER_LAT_SEQ = DEC_SEQ // CHUNK
GRID_ROWS = DEC_SEQ // GRID_W
LAT_ROWS = 2
FF_TILE = D_FF // 2
GROUP = 4
GROUP_W = GROUP * HEAD_DIM
VMEM_LIMIT = 56 * 1024 * 1024


def _params(n_axes, limit=VMEM_LIMIT):
    return pltpu.CompilerParams(dimension_semantics=("arbitrary",) * n_axes, vmem_limit_bytes=limit)


def _const_spec(shape):
    zeros = (0,) * len(shape)
    return pl.BlockSpec(shape, lambda *_: zeros, pipeline_mode=pl.Buffered(1))


def _tok_spec(cols, blk=TOK_BLOCK):
    return pl.BlockSpec((blk, cols), lambda i: (i, 0))


def _ctx_tok_spec(cols, blk=TOK_BLOCK):
    return pl.BlockSpec((blk, cols), lambda i: (jnp.minimum(i, N_CTX_TOK // blk - 1), 0))


def _lat_tok_spec(cols, blk=TOK_BLOCK):
    return pl.BlockSpec((blk, cols), lambda i: (jnp.maximum(i - N_CTX_TOK // blk, 0), 0))


def _ctx_or_lat(i, ctx_ref, lat_ref, blk=TOK_BLOCK):
    return jnp.where(i < N_CTX_TOK // blk, ctx_ref[...], lat_ref[...])


def _mod_spec(blk=TOK_BLOCK):
    def row(i):
        return jnp.where(i < N_CTX_TOK // blk, 0, 1 + (i - N_CTX_TOK // blk) // (DEC_SEQ // blk))
    return pl.BlockSpec((1, 6, D_MODEL), lambda i: (row(i), 0, 0))


def _seq_neighbours(i, blk=TOK_BLOCK):
    per_seq = DEC_SEQ // blk
    j = (i - N_CTX_TOK // blk) % per_seq
    lat = i >= N_CTX_TOK // blk
    return lat & (j != 0), lat & (j != per_seq - 1)


def _silu(x):
    return x * jax.nn.sigmoid(x)


def _softplus(x):
    return jnp.maximum(x, 0.0) + jnp.log1p(jnp.exp(-jnp.abs(x)))


def _rmsnorm(x, g):
    return x * lax.rsqrt(jnp.mean(x * x, axis=-1, keepdims=True) + EPS) * g


def _dot(a, b, precision=None):
    return jnp.dot(a, b, precision=precision, preferred_element_type=F32)


def _dot_nt(a, b, precision=None):
    return lax.dot_general(a, b, (((1,), (1,)), ((), ())), precision=precision, preferred_element_type=F32)


def _split(x):
    hi = x.astype(BF16)
    return hi, (x - hi.astype(F32)).astype(BF16)


def _mm3(dot, a, b):
    m = a[0].shape[0]
    hi = dot(jnp.concatenate([a[0], a[1]], axis=0), b[0])
    return hi[:m] + hi[m:] + dot(a[0], b[1])


def _mm(a, b):
    return _mm3(_dot, a, b)


def _mm_nt(a, b):
    return _mm3(_dot_nt, a, b)


def _segsum(x, ones_bf):
    m = x.shape[0]
    hi, lo = _split(x)
    s = _dot(jnp.concatenate([hi, lo], axis=0), ones_bf)
    return s[:m] + s[m:]


def _shifted(a, prev_row, next_row):
    t = a.shape[0]
    rid = lax.broadcasted_iota(jnp.int32, (t, 1), 0)
    prev = jnp.where(rid == 0, prev_row, pltpu.roll(a, 1, axis=0))
    nxt = jnp.where(rid == t - 1, next_row, pltpu.roll(a, t - 1, axis=0))
    return prev, nxt


def _mod_kernel(cond_ref, w_ref, b_ref, o_ref):
    o_ref[...] = _dot(_silu(cond_ref[...]), w_ref[...], HIGHEST) + b_ref[...]


def _modulation(cond, w_ada, b_ada):
    n = 6 * D_MODEL
    return pl.pallas_call(
        _mod_kernel,
        grid=(6,),
        in_specs=[pl.BlockSpec((8, D_MODEL), lambda j: (0, 0)),
                  pl.BlockSpec((D_MODEL, D_MODEL), lambda j: (0, j)),
                  pl.BlockSpec((1, D_MODEL), lambda j: (0, j))],
        out_specs=pl.BlockSpec((8, D_MODEL), lambda j: (0, j)),
        out_shape=jax.ShapeDtypeStruct((8, n), F32),
        compiler_params=_params(1),
        name="mod",
    )(cond, w_ada, b_ada.reshape(1, n))


def _inproj_kernel(xp_ref, xs_ref, xprev_ref, xnext_ref, mod_ref, n1_ref, w_ref, *rest):
    rwkv_refs, (q_ref, kc_ref, vc_ref, kl_ref, vl_ref), rwkv_outs, (nk_ref, nv_ref) = (
        rest[:10], rest[10:15], rest[15:19], rest[19:])
    i = pl.program_id(0)
    has_prev, has_next = _seq_neighbours(i)

    def modulated(x):
        h = _rmsnorm(x, n1_ref[...]) * (1.0 + mod_ref[0, 1:2, :]) + mod_ref[0, 0:1, :]
        return h.astype(BF16)

    h = modulated(_ctx_or_lat(i, xp_ref, xs_ref))
    edge = modulated(jnp.concatenate([xprev_ref[...], xnext_ref[...]], axis=0))
    lhs = jnp.concatenate([h, edge], axis=0)

    def zcols(c0, c1):
        z_all = _dot(lhs, w_ref[:, 3 * WIDTH + c0:3 * WIDTH + c1])
        return (z_all[:TOK_BLOCK],
                jnp.where(has_prev, z_all[TOK_BLOCK + 7:TOK_BLOCK + 8], 0.0),
                jnp.where(has_next, z_all[TOK_BLOCK + 8:TOK_BLOCK + 9], 0.0))

    kv = []

    def project_q():
        q_ref[...] = _dot(h, w_ref[:, 0:WIDTH])

    def project_k():
        kv.append(_dot(h, w_ref[:, WIDTH:2 * WIDTH]))

    def project_v():
        kv.append(_dot(h, w_ref[:, 2 * WIDTH:3 * WIDTH]))

    _rwkv_tokens(zcols, (project_q, project_k, project_v), *rwkv_refs, *rwkv_outs)
    k, v = kv

    @pl.when(i < N_CTX_TOK // TOK_BLOCK)
    def _():
        kc_ref[...] = k
        vc_ref[...] = v
        nk_ref[0, 0] = k.reshape(SEQ, N_HEADS, HEAD_DIM)
        nv_ref[0, 0] = v.reshape(SEQ, N_HEADS, HEAD_DIM)

    @pl.when(i >= N_CTX_TOK // TOK_BLOCK)
    def _():
        kl_ref[...] = k
        vl_ref[...] = v


def _inproj(xp, xs, mod, norm1, w_in_bf, w_ts, w0, w2, a0, a2, g2, k_k, k_a, r_k, ones):
    half = jax.ShapeDtypeStruct((N_CTX_TOK, WIDTH), F32)
    tok = jax.ShapeDtypeStruct((N_TOK, WIDTH), F32)
    lat0 = N_CTX_TOK // TOK_BLOCK
    rows8 = TOK_BLOCK // 8
    n8 = N_LAT_TOK // 8
    prev8 = pl.BlockSpec((8, D_MODEL), lambda i: (jnp.clip((i - lat0) * rows8 - 1, 0, n8 - 1), 0))
    next8 = pl.BlockSpec((8, D_MODEL), lambda i: (jnp.clip((i - lat0 + 1) * rows8, 0, n8 - 1), 0))
    cache = pl.BlockSpec((1, 1, SEQ, N_HEADS, HEAD_DIM), lambda i: (jnp.minimum(i, BATCH - 1), 0, 0, 0, 0))
    return pl.pallas_call(
        _inproj_kernel,
        grid=(N_BLOCKS,),
        in_specs=[_ctx_tok_spec(D_MODEL), _lat_tok_spec(D_MODEL), prev8, next8, _mod_spec(),
                  _const_spec((1, D_MODEL)),
                  _const_spec((D_MODEL, 3 * WIDTH + RWKV_COLS)),
                  _const_spec((3, RWKV_COLS)),
                  _const_spec((N_DIRS, WIDTH)), _const_spec((N_DIRS, LORA, WIDTH)),
                  _const_spec((N_DIRS, WIDTH)), _const_spec((N_DIRS, LORA, WIDTH)),
                  _const_spec((GATE_LORA, WIDTH)),
                  _const_spec((1, WIDTH)), _const_spec((1, WIDTH)), _const_spec((1, WIDTH)),
                  _const_spec((WIDTH, WIDTH))],
        out_specs=[_tok_spec(WIDTH), _ctx_tok_spec(WIDTH), _ctx_tok_spec(WIDTH),
                   _lat_tok_spec(WIDTH), _lat_tok_spec(WIDTH),
                   _tok_spec(3 * WIDTH), _tok_spec(WIDTH), _tok_spec(WIDTH),
                   pl.BlockSpec((N_DIRS, TOK_BLOCK, 3 * WIDTH), lambda i: (0, i, 0)),
                   cache, cache],
        out_shape=[tok, half, half, half, half,
                   jax.ShapeDtypeStruct((N_TOK, 3 * WIDTH), F32), tok, tok,
                   jax.ShapeDtypeStruct((N_DIRS, N_TOK, 3 * WIDTH), F32),
                   jax.ShapeDtypeStruct((BATCH, 1, SEQ, N_HEADS, HEAD_DIM), F32),
                   jax.ShapeDtypeStruct((BATCH, 1, SEQ, N_HEADS, HEAD_DIM), F32)],
        compiler_params=_params(1),
        name="inproj",
    )(xp, xs, xs, xs, mod, norm1, w_in_bf, w_ts, w0, w2, a0, a2, g2, k_k, k_a, r_k, ones)


def _pair_queries(q):
    lo_half = lax.broadcasted_iota(jnp.int32, q.shape, 1) < HEAD_DIM
    return jnp.concatenate([jnp.where(lo_half, q, 0.0), jnp.where(lo_half, 0.0, q)], axis=0).astype(BF16)


def _pair_outputs(o):
    t = o.shape[0] // 2
    lo_half = lax.broadcasted_iota(jnp.int32, (t, PAIR_W), 1) < HEAD_DIM
    return jnp.where(lo_half, o[:t], o[t:])


def _ctx_attn_kernel(q_ref, k_ref, v_ref, o_ref):
    pairs = [slice(p * PAIR_W, (p + 1) * PAIR_W) for p in range(N_HEADS // 2)]
    scores = [_dot_nt(_pair_queries(q_ref[:, ps]), k_ref[:, ps].astype(BF16)) * ATT_SCALE for ps in pairs]
    probs = []
    for s in scores:
        e = jnp.exp(s - jnp.max(s, axis=-1, keepdims=True))
        probs.append((e.astype(BF16), jnp.sum(e, axis=-1, keepdims=True)))
    for ps, (e, l) in zip(pairs, probs):
        o_ref[:, ps] = _pair_outputs(_dot(e, v_ref[:, ps].astype(BF16)) / l).astype(BF16)


def _ctx_attention(q, k, v):
    spec = pl.BlockSpec((SEQ, WIDTH), lambda b: (b, 0))
    return pl.pallas_call(
        _ctx_attn_kernel,
        grid=(BATCH,),
        in_specs=[spec, spec, spec],
        out_specs=spec,
        out_shape=jax.ShapeDtypeStruct((N_CTX_TOK, WIDTH), BF16),
        compiler_params=_params(1),
        name="ctxattn",
    )(q, k, v)


def _lat_attn_kernel(q_ref, k_ref, v_ref, ck_ref, cv_ref, bias_ref, o_ref, kbf, vbf, ckbf, cvbf):
    i = pl.program_id(1)

    @pl.when(i == 0)
    def _():
        kbf[...] = k_ref[...].astype(BF16)
        vbf[...] = v_ref[...].astype(BF16)
        ckbf[...] = ck_ref[0].astype(BF16)
        cvbf[...] = cv_ref[0].astype(BF16)

    win = NA_ROWS * GRID_W
    rows = []
    for rr in range(LAT_ROWS):
        gr = i * LAT_ROWS + rr
        first_row = jnp.clip(gr - NA_ROWS // 2, 0, GRID_ROWS - NA_ROWS)
        start = pl.multiple_of(first_row * GRID_W, GRID_W)
        off0 = first_row - gr + NA_ROWS - 1
        rows.append((slice(rr * GRID_W, (rr + 1) * GRID_W), start, off0))
    pairs = [slice(p * PAIR_W, (p + 1) * PAIR_W) for p in range(N_HEADS // 2)]
    scores = []
    for p, ps in enumerate(pairs):
        lhs = [_pair_queries(q_ref[rs, ps]) for rs, _, _ in rows]
        s_loc = []
        for (_, start, off0), lhs_r in zip(rows, lhs):
            bias = jnp.concatenate(
                [jnp.concatenate([bias_ref[2 * p + hh, off0 + 2 * j] for j in range(NA_ROWS // 2)], axis=-1)
                 for hh in range(2)], axis=0)
            s_loc.append(_dot_nt(lhs_r, kbf[pl.ds(start, win), ps]) * ATT_SCALE + bias)
        s_ctx = _dot_nt(jnp.concatenate(lhs, axis=0), ckbf[:, ps]) * ATT_SCALE
        scores.append((jnp.concatenate(s_loc, axis=0), s_ctx))
    probs = []
    for s_loc, s_ctx in scores:
        m = jnp.maximum(jnp.max(s_loc, axis=-1, keepdims=True), jnp.max(s_ctx, axis=-1, keepdims=True))
        e_loc = jnp.exp(s_loc - m)
        e_ctx = jnp.exp(s_ctx - m)
        l = jnp.sum(e_loc, axis=-1, keepdims=True) + jnp.sum(e_ctx, axis=-1, keepdims=True)
        probs.append((e_loc.astype(BF16), e_ctx.astype(BF16), l))
    for ps, (e_loc, e_ctx, l) in zip(pairs, probs):
        o_ctx = _dot(e_ctx, cvbf[:, ps])
        for rr, (rs, start, _) in enumerate(rows):
            both = slice(rr * 2 * GRID_W, (rr + 1) * 2 * GRID_W)
            o = _dot(e_loc[both], vbf[pl.ds(start, win), ps]) + o_ctx[both]
            o_ref[rs, ps] = _pair_outputs(o / l[both]).astype(BF16)


def _na_bias_table(rpb):
    h, d, n = rpb.shape
    w = GRID_W
    lead = w - NA_COLS
    p = jnp.pad(rpb.astype(F32), ((0, 0), (0, 0), (lead, 2 * w - lead - n)), constant_values=MASK_VALUE)
    flat = jnp.broadcast_to(p[:, :, None, :], (h, d, w, 2 * w)).reshape(h, d, 2 * w * w)
    t = flat[:, :, :w * (2 * w - 1)].reshape(h, d, w, 2 * w - 1)[:, :, :, w - 1:]
    cq = jnp.arange(w)[:, None]
    ck = jnp.arange(w)[None, :]
    cs = jnp.clip(cq - NA_COLS // 2, 0, w - NA_COLS)
    t = jnp.where((ck >= cs) & (ck < cs + NA_COLS), t, MASK_VALUE)
    return jnp.concatenate([t[:, :-1], t[:, 1:]], axis=-1)


def _lat_attention(q, k, v, cache_k, cache_v, bias):
    blk = LAT_ROWS * GRID_W
    steps = GRID_ROWS // LAT_ROWS
    lat0 = N_CTX_TOK // blk
    seq = pl.BlockSpec((DEC_SEQ, WIDTH), lambda b, i: (b, 0))
    ctx = pl.BlockSpec((1, PAST_LEN, WIDTH), lambda b, i: (b, 0, 0))
    return pl.pallas_call(
        _lat_attn_kernel,
        grid=(DEC_BATCH, steps),
        in_specs=[pl.BlockSpec((blk, WIDTH), lambda b, i: (lat0 + b * steps + i, 0)),
                  seq, seq, ctx, ctx,
                  _const_spec((N_HEADS, 2 * NA_ROWS - 2, GRID_W, PAIR_W))],
        out_specs=pl.BlockSpec((blk, WIDTH), lambda b, i: (b * steps + i, 0)),
        out_shape=jax.ShapeDtypeStruct((N_LAT_TOK, WIDTH), BF16),
        scratch_shapes=[pltpu.VMEM((DEC_SEQ, WIDTH), BF16), pltpu.VMEM((DEC_SEQ, WIDTH), BF16),
                        pltpu.VMEM((PAST_LEN, WIDTH), BF16), pltpu.VMEM((PAST_LEN, WIDTH), BF16)],
        compiler_params=_params(2),
        name="latattn",
    )(q, k, v, cache_k, cache_v, bias)


def _rwkv_tokens(zcols, fillers, wts_ref, w0_ref, w2_ref, a0_ref, a2_ref, g2_ref, kk_ref, ka_ref, rk_ref, ones_ref,
                 shared_out, g_out, bonus_out, dir_out):
    def conv(c0, c1):
        zc, prow, nrow = zcols(c0, c1)
        zprev, znext = _shifted(zc, prow, nrow)
        return zprev * wts_ref[0:1, c0:c1] + zc * wts_ref[1:2, c0:c1] + znext * wts_ref[2:3, c0:c1]

    fill = iter(fillers)
    dirs = [slice(d * LORA, (d + 1) * LORA) for d in range(N_DIRS)]
    ones = ones_ref[...]
    o = 3 * WIDTH
    lora_in = conv(o, RWKV_COLS)
    kr = conv(WIDTH, 2 * WIDTH)
    tw = jnp.tanh(lora_in[:, 0:N_DIRS * LORA])
    xa = lora_in[:, N_DIRS * LORA:2 * N_DIRS * LORA]
    xg = lora_in[:, 2 * N_DIRS * LORA:]
    kkraw = kr * kk_ref[...]
    next(fill)()
    kk = kkraw * lax.rsqrt(_segsum(kkraw * kkraw, ones) + 1e-12)
    lora_w = [_mm(_split(tw[:, ls]), _split(w2_ref[d])) for d, ls in enumerate(dirs)]
    r = conv(0, WIDTH)
    next(fill)()
    lora_a = [_mm(_split(xa[:, ls]), _split(a2_ref[d])) for d, ls in enumerate(dirs)]
    g_out[...] = _mm(_split(jax.nn.sigmoid(xg)), _split(g2_ref[...]))
    v = conv(2 * WIDTH, 3 * WIDTH)
    next(fill)()
    kd_sum = jnp.zeros_like(kr)
    for d in range(N_DIRS):
        w_log = -_softplus(-(w0_ref[d:d + 1, :] + lora_w[d])) - 0.5
        a = jax.nn.sigmoid(a0_ref[d:d + 1, :] + lora_a[d])
        kd = kr * (1.0 + (a - 1.0) * ka_ref[...])
        dir_out[d, :, 0:WIDTH] = -jnp.exp(w_log)
        dir_out[d, :, WIDTH:2 * WIDTH] = kk * a
        dir_out[d, :, 2 * WIDTH:] = kd
        kd_sum = kd_sum + kd
    shared_out[:, 0:WIDTH] = r
    shared_out[:, WIDTH:2 * WIDTH] = kk
    shared_out[:, 2 * WIDTH:] = v
    bonus_out[...] = _segsum(r * kd_sum * rk_ref[...], ones) * v


HALF = CHUNK // 2


def _blockdiag(x, block=HEAD_DIM):
    def one(p):
        blk = lax.broadcasted_iota(jnp.int32, p.shape, 1) // block
        return jnp.concatenate([jnp.where(blk == q, p, jnp.zeros_like(p)) for q in range(GROUP_W // block)], axis=0)
    return one(x[0]), one(x[1])


def _head_transpose(x):
    xt = jnp.transpose(x)
    return jnp.concatenate([xt[h * HEAD_DIM:(h + 1) * HEAD_DIM] for h in range(GROUP)], axis=1)


def _scan_prepare(d, shared_ref, dir_ref, y_ref, s_ref):
    row = lax.broadcasted_iota(jnp.int32, (CHUNK, GROUP_W), 0)
    lane = lax.broadcasted_iota(jnp.int32, (CHUNK, GROUP_W), 1)
    col = lane % CHUNK
    strict = col > row if d else col < row
    incl = col >= row if d else col <= row
    row_h = lax.broadcasted_iota(jnp.int32, (HALF, GROUP_W), 0)
    lane_h = lax.broadcasted_iota(jnp.int32, (HALF, GROUP_W), 1)
    lo_s = lane_h % CHUNK < HALF
    eye_half = (lane_h % HALF == row_h).astype(F32)

    lw = dir_ref[0, :, 0:WIDTH]
    mask = jnp.where(incl[:, :CHUNK], 1.0, 0.0).astype(BF16)
    lw_hi, lw_mid = _split(lw)
    lw_lo = (lw - lw_hi.astype(F32) - lw_mid.astype(F32)).astype(BF16)
    cum = _dot(mask, lw_hi) + _dot(mask, lw_mid) + _dot(mask, lw_lo)
    tot = jnp.sum(lw, axis=0, keepdims=True)
    p_in = jnp.exp(cum)
    p_neg = jnp.exp(-cum)
    p_rem = jnp.exp(tot - cum)
    kt = shared_ref[:, WIDTH:2 * WIDTH] * jnp.exp(cum - lw)
    rt = shared_ref[:, 0:WIDTH] * p_in
    b = dir_ref[0, :, WIDTH:2 * WIDTH]
    kd = dir_ref[0, :, 2 * WIDTH:]
    bt = b * p_neg
    kdt = kd * p_neg
    bh = b * p_rem
    kh = kd * p_rem
    p_end = jnp.exp(tot)
    v = shared_ref[:, 2 * WIDTH:]

    chains = []
    for gi in range(N_HEADS // GROUP):
        sl = slice(gi * GROUP_W, (gi + 1) * GROUP_W)
        chains.append(dict(d=d, sl=sl, strict=strict, incl=incl, lo_s=lo_s, eye_half=eye_half, y_ref=y_ref, s_ref=s_ref,
                           kt=kt[:, sl], rt=rt[:, sl], v=v[:, sl], bt=bt[:, sl], kdt=kdt[:, sl],
                           kh=kh[:, sl], bh=bh[:, sl], p_end=p_end[:, sl]))
    return chains


def _scan_chains(chains):
    zeros = jnp.zeros((HALF, GROUP_W), F32)
    for c in chains:
        lhs = _split(jnp.concatenate([c["kt"], c["rt"]], axis=0))
        g_b = _mm_nt(lhs, _blockdiag(_split(c["bt"])))
        g_k = _mm_nt(lhs, _blockdiag(_split(c["kdt"])))
        c["a_k"] = jnp.where(c["strict"], g_k[:CHUNK], 0.0)
        c["a_rb"] = jnp.where(c["incl"], g_b[CHUNK:], 0.0)
        c["a_rk"] = jnp.where(c["incl"], g_k[CHUNK:], 0.0)
        n = jnp.where(c["strict"], -g_b[:CHUNK], 0.0)
        c["n"] = n
        c["pw"] = jnp.where(c["lo_s"], n[:HALF], n[HALF:])
        c["tp"] = c["eye_half"] + c["pw"]
    for c in chains:
        c["pw"] = _mm(_split(c["pw"]), _blockdiag(_split(c["pw"]), HALF))
    for _ in range(3):
        for c in chains:
            both = _mm(_split(jnp.concatenate([c["tp"], c["pw"]], axis=0)), _blockdiag(_split(c["pw"]), HALF))
            c["tp"] = c["tp"] + both[:HALF]
            c["pw"] = both[HALF:]
    for c in chains:
        c["tp"] = c["tp"] + _mm(_split(c["tp"]), _blockdiag(_split(c["pw"]), HALF))
    for c in chains:
        c["t1"] = jnp.where(c["lo_s"], c["tp"], 0.0)
        c["t2"] = jnp.where(c["lo_s"], 0.0, c["tp"])
        if c["d"] == 0:
            off, inner = jnp.where(c["lo_s"], c["n"][HALF:], 0.0), jnp.concatenate([c["t1"], zeros], axis=0)
        else:
            off, inner = jnp.where(c["lo_s"], 0.0, c["n"][:HALF]), jnp.concatenate([zeros, c["t2"]], axis=0)
        c["off"] = _mm(_split(off), _blockdiag(_split(inner)))
    for c in chains:
        if c["d"] == 0:
            x = _mm(_split(c["t2"]), _blockdiag(_split(jnp.concatenate([zeros, c["off"]], axis=0))))
            c["inv"] = jnp.concatenate([c["t1"], jnp.where(c["lo_s"], x, c["tp"])], axis=0)
        else:
            x = _mm(_split(c["t1"]), _blockdiag(_split(jnp.concatenate([c["off"], zeros], axis=0))))
            c["inv"] = jnp.concatenate([jnp.where(c["lo_s"], c["tp"], x), c["t2"]], axis=0)
    for c in chains:
        c["av"] = _mm(_split(jnp.concatenate([c["a_k"], c["a_rk"]], axis=0)), _blockdiag(_split(c["v"])))
        c["vt"] = _head_transpose(c["v"])
    for c in chains:
        c["s"] = c["s_ref"][0, :, c["sl"]]
        c["z"] = _mm_nt(_split(jnp.concatenate([c["rt"], c["kt"]], axis=0)), _blockdiag(_split(c["s"])))
    for c in chains:
        c["u"] = _mm(_split(c["inv"]), _blockdiag(_split(c["z"][CHUNK:] + c["av"][:CHUNK])))
    for c in chains:
        c["y_ref"][:, c["sl"]] = (c["z"][:CHUNK] + c["av"][CHUNK:]
                                  - _mm(_split(c["a_rb"]), _blockdiag(_split(c["u"]))))
    for c in chains:
        upd = (_mm(_split(c["vt"]), _blockdiag(_split(c["kh"])))
               - _mm(_split(_head_transpose(c["u"])), _blockdiag(_split(c["bh"]))))
        c["s_ref"][0, :, c["sl"]] = c["s"] * c["p_end"] + upd


REGION_CHUNKS = CTX_CHUNKS
SCAN_STREAMS = tuple((d, base, per_seq) for d in range(N_DIRS)
                     for base, per_seq in ((0, CHUNKS_PER_CTX_SEQ), (CTX_CHUNKS, CHUNKS_PER_LAT_SEQ)))
STREAM_INS = 3


def _stream_local_chunk(stream, j):
    return REGION_CHUNKS - 1 - j if stream[0] else j


def _scan_kernel(*refs):
    n_in = STREAM_INS * len(SCAN_STREAMS)
    j = pl.program_id(0)
    per_stream = []
    for s, stream in enumerate(SCAN_STREAMS):
        d, _, per_seq = stream
        ins = refs[STREAM_INS * s:STREAM_INS * (s + 1)]
        y_ref, s_ref = refs[n_in + 2 * s:n_in + 2 * s + 2]
        s0_ref = ins[-1]
        local = _stream_local_chunk(stream, j) % per_seq

        @pl.when(local == (per_seq - 1 if d else 0))
        def _():
            s_ref[...] = s0_ref[0]

        per_stream.append((d,) + ins[:-1] + (y_ref, s_ref))

    chains = []
    for args in per_stream:
        chains += _scan_prepare(*args)
    _scan_chains(chains)


def _scan(shared, perdir, s0):
    in_specs, out_specs, out_shape, operands = [], [], [], []
    for stream in SCAN_STREAMS:
        d, base, per_seq = stream
        seq0 = 0 if base == 0 else BATCH
        n_seq = REGION_CHUNKS // per_seq
        loc = lambda j, stream=stream: _stream_local_chunk(stream, j)
        tok = pl.BlockSpec((CHUNK, 3 * WIDTH), lambda j, loc=loc, base=base: (base + loc(j), 0))
        dtok = pl.BlockSpec((1, CHUNK, 3 * WIDTH), lambda j, loc=loc, base=base, d=d: (d, base + loc(j), 0))
        st_in = pl.BlockSpec((1, 1, HEAD_DIM, WIDTH),
                             lambda j, loc=loc, seq0=seq0, per_seq=per_seq, d=d: (seq0 + loc(j) // per_seq, d, 0, 0))
        in_specs += [tok, dtok, st_in]
        operands += [shared, perdir, s0]
        out_specs += [pl.BlockSpec((CHUNK, WIDTH), lambda j, loc=loc: (loc(j), 0)),
                      pl.BlockSpec((1, HEAD_DIM, WIDTH), lambda j, loc=loc, per_seq=per_seq: (loc(j) // per_seq, 0, 0))]
        out_shape += [jax.ShapeDtypeStruct((REGION_CHUNKS * CHUNK, WIDTH), F32),
                      jax.ShapeDtypeStruct((n_seq, HEAD_DIM, WIDTH), F32)]
    return pl.pallas_call(
        _scan_kernel,
        grid=(REGION_CHUNKS,),
        in_specs=in_specs,
        out_specs=out_specs,
        out_shape=out_shape,
        compiler_params=_params(1),
        name="scan",
    )(*operands)


def _post_kernel(yfc_ref, yfl_ref, ybc_ref, ybl_ref, bonus_ref, g_ref, attc_ref, attl_ref, xp_ref, xs_ref, mod_ref,
                 lng_ref, lnb_ref, ones_ref, wout_ref, n2_ref, x1_ref, h2_ref):
    i = pl.program_id(0)
    def pick(ctx_ref, lat_ref):
        return _ctx_or_lat(i, ctx_ref, lat_ref, BIG_BLOCK)

    ones = ones_ref[...]
    inv_n = 1.0 / HEAD_DIM
    y = pick(yfc_ref, yfl_ref) + pick(ybc_ref, ybl_ref)
    yc = y - _segsum(y, ones) * inv_n
    var = _segsum(yc * yc, ones) * inv_n
    yn = yc * lax.rsqrt(var + GN_EPS) * lng_ref[...] + lnb_ref[...]
    r_out = ((yn + bonus_ref[...]) * g_ref[...]).astype(BF16)
    o = _dot(pick(attc_ref, attl_ref), wout_ref[0:WIDTH, :]) + _dot(r_out, wout_ref[WIDTH:, :])
    x1 = pick(xp_ref, xs_ref) + mod_ref[0, 2:3, :] * o
    x1_ref[...] = x1
    h2 = _rmsnorm(x1, n2_ref[...]) * (1.0 + mod_ref[0, 4:5, :]) + mod_ref[0, 3:4, :]
    h2_ref[...] = h2.astype(BF16)


def _post(ys, bonus, g, att_ctx, att_lat, xp, xs, mod, ln_g, ln_b, ones, w_out_bf, norm2):
    blk = BIG_BLOCK
    tok = lambda cols: _tok_spec(cols, blk)
    ctx_lat = [_ctx_tok_spec(WIDTH, blk), _lat_tok_spec(WIDTH, blk)]
    return pl.pallas_call(
        _post_kernel,
        grid=(N_TOK // blk,),
        in_specs=ctx_lat + ctx_lat + [tok(WIDTH), tok(WIDTH)]
        + ctx_lat + [_ctx_tok_spec(D_MODEL, blk), _lat_tok_spec(D_MODEL, blk),
                  _mod_spec(blk),
                  _const_spec((1, WIDTH)), _const_spec((1, WIDTH)), _const_spec((WIDTH, WIDTH)),
                  _const_spec((2 * WIDTH, D_MODEL)), _const_spec((1, D_MODEL))],
        out_specs=[tok(D_MODEL), tok(D_MODEL)],
        out_shape=[jax.ShapeDtypeStruct((N_TOK, D_MODEL), F32),
                   jax.ShapeDtypeStruct((N_TOK, D_MODEL), BF16)],
        compiler_params=_params(1),
        name="post",
    )(*ys, bonus, g, att_ctx, att_lat, xp, xs, mod, ln_g, ln_b, ones, w_out_bf, norm2)


def _ffn_kernel(x1_ref, h2_ref, hp_ref, hn_ref, mod_ref, w1_ref, w3_ref, wc_ref, w2_ref, nf_ref, yp_ref, ys_ref):
    i = pl.program_id(0)
    blk = BIG_BLOCK
    is_ctx = i < N_CTX_TOK // blk
    has_prev, has_next = _seq_neighbours(i, blk)
    lhs = jnp.concatenate([h2_ref[...], hp_ref[...], hn_ref[...]], axis=0)
    h2 = lhs[:blk]
    rid = lax.broadcasted_iota(jnp.int32, (blk, 1), 0)
    inner = jnp.where(is_ctx, SEQ, -1)
    seq_start = rid == inner
    seq_end = rid == inner - 1
    acc = jnp.zeros((blk, D_MODEL), F32)
    for f in range(D_FF // FF_TILE):
        fs = slice(f * FF_TILE, (f + 1) * FF_TILE)
        a_all = _dot(lhs, w1_ref[:, fs])
        a = a_all[:blk]
        prow = jnp.where(has_prev, a_all[blk + 15:blk + 16, :], 0.0)
        nrow = jnp.where(has_next, a_all[blk + 16:blk + 17, :], 0.0)
        aprev, anext = _shifted(a, prow, nrow)
        aprev = jnp.where(seq_start, 0.0, aprev)
        anext = jnp.where(seq_end, 0.0, anext)
        cv = aprev * wc_ref[0:1, fs] + a * wc_ref[1:2, fs] + anext * wc_ref[2:3, fs]
        act = _silu(cv) * _dot(h2, w3_ref[:, fs])
        acc = acc + _dot(act.astype(BF16), w2_ref[fs, :])
    x2 = x1_ref[...] + mod_ref[0, 5:6, :] * acc
    y = _rmsnorm(x2, nf_ref[...])

    @pl.when(is_ctx)
    def _():
        yp_ref[...] = y

    @pl.when(jnp.logical_not(is_ctx))
    def _():
        ys_ref[...] = y


def _ffn(x1, h2, mod, w1_bf, w3_bf, wc, w2_bf, norm_f):
    blk = BIG_BLOCK
    rows16 = blk // 16
    return pl.pallas_call(
        _ffn_kernel,
        grid=(N_TOK // blk,),
        in_specs=[_tok_spec(D_MODEL, blk), _tok_spec(D_MODEL, blk),
                  pl.BlockSpec((16, D_MODEL), lambda i: (jnp.maximum(i * rows16 - 1, 0), 0)),
                  pl.BlockSpec((16, D_MODEL), lambda i: (jnp.minimum((i + 1) * rows16, N_TOK // 16 - 1), 0)),
                  _mod_spec(blk),
                  _const_spec((D_MODEL, D_FF)), _const_spec((D_MODEL, D_FF)), _const_spec((3, D_FF)),
                  _const_spec((D_FF, D_MODEL)), _const_spec((1, D_MODEL))],
        out_specs=[_ctx_tok_spec(D_MODEL, blk), _lat_tok_spec(D_MODEL, blk)],
        out_shape=[jax.ShapeDtypeStruct((N_CTX_TOK, D_MODEL), F32),
                   jax.ShapeDtypeStruct((N_LAT_TOK, D_MODEL), F32)],
        compiler_params=_params(1),
        name="ffn",
    )(x1, h2, h2, h2, mod, w1_bf, w3_bf, wc, w2_bf, norm_f)


def kernel(x_prompt, x_sample, cache_k, cache_v, state_rwkv, c, c_ctx, w_ada, b_ada, norm1, norm2,
           w_in, w_ts, w0, w2, a0, a2, g2, k_k, k_a, r_k, ln_x_g, ln_x_b, rpb, w_out,
           w_ffn1, w_ffn3, w_ffn_conv, w_ffn2, norm_f):
    xp = x_prompt.reshape(N_CTX_TOK, D_MODEL)
    xs = x_sample.reshape(N_LAT_TOK, D_MODEL)
    row = lambda t: t.reshape(1, -1)
    cond = jnp.concatenate([c_ctx[None, :], c, jnp.zeros((8 - 1 - DEC_BATCH, D_MODEL), F32)], axis=0)
    mod = _modulation(cond, w_ada[0], b_ada[0]).reshape(8, 6, D_MODEL)

    head_id = jnp.arange(WIDTH) // HEAD_DIM
    ones = (head_id[:, None] == head_id[None, :]).astype(BF16)
    q, k_ctx, v_ctx, k_lat, v_lat, shared, g, bonus, perdir, new_k, new_v = _inproj(
        xp, xs, mod, row(norm1[0]), w_in[0].astype(BF16), w_ts[0], w0[0], w2[0], a0[0], a2[0], g2[0],
        row(k_k[0]), row(k_a[0]), row(r_k[0]), ones)

    att_ctx = _ctx_attention(q, k_ctx, v_ctx)
    att_lat = _lat_attention(q, k_lat, v_lat, cache_k[:, 0].reshape(DEC_BATCH, PAST_LEN, WIDTH),
                             cache_v[:, 0].reshape(DEC_BATCH, PAST_LEN, WIDTH), _na_bias_table(rpb[0]))
    s_lat = jnp.transpose(state_rwkv[:, 0], (0, 1, 3, 2, 4)).reshape(DEC_BATCH, N_DIRS, HEAD_DIM, WIDTH)
    s0 = jnp.concatenate([jnp.zeros((BATCH, N_DIRS, HEAD_DIM, WIDTH), F32), s_lat], axis=0)
    y_fc, s_fc, y_fl, _, y_bc, s_bc, y_bl, _ = _scan(shared, perdir, s0)
    s_fin = jnp.stack([s_fc, s_bc], axis=1)

    x1, h2 = _post((y_fc, y_fl, y_bc, y_bl), bonus, g, att_ctx, att_lat, xp, xs, mod, row(ln_x_g[0]), row(ln_x_b[0]), ones,
                   w_out[0].astype(BF16), row(norm2[0]))
    yp, ys = _ffn(x1, h2, mod, w_ffn1[0].astype(BF16), w_ffn3[0].astype(BF16), w_ffn_conv[0],
                  w_ffn2[0].astype(BF16), row(norm_f))

    new_s = jnp.transpose(s_fin.reshape(BATCH, N_DIRS, HEAD_DIM, N_HEADS, HEAD_DIM),
                          (0, 1, 3, 2, 4)).reshape(BATCH, 1, N_DIRS, N_HEADS, HEAD_DIM, HEAD_DIM)
    return (yp.reshape(BATCH, SEQ, D_MODEL), ys.reshape(DEC_BATCH, DEC_SEQ, D_MODEL), new_k, new_v, new_s)
```

```python
import functools

import jax
import jax.numpy as jnp
from jax import lax
from jax.experimental import pallas as pl
from jax.experimental.pallas import tpu as pltpu

F32 = jnp.float32
BF16 = jnp.bfloat16
HIGHEST = lax.Precision.HIGHEST

D_MODEL = 1024
BATCH = 16
SEQ = 256
DEC_BATCH = 2
DEC_SEQ = 2048
PAST_LEN = 512
GRID_W = 64
HEAD_DIM = 64
N_HEADS = 8
WIDTH = N_HEADS * HEAD_DIM
PAIR_W = 2 * HEAD_DIM
NA_ROWS = 8
NA_COLS = 16
N_DIRS = 2
LORA = 64
GATE_LORA = 128
D_FF = 2816
EPS = 1e-6
GN_EPS = 64e-5
ATT_SCALE = HEAD_DIM ** -0.5
RWKV_COLS = 3 * WIDTH + N_DIRS * 2 * LORA + GATE_LORA
MASK_VALUE = -1e30

N_CTX_TOK = BATCH * SEQ
N_LAT_TOK = DEC_BATCH * DEC_SEQ
N_TOK = N_CTX_TOK + N_LAT_TOK
TOK_BLOCK = 256
N_BLOCKS = N_TOK // TOK_BLOCK
BIG_BLOCK = 512
N_SEQS = BATCH + DEC_BATCH
CHUNK = 64
N_CHUNKS = N_TOK // CHUNK
CTX_CHUNKS = N_CTX_TOK // CHUNK
CHUNKS_PER_CTX_SEQ = SEQ // CHUNK
CHUNKS_PER_LAT_SEQ = DEC_SEQ // CHUNK
GRID_ROWS = DEC_SEQ // GRID_W
LAT_ROWS = 2
FF_TILE = D_FF // 2
GROUP = 4
GROUP_W = GROUP * HEAD_DIM
VMEM_LIMIT = 56 * 1024 * 1024


def _params(n_axes, limit=VMEM_LIMIT):
    return pltpu.CompilerParams(dimension_semantics=("arbitrary",) * n_axes, vmem_limit_bytes=limit)


def _const_spec(shape):
    zeros = (0,) * len(shape)
    return pl.BlockSpec(shape, lambda *_: zeros, pipeline_mode=pl.Buffered(1))


def _tok_spec(cols, blk=TOK_BLOCK):
    return pl.BlockSpec((blk, cols), lambda i: (i, 0))


def _ctx_tok_spec(cols, blk=TOK_BLOCK):
    return pl.BlockSpec((blk, cols), lambda i: (jnp.minimum(i, N_CTX_TOK // blk - 1), 0))


def _lat_tok_spec(cols, blk=TOK_BLOCK):
    return pl.BlockSpec((blk, cols), lambda i: (jnp.maximum(i - N_CTX_TOK // blk, 0), 0))


def _ctx_or_lat(i, ctx_ref, lat_ref, blk=TOK_BLOCK):
    return jnp.where(i < N_CTX_TOK // blk, ctx_ref[...], lat_ref[...])


def _mod_spec(blk=TOK_BLOCK):
    def row(i):
        return jnp.where(i < N_CTX_TOK // blk, 0, 1 + (i - N_CTX_TOK // blk) // (DEC_SEQ // blk))
    return pl.BlockSpec((1, 6, D_MODEL), lambda i: (row(i), 0, 0))


def _seq_neighbours(i, blk=TOK_BLOCK):
    per_seq = DEC_SEQ // blk
    j = (i - N_CTX_TOK // blk) % per_seq
    lat = i >= N_CTX_TOK // blk
    return lat & (j != 0), lat & (j != per_seq - 1)


def _silu(x):
    return x * jax.nn.sigmoid(x)


def _softplus(x):
    return jnp.maximum(x, 0.0) + jnp.log1p(jnp.exp(-jnp.abs(x)))


def _rmsnorm(x, g):
    return x * lax.rsqrt(jnp.mean(x * x, axis=-1, keepdims=True) + EPS) * g


def _dot(a, b, precision=None):
    return jnp.dot(a, b, precision=precision, preferred_element_type=F32)


def _dot_nt(a, b, precision=None):
    return lax.dot_general(a, b, (((1,), (1,)), ((), ())), precision=precision, preferred_element_type=F32)


def _split(x):
    hi = x.astype(BF16)
    return hi, (x - hi.astype(F32)).astype(BF16)


def _mm3(dot, a, b):
    m = a[0].shape[0]
    hi = dot(jnp.concatenate([a[0], a[1]], axis=0), b[0])
    return hi[:m] + hi[m:] + dot(a[0], b[1])


def _mm(a, b):
    return _mm3(_dot, a, b)


def _mm_nt(a, b):
    return _mm3(_dot_nt, a, b)


def _segsum(x, ones_bf):
    m = x.shape[0]
    hi, lo = _split(x)
    s = _dot(jnp.concatenate([hi, lo], axis=0), ones_bf)
    return s[:m] + s[m:]


def _shifted(a, prev_row, next_row):
    t = a.shape[0]
    rid = lax.broadcasted_iota(jnp.int32, (t, 1), 0)
    prev = jnp.where(rid == 0, prev_row, pltpu.roll(a, 1, axis=0))
    nxt = jnp.where(rid == t - 1, next_row, pltpu.roll(a, t - 1, axis=0))
    return prev, nxt


def _mod_kernel(cond_ref, w_ref, b_ref, o_ref):
    o_ref[...] = _dot(_silu(cond_ref[...]), w_ref[...], HIGHEST) + b_ref[...]


def _modulation(cond, w_ada, b_ada):
    n = 6 * D_MODEL
    return pl.pallas_call(
        _mod_kernel,
        grid=(6,),
        in_specs=[pl.BlockSpec((8, D_MODEL), lambda j: (0, 0)),
                  pl.BlockSpec((D_MODEL, D_MODEL), lambda j: (0, j)),
                  pl.BlockSpec((1, D_MODEL), lambda j: (0, j))],
        out_specs=pl.BlockSpec((8, D_MODEL), lambda j: (0, j)),
        out_shape=jax.ShapeDtypeStruct((8, n), F32),
        compiler_params=_params(1),
        name="mod",
    )(cond, w_ada, b_ada.reshape(1, n))


def _inproj_kernel(xp_ref, xs_ref, xprev_ref, xnext_ref, mod_ref, n1_ref, w_ref, *rest):
    rwkv_refs, (q_ref, kc_ref, vc_ref, kl_ref, vl_ref), rwkv_outs, (nk_ref, nv_ref) = (
        rest[:10], rest[10:15], rest[15:19], rest[19:])
    i = pl.program_id(0)
    has_prev, has_next = _seq_neighbours(i)

    def modulated(x):
        h = _rmsnorm(x, n1_ref[...]) * (1.0 + mod_ref[0, 1:2, :]) + mod_ref[0, 0:1, :]
        return h.astype(BF16)

    h = modulated(_ctx_or_lat(i, xp_ref, xs_ref))
    edge = modulated(jnp.concatenate([xprev_ref[...], xnext_ref[...]], axis=0))
    lhs = jnp.concatenate([h, edge], axis=0)

    def zcols(c0, c1):
        z_all = _dot(lhs, w_ref[:, 3 * WIDTH + c0:3 * WIDTH + c1])
        return (z_all[:TOK_BLOCK],
                jnp.where(has_prev, z_all[TOK_BLOCK + 7:TOK_BLOCK + 8], 0.0),
                jnp.where(has_next, z_all[TOK_BLOCK + 8:TOK_BLOCK + 9], 0.0))

    kv = []

    def project_q():
        q_ref[...] = _dot(h, w_ref[:, 0:WIDTH])

    def project_k():
        kv.append(_dot(h, w_ref[:, WIDTH:2 * WIDTH]))

    def project_v():
        kv.append(_dot(h, w_ref[:, 2 * WIDTH:3 * WIDTH]))

    _rwkv_tokens(zcols, (project_q, project_k, project_v), *rwkv_refs, *rwkv_outs)
    k, v = kv

    @pl.when(i < N_CTX_TOK // TOK_BLOCK)
    def _():
        kc_ref[...] = k
        vc_ref[...] = v
        nk_ref[0, 0] = k.reshape(SEQ, N_HEADS, HEAD_DIM)
        nv_ref[0, 0] = v.reshape(SEQ, N_HEADS, HEAD_DIM)

    @pl.when(i >= N_CTX_TOK // TOK_BLOCK)
    def _():
        kl_ref[...] = k
        vl_ref[...] = v


def _inproj(xp, xs, mod, norm1, w_in_bf, w_ts, w0, w2, a0, a2, g2, k_k, k_a, r_k, ones):
    half = jax.ShapeDtypeStruct((N_CTX_TOK, WIDTH), F32)
    tok = jax.ShapeDtypeStruct((N_TOK, WIDTH), F32)
    lat0 = N_CTX_TOK // TOK_BLOCK
    rows8 = TOK_BLOCK // 8
    n8 = N_LAT_TOK // 8
    prev8 = pl.BlockSpec((8, D_MODEL), lambda i: (jnp.clip((i - lat0) * rows8 - 1, 0, n8 - 1), 0))
    next8 = pl.BlockSpec((8, D_MODEL), lambda i: (jnp.clip((i - lat0 + 1) * rows8, 0, n8 - 1), 0))
    cache = pl.BlockSpec((1, 1, SEQ, N_HEADS, HEAD_DIM), lambda i: (jnp.minimum(i, BATCH - 1), 0, 0, 0, 0))
    return pl.pallas_call(
        _inproj_kernel,
        grid=(N_BLOCKS,),
        in_specs=[_ctx_tok_spec(D_MODEL), _lat_tok_spec(D_MODEL), prev8, next8, _mod_spec(),
                  _const_spec((1, D_MODEL)),
                  _const_spec((D_MODEL, 3 * WIDTH + RWKV_COLS)),
                  _const_spec((3, RWKV_COLS)),
                  _const_spec((N_DIRS, WIDTH)), _const_spec((N_DIRS, LORA, WIDTH)),
                  _const_spec((N_DIRS, WIDTH)), _const_spec((N_DIRS, LORA, WIDTH)),
                  _const_spec((GATE_LORA, WIDTH)),
                  _const_spec((1, WIDTH)), _const_spec((1, WIDTH)), _const_spec((1, WIDTH)),
                  _const_spec((WIDTH, WIDTH))],
        out_specs=[_tok_spec(WIDTH), _ctx_tok_spec(WIDTH), _ctx_tok_spec(WIDTH),
                   _lat_tok_spec(WIDTH), _lat_tok_spec(WIDTH),
                   _tok_spec(3 * WIDTH), _tok_spec(WIDTH), _tok_spec(WIDTH),
                   pl.BlockSpec((N_DIRS, TOK_BLOCK, 3 * WIDTH), lambda i: (0, i, 0)),
                   cache, cache],
        out_shape=[tok, half, half, half, half,
                   jax.ShapeDtypeStruct((N_TOK, 3 * WIDTH), F32), tok, tok,
                   jax.ShapeDtypeStruct((N_DIRS, N_TOK, 3 * WIDTH), F32),
                   jax.ShapeDtypeStruct((BATCH, 1, SEQ, N_HEADS, HEAD_DIM), F32),
                   jax.ShapeDtypeStruct((BATCH, 1, SEQ, N_HEADS, HEAD_DIM), F32)],
        compiler_params=_params(1),
        name="inproj",
    )(xp, xs, xs, xs, mod, norm1, w_in_bf, w_ts, w0, w2, a0, a2, g2, k_k, k_a, r_k, ones)


def _pair_queries(q):
    lo_half = lax.broadcasted_iota(jnp.int32, q.shape, 1) < HEAD_DIM
    return jnp.concatenate([jnp.where(lo_half, q, 0.0), jnp.where(lo_half, 0.0, q)], axis=0).astype(BF16)


def _pair_outputs(o):
    t = o.shape[0] // 2
    lo_half = lax.broadcasted_iota(jnp.int32, (t, PAIR_W), 1) < HEAD_DIM
    return jnp.where(lo_half, o[:t], o[t:])


def _ctx_attn_kernel(q_ref, k_ref, v_ref, o_ref):
    pairs = [slice(p * PAIR_W, (p + 1) * PAIR_W) for p in range(N_HEADS // 2)]
    scores = [_dot_nt(_pair_queries(q_ref[:, ps]), k_ref[:, ps].astype(BF16)) * ATT_SCALE for ps in pairs]
    probs = []
    for s in scores:
        e = jnp.exp(s - jnp.max(s, axis=-1, keepdims=True))
        probs.append((e.astype(BF16), jnp.sum(e, axis=-1, keepdims=True)))
    for ps, (e, l) in zip(pairs, probs):
        o_ref[:, ps] = _pair_outputs(_dot(e, v_ref[:, ps].astype(BF16)) / l).astype(BF16)


def _ctx_attention(q, k, v):
    spec = pl.BlockSpec((SEQ, WIDTH), lambda b: (b, 0))
    return pl.pallas_call(
        _ctx_attn_kernel,
        grid=(BATCH,),
        in_specs=[spec, spec, spec],
        out_specs=spec,
        out_shape=jax.ShapeDtypeStruct((N_CTX_TOK, WIDTH), BF16),
        compiler_params=_params(1),
        name="ctxattn",
    )(q, k, v)


def _lat_attn_kernel(q_ref, k_ref, v_ref, ck_ref, cv_ref, bias_ref, o_ref, kbf, vbf, ckbf, cvbf):
    i = pl.program_id(1)

    @pl.when(i == 0)
    def _():
        kbf[...] = k_ref[...].astype(BF16)
        vbf[...] = v_ref[...].astype(BF16)
        ckbf[...] = ck_ref[0].astype(BF16)
        cvbf[...] = cv_ref[0].astype(BF16)

    win = NA_ROWS * GRID_W
    rows = []
    for rr in range(LAT_ROWS):
        gr = i * LAT_ROWS + rr
        first_row = jnp.clip(gr - NA_ROWS // 2, 0, GRID_ROWS - NA_ROWS)
        start = pl.multiple_of(first_row * GRID_W, GRID_W)
        off0 = first_row - gr + NA_ROWS - 1
        rows.append((slice(rr * GRID_W, (rr + 1) * GRID_W), start, off0))
    pairs = [slice(p * PAIR_W, (p + 1) * PAIR_W) for p in range(N_HEADS // 2)]
    scores = []
    for p, ps in enumerate(pairs):
        lhs = [_pair_queries(q_ref[rs, ps]) for rs, _, _ in rows]
        s_loc = []
        for (_, start, off0), lhs_r in zip(rows, lhs):
            bias = jnp.concatenate(
                [jnp.concatenate([bias_ref[2 * p + hh, off0 + 2 * j] for j in range(NA_ROWS // 2)], axis=-1)
                 for hh in range(2)], axis=0)
            s_loc.append(_dot_nt(lhs_r, kbf[pl.ds(start, win), ps]) * ATT_SCALE + bias)
        s_ctx = _dot_nt(jnp.concatenate(lhs, axis=0), ckbf[:, ps]) * ATT_SCALE
        scores.append((jnp.concatenate(s_loc, axis=0), s_ctx))
    probs = []
    for s_loc, s_ctx in scores:
        m = jnp.maximum(jnp.max(s_loc, axis=-1, keepdims=True), jnp.max(s_ctx, axis=-1, keepdims=True))
        e_loc = jnp.exp(s_loc - m)
        e_ctx = jnp.exp(s_ctx - m)
        l = jnp.sum(e_loc, axis=-1, keepdims=True) + jnp.sum(e_ctx, axis=-1, keepdims=True)
        probs.append((e_loc.astype(BF16), e_ctx.astype(BF16), l))
    for ps, (e_loc, e_ctx, l) in zip(pairs, probs):
        o_ctx = _dot(e_ctx, cvbf[:, ps])
        for rr, (rs, start, _) in enumerate(rows):
            both = slice(rr * 2 * GRID_W, (rr + 1) * 2 * GRID_W)
            o = _dot(e_loc[both], vbf[pl.ds(start, win), ps]) + o_ctx[both]
            o_ref[rs, ps] = _pair_outputs(o / l[both]).astype(BF16)


def _na_bias_table(rpb):
    n = rpb.shape[-1]
    w = GRID_W
    pair = jnp.concatenate([rpb[:, :-1], rpb[:, 1:]], axis=-1).astype(F32)
    x = jnp.arange(2 * n)
    y = jnp.arange(2 * w)
    cq = jnp.arange(w)
    ck = y % w
    sel = ((x // n)[:, None, None] == (y // w)[None, None, :]) & (
        (x % n)[:, None, None] == ck[None, None, :] - cq[None, :, None] + NA_COLS - 1)
    t = jnp.einsum("hdx,xqy->hdqy", pair, sel.astype(F32), precision=HIGHEST)
    cs = jnp.clip(cq - NA_COLS // 2, 0, w - NA_COLS)[:, None]
    return jnp.where((ck[None, :] >= cs) & (ck[None, :] < cs + NA_COLS), t, MASK_VALUE)


def _lat_attention(q, k, v, cache_k, cache_v, bias):
    blk = LAT_ROWS * GRID_W
    steps = GRID_ROWS // LAT_ROWS
    lat0 = N_CTX_TOK // blk
    seq = pl.BlockSpec((DEC_SEQ, WIDTH), lambda b, i: (b, 0))
    ctx = pl.BlockSpec((1, PAST_LEN, WIDTH), lambda b, i: (b, 0, 0))
    return pl.pallas_call(
        _lat_attn_kernel,
        grid=(DEC_BATCH, steps),
        in_specs=[pl.BlockSpec((blk, WIDTH), lambda b, i: (lat0 + b * steps + i, 0)),
                  seq, seq, ctx, ctx,
                  _const_spec((N_HEADS, 2 * NA_ROWS - 2, GRID_W, PAIR_W))],
        out_specs=pl.BlockSpec((blk, WIDTH), lambda b, i: (b * steps + i, 0)),
        out_shape=jax.ShapeDtypeStruct((N_LAT_TOK, WIDTH), BF16),
        scratch_shapes=[pltpu.VMEM((DEC_SEQ, WIDTH), BF16), pltpu.VMEM((DEC_SEQ, WIDTH), BF16),
                        pltpu.VMEM((PAST_LEN, WIDTH), BF16), pltpu.VMEM((PAST_LEN, WIDTH), BF16)],
        compiler_params=_params(2),
        name="latattn",
    )(q, k, v, cache_k, cache_v, bias)


def _rwkv_tokens(zcols, fillers, wts_ref, w0_ref, w2_ref, a0_ref, a2_ref, g2_ref, kk_ref, ka_ref, rk_ref, ones_ref,
                 shared_out, g_out, bonus_out, dir_out):
    def conv(c0, c1):
        zc, prow, nrow = zcols(c0, c1)
        zprev, znext = _shifted(zc, prow, nrow)
        return zprev * wts_ref[0:1, c0:c1] + zc * wts_ref[1:2, c0:c1] + znext * wts_ref[2:3, c0:c1]

    fill = iter(fillers)
    dirs = [slice(d * LORA, (d + 1) * LORA) for d in range(N_DIRS)]
    ones = ones_ref[...]
    o = 3 * WIDTH
    lora_in = conv(o, RWKV_COLS)
    kr = conv(WIDTH, 2 * WIDTH)
    tw = jnp.tanh(lora_in[:, 0:N_DIRS * LORA])
    xa = lora_in[:, N_DIRS * LORA:2 * N_DIRS * LORA]
    xg = lora_in[:, 2 * N_DIRS * LORA:]
    kkraw = kr * kk_ref[...]
    next(fill)()
    kk = kkraw * lax.rsqrt(_segsum(kkraw * kkraw, ones) + 1e-12)
    lora_w = [_mm(_split(tw[:, ls]), _split(w2_ref[d])) for d, ls in enumerate(dirs)]
    r = conv(0, WIDTH)
    next(fill)()
    lora_a = [_mm(_split(xa[:, ls]), _split(a2_ref[d])) for d, ls in enumerate(dirs)]
    g_out[...] = _mm(_split(jax.nn.sigmoid(xg)), _split(g2_ref[...]))
    v = conv(2 * WIDTH, 3 * WIDTH)
    next(fill)()
    kd_sum = jnp.zeros_like(kr)
    for d in range(N_DIRS):
        w_log = -_softplus(-(w0_ref[d:d + 1, :] + lora_w[d])) - 0.5
        a = jax.nn.sigmoid(a0_ref[d:d + 1, :] + lora_a[d])
        kd = kr * (1.0 + (a - 1.0) * ka_ref[...])
        dir_out[d, :, 0:WIDTH] = -jnp.exp(w_log)
        dir_out[d, :, WIDTH:2 * WIDTH] = kk * a
        dir_out[d, :, 2 * WIDTH:] = kd
        kd_sum = kd_sum + kd
    shared_out[:, 0:WIDTH] = r
    shared_out[:, WIDTH:2 * WIDTH] = kk
    shared_out[:, 2 * WIDTH:] = v
    bonus_out[...] = _segsum(r * kd_sum * rk_ref[...], ones) * v


HALF = CHUNK // 2


def _blockdiag(x, block=HEAD_DIM):
    def one(p):
        blk = lax.broadcasted_iota(jnp.int32, p.shape, 1) // block
        return jnp.concatenate([jnp.where(blk == q, p, jnp.zeros_like(p)) for q in range(GROUP_W // block)], axis=0)
    return one(x[0]), one(x[1])


def _head_transpose(x):
    xt = jnp.transpose(x)
    return jnp.concatenate([xt[h * HEAD_DIM:(h + 1) * HEAD_DIM] for h in range(GROUP)], axis=1)


def _scan_prepare(d, shared_ref, dir_ref, y_ref, s_ref):
    row = lax.broadcasted_iota(jnp.int32, (CHUNK, GROUP_W), 0)
    lane = lax.broadcasted_iota(jnp.int32, (CHUNK, GROUP_W), 1)
    col = lane % CHUNK
    strict = col > row if d else col < row
    incl = col >= row if d else col <= row
    row_h = lax.broadcasted_iota(jnp.int32, (HALF, GROUP_W), 0)
    lane_h = lax.broadcasted_iota(jnp.int32, (HALF, GROUP_W), 1)
    lo_s = lane_h % CHUNK < HALF
    eye_half = (lane_h % HALF == row_h).astype(F32)

    lw = dir_ref[0, :, 0:WIDTH]
    mask = jnp.where(incl[:, :CHUNK], 1.0, 0.0).astype(BF16)
    lw_hi, lw_mid = _split(lw)
    lw_lo = (lw - lw_hi.astype(F32) - lw_mid.astype(F32)).astype(BF16)
    cum = _dot(mask, lw_hi) + _dot(mask, lw_mid) + _dot(mask, lw_lo)
    tot = jnp.sum(lw, axis=0, keepdims=True)
    p_in = jnp.exp(cum)
    p_neg = jnp.exp(-cum)
    p_rem = jnp.exp(tot - cum)
    kt = shared_ref[:, WIDTH:2 * WIDTH] * jnp.exp(cum - lw)
    rt = shared_ref[:, 0:WIDTH] * p_in
    b = dir_ref[0, :, WIDTH:2 * WIDTH]
    kd = dir_ref[0, :, 2 * WIDTH:]
    bt = b * p_neg
    kdt = kd * p_neg
    bh = b * p_rem
    kh = kd * p_rem
    p_end = jnp.exp(tot)
    v = shared_ref[:, 2 * WIDTH:]

    chains = []
    for gi in range(N_HEADS // GROUP):
        sl = slice(gi * GROUP_W, (gi + 1) * GROUP_W)
        chains.append(dict(d=d, sl=sl, strict=strict, incl=incl, lo_s=lo_s, eye_half=eye_half, y_ref=y_ref, s_ref=s_ref,
                           kt=kt[:, sl], rt=rt[:, sl], v=v[:, sl], bt=bt[:, sl], kdt=kdt[:, sl],
                           kh=kh[:, sl], bh=bh[:, sl], p_end=p_end[:, sl]))
    return chains


def _scan_chains(chains):
    zeros = jnp.zeros((HALF, GROUP_W), F32)
    for c in chains:
        lhs = _split(jnp.concatenate([c["kt"], c["rt"]], axis=0))
        g_b = _mm_nt(lhs, _blockdiag(_split(c["bt"])))
        g_k = _mm_nt(lhs, _blockdiag(_split(c["kdt"])))
        c["a_k"] = jnp.where(c["strict"], g_k[:CHUNK], 0.0)
        c["a_rb"] = jnp.where(c["incl"], g_b[CHUNK:], 0.0)
        c["a_rk"] = jnp.where(c["incl"], g_k[CHUNK:], 0.0)
        n = jnp.where(c["strict"], -g_b[:CHUNK], 0.0)
        c["n"] = n
        c["pw"] = jnp.where(c["lo_s"], n[:HALF], n[HALF:])
        c["tp"] = c["eye_half"] + c["pw"]
    for c in chains:
        c["pw"] = _mm(_split(c["pw"]), _blockdiag(_split(c["pw"]), HALF))
    for _ in range(3):
        for c in chains:
            both = _mm(_split(jnp.concatenate([c["tp"], c["pw"]], axis=0)), _blockdiag(_split(c["pw"]), HALF))
            c["tp"] = c["tp"] + both[:HALF]
            c["pw"] = both[HALF:]
    for c in chains:
        c["tp"] = c["tp"] + _mm(_split(c["tp"]), _blockdiag(_split(c["pw"]), HALF))
    for c in chains:
        c["t1"] = jnp.where(c["lo_s"], c["tp"], 0.0)
        c["t2"] = jnp.where(c["lo_s"], 0.0, c["tp"])
        if c["d"] == 0:
            off, inner = jnp.where(c["lo_s"], c["n"][HALF:], 0.0), jnp.concatenate([c["t1"], zeros], axis=0)
        else:
            off, inner = jnp.where(c["lo_s"], 0.0, c["n"][:HALF]), jnp.concatenate([zeros, c["t2"]], axis=0)
        c["off"] = _mm(_split(off), _blockdiag(_split(inner)))
    for c in chains:
        if c["d"] == 0:
            x = _mm(_split(c["t2"]), _blockdiag(_split(jnp.concatenate([zeros, c["off"]], axis=0))))
            c["inv"] = jnp.concatenate([c["t1"], jnp.where(c["lo_s"], x, c["tp"])], axis=0)
        else:
            x = _mm(_split(c["t1"]), _blockdiag(_split(jnp.concatenate([c["off"], zeros], axis=0))))
            c["inv"] = jnp.concatenate([jnp.where(c["lo_s"], c["tp"], x), c["t2"]], axis=0)
    for c in chains:
        c["av"] = _mm(_split(jnp.concatenate([c["a_k"], c["a_rk"]], axis=0)), _blockdiag(_split(c["v"])))
        c["vt"] = _head_transpose(c["v"])
    for c in chains:
        c["s"] = c["s_ref"][0, :, c["sl"]]
        c["z"] = _mm_nt(_split(jnp.concatenate([c["rt"], c["kt"]], axis=0)), _blockdiag(_split(c["s"])))
    for c in chains:
        c["u"] = _mm(_split(c["inv"]), _blockdiag(_split(c["z"][CHUNK:] + c["av"][:CHUNK])))
    for c in chains:
        c["y_ref"][:, c["sl"]] = (c["z"][:CHUNK] + c["av"][CHUNK:]
                                  - _mm(_split(c["a_rb"]), _blockdiag(_split(c["u"]))))
    for c in chains:
        upd = (_mm(_split(c["vt"]), _blockdiag(_split(c["kh"])))
               - _mm(_split(_head_transpose(c["u"])), _blockdiag(_split(c["bh"]))))
        c["s_ref"][0, :, c["sl"]] = c["s"] * c["p_end"] + upd


REGION_CHUNKS = CTX_CHUNKS
SCAN_STREAMS = tuple((d, base, per_seq) for d in range(N_DIRS)
                     for base, per_seq in ((0, CHUNKS_PER_CTX_SEQ), (CTX_CHUNKS, CHUNKS_PER_LAT_SEQ)))
STREAM_INS = 3


def _stream_local_chunk(stream, j):
    return REGION_CHUNKS - 1 - j if stream[0] else j


def _scan_kernel(n_cast, *refs):
    n_stream_in = STREAM_INS * len(SCAN_STREAMS)
    n_in = n_stream_in + n_cast
    j = pl.program_id(0)
    for w_ref, o_ref in zip(refs[n_stream_in:n_in], refs[len(refs) - n_cast:]):
        o_ref[...] = w_ref[...].astype(BF16)
    per_stream = []
    for s, stream in enumerate(SCAN_STREAMS):
        d, _, per_seq = stream
        ins = refs[STREAM_INS * s:STREAM_INS * (s + 1)]
        y_ref, s_ref = refs[n_in + 2 * s:n_in + 2 * s + 2]
        s0_ref = ins[-1]
        local = _stream_local_chunk(stream, j) % per_seq

        @pl.when(local == (per_seq - 1 if d else 0))
        def _():
            s_ref[...] = s0_ref[0]

        per_stream.append((d,) + ins[:-1] + (y_ref, s_ref))

    chains = []
    for args in per_stream:
        chains += _scan_prepare(*args)
    _scan_chains(chains)


def _scan(shared, perdir, s0, weights):
    in_specs, out_specs, out_shape, operands = [], [], [], []
    cast_in, cast_out, cast_shape = [], [], []
    for w in weights:
        rows = w.shape[0] // REGION_CHUNKS
        if rows % 16 or rows * REGION_CHUNKS != w.shape[0]:
            rows = 128
        assert w.shape[0] % rows == 0 and w.shape[0] // rows <= REGION_CHUNKS
        spec = pl.BlockSpec((rows, w.shape[1]), lambda j, n=w.shape[0] // rows: (jnp.minimum(j, n - 1), 0))
        cast_in.append(spec)
        cast_out.append(spec)
        cast_shape.append(jax.ShapeDtypeStruct(w.shape, BF16))
    for stream in SCAN_STREAMS:
        d, base, per_seq = stream
        seq0 = 0 if base == 0 else BATCH
        n_seq = REGION_CHUNKS // per_seq
        loc = lambda j, stream=stream: _stream_local_chunk(stream, j)
        tok = pl.BlockSpec((CHUNK, 3 * WIDTH), lambda j, loc=loc, base=base: (base + loc(j), 0))
        dtok = pl.BlockSpec((1, CHUNK, 3 * WIDTH), lambda j, loc=loc, base=base, d=d: (d, base + loc(j), 0))
        st_in = pl.BlockSpec((1, 1, HEAD_DIM, WIDTH),
                             lambda j, loc=loc, seq0=seq0, per_seq=per_seq, d=d: (seq0 + loc(j) // per_seq, d, 0, 0))
        in_specs += [tok, dtok, st_in]
        operands += [shared, perdir, s0]
        out_specs += [pl.BlockSpec((CHUNK, WIDTH), lambda j, loc=loc: (loc(j), 0)),
                      pl.BlockSpec((1, HEAD_DIM, WIDTH), lambda j, loc=loc, per_seq=per_seq: (loc(j) // per_seq, 0, 0))]
        out_shape += [jax.ShapeDtypeStruct((REGION_CHUNKS * CHUNK, WIDTH), F32),
                      jax.ShapeDtypeStruct((n_seq, HEAD_DIM, WIDTH), F32)]
    return pl.pallas_call(
        functools.partial(_scan_kernel, len(weights)),
        grid=(REGION_CHUNKS,),
        in_specs=in_specs + cast_in,
        out_specs=out_specs + cast_out,
        out_shape=out_shape + cast_shape,
        compiler_params=_params(1),
        name="scan",
    )(*operands, *weights)


def _post_kernel(yfc_ref, yfl_ref, ybc_ref, ybl_ref, bonus_ref, g_ref, attc_ref, attl_ref, xp_ref, xs_ref, mod_ref,
                 lng_ref, lnb_ref, ones_ref, wout_ref, n2_ref, x1_ref, h2_ref):
    i = pl.program_id(0)
    def pick(ctx_ref, lat_ref):
        return _ctx_or_lat(i, ctx_ref, lat_ref, BIG_BLOCK)

    ones = ones_ref[...]
    inv_n = 1.0 / HEAD_DIM
    y = pick(yfc_ref, yfl_ref) + pick(ybc_ref, ybl_ref)
    yc = y - _segsum(y, ones) * inv_n
    var = _segsum(yc * yc, ones) * inv_n
    yn = yc * lax.rsqrt(var + GN_EPS) * lng_ref[...] + lnb_ref[...]
    r_out = ((yn + bonus_ref[...]) * g_ref[...]).astype(BF16)
    o = _dot(pick(attc_ref, attl_ref), wout_ref[0:WIDTH, :]) + _dot(r_out, wout_ref[WIDTH:, :])
    x1 = pick(xp_ref, xs_ref) + mod_ref[0, 2:3, :] * o
    x1_ref[...] = x1
    h2 = _rmsnorm(x1, n2_ref[...]) * (1.0 + mod_ref[0, 4:5, :]) + mod_ref[0, 3:4, :]
    h2_ref[...] = h2.astype(BF16)


def _post(ys, bonus, g, att_ctx, att_lat, xp, xs, mod, ln_g, ln_b, ones, w_out_bf, norm2):
    blk = BIG_BLOCK
    tok = lambda cols: _tok_spec(cols, blk)
    ctx_lat = [_ctx_tok_spec(WIDTH, blk), _lat_tok_spec(WIDTH, blk)]
    return pl.pallas_call(
        _post_kernel,
        grid=(N_TOK // blk,),
        in_specs=ctx_lat + ctx_lat + [tok(WIDTH), tok(WIDTH)]
        + ctx_lat + [_ctx_tok_spec(D_MODEL, blk), _lat_tok_spec(D_MODEL, blk),
                  _mod_spec(blk),
                  _const_spec((1, WIDTH)), _const_spec((1, WIDTH)), _const_spec((WIDTH, WIDTH)),
                  _const_spec((2 * WIDTH, D_MODEL)), _const_spec((1, D_MODEL))],
        out_specs=[tok(D_MODEL), tok(D_MODEL)],
        out_shape=[jax.ShapeDtypeStruct((N_TOK, D_MODEL), F32),
                   jax.ShapeDtypeStruct((N_TOK, D_MODEL), BF16)],
        compiler_params=_params(1),
        name="post",
    )(*ys, bonus, g, att_ctx, att_lat, xp, xs, mod, ln_g, ln_b, ones, w_out_bf, norm2)


def _ffn_kernel(x1_ref, h2_ref, hp_ref, hn_ref, mod_ref, w1_ref, w3_ref, wc_ref, w2_ref, nf_ref, yp_ref, ys_ref):
    i = pl.program_id(0)
    blk = BIG_BLOCK
    is_ctx = i < N_CTX_TOK // blk
    has_prev, has_next = _seq_neighbours(i, blk)
    lhs = jnp.concatenate([h2_ref[...], hp_ref[...], hn_ref[...]], axis=0)
    h2 = lhs[:blk]
    rid = lax.broadcasted_iota(jnp.int32, (blk, 1), 0)
    inner = jnp.where(is_ctx, SEQ, -1)
    seq_start = rid == inner
    seq_end = rid == inner - 1
    acc = jnp.zeros((blk, D_MODEL), F32)
    for f in range(D_FF // FF_TILE):
        fs = slice(f * FF_TILE, (f + 1) * FF_TILE)
        a_all = _dot(lhs, w1_ref[:, fs])
        a = a_all[:blk]
        prow = jnp.where(has_prev, a_all[blk + 15:blk + 16, :], 0.0)
        nrow = jnp.where(has_next, a_all[blk + 16:blk + 17, :], 0.0)
        aprev, anext = _shifted(a, prow, nrow)
        aprev = jnp.where(seq_start, 0.0, aprev)
        anext = jnp.where(seq_end, 0.0, anext)
        cv = aprev * wc_ref[0:1, fs] + a * wc_ref[1:2, fs] + anext * wc_ref[2:3, fs]
        act = _silu(cv) * _dot(h2, w3_ref[:, fs])
        acc = acc + _dot(act.astype(BF16), w2_ref[fs, :])
    x2 = x1_ref[...] + mod_ref[0, 5:6, :] * acc
    y = _rmsnorm(x2, nf_ref[...])

    @pl.when(is_ctx)
    def _():
        yp_ref[...] = y

    @pl.when(jnp.logical_not(is_ctx))
    def _():
        ys_ref[...] = y


def _ffn(x1, h2, mod, w1_bf, w3_bf, wc, w2_bf, norm_f):
    blk = BIG_BLOCK
    rows16 = blk // 16
    return pl.pallas_call(
        _ffn_kernel,
        grid=(N_TOK // blk,),
        in_specs=[_tok_spec(D_MODEL, blk), _tok_spec(D_MODEL, blk),
                  pl.BlockSpec((16, D_MODEL), lambda i: (jnp.maximum(i * rows16 - 1, 0), 0)),
                  pl.BlockSpec((16, D_MODEL), lambda i: (jnp.minimum((i + 1) * rows16, N_TOK // 16 - 1), 0)),
                  _mod_spec(blk),
                  _const_spec((D_MODEL, D_FF)), _const_spec((D_MODEL, D_FF)), _const_spec((3, D_FF)),
                  _const_spec((D_FF, D_MODEL)), _const_spec((1, D_MODEL))],
        out_specs=[_ctx_tok_spec(D_MODEL, blk), _lat_tok_spec(D_MODEL, blk)],
        out_shape=[jax.ShapeDtypeStruct((N_CTX_TOK, D_MODEL), F32),
                   jax.ShapeDtypeStruct((N_LAT_TOK, D_MODEL), F32)],
        compiler_params=_params(1),
        name="ffn",
    )(x1, h2, h2, h2, mod, w1_bf, w3_bf, wc, w2_bf, norm_f)


def kernel(x_prompt, x_sample, cache_k, cache_v, state_rwkv, c, c_ctx, w_ada, b_ada, norm1, norm2,
           w_in, w_ts, w0, w2, a0, a2, g2, k_k, k_a, r_k, ln_x_g, ln_x_b, rpb, w_out,
           w_ffn1, w_ffn3, w_ffn_conv, w_ffn2, norm_f):
    xp = x_prompt.reshape(N_CTX_TOK, D_MODEL)
    xs = x_sample.reshape(N_LAT_TOK, D_MODEL)
    row = lambda t: t.reshape(1, -1)
    cond = jnp.concatenate([c_ctx[None, :], c, jnp.zeros((8 - 1 - DEC_BATCH, D_MODEL), F32)], axis=0)
    mod = _modulation(cond, w_ada[0], b_ada[0]).reshape(8, 6, D_MODEL)

    head_id = jnp.arange(WIDTH) // HEAD_DIM
    ones = (head_id[:, None] == head_id[None, :]).astype(BF16)
    q, k_ctx, v_ctx, k_lat, v_lat, shared, g, bonus, perdir, new_k, new_v = _inproj(
        xp, xs, mod, row(norm1[0]), w_in[0].astype(BF16), w_ts[0], w0[0], w2[0], a0[0], a2[0], g2[0],
        row(k_k[0]), row(k_a[0]), row(r_k[0]), ones)

    att_ctx = _ctx_attention(q, k_ctx, v_ctx)
    att_lat = _lat_attention(q, k_lat, v_lat, cache_k[:, 0].reshape(DEC_BATCH, PAST_LEN, WIDTH),
                             cache_v[:, 0].reshape(DEC_BATCH, PAST_LEN, WIDTH), _na_bias_table(rpb[0]))
    s_lat = jnp.transpose(state_rwkv[:, 0], (0, 1, 3, 2, 4)).reshape(DEC_BATCH, N_DIRS, HEAD_DIM, WIDTH)
    s0 = jnp.concatenate([jnp.zeros((BATCH, N_DIRS, HEAD_DIM, WIDTH), F32), s_lat], axis=0)
    (y_fc, s_fc, y_fl, _, y_bc, s_bc, y_bl, _,
     w_out_bf, w1_bf, w3_bf, w2_bf) = _scan(shared, perdir, s0, (w_out[0], w_ffn1[0], w_ffn3[0], w_ffn2[0]))
    s_fin = jnp.stack([s_fc, s_bc], axis=1)

    x1, h2 = _post((y_fc, y_fl, y_bc, y_bl), bonus, g, att_ctx, att_lat, xp, xs, mod, row(ln_x_g[0]), row(ln_x_b[0]), ones,
                   w_out_bf, row(norm2[0]))
    yp, ys = _ffn(x1, h2, mod, w1_bf, w3_bf, w_ffn_conv[0], w2_bf, row(norm_f))

    new_s = jnp.transpose(s_fin.reshape(BATCH, N_DIRS, HEAD_DIM, N_HEADS, HEAD_DIM),
                          (0, 1, 3, 2, 4)).reshape(BATCH, 1, N_DIRS, N_HEADS, HEAD_DIM, HEAD_DIM)
    return (yp.reshape(BATCH, SEQ, D_MODEL), ys.reshape(DEC_BATCH, DEC_SEQ, D_MODEL), new_k, new_v, new_s)
```

```python
import functools

import jax
import jax.numpy as jnp
from jax import lax
from jax.experimental import pallas as pl
from jax.experimental.pallas import tpu as pltpu

F32 = jnp.float32
BF16 = jnp.bfloat16
HIGHEST = lax.Precision.HIGHEST

D_MODEL = 1024
BATCH = 16
SEQ = 256
DEC_BATCH = 2
DEC_SEQ = 2048
PAST_LEN = 512
GRID_W = 64
HEAD_DIM = 64
N_HEADS = 8
WIDTH = N_HEADS * HEAD_DIM
PAIR_W = 2 * HEAD_DIM
NA_ROWS = 8
NA_COLS = 16
N_DIRS = 2
LORA = 64
GATE_LORA = 128
D_FF = 2816
EPS = 1e-6
GN_EPS = 64e-5
ATT_SCALE = HEAD_DIM ** -0.5
RWKV_COLS = 3 * WIDTH + N_DIRS * 2 * LORA + GATE_LORA
MASK_VALUE = -1e30

N_CTX_TOK = BATCH * SEQ
N_LAT_TOK = DEC_BATCH * DEC_SEQ
N_TOK = N_CTX_TOK + N_LAT_TOK
TOK_BLOCK = 256
N_BLOCKS = N_TOK // TOK_BLOCK
BIG_BLOCK = 512
N_SEQS = BATCH + DEC_BATCH
CHUNK = 64
N_CHUNKS = N_TOK // CHUNK
CTX_CHUNKS = N_CTX_TOK // CHUNK
CHUNKS_PER_CTX_SEQ = SEQ // CHUNK
CHUNKS_PER_LAT_SEQ = DEC_SEQ // CHUNK
GRID_ROWS = DEC_SEQ // GRID_W
LAT_ROWS = 2
FF_TILE = D_FF // 2
GROUP = 4
GROUP_W = GROUP * HEAD_DIM
VMEM_LIMIT = 56 * 1024 * 1024


def _params(n_axes, limit=VMEM_LIMIT):
    return pltpu.CompilerParams(dimension_semantics=("arbitrary",) * n_axes, vmem_limit_bytes=limit)


def _const_spec(shape):
    zeros = (0,) * len(shape)
    return pl.BlockSpec(shape, lambda *_: zeros, pipeline_mode=pl.Buffered(1))


def _tok_spec(cols, blk=TOK_BLOCK):
    return pl.BlockSpec((blk, cols), lambda i: (i, 0))


def _ctx_tok_spec(cols, blk=TOK_BLOCK):
    return pl.BlockSpec((blk, cols), lambda i: (jnp.minimum(i, N_CTX_TOK // blk - 1), 0))


def _lat_tok_spec(cols, blk=TOK_BLOCK):
    return pl.BlockSpec((blk, cols), lambda i: (jnp.maximum(i - N_CTX_TOK // blk, 0), 0))


def _ctx_or_lat(i, ctx_ref, lat_ref, blk=TOK_BLOCK):
    return jnp.where(i < N_CTX_TOK // blk, ctx_ref[...], lat_ref[...])


def _mod_spec(blk=TOK_BLOCK):
    def row(i):
        return jnp.where(i < N_CTX_TOK // blk, 0, 1 + (i - N_CTX_TOK // blk) // (DEC_SEQ // blk))
    return pl.BlockSpec((1, 6, D_MODEL), lambda i: (row(i), 0, 0))


def _seq_neighbours(i, blk=TOK_BLOCK):
    per_seq = DEC_SEQ // blk
    j = (i - N_CTX_TOK // blk) % per_seq
    lat = i >= N_CTX_TOK // blk
    return lat & (j != 0), lat & (j != per_seq - 1)


def _silu(x):
    return x * jax.nn.sigmoid(x)


def _softplus(x):
    return jnp.maximum(x, 0.0) + jnp.log1p(jnp.exp(-jnp.abs(x)))


def _rmsnorm(x, g):
    return x * lax.rsqrt(jnp.mean(x * x, axis=-1, keepdims=True) + EPS) * g


def _dot(a, b, precision=None):
    return jnp.dot(a, b, precision=precision, preferred_element_type=F32)


def _dot_nt(a, b, precision=None):
    return lax.dot_general(a, b, (((1,), (1,)), ((), ())), precision=precision, preferred_element_type=F32)


def _split(x):
    hi = x.astype(BF16)
    return hi, (x - hi.astype(F32)).astype(BF16)


def _mm3(dot, a, b):
    m = a[0].shape[0]
    hi = dot(jnp.concatenate([a[0], a[1]], axis=0), b[0])
    return hi[:m] + hi[m:] + dot(a[0], b[1])


def _mm(a, b):
    return _mm3(_dot, a, b)


def _segsum(x, ones_bf):
    m = x.shape[0]
    hi, lo = _split(x)
    s = _dot(jnp.concatenate([hi, lo], axis=0), ones_bf)
    return s[:m] + s[m:]


def _shifted(a, prev_row, next_row):
    t = a.shape[0]
    rid = lax.broadcasted_iota(jnp.int32, (t, 1), 0)
    prev = jnp.where(rid == 0, prev_row, pltpu.roll(a, 1, axis=0))
    nxt = jnp.where(rid == t - 1, next_row, pltpu.roll(a, t - 1, axis=0))
    return prev, nxt


def _mod_kernel(cond_ref, w_ref, b_ref, o_ref):
    o_ref[...] = _dot(_silu(cond_ref[...]), w_ref[...], HIGHEST) + b_ref[...]


def _modulation(cond, w_ada, b_ada):
    n = 6 * D_MODEL
    return pl.pallas_call(
        _mod_kernel,
        grid=(6,),
        in_specs=[pl.BlockSpec((8, D_MODEL), lambda j: (0, 0)),
                  pl.BlockSpec((D_MODEL, D_MODEL), lambda j: (0, j)),
                  pl.BlockSpec((1, D_MODEL), lambda j: (0, j))],
        out_specs=pl.BlockSpec((8, D_MODEL), lambda j: (0, j)),
        out_shape=jax.ShapeDtypeStruct((8, n), F32),
        compiler_params=_params(1),
        name="mod",
    )(cond, w_ada, b_ada.reshape(1, n))


def _inproj_kernel(xp_ref, xs_ref, xprev_ref, xnext_ref, mod_ref, n1_ref, w_ref, *rest):
    rwkv_refs, (q_ref, kc_ref, vc_ref, kl_ref, vl_ref), rwkv_outs, (nk_ref, nv_ref) = (
        rest[:10], rest[10:15], rest[15:19], rest[19:])
    i = pl.program_id(0)
    has_prev, has_next = _seq_neighbours(i)

    def modulated(x):
        h = _rmsnorm(x, n1_ref[...]) * (1.0 + mod_ref[0, 1:2, :]) + mod_ref[0, 0:1, :]
        return h.astype(BF16)

    h = modulated(_ctx_or_lat(i, xp_ref, xs_ref))
    edge = modulated(jnp.concatenate([xprev_ref[...], xnext_ref[...]], axis=0))
    lhs = jnp.concatenate([h, edge], axis=0)

    def zcols(c0, c1):
        z_all = _dot(lhs, w_ref[:, 3 * WIDTH + c0:3 * WIDTH + c1])
        return (z_all[:TOK_BLOCK],
                jnp.where(has_prev, z_all[TOK_BLOCK + 7:TOK_BLOCK + 8], 0.0),
                jnp.where(has_next, z_all[TOK_BLOCK + 8:TOK_BLOCK + 9], 0.0))

    kv = []

    def project_q():
        q_ref[...] = _dot(h, w_ref[:, 0:WIDTH])

    def project_k():
        kv.append(_dot(h, w_ref[:, WIDTH:2 * WIDTH]))

    def project_v():
        kv.append(_dot(h, w_ref[:, 2 * WIDTH:3 * WIDTH]))

    _rwkv_tokens(zcols, (project_q, project_k, project_v), *rwkv_refs, *rwkv_outs)
    k, v = kv

    @pl.when(i < N_CTX_TOK // TOK_BLOCK)
    def _():
        kc_ref[...] = k
        vc_ref[...] = v
        nk_ref[0, 0] = k.reshape(SEQ, N_HEADS, HEAD_DIM)
        nv_ref[0, 0] = v.reshape(SEQ, N_HEADS, HEAD_DIM)

    @pl.when(i >= N_CTX_TOK // TOK_BLOCK)
    def _():
        kl_ref[...] = k
        vl_ref[...] = v


def _inproj(xp, xs, mod, norm1, w_in_bf, w_ts, w0, w2, a0, a2, g2, k_k, k_a, r_k, ones):
    half = jax.ShapeDtypeStruct((N_CTX_TOK, WIDTH), F32)
    tok = jax.ShapeDtypeStruct((N_TOK, WIDTH), F32)
    lat0 = N_CTX_TOK // TOK_BLOCK
    rows8 = TOK_BLOCK // 8
    n8 = N_LAT_TOK // 8
    prev8 = pl.BlockSpec((8, D_MODEL), lambda i: (jnp.clip((i - lat0) * rows8 - 1, 0, n8 - 1), 0))
    next8 = pl.BlockSpec((8, D_MODEL), lambda i: (jnp.clip((i - lat0 + 1) * rows8, 0, n8 - 1), 0))
    cache = pl.BlockSpec((1, 1, SEQ, N_HEADS, HEAD_DIM), lambda i: (jnp.minimum(i, BATCH - 1), 0, 0, 0, 0))
    return pl.pallas_call(
        _inproj_kernel,
        grid=(N_BLOCKS,),
        in_specs=[_ctx_tok_spec(D_MODEL), _lat_tok_spec(D_MODEL), prev8, next8, _mod_spec(),
                  _const_spec((1, D_MODEL)),
                  _const_spec((D_MODEL, 3 * WIDTH + RWKV_COLS)),
                  _const_spec((3, RWKV_COLS)),
                  _const_spec((N_DIRS, WIDTH)), _const_spec((N_DIRS, LORA, WIDTH)),
                  _const_spec((N_DIRS, WIDTH)), _const_spec((N_DIRS, LORA, WIDTH)),
                  _const_spec((GATE_LORA, WIDTH)),
                  _const_spec((1, WIDTH)), _const_spec((1, WIDTH)), _const_spec((1, WIDTH)),
                  _const_spec((WIDTH, WIDTH))],
        out_specs=[_tok_spec(WIDTH), _ctx_tok_spec(WIDTH), _ctx_tok_spec(WIDTH),
                   _lat_tok_spec(WIDTH), _lat_tok_spec(WIDTH),
                   _tok_spec(3 * WIDTH), _tok_spec(WIDTH), _tok_spec(WIDTH),
                   pl.BlockSpec((N_DIRS, TOK_BLOCK, 3 * WIDTH), lambda i: (0, i, 0)),
                   cache, cache],
        out_shape=[tok, half, half, half, half,
                   jax.ShapeDtypeStruct((N_TOK, 3 * WIDTH), F32), tok, tok,
                   jax.ShapeDtypeStruct((N_DIRS, N_TOK, 3 * WIDTH), F32),
                   jax.ShapeDtypeStruct((BATCH, 1, SEQ, N_HEADS, HEAD_DIM), F32),
                   jax.ShapeDtypeStruct((BATCH, 1, SEQ, N_HEADS, HEAD_DIM), F32)],
        compiler_params=_params(1),
        name="inproj",
    )(xp, xs, xs, xs, mod, norm1, w_in_bf, w_ts, w0, w2, a0, a2, g2, k_k, k_a, r_k, ones)


def _pair_queries(q):
    lo_half = lax.broadcasted_iota(jnp.int32, q.shape, 1) < HEAD_DIM
    return jnp.concatenate([jnp.where(lo_half, q, 0.0), jnp.where(lo_half, 0.0, q)], axis=0).astype(BF16)


def _pair_outputs(o):
    t = o.shape[0] // 2
    lo_half = lax.broadcasted_iota(jnp.int32, (t, PAIR_W), 1) < HEAD_DIM
    return jnp.where(lo_half, o[:t], o[t:])


def _ctx_attn_kernel(q_ref, k_ref, v_ref, o_ref):
    pairs = [slice(p * PAIR_W, (p + 1) * PAIR_W) for p in range(N_HEADS // 2)]
    scores = [_dot_nt(_pair_queries(q_ref[:, ps]), k_ref[:, ps].astype(BF16)) * ATT_SCALE for ps in pairs]
    probs = []
    for s in scores:
        e = jnp.exp(s - jnp.max(s, axis=-1, keepdims=True))
        probs.append((e.astype(BF16), jnp.sum(e, axis=-1, keepdims=True)))
    for ps, (e, l) in zip(pairs, probs):
        o_ref[:, ps] = _pair_outputs(_dot(e, v_ref[:, ps].astype(BF16)) / l).astype(BF16)


def _ctx_attention(q, k, v):
    spec = pl.BlockSpec((SEQ, WIDTH), lambda b: (b, 0))
    return pl.pallas_call(
        _ctx_attn_kernel,
        grid=(BATCH,),
        in_specs=[spec, spec, spec],
        out_specs=spec,
        out_shape=jax.ShapeDtypeStruct((N_CTX_TOK, WIDTH), BF16),
        compiler_params=_params(1),
        name="ctxattn",
    )(q, k, v)


def _lat_attn_kernel(q_ref, k_ref, v_ref, ck_ref, cv_ref, bias_ref, o_ref, kbf, vbf, ckbf, cvbf):
    i = pl.program_id(1)

    @pl.when(i == 0)
    def _():
        kbf[...] = k_ref[...].astype(BF16)
        vbf[...] = v_ref[...].astype(BF16)
        ckbf[...] = ck_ref[0].astype(BF16)
        cvbf[...] = cv_ref[0].astype(BF16)

    win = NA_ROWS * GRID_W
    rows = []
    for rr in range(LAT_ROWS):
        gr = i * LAT_ROWS + rr
        first_row = jnp.clip(gr - NA_ROWS // 2, 0, GRID_ROWS - NA_ROWS)
        start = pl.multiple_of(first_row * GRID_W, GRID_W)
        off0 = first_row - gr + NA_ROWS - 1
        rows.append((slice(rr * GRID_W, (rr + 1) * GRID_W), start, off0))
    pairs = [slice(p * PAIR_W, (p + 1) * PAIR_W) for p in range(N_HEADS // 2)]
    scores = []
    for p, ps in enumerate(pairs):
        lhs = [_pair_queries(q_ref[rs, ps]) for rs, _, _ in rows]
        s_loc = []
        for (_, start, off0), lhs_r in zip(rows, lhs):
            bias = jnp.concatenate(
                [jnp.concatenate([bias_ref[2 * p + hh, off0 + 2 * j] for j in range(NA_ROWS // 2)], axis=-1)
                 for hh in range(2)], axis=0)
            s_loc.append(_dot_nt(lhs_r, kbf[pl.ds(start, win), ps]) * ATT_SCALE + bias)
        s_ctx = _dot_nt(jnp.concatenate(lhs, axis=0), ckbf[:, ps]) * ATT_SCALE
        scores.append((jnp.concatenate(s_loc, axis=0), s_ctx))
    probs = []
    for s_loc, s_ctx in scores:
        m = jnp.maximum(jnp.max(s_loc, axis=-1, keepdims=True), jnp.max(s_ctx, axis=-1, keepdims=True))
        e_loc = jnp.exp(s_loc - m)
        e_ctx = jnp.exp(s_ctx - m)
        l = jnp.sum(e_loc, axis=-1, keepdims=True) + jnp.sum(e_ctx, axis=-1, keepdims=True)
        probs.append((e_loc.astype(BF16), e_ctx.astype(BF16), l))
    for ps, (e_loc, e_ctx, l) in zip(pairs, probs):
        o_ctx = _dot(e_ctx, cvbf[:, ps])
        for rr, (rs, start, _) in enumerate(rows):
            both = slice(rr * 2 * GRID_W, (rr + 1) * 2 * GRID_W)
            o = _dot(e_loc[both], vbf[pl.ds(start, win), ps]) + o_ctx[both]
            o_ref[rs, ps] = _pair_outputs(o / l[both]).astype(BF16)


def _na_bias_table(rpb):
    n = rpb.shape[-1]
    w = GRID_W
    pair = jnp.concatenate([rpb[:, :-1], rpb[:, 1:]], axis=-1).astype(F32)
    x = jnp.arange(2 * n)
    y = jnp.arange(2 * w)
    cq = jnp.arange(w)
    ck = y % w
    sel = ((x // n)[:, None, None] == (y // w)[None, None, :]) & (
        (x % n)[:, None, None] == ck[None, None, :] - cq[None, :, None] + NA_COLS - 1)
    t = jnp.einsum("hdx,xqy->hdqy", pair, sel.astype(F32), precision=HIGHEST)
    cs = jnp.clip(cq - NA_COLS // 2, 0, w - NA_COLS)[:, None]
    return jnp.where((ck[None, :] >= cs) & (ck[None, :] < cs + NA_COLS), t, MASK_VALUE)


def _lat_attention(q, k, v, cache_k, cache_v, bias):
    blk = LAT_ROWS * GRID_W
    steps = GRID_ROWS // LAT_ROWS
    lat0 = N_CTX_TOK // blk
    seq = pl.BlockSpec((DEC_SEQ, WIDTH), lambda b, i: (b, 0))
    ctx = pl.BlockSpec((1, PAST_LEN, WIDTH), lambda b, i: (b, 0, 0))
    return pl.pallas_call(
        _lat_attn_kernel,
        grid=(DEC_BATCH, steps),
        in_specs=[pl.BlockSpec((blk, WIDTH), lambda b, i: (lat0 + b * steps + i, 0)),
                  seq, seq, ctx, ctx,
                  _const_spec((N_HEADS, 2 * NA_ROWS - 2, GRID_W, PAIR_W))],
        out_specs=pl.BlockSpec((blk, WIDTH), lambda b, i: (b * steps + i, 0)),
        out_shape=jax.ShapeDtypeStruct((N_LAT_TOK, WIDTH), BF16),
        scratch_shapes=[pltpu.VMEM((DEC_SEQ, WIDTH), BF16), pltpu.VMEM((DEC_SEQ, WIDTH), BF16),
                        pltpu.VMEM((PAST_LEN, WIDTH), BF16), pltpu.VMEM((PAST_LEN, WIDTH), BF16)],
        compiler_params=_params(2),
        name="latattn",
    )(q, k, v, cache_k, cache_v, bias)


def _rwkv_tokens(zcols, fillers, wts_ref, w0_ref, w2_ref, a0_ref, a2_ref, g2_ref, kk_ref, ka_ref, rk_ref, ones_ref,
                 shared_out, g_out, bonus_out, dir_out):
    def conv(c0, c1):
        zc, prow, nrow = zcols(c0, c1)
        zprev, znext = _shifted(zc, prow, nrow)
        return zprev * wts_ref[0:1, c0:c1] + zc * wts_ref[1:2, c0:c1] + znext * wts_ref[2:3, c0:c1]

    fill = iter(fillers)
    dirs = [slice(d * LORA, (d + 1) * LORA) for d in range(N_DIRS)]
    ones = ones_ref[...]
    o = 3 * WIDTH
    lora_in = conv(o, RWKV_COLS)
    kr = conv(WIDTH, 2 * WIDTH)
    tw = jnp.tanh(lora_in[:, 0:N_DIRS * LORA])
    xa = lora_in[:, N_DIRS * LORA:2 * N_DIRS * LORA]
    xg = lora_in[:, 2 * N_DIRS * LORA:]
    kkraw = kr * kk_ref[...]
    next(fill)()
    kk = kkraw * lax.rsqrt(_segsum(kkraw * kkraw, ones) + 1e-12)
    lora_w = [_mm(_split(tw[:, ls]), _split(w2_ref[d])) for d, ls in enumerate(dirs)]
    r = conv(0, WIDTH)
    next(fill)()
    lora_a = [_mm(_split(xa[:, ls]), _split(a2_ref[d])) for d, ls in enumerate(dirs)]
    g_out[...] = _mm(_split(jax.nn.sigmoid(xg)), _split(g2_ref[...]))
    v = conv(2 * WIDTH, 3 * WIDTH)
    next(fill)()
    kd_sum = jnp.zeros_like(kr)
    for d in range(N_DIRS):
        w_log = -_softplus(-(w0_ref[d:d + 1, :] + lora_w[d])) - 0.5
        a = jax.nn.sigmoid(a0_ref[d:d + 1, :] + lora_a[d])
        kd = kr * (1.0 + (a - 1.0) * ka_ref[...])
        dir_out[d, :, 0:WIDTH] = -jnp.exp(w_log)
        dir_out[d, :, WIDTH:2 * WIDTH] = kk * a
        dir_out[d, :, 2 * WIDTH:] = kd
        kd_sum = kd_sum + kd
    shared_out[:, 0:WIDTH] = r
    shared_out[:, WIDTH:2 * WIDTH] = kk
    shared_out[:, 2 * WIDTH:] = v
    bonus_out[...] = _segsum(r * kd_sum * rk_ref[...], ones) * v


HALF = CHUNK // 2


def _blockdiag(x, block=HEAD_DIM):
    p = x.astype(BF16)
    blk = lax.broadcasted_iota(jnp.int32, p.shape, 1) // block
    return jnp.concatenate([jnp.where(blk == q, p, jnp.zeros_like(p)) for q in range(GROUP_W // block)], axis=0)


def _hp(a, b, block=HEAD_DIM):
    return _dot(a.astype(BF16), _blockdiag(b, block))


def _hp_nt(a, b):
    return _dot_nt(a.astype(BF16), _blockdiag(b))


def _head_transpose(x):
    xt = jnp.transpose(x)
    return jnp.concatenate([xt[h * HEAD_DIM:(h + 1) * HEAD_DIM] for h in range(GROUP)], axis=1)


def _scan_prepare(d, shared_ref, dir_ref, y_ref, s_ref):
    row = lax.broadcasted_iota(jnp.int32, (CHUNK, GROUP_W), 0)
    lane = lax.broadcasted_iota(jnp.int32, (CHUNK, GROUP_W), 1)
    col = lane % CHUNK
    strict = col > row if d else col < row
    incl = col >= row if d else col <= row
    row_h = lax.broadcasted_iota(jnp.int32, (HALF, GROUP_W), 0)
    lane_h = lax.broadcasted_iota(jnp.int32, (HALF, GROUP_W), 1)
    lo_s = lane_h % CHUNK < HALF
    eye_half = (lane_h % HALF == row_h).astype(F32)

    lw = dir_ref[0, :, 0:WIDTH]
    mask = jnp.where(incl[:, :CHUNK], 1.0, 0.0).astype(BF16)
    lw_hi, lw_mid = _split(lw)
    lw_lo = (lw - lw_hi.astype(F32) - lw_mid.astype(F32)).astype(BF16)
    cum = _dot(mask, lw_hi) + _dot(mask, lw_mid) + _dot(mask, lw_lo)
    tot = jnp.sum(lw, axis=0, keepdims=True)
    p_in = jnp.exp(cum)
    p_neg = jnp.exp(-cum)
    p_rem = jnp.exp(tot - cum)
    kt = shared_ref[:, WIDTH:2 * WIDTH] * jnp.exp(cum - lw)
    rt = shared_ref[:, 0:WIDTH] * p_in
    b = dir_ref[0, :, WIDTH:2 * WIDTH]
    kd = dir_ref[0, :, 2 * WIDTH:]
    bt = b * p_neg
    kdt = kd * p_neg
    bh = b * p_rem
    kh = kd * p_rem
    p_end = jnp.exp(tot)
    v = shared_ref[:, 2 * WIDTH:]

    chains = []
    for gi in range(N_HEADS // GROUP):
        sl = slice(gi * GROUP_W, (gi + 1) * GROUP_W)
        chains.append(dict(d=d, sl=sl, strict=strict, incl=incl, lo_s=lo_s, eye_half=eye_half, y_ref=y_ref, s_ref=s_ref,
                           kt=kt[:, sl], rt=rt[:, sl], v=v[:, sl], bt=bt[:, sl], kdt=kdt[:, sl],
                           kh=kh[:, sl], bh=bh[:, sl], p_end=p_end[:, sl]))
    return chains


def _scan_scores(chains):
    for c in chains:
        lhs = jnp.concatenate([c["kt"], c["rt"]], axis=0)
        g_b = _hp_nt(lhs, c["bt"])
        g_k = _hp_nt(lhs, c["kdt"])
        c["a_k"] = jnp.where(c["strict"], g_k[:CHUNK], 0.0)
        c["a_rb"] = jnp.where(c["incl"], g_b[CHUNK:], 0.0)
        c["a_rk"] = jnp.where(c["incl"], g_k[CHUNK:], 0.0)
        n = jnp.where(c["strict"], -g_b[:CHUNK], 0.0)
        c["n"] = n
        c["pw"] = jnp.where(c["lo_s"], n[:HALF], n[HALF:])
        c["tp"] = c["eye_half"] + c["pw"]
    return chains


def _scan_chains(chains):
    zeros = jnp.zeros((HALF, GROUP_W), F32)
    for c in chains:
        c["pw"] = _hp(c["pw"], c["pw"], HALF)
    for _ in range(3):
        for c in chains:
            both = _hp(jnp.concatenate([c["tp"], c["pw"]], axis=0), c["pw"], HALF)
            c["tp"] = c["tp"] + both[:HALF]
            c["pw"] = both[HALF:]
    for c in chains:
        c["tp"] = c["tp"] + _hp(c["tp"], c["pw"], HALF)
    for c in chains:
        c["t1"] = jnp.where(c["lo_s"], c["tp"], 0.0)
        c["t2"] = jnp.where(c["lo_s"], 0.0, c["tp"])
        if c["d"] == 0:
            off, inner = jnp.where(c["lo_s"], c["n"][HALF:], 0.0), jnp.concatenate([c["t1"], zeros], axis=0)
        else:
            off, inner = jnp.where(c["lo_s"], 0.0, c["n"][:HALF]), jnp.concatenate([zeros, c["t2"]], axis=0)
        c["off"] = _hp(off, inner)
    for c in chains:
        if c["d"] == 0:
            x = _hp(c["t2"], jnp.concatenate([zeros, c["off"]], axis=0))
            c["inv"] = jnp.concatenate([c["t1"], jnp.where(c["lo_s"], x, c["tp"])], axis=0)
        else:
            x = _hp(c["t1"], jnp.concatenate([c["off"], zeros], axis=0))
            c["inv"] = jnp.concatenate([jnp.where(c["lo_s"], c["tp"], x), c["t2"]], axis=0)
    for c in chains:
        c["av"] = _hp(jnp.concatenate([c["a_k"], c["a_rk"]], axis=0), c["v"])
        c["vt"] = _head_transpose(c["v"])
    for c in chains:
        c["s"] = c["s_ref"][0, :, c["sl"]]
        c["z"] = _hp_nt(jnp.concatenate([c["rt"], c["kt"]], axis=0), c["s"])
    for c in chains:
        c["u"] = _hp(c["inv"], c["z"][CHUNK:] + c["av"][:CHUNK])
    for c in chains:
        c["y_ref"][:, c["sl"]] = c["z"][:CHUNK] + c["av"][CHUNK:] - _hp(c["a_rb"], c["u"])
    for c in chains:
        upd = _hp(c["vt"], c["kh"]) - _hp(_head_transpose(c["u"]), c["bh"])
        c["s_ref"][0, :, c["sl"]] = c["s"] * c["p_end"] + upd


REGION_CHUNKS = CTX_CHUNKS
SCAN_STREAMS = tuple((d, base, per_seq) for d in range(N_DIRS)
                     for base, per_seq in ((0, CHUNKS_PER_CTX_SEQ), (CTX_CHUNKS, CHUNKS_PER_LAT_SEQ)))
STREAM_INS = 3


def _stream_local_chunk(stream, j):
    return REGION_CHUNKS - 1 - j if stream[0] else j


def _scan_kernel(n_cast, *refs):
    n_stream_in = STREAM_INS * len(SCAN_STREAMS)
    n_in = n_stream_in + n_cast
    j = pl.program_id(0)
    for w_ref, o_ref in zip(refs[n_stream_in:n_in], refs[len(refs) - n_cast:]):
        o_ref[...] = w_ref[...].astype(BF16)
    per_stream = []
    for s, stream in enumerate(SCAN_STREAMS):
        d, _, per_seq = stream
        ins = refs[STREAM_INS * s:STREAM_INS * (s + 1)]
        y_ref, s_ref = refs[n_in + 2 * s:n_in + 2 * s + 2]
        s0_ref = ins[-1]
        local = _stream_local_chunk(stream, j) % per_seq

        @pl.when(local == (per_seq - 1 if d else 0))
        def _():
            s_ref[...] = s0_ref[0]

        per_stream.append((d,) + ins[:-1] + (y_ref, s_ref))

    chains = []
    for args in per_stream:
        chains += _scan_prepare(*args)
    _scan_chains(_scan_scores(chains))


def _scan(shared, perdir, s0, weights):
    in_specs, out_specs, out_shape, operands = [], [], [], []
    cast_in, cast_out, cast_shape = [], [], []
    for w in weights:
        rows = w.shape[0] // REGION_CHUNKS
        if rows % 16 or rows * REGION_CHUNKS != w.shape[0]:
            rows = 128
        assert w.shape[0] % rows == 0 and w.shape[0] // rows <= REGION_CHUNKS
        spec = pl.BlockSpec((rows, w.shape[1]), lambda j, n=w.shape[0] // rows: (jnp.minimum(j, n - 1), 0))
        cast_in.append(spec)
        cast_out.append(spec)
        cast_shape.append(jax.ShapeDtypeStruct(w.shape, BF16))
    for stream in SCAN_STREAMS:
        d, base, per_seq = stream
        seq0 = 0 if base == 0 else BATCH
        n_seq = REGION_CHUNKS // per_seq
        loc = lambda j, stream=stream: _stream_local_chunk(stream, j)
        tok = pl.BlockSpec((CHUNK, 3 * WIDTH), lambda j, loc=loc, base=base: (base + loc(j), 0))
        dtok = pl.BlockSpec((1, CHUNK, 3 * WIDTH), lambda j, loc=loc, base=base, d=d: (d, base + loc(j), 0))
        st_in = pl.BlockSpec((1, 1, HEAD_DIM, WIDTH),
                             lambda j, loc=loc, seq0=seq0, per_seq=per_seq, d=d: (seq0 + loc(j) // per_seq, d, 0, 0))
        in_specs += [tok, dtok, st_in]
        operands += [shared, perdir, s0]
        out_specs += [pl.BlockSpec((CHUNK, WIDTH), lambda j, loc=loc: (loc(j), 0)),
                      pl.BlockSpec((1, HEAD_DIM, WIDTH), lambda j, loc=loc, per_seq=per_seq: (loc(j) // per_seq, 0, 0))]
        out_shape += [jax.ShapeDtypeStruct((REGION_CHUNKS * CHUNK, WIDTH), F32),
                      jax.ShapeDtypeStruct((n_seq, HEAD_DIM, WIDTH), F32)]
    return pl.pallas_call(
        functools.partial(_scan_kernel, len(weights)),
        grid=(REGION_CHUNKS,),
        in_specs=in_specs + cast_in,
        out_specs=out_specs + cast_out,
        out_shape=out_shape + cast_shape,
        compiler_params=_params(1),
        name="scan",
    )(*operands, *weights)


def _post_kernel(yfc_ref, yfl_ref, ybc_ref, ybl_ref, bonus_ref, g_ref, attc_ref, attl_ref, xp_ref, xs_ref, mod_ref,
                 lng_ref, lnb_ref, ones_ref, wout_ref, n2_ref, x1_ref, h2_ref):
    i = pl.program_id(0)
    def pick(ctx_ref, lat_ref):
        return _ctx_or_lat(i, ctx_ref, lat_ref, BIG_BLOCK)

    ones = ones_ref[...]
    inv_n = 1.0 / HEAD_DIM
    y = pick(yfc_ref, yfl_ref) + pick(ybc_ref, ybl_ref)
    yc = y - _segsum(y, ones) * inv_n
    var = _segsum(yc * yc, ones) * inv_n
    yn = yc * lax.rsqrt(var + GN_EPS) * lng_ref[...] + lnb_ref[...]
    r_out = ((yn + bonus_ref[...]) * g_ref[...]).astype(BF16)
    o = _dot(pick(attc_ref, attl_ref), wout_ref[0:WIDTH, :]) + _dot(r_out, wout_ref[WIDTH:, :])
    x1 = pick(xp_ref, xs_ref) + mod_ref[0, 2:3, :] * o
    x1_ref[...] = x1
    h2 = _rmsnorm(x1, n2_ref[...]) * (1.0 + mod_ref[0, 4:5, :]) + mod_ref[0, 3:4, :]
    h2_ref[...] = h2.astype(BF16)


def _post(ys, bonus, g, att_ctx, att_lat, xp, xs, mod, ln_g, ln_b, ones, w_out_bf, norm2):
    blk = BIG_BLOCK
    tok = lambda cols: _tok_spec(cols, blk)
    ctx_lat = [_ctx_tok_spec(WIDTH, blk), _lat_tok_spec(WIDTH, blk)]
    return pl.pallas_call(
        _post_kernel,
        grid=(N_TOK // blk,),
        in_specs=ctx_lat + ctx_lat + [tok(WIDTH), tok(WIDTH)]
        + ctx_lat + [_ctx_tok_spec(D_MODEL, blk), _lat_tok_spec(D_MODEL, blk),
                  _mod_spec(blk),
                  _const_spec((1, WIDTH)), _const_spec((1, WIDTH)), _const_spec((WIDTH, WIDTH)),
                  _const_spec((2 * WIDTH, D_MODEL)), _const_spec((1, D_MODEL))],
        out_specs=[tok(D_MODEL), tok(D_MODEL)],
        out_shape=[jax.ShapeDtypeStruct((N_TOK, D_MODEL), F32),
                   jax.ShapeDtypeStruct((N_TOK, D_MODEL), BF16)],
        compiler_params=_params(1),
        name="post",
    )(*ys, bonus, g, att_ctx, att_lat, xp, xs, mod, ln_g, ln_b, ones, w_out_bf, norm2)


def _ffn_kernel(x1_ref, h2_ref, hp_ref, hn_ref, mod_ref, w1_ref, w3_ref, wc_ref, w2_ref, nf_ref, yp_ref, ys_ref):
    i = pl.program_id(0)
    blk = BIG_BLOCK
    is_ctx = i < N_CTX_TOK // blk
    has_prev, has_next = _seq_neighbours(i, blk)
    lhs = jnp.concatenate([h2_ref[...], hp_ref[...], hn_ref[...]], axis=0)
    h2 = lhs[:blk]
    rid = lax.broadcasted_iota(jnp.int32, (blk, 1), 0)
    inner = jnp.where(is_ctx, SEQ, -1)
    seq_start = rid == inner
    seq_end = rid == inner - 1
    acc = jnp.zeros((blk, D_MODEL), F32)
    for f in range(D_FF // FF_TILE):
        fs = slice(f * FF_TILE, (f + 1) * FF_TILE)
        a_all = _dot(lhs, w1_ref[:, fs])
        a = a_all[:blk]
        prow = jnp.where(has_prev, a_all[blk + 15:blk + 16, :], 0.0)
        nrow = jnp.where(has_next, a_all[blk + 16:blk + 17, :], 0.0)
        aprev, anext = _shifted(a, prow, nrow)
        aprev = jnp.where(seq_start, 0.0, aprev)
        anext = jnp.where(seq_end, 0.0, anext)
        cv = aprev * wc_ref[0:1, fs] + a * wc_ref[1:2, fs] + anext * wc_ref[2:3, fs]
        act = _silu(cv) * _dot(h2, w3_ref[:, fs])
        acc = acc + _dot(act.astype(BF16), w2_ref[fs, :])
    x2 = x1_ref[...] + mod_ref[0, 5:6, :] * acc
    y = _rmsnorm(x2, nf_ref[...])

    @pl.when(is_ctx)
    def _():
        yp_ref[...] = y

    @pl.when(jnp.logical_not(is_ctx))
    def _():
        ys_ref[...] = y


def _ffn(x1, h2, mod, w1_bf, w3_bf, wc, w2_bf, norm_f):
    blk = BIG_BLOCK
    rows16 = blk // 16
    return pl.pallas_call(
        _ffn_kernel,
        grid=(N_TOK // blk,),
        in_specs=[_tok_spec(D_MODEL, blk), _tok_spec(D_MODEL, blk),
                  pl.BlockSpec((16, D_MODEL), lambda i: (jnp.maximum(i * rows16 - 1, 0), 0)),
                  pl.BlockSpec((16, D_MODEL), lambda i: (jnp.minimum((i + 1) * rows16, N_TOK // 16 - 1), 0)),
                  _mod_spec(blk),
                  _const_spec((D_MODEL, D_FF)), _const_spec((D_MODEL, D_FF)), _const_spec((3, D_FF)),
                  _const_spec((D_FF, D_MODEL)), _const_spec((1, D_MODEL))],
        out_specs=[_ctx_tok_spec(D_MODEL, blk), _lat_tok_spec(D_MODEL, blk)],
        out_shape=[jax.ShapeDtypeStruct((N_CTX_TOK, D_MODEL), F32),
                   jax.ShapeDtypeStruct((N_LAT_TOK, D_MODEL), F32)],
        compiler_params=_params(1),
        name="ffn",
    )(x1, h2, h2, h2, mod, w1_bf, w3_bf, wc, w2_bf, norm_f)


def kernel(x_prompt, x_sample, cache_k, cache_v, state_rwkv, c, c_ctx, w_ada, b_ada, norm1, norm2,
           w_in, w_ts, w0, w2, a0, a2, g2, k_k, k_a, r_k, ln_x_g, ln_x_b, rpb, w_out,
           w_ffn1, w_ffn3, w_ffn_conv, w_ffn2, norm_f):
    xp = x_prompt.reshape(N_CTX_TOK, D_MODEL)
    xs = x_sample.reshape(N_LAT_TOK, D_MODEL)
    row = lambda t: t.reshape(1, -1)
    cond = jnp.concatenate([c_ctx[None, :], c, jnp.zeros((8 - 1 - DEC_BATCH, D_MODEL), F32)], axis=0)
    mod = _modulation(cond, w_ada[0], b_ada[0]).reshape(8, 6, D_MODEL)

    head_id = jnp.arange(WIDTH) // HEAD_DIM
    ones = (head_id[:, None] == head_id[None, :]).astype(BF16)
    q, k_ctx, v_ctx, k_lat, v_lat, shared, g, bonus, perdir, new_k, new_v = _inproj(
        xp, xs, mod, row(norm1[0]), w_in[0].astype(BF16), w_ts[0], w0[0], w2[0], a0[0], a2[0], g2[0],
        row(k_k[0]), row(k_a[0]), row(r_k[0]), ones)

    att_ctx = _ctx_attention(q, k_ctx, v_ctx)
    att_lat = _lat_attention(q, k_lat, v_lat, cache_k[:, 0].reshape(DEC_BATCH, PAST_LEN, WIDTH),
                             cache_v[:, 0].reshape(DEC_BATCH, PAST_LEN, WIDTH), _na_bias_table(rpb[0]))
    s_lat = jnp.transpose(state_rwkv[:, 0], (0, 1, 3, 2, 4)).reshape(DEC_BATCH, N_DIRS, HEAD_DIM, WIDTH)
    s0 = jnp.concatenate([jnp.zeros((BATCH, N_DIRS, HEAD_DIM, WIDTH), F32), s_lat], axis=0)
    (y_fc, s_fc, y_fl, _, y_bc, s_bc, y_bl, _,
     w_out_bf, w1_bf, w3_bf, w2_bf) = _scan(shared, perdir, s0, (w_out[0], w_ffn1[0], w_ffn3[0], w_ffn2[0]))
    s_fin = jnp.stack([s_fc, s_bc], axis=1)

    x1, h2 = _post((y_fc, y_fl, y_bc, y_bl), bonus, g, att_ctx, att_lat, xp, xs, mod, row(ln_x_g[0]), row(ln_x_b[0]), ones,
                   w_out_bf, row(norm2[0]))
    yp, ys = _ffn(x1, h2, mod, w1_bf, w3_bf, w_ffn_conv[0], w2_bf, row(norm_f))

    new_s = jnp.transpose(s_fin.reshape(BATCH, N_DIRS, HEAD_DIM, N_HEADS, HEAD_DIM),
                          (0, 1, 3, 2, 4)).reshape(BATCH, 1, N_DIRS, N_HEADS, HEAD_DIM, HEAD_DIM)
    return (yp.reshape(BATCH, SEQ, D_MODEL), ys.reshape(DEC_BATCH, DEC_SEQ, D_MODEL), new_k, new_v, new_s)
```

```python
import functools

import jax
import jax.numpy as jnp
from jax import lax
from jax.experimental import pallas as pl
from jax.experimental.pallas import tpu as pltpu

F32 = jnp.float32
BF16 = jnp.bfloat16
HIGHEST = lax.Precision.HIGHEST

D_MODEL = 1024
BATCH = 16
SEQ = 256
DEC_BATCH = 2
DEC_SEQ = 2048
PAST_LEN = 512
GRID_W = 64
HEAD_DIM = 64
N_HEADS = 8
WIDTH = N_HEADS * HEAD_DIM
PAIR_W = 2 * HEAD_DIM
NA_ROWS = 8
NA_COLS = 16
N_DIRS = 2
LORA = 64
GATE_LORA = 128
D_FF = 2816
EPS = 1e-6
GN_EPS = 64e-5
ATT_SCALE = HEAD_DIM ** -0.5
RWKV_COLS = 3 * WIDTH + N_DIRS * 2 * LORA + GATE_LORA
MASK_VALUE = -1e30

N_CTX_TOK = BATCH * SEQ
N_LAT_TOK = DEC_BATCH * DEC_SEQ
N_TOK = N_CTX_TOK + N_LAT_TOK
TOK_BLOCK = 256
N_BLOCKS = N_TOK // TOK_BLOCK
BIG_BLOCK = 512
N_SEQS = BATCH + DEC_BATCH
CHUNK = 64
N_CHUNKS = N_TOK // CHUNK
CTX_CHUNKS = N_CTX_TOK // CHUNK
CHUNKS_PER_CTX_SEQ = SEQ // CHUNK
CHUNKS_PER_LAT_SEQ = DEC_SEQ // CHUNK
GRID_ROWS = DEC_SEQ // GRID_W
LAT_ROWS = 2
FF_TILE = D_FF // 2
GROUP = 4
GROUP_W = GROUP * HEAD_DIM
VMEM_LIMIT = 56 * 1024 * 1024


def _params(n_axes, limit=VMEM_LIMIT):
    return pltpu.CompilerParams(dimension_semantics=("arbitrary",) * n_axes, vmem_limit_bytes=limit)


def _const_spec(shape):
    zeros = (0,) * len(shape)
    return pl.BlockSpec(shape, lambda *_: zeros, pipeline_mode=pl.Buffered(1))


def _tok_spec(cols, blk=TOK_BLOCK):
    return pl.BlockSpec((blk, cols), lambda i: (i, 0))


def _ctx_tok_spec(cols, blk=TOK_BLOCK):
    return pl.BlockSpec((blk, cols), lambda i: (jnp.minimum(i, N_CTX_TOK // blk - 1), 0))


def _lat_tok_spec(cols, blk=TOK_BLOCK):
    return pl.BlockSpec((blk, cols), lambda i: (jnp.maximum(i - N_CTX_TOK // blk, 0), 0))


def _ctx_or_lat(i, ctx_ref, lat_ref, blk=TOK_BLOCK):
    return jnp.where(i < N_CTX_TOK // blk, ctx_ref[...], lat_ref[...])


def _mod_spec(blk=TOK_BLOCK):
    def row(i):
        return jnp.where(i < N_CTX_TOK // blk, 0, 1 + (i - N_CTX_TOK // blk) // (DEC_SEQ // blk))
    return pl.BlockSpec((1, 6, D_MODEL), lambda i: (row(i), 0, 0))


def _seq_neighbours(i, blk=TOK_BLOCK):
    per_seq = DEC_SEQ // blk
    j = (i - N_CTX_TOK // blk) % per_seq
    lat = i >= N_CTX_TOK // blk
    return lat & (j != 0), lat & (j != per_seq - 1)


def _silu(x):
    return x * jax.nn.sigmoid(x)


def _softplus(x):
    return jnp.maximum(x, 0.0) + jnp.log(1.0 + jnp.exp(-jnp.abs(x)))


def _rmsnorm(x, g):
    return x * lax.rsqrt(jnp.mean(x * x, axis=-1, keepdims=True) + EPS) * g


def _dot(a, b, precision=None):
    return jnp.dot(a, b, precision=precision, preferred_element_type=F32)


def _dot_nt(a, b, precision=None):
    return lax.dot_general(a, b, (((1,), (1,)), ((), ())), precision=precision, preferred_element_type=F32)


def _split(x):
    hi = x.astype(BF16)
    return hi, (x - hi.astype(F32)).astype(BF16)


def _mm3(dot, a, b):
    m = a[0].shape[0]
    hi = dot(jnp.concatenate([a[0], a[1]], axis=0), b[0])
    return hi[:m] + hi[m:] + dot(a[0], b[1])


def _mm(a, b):
    return _mm3(_dot, a, b)


def _segsum(x, ones_bf):
    m = x.shape[0]
    hi, lo = _split(x)
    s = _dot(jnp.concatenate([hi, lo], axis=0), ones_bf)
    return s[:m] + s[m:]


def _shifted(a, prev_row, next_row):
    t = a.shape[0]
    rid = lax.broadcasted_iota(jnp.int32, (t, 1), 0)
    prev = jnp.where(rid == 0, prev_row, pltpu.roll(a, 1, axis=0))
    nxt = jnp.where(rid == t - 1, next_row, pltpu.roll(a, t - 1, axis=0))
    return prev, nxt


def _mod_kernel(cond_ref, w_ref, b_ref, o_ref):
    o_ref[...] = _dot(_silu(cond_ref[...]), w_ref[...], HIGHEST) + b_ref[...]


def _modulation(cond, w_ada, b_ada):
    n = 6 * D_MODEL
    return pl.pallas_call(
        _mod_kernel,
        grid=(6,),
        in_specs=[pl.BlockSpec((8, D_MODEL), lambda j: (0, 0)),
                  pl.BlockSpec((D_MODEL, D_MODEL), lambda j: (0, j)),
                  pl.BlockSpec((1, D_MODEL), lambda j: (0, j))],
        out_specs=pl.BlockSpec((8, D_MODEL), lambda j: (0, j)),
        out_shape=jax.ShapeDtypeStruct((8, n), F32),
        compiler_params=_params(1),
        name="mod",
    )(cond, w_ada, b_ada.reshape(1, n))


def _inproj_kernel(xp_ref, xs_ref, xprev_ref, xnext_ref, mod_ref, n1_ref, w_ref, *rest):
    rwkv_refs, (q_ref, kc_ref, vc_ref, kl_ref, vl_ref), rwkv_outs, (nk_ref, nv_ref) = (
        rest[:10], rest[10:15], rest[15:19], rest[19:])
    i = pl.program_id(0)
    has_prev, has_next = _seq_neighbours(i)

    def modulated(x):
        h = _rmsnorm(x, n1_ref[...]) * (1.0 + mod_ref[0, 1:2, :]) + mod_ref[0, 0:1, :]
        return h.astype(BF16)

    h = modulated(_ctx_or_lat(i, xp_ref, xs_ref))
    edge = modulated(jnp.concatenate([xprev_ref[...], xnext_ref[...]], axis=0))
    lhs = jnp.concatenate([h, edge], axis=0)

    def zcols(c0, c1):
        z_all = _dot(lhs, w_ref[:, 3 * WIDTH + c0:3 * WIDTH + c1])
        return (z_all[:TOK_BLOCK],
                jnp.where(has_prev, z_all[TOK_BLOCK + 7:TOK_BLOCK + 8], 0.0),
                jnp.where(has_next, z_all[TOK_BLOCK + 8:TOK_BLOCK + 9], 0.0))

    kv = []

    def project_q():
        q_ref[...] = _dot(h, w_ref[:, 0:WIDTH])

    def project_k():
        kv.append(_dot(h, w_ref[:, WIDTH:2 * WIDTH]))

    def project_v():
        kv.append(_dot(h, w_ref[:, 2 * WIDTH:3 * WIDTH]))

    _rwkv_tokens(zcols, (project_q, project_k, project_v), *rwkv_refs, *rwkv_outs)
    k, v = kv

    @pl.when(i < N_CTX_TOK // TOK_BLOCK)
    def _():
        kc_ref[...] = k
        vc_ref[...] = v
        nk_ref[0, 0] = k.reshape(SEQ, N_HEADS, HEAD_DIM)
        nv_ref[0, 0] = v.reshape(SEQ, N_HEADS, HEAD_DIM)

    @pl.when(i >= N_CTX_TOK // TOK_BLOCK)
    def _():
        kl_ref[...] = k
        vl_ref[...] = v


def _inproj(xp, xs, mod, norm1, w_in_bf, w_ts, w0, w2, a0, a2, g2, k_k, k_a, r_k, ones):
    half = jax.ShapeDtypeStruct((N_CTX_TOK, WIDTH), F32)
    tok = jax.ShapeDtypeStruct((N_TOK, WIDTH), F32)
    lat0 = N_CTX_TOK // TOK_BLOCK
    rows8 = TOK_BLOCK // 8
    n8 = N_LAT_TOK // 8
    prev8 = pl.BlockSpec((8, D_MODEL), lambda i: (jnp.clip((i - lat0) * rows8 - 1, 0, n8 - 1), 0))
    next8 = pl.BlockSpec((8, D_MODEL), lambda i: (jnp.clip((i - lat0 + 1) * rows8, 0, n8 - 1), 0))
    cache = pl.BlockSpec((1, 1, SEQ, N_HEADS, HEAD_DIM), lambda i: (jnp.minimum(i, BATCH - 1), 0, 0, 0, 0))
    return pl.pallas_call(
        _inproj_kernel,
        grid=(N_BLOCKS,),
        in_specs=[_ctx_tok_spec(D_MODEL), _lat_tok_spec(D_MODEL), prev8, next8, _mod_spec(),
                  _const_spec((1, D_MODEL)),
                  _const_spec((D_MODEL, 3 * WIDTH + RWKV_COLS)),
                  _const_spec((3, RWKV_COLS)),
                  _const_spec((N_DIRS, WIDTH)), _const_spec((N_DIRS, LORA, WIDTH)),
                  _const_spec((N_DIRS, WIDTH)), _const_spec((N_DIRS, LORA, WIDTH)),
                  _const_spec((GATE_LORA, WIDTH)),
                  _const_spec((1, WIDTH)), _const_spec((1, WIDTH)), _const_spec((1, WIDTH)),
                  _const_spec((WIDTH, WIDTH))],
        out_specs=[_tok_spec(WIDTH), _ctx_tok_spec(WIDTH), _ctx_tok_spec(WIDTH),
                   _lat_tok_spec(WIDTH), _lat_tok_spec(WIDTH),
                   _tok_spec(3 * WIDTH), _tok_spec(WIDTH), _tok_spec(WIDTH),
                   pl.BlockSpec((N_DIRS, TOK_BLOCK, 3 * WIDTH), lambda i: (0, i, 0)),
                   cache, cache],
        out_shape=[tok, half, half, half, half,
                   jax.ShapeDtypeStruct((N_TOK, 3 * WIDTH), F32), tok, tok,
                   jax.ShapeDtypeStruct((N_DIRS, N_TOK, 3 * WIDTH), F32),
                   jax.ShapeDtypeStruct((BATCH, 1, SEQ, N_HEADS, HEAD_DIM), F32),
                   jax.ShapeDtypeStruct((BATCH, 1, SEQ, N_HEADS, HEAD_DIM), F32)],
        compiler_params=_params(1),
        name="inproj",
    )(xp, xs, xs, xs, mod, norm1, w_in_bf, w_ts, w0, w2, a0, a2, g2, k_k, k_a, r_k, ones)


def _pair_queries(q):
    lo_half = lax.broadcasted_iota(jnp.int32, q.shape, 1) < HEAD_DIM
    return jnp.concatenate([jnp.where(lo_half, q, 0.0), jnp.where(lo_half, 0.0, q)], axis=0).astype(BF16)


def _pair_outputs(o):
    t = o.shape[0] // 2
    lo_half = lax.broadcasted_iota(jnp.int32, (t, PAIR_W), 1) < HEAD_DIM
    return jnp.where(lo_half, o[:t], o[t:])


def _ctx_attn_kernel(q_ref, k_ref, v_ref, o_ref):
    pairs = [slice(p * PAIR_W, (p + 1) * PAIR_W) for p in range(N_HEADS // 2)]
    scores = [_dot_nt(_pair_queries(q_ref[:, ps]), k_ref[:, ps].astype(BF16)) * ATT_SCALE for ps in pairs]
    probs = []
    for s in scores:
        e = jnp.exp(s - jnp.max(s, axis=-1, keepdims=True))
        probs.append((e.astype(BF16), jnp.sum(e, axis=-1, keepdims=True)))
    for ps, (e, l) in zip(pairs, probs):
        o_ref[:, ps] = _pair_outputs(_dot(e, v_ref[:, ps].astype(BF16)) / l).astype(BF16)


def _ctx_attention(q, k, v):
    spec = pl.BlockSpec((SEQ, WIDTH), lambda b: (b, 0))
    return pl.pallas_call(
        _ctx_attn_kernel,
        grid=(BATCH,),
        in_specs=[spec, spec, spec],
        out_specs=spec,
        out_shape=jax.ShapeDtypeStruct((N_CTX_TOK, WIDTH), BF16),
        compiler_params=_params(1),
        name="ctxattn",
    )(q, k, v)


def _lat_attn_kernel(q_ref, k_ref, v_ref, ck_ref, cv_ref, bias_ref, o_ref, kbf, vbf, ckbf, cvbf):
    i = pl.program_id(1)

    @pl.when(i == 0)
    def _():
        kbf[...] = k_ref[...].astype(BF16)
        vbf[...] = v_ref[...].astype(BF16)
        ckbf[...] = ck_ref[0].astype(BF16)
        cvbf[...] = cv_ref[0].astype(BF16)

    win = NA_ROWS * GRID_W
    rows = []
    for rr in range(LAT_ROWS):
        gr = i * LAT_ROWS + rr
        first_row = jnp.clip(gr - NA_ROWS // 2, 0, GRID_ROWS - NA_ROWS)
        start = pl.multiple_of(first_row * GRID_W, GRID_W)
        off0 = first_row - gr + NA_ROWS - 1
        rows.append((slice(rr * GRID_W, (rr + 1) * GRID_W), start, off0))
    pairs = [slice(p * PAIR_W, (p + 1) * PAIR_W) for p in range(N_HEADS // 2)]
    scores = []
    for p, ps in enumerate(pairs):
        lhs = [_pair_queries(q_ref[rs, ps]) for rs, _, _ in rows]
        s_loc = []
        for (_, start, off0), lhs_r in zip(rows, lhs):
            bias = jnp.concatenate(
                [jnp.concatenate([bias_ref[2 * p + hh, off0 + 2 * j] for j in range(NA_ROWS // 2)], axis=-1)
                 for hh in range(2)], axis=0)
            s_loc.append(_dot_nt(lhs_r, kbf[pl.ds(start, win), ps]) * ATT_SCALE + bias)
        s_ctx = _dot_nt(jnp.concatenate(lhs, axis=0), ckbf[:, ps]) * ATT_SCALE
        scores.append((jnp.concatenate(s_loc, axis=0), s_ctx))
    probs = []
    for s_loc, s_ctx in scores:
        m = jnp.maximum(jnp.max(s_loc, axis=-1, keepdims=True), jnp.max(s_ctx, axis=-1, keepdims=True))
        e_loc = jnp.exp(s_loc - m)
        e_ctx = jnp.exp(s_ctx - m)
        l = jnp.sum(e_loc, axis=-1, keepdims=True) + jnp.sum(e_ctx, axis=-1, keepdims=True)
        probs.append((e_loc.astype(BF16), e_ctx.astype(BF16), l))
    for ps, (e_loc, e_ctx, l) in zip(pairs, probs):
        o_ctx = _dot(e_ctx, cvbf[:, ps])
        for rr, (rs, start, _) in enumerate(rows):
            both = slice(rr * 2 * GRID_W, (rr + 1) * 2 * GRID_W)
            o = _dot(e_loc[both], vbf[pl.ds(start, win), ps]) + o_ctx[both]
            o_ref[rs, ps] = _pair_outputs(o / l[both]).astype(BF16)


def _na_bias_table(rpb):
    n = rpb.shape[-1]
    w = GRID_W
    pair = jnp.concatenate([rpb[:, :-1], rpb[:, 1:]], axis=-1).astype(F32)
    x = jnp.arange(2 * n)
    y = jnp.arange(2 * w)
    cq = jnp.arange(w)
    ck = y % w
    sel = ((x // n)[:, None, None] == (y // w)[None, None, :]) & (
        (x % n)[:, None, None] == ck[None, None, :] - cq[None, :, None] + NA_COLS - 1)
    t = jnp.einsum("hdx,xqy->hdqy", pair, sel.astype(F32), precision=HIGHEST)
    cs = jnp.clip(cq - NA_COLS // 2, 0, w - NA_COLS)[:, None]
    return jnp.where((ck[None, :] >= cs) & (ck[None, :] < cs + NA_COLS), t, MASK_VALUE)


def _lat_attention(q, k, v, cache_k, cache_v, bias):
    blk = LAT_ROWS * GRID_W
    steps = GRID_ROWS // LAT_ROWS
    lat0 = N_CTX_TOK // blk
    seq = pl.BlockSpec((DEC_SEQ, WIDTH), lambda b, i: (b, 0))
    ctx = pl.BlockSpec((1, PAST_LEN, WIDTH), lambda b, i: (b, 0, 0))
    return pl.pallas_call(
        _lat_attn_kernel,
        grid=(DEC_BATCH, steps),
        in_specs=[pl.BlockSpec((blk, WIDTH), lambda b, i: (lat0 + b * steps + i, 0)),
                  seq, seq, ctx, ctx,
                  _const_spec((N_HEADS, 2 * NA_ROWS - 2, GRID_W, PAIR_W))],
        out_specs=pl.BlockSpec((blk, WIDTH), lambda b, i: (b * steps + i, 0)),
        out_shape=jax.ShapeDtypeStruct((N_LAT_TOK, WIDTH), BF16),
        scratch_shapes=[pltpu.VMEM((DEC_SEQ, WIDTH), BF16), pltpu.VMEM((DEC_SEQ, WIDTH), BF16),
                        pltpu.VMEM((PAST_LEN, WIDTH), BF16), pltpu.VMEM((PAST_LEN, WIDTH), BF16)],
        compiler_params=_params(2),
        name="latattn",
    )(q, k, v, cache_k, cache_v, bias)


def _rwkv_tokens(zcols, fillers, wts_ref, w0_ref, w2_ref, a0_ref, a2_ref, g2_ref, kk_ref, ka_ref, rk_ref, ones_ref,
                 shared_out, g_out, bonus_out, dir_out):
    def conv(c0, c1):
        zc, prow, nrow = zcols(c0, c1)
        zprev, znext = _shifted(zc, prow, nrow)
        return zprev * wts_ref[0:1, c0:c1] + zc * wts_ref[1:2, c0:c1] + znext * wts_ref[2:3, c0:c1]

    fill = iter(fillers)
    dirs = [slice(d * LORA, (d + 1) * LORA) for d in range(N_DIRS)]
    ones = ones_ref[...]
    o = 3 * WIDTH
    lora_in = conv(o, RWKV_COLS)
    kr = conv(WIDTH, 2 * WIDTH)
    tw = jnp.tanh(lora_in[:, 0:N_DIRS * LORA])
    xa = lora_in[:, N_DIRS * LORA:2 * N_DIRS * LORA]
    xg = lora_in[:, 2 * N_DIRS * LORA:]
    kkraw = kr * kk_ref[...]
    next(fill)()
    kk = kkraw * lax.rsqrt(_segsum(kkraw * kkraw, ones) + 1e-12)
    lora_w = [_mm(_split(tw[:, ls]), _split(w2_ref[d])) for d, ls in enumerate(dirs)]
    r = conv(0, WIDTH)
    next(fill)()
    lora_a = [_mm(_split(xa[:, ls]), _split(a2_ref[d])) for d, ls in enumerate(dirs)]
    g_out[...] = _mm(_split(jax.nn.sigmoid(xg)), _split(g2_ref[...]))
    v = conv(2 * WIDTH, 3 * WIDTH)
    next(fill)()
    kd_sum = jnp.zeros_like(kr)
    for d in range(N_DIRS):
        w_log = -_softplus(-(w0_ref[d:d + 1, :] + lora_w[d])) - 0.5
        a = jax.nn.sigmoid(a0_ref[d:d + 1, :] + lora_a[d])
        kd = kr * (1.0 + (a - 1.0) * ka_ref[...])
        dir_out[d, :, 0:WIDTH] = -jnp.exp(w_log)
        dir_out[d, :, WIDTH:2 * WIDTH] = kk * a
        dir_out[d, :, 2 * WIDTH:] = kd
        kd_sum = kd_sum + kd
    shared_out[:, 0:WIDTH] = r
    shared_out[:, WIDTH:2 * WIDTH] = kk
    shared_out[:, 2 * WIDTH:] = v
    bonus_out[...] = _segsum(r * kd_sum * rk_ref[...], ones) * v


HALF = CHUNK // 2


def _blockdiag(x, block=HEAD_DIM):
    p = x.astype(BF16)
    blk = lax.broadcasted_iota(jnp.int32, p.shape, 1) // block
    return jnp.concatenate([jnp.where(blk == q, p, jnp.zeros_like(p)) for q in range(GROUP_W // block)], axis=0)


def _hp(a, b, block=HEAD_DIM):
    return _dot(a.astype(BF16), _blockdiag(b, block))


def _hp_nt(a, b):
    return _dot_nt(a.astype(BF16), _blockdiag(b))


def _head_transpose(x):
    xt = jnp.transpose(x)
    return jnp.concatenate([xt[h * HEAD_DIM:(h + 1) * HEAD_DIM] for h in range(GROUP)], axis=1)


def _scan_prepare(d, rows, shared_ref, dir_ref, y_ref, s_ref):
    row = lax.broadcasted_iota(jnp.int32, (CHUNK, GROUP_W), 0)
    lane = lax.broadcasted_iota(jnp.int32, (CHUNK, GROUP_W), 1)
    col = lane % CHUNK
    strict = col > row if d else col < row
    incl = col >= row if d else col <= row
    row_h = lax.broadcasted_iota(jnp.int32, (HALF, GROUP_W), 0)
    lane_h = lax.broadcasted_iota(jnp.int32, (HALF, GROUP_W), 1)
    lo_s = lane_h % CHUNK < HALF
    eye_half = (lane_h % HALF == row_h).astype(F32)

    lw = dir_ref[0, rows, 0:WIDTH]
    mask = jnp.where(incl[:, :CHUNK], 1.0, 0.0).astype(BF16)
    lw_hi, lw_mid = _split(lw)
    lw_lo = (lw - lw_hi.astype(F32) - lw_mid.astype(F32)).astype(BF16)
    cum = _dot(mask, lw_hi) + _dot(mask, lw_mid) + _dot(mask, lw_lo)
    tot = jnp.sum(lw, axis=0, keepdims=True)
    p_in = jnp.exp(cum)
    p_neg = jnp.exp(-cum)
    p_rem = jnp.exp(tot - cum)
    kt = shared_ref[rows, WIDTH:2 * WIDTH] * jnp.exp(cum - lw)
    rt = shared_ref[rows, 0:WIDTH] * p_in
    b = dir_ref[0, rows, WIDTH:2 * WIDTH]
    kd = dir_ref[0, rows, 2 * WIDTH:]
    bt = b * p_neg
    kdt = kd * p_neg
    bh = b * p_rem
    kh = kd * p_rem
    p_end = jnp.exp(tot)
    v = shared_ref[rows, 2 * WIDTH:]

    chains = []
    for gi in range(N_HEADS // GROUP):
        sl = slice(gi * GROUP_W, (gi + 1) * GROUP_W)
        chains.append(dict(d=d, rows=rows, sl=sl, strict=strict, incl=incl, lo_s=lo_s, eye_half=eye_half, y_ref=y_ref, s_ref=s_ref,
                           kt=kt[:, sl], rt=rt[:, sl], v=v[:, sl], bt=bt[:, sl], kdt=kdt[:, sl],
                           kh=kh[:, sl], bh=bh[:, sl], p_end=p_end[:, sl]))
    return chains


def _scan_scores(chains):
    for c in chains:
        lhs = jnp.concatenate([c["kt"], c["rt"]], axis=0)
        g_b = _hp_nt(lhs, c["bt"])
        g_k = _hp_nt(lhs, c["kdt"])
        c["a_k"] = jnp.where(c["strict"], g_k[:CHUNK], 0.0)
        c["a_rb"] = jnp.where(c["incl"], g_b[CHUNK:], 0.0)
        c["a_rk"] = jnp.where(c["incl"], g_k[CHUNK:], 0.0)
        n = jnp.where(c["strict"], -g_b[:CHUNK], 0.0)
        c["n"] = n
        c["pw"] = jnp.where(c["lo_s"], n[:HALF], n[HALF:])
        c["tp"] = c["eye_half"] + c["pw"]
    return chains


def _scan_local(chains):
    zeros = jnp.zeros((HALF, GROUP_W), F32)
    for c in chains:
        c["pw"] = _hp(c["pw"], c["pw"], HALF)
    for _ in range(3):
        for c in chains:
            both = _hp(jnp.concatenate([c["tp"], c["pw"]], axis=0), c["pw"], HALF)
            c["tp"] = c["tp"] + both[:HALF]
            c["pw"] = both[HALF:]
    for c in chains:
        c["tp"] = c["tp"] + _hp(c["tp"], c["pw"], HALF)
    for c in chains:
        c["t1"] = jnp.where(c["lo_s"], c["tp"], 0.0)
        c["t2"] = jnp.where(c["lo_s"], 0.0, c["tp"])
        if c["d"] == 0:
            off, inner = jnp.where(c["lo_s"], c["n"][HALF:], 0.0), jnp.concatenate([c["t1"], zeros], axis=0)
        else:
            off, inner = jnp.where(c["lo_s"], 0.0, c["n"][:HALF]), jnp.concatenate([zeros, c["t2"]], axis=0)
        c["off"] = _hp(off, inner)
    for c in chains:
        if c["d"] == 0:
            x = _hp(c["t2"], jnp.concatenate([zeros, c["off"]], axis=0))
            c["inv"] = jnp.concatenate([c["t1"], jnp.where(c["lo_s"], x, c["tp"])], axis=0)
        else:
            x = _hp(c["t1"], jnp.concatenate([c["off"], zeros], axis=0))
            c["inv"] = jnp.concatenate([jnp.where(c["lo_s"], c["tp"], x), c["t2"]], axis=0)
    for c in chains:
        c["av"] = _hp(jnp.concatenate([c["a_k"], c["a_rk"]], axis=0), c["v"])
        c["vt"] = _head_transpose(c["v"])
    return chains


def _scan_carry(chains):
    for c in chains:
        c["s"] = c["s_ref"][0, :, c["sl"]]
        c["z"] = _hp_nt(jnp.concatenate([c["rt"], c["kt"]], axis=0), c["s"])
    for c in chains:
        c["u"] = _hp(c["inv"], c["z"][CHUNK:] + c["av"][:CHUNK])
    for c in chains:
        c["y_ref"][c["rows"], c["sl"]] = c["z"][:CHUNK] + c["av"][CHUNK:] - _hp(c["a_rb"], c["u"])
    for c in chains:
        upd = _hp(c["vt"], c["kh"]) - _hp(_head_transpose(c["u"]), c["bh"])
        c["s_ref"][0, :, c["sl"]] = c["s"] * c["p_end"] + upd


REGION_CHUNKS = CTX_CHUNKS
STEP_CHUNKS = 2
STEP_ROWS = STEP_CHUNKS * CHUNK
SCAN_STEPS = REGION_CHUNKS // STEP_CHUNKS
SCAN_STREAMS = tuple((d, base // STEP_CHUNKS, per_seq // STEP_CHUNKS) for d in range(N_DIRS)
                     for base, per_seq in ((0, CHUNKS_PER_CTX_SEQ), (CTX_CHUNKS, CHUNKS_PER_LAT_SEQ)))
STREAM_INS = 3


def _stream_local_block(stream, j):
    return SCAN_STEPS - 1 - j if stream[0] else j


def _scan_kernel(n_cast, *refs):
    n_stream_in = STREAM_INS * len(SCAN_STREAMS)
    n_in = n_stream_in + n_cast
    j = pl.program_id(0)
    for w_ref, o_ref in zip(refs[n_stream_in:n_in], refs[len(refs) - n_cast:]):
        o_ref[...] = w_ref[...].astype(BF16)
    per_stream = []
    for s, stream in enumerate(SCAN_STREAMS):
        d, _, per_seq = stream
        ins = refs[STREAM_INS * s:STREAM_INS * (s + 1)]
        y_ref, s_ref = refs[n_in + 2 * s:n_in + 2 * s + 2]
        s0_ref = ins[-1]
        local = _stream_local_block(stream, j) % per_seq

        @pl.when(local == (per_seq - 1 if d else 0))
        def _():
            s_ref[...] = s0_ref[0]

        per_stream.append((d,) + ins[:-1] + (y_ref, s_ref))

    chains_by_chunk = []
    for k in range(STEP_CHUNKS):
        chains = []
        for d, shared_ref, dir_ref, y_ref, s_ref in per_stream:
            first = (STEP_CHUNKS - 1 - k if d else k) * CHUNK
            chains += _scan_prepare(d, slice(first, first + CHUNK), shared_ref, dir_ref, y_ref, s_ref)
        chains_by_chunk.append(_scan_local(_scan_scores(chains)))
    for chains in chains_by_chunk:
        _scan_carry(chains)


def _scan(shared, perdir, s0, weights):
    in_specs, out_specs, out_shape, operands = [], [], [], []
    cast_in, cast_out, cast_shape = [], [], []
    for w in weights:
        rows = w.shape[0] // SCAN_STEPS
        if rows % 16 or rows * SCAN_STEPS != w.shape[0]:
            rows = 128
        assert w.shape[0] % rows == 0 and w.shape[0] // rows <= SCAN_STEPS
        spec = pl.BlockSpec((rows, w.shape[1]), lambda j, n=w.shape[0] // rows: (jnp.minimum(j, n - 1), 0))
        cast_in.append(spec)
        cast_out.append(spec)
        cast_shape.append(jax.ShapeDtypeStruct(w.shape, BF16))
    for stream in SCAN_STREAMS:
        d, base, per_seq = stream
        seq0 = 0 if base == 0 else BATCH
        n_seq = SCAN_STEPS // per_seq
        loc = lambda j, stream=stream: _stream_local_block(stream, j)
        tok = pl.BlockSpec((STEP_ROWS, 3 * WIDTH), lambda j, loc=loc, base=base: (base + loc(j), 0))
        dtok = pl.BlockSpec((1, STEP_ROWS, 3 * WIDTH), lambda j, loc=loc, base=base, d=d: (d, base + loc(j), 0))
        st_in = pl.BlockSpec((1, 1, HEAD_DIM, WIDTH),
                             lambda j, loc=loc, seq0=seq0, per_seq=per_seq, d=d: (seq0 + loc(j) // per_seq, d, 0, 0))
        in_specs += [tok, dtok, st_in]
        operands += [shared, perdir, s0]
        out_specs += [pl.BlockSpec((STEP_ROWS, WIDTH), lambda j, loc=loc: (loc(j), 0)),
                      pl.BlockSpec((1, HEAD_DIM, WIDTH), lambda j, loc=loc, per_seq=per_seq: (loc(j) // per_seq, 0, 0))]
        out_shape += [jax.ShapeDtypeStruct((REGION_CHUNKS * CHUNK, WIDTH), F32),
                      jax.ShapeDtypeStruct((n_seq, HEAD_DIM, WIDTH), F32)]
    return pl.pallas_call(
        functools.partial(_scan_kernel, len(weights)),
        grid=(SCAN_STEPS,),
        in_specs=in_specs + cast_in,
        out_specs=out_specs + cast_out,
        out_shape=out_shape + cast_shape,
        compiler_params=_params(1),
        name="scan",
    )(*operands, *weights)


def _post_kernel(yfc_ref, yfl_ref, ybc_ref, ybl_ref, bonus_ref, g_ref, attc_ref, attl_ref, xp_ref, xs_ref, mod_ref,
                 lng_ref, lnb_ref, ones_ref, wout_ref, n2_ref, x1_ref, h2_ref):
    i = pl.program_id(0)
    def pick(ctx_ref, lat_ref):
        return _ctx_or_lat(i, ctx_ref, lat_ref, BIG_BLOCK)

    ones = ones_ref[...]
    inv_n = 1.0 / HEAD_DIM
    y = pick(yfc_ref, yfl_ref) + pick(ybc_ref, ybl_ref)
    yc = y - _segsum(y, ones) * inv_n
    var = _segsum(yc * yc, ones) * inv_n
    yn = yc * lax.rsqrt(var + GN_EPS) * lng_ref[...] + lnb_ref[...]
    r_out = ((yn + bonus_ref[...]) * g_ref[...]).astype(BF16)
    o = _dot(pick(attc_ref, attl_ref), wout_ref[0:WIDTH, :]) + _dot(r_out, wout_ref[WIDTH:, :])
    x1 = pick(xp_ref, xs_ref) + mod_ref[0, 2:3, :] * o
    x1_ref[...] = x1
    h2 = _rmsnorm(x1, n2_ref[...]) * (1.0 + mod_ref[0, 4:5, :]) + mod_ref[0, 3:4, :]
    h2_ref[...] = h2.astype(BF16)


def _post(ys, bonus, g, att_ctx, att_lat, xp, xs, mod, ln_g, ln_b, ones, w_out_bf, norm2):
    blk = BIG_BLOCK
    tok = lambda cols: _tok_spec(cols, blk)
    ctx_lat = [_ctx_tok_spec(WIDTH, blk), _lat_tok_spec(WIDTH, blk)]
    return pl.pallas_call(
        _post_kernel,
        grid=(N_TOK // blk,),
        in_specs=ctx_lat + ctx_lat + [tok(WIDTH), tok(WIDTH)]
        + ctx_lat + [_ctx_tok_spec(D_MODEL, blk), _lat_tok_spec(D_MODEL, blk),
                  _mod_spec(blk),
                  _const_spec((1, WIDTH)), _const_spec((1, WIDTH)), _const_spec((WIDTH, WIDTH)),
                  _const_spec((2 * WIDTH, D_MODEL)), _const_spec((1, D_MODEL))],
        out_specs=[tok(D_MODEL), tok(D_MODEL)],
        out_shape=[jax.ShapeDtypeStruct((N_TOK, D_MODEL), F32),
                   jax.ShapeDtypeStruct((N_TOK, D_MODEL), BF16)],
        compiler_params=_params(1),
        name="post",
    )(*ys, bonus, g, att_ctx, att_lat, xp, xs, mod, ln_g, ln_b, ones, w_out_bf, norm2)


def _ffn_kernel(x1_ref, h2_ref, hp_ref, hn_ref, mod_ref, w1_ref, w3_ref, wc_ref, w2_ref, nf_ref, yp_ref, ys_ref):
    i = pl.program_id(0)
    blk = BIG_BLOCK
    is_ctx = i < N_CTX_TOK // blk
    has_prev, has_next = _seq_neighbours(i, blk)
    lhs = jnp.concatenate([h2_ref[...], hp_ref[...], hn_ref[...]], axis=0)
    h2 = lhs[:blk]
    rid = lax.broadcasted_iota(jnp.int32, (blk, 1), 0)
    inner = jnp.where(is_ctx, SEQ, -1)
    seq_start = rid == inner
    seq_end = rid == inner - 1
    acc = jnp.zeros((blk, D_MODEL), F32)
    for f in range(D_FF // FF_TILE):
        fs = slice(f * FF_TILE, (f + 1) * FF_TILE)
        a_all = _dot(lhs, w1_ref[:, fs])
        a = a_all[:blk]
        prow = jnp.where(has_prev, a_all[blk + 15:blk + 16, :], 0.0)
        nrow = jnp.where(has_next, a_all[blk + 16:blk + 17, :], 0.0)
        aprev, anext = _shifted(a, prow, nrow)
        aprev = jnp.where(seq_start, 0.0, aprev)
        anext = jnp.where(seq_end, 0.0, anext)
        cv = aprev * wc_ref[0:1, fs] + a * wc_ref[1:2, fs] + anext * wc_ref[2:3, fs]
        act = _silu(cv) * _dot(h2, w3_ref[:, fs])
        acc = acc + _dot(act.astype(BF16), w2_ref[fs, :])
    x2 = x1_ref[...] + mod_ref[0, 5:6, :] * acc
    y = _rmsnorm(x2, nf_ref[...])

    @pl.when(is_ctx)
    def _():
        yp_ref[...] = y

    @pl.when(jnp.logical_not(is_ctx))
    def _():
        ys_ref[...] = y


def _ffn(x1, h2, mod, w1_bf, w3_bf, wc, w2_bf, norm_f):
    blk = BIG_BLOCK
    rows16 = blk // 16
    return pl.pallas_call(
        _ffn_kernel,
        grid=(N_TOK // blk,),
        in_specs=[_tok_spec(D_MODEL, blk), _tok_spec(D_MODEL, blk),
                  pl.BlockSpec((16, D_MODEL), lambda i: (jnp.maximum(i * rows16 - 1, 0), 0)),
                  pl.BlockSpec((16, D_MODEL), lambda i: (jnp.minimum((i + 1) * rows16, N_TOK // 16 - 1), 0)),
                  _mod_spec(blk),
                  _const_spec((D_MODEL, D_FF)), _const_spec((D_MODEL, D_FF)), _const_spec((3, D_FF)),
                  _const_spec((D_FF, D_MODEL)), _const_spec((1, D_MODEL))],
        out_specs=[_ctx_tok_spec(D_MODEL, blk), _lat_tok_spec(D_MODEL, blk)],
        out_shape=[jax.ShapeDtypeStruct((N_CTX_TOK, D_MODEL), F32),
                   jax.ShapeDtypeStruct((N_LAT_TOK, D_MODEL), F32)],
        compiler_params=_params(1),
        name="ffn",
    )(x1, h2, h2, h2, mod, w1_bf, w3_bf, wc, w2_bf, norm_f)


def kernel(x_prompt, x_sample, cache_k, cache_v, state_rwkv, c, c_ctx, w_ada, b_ada, norm1, norm2,
           w_in, w_ts, w0, w2, a0, a2, g2, k_k, k_a, r_k, ln_x_g, ln_x_b, rpb, w_out,
           w_ffn1, w_ffn3, w_ffn_conv, w_ffn2, norm_f):
    xp = x_prompt.reshape(N_CTX_TOK, D_MODEL)
    xs = x_sample.reshape(N_LAT_TOK, D_MODEL)
    row = lambda t: t.reshape(1, -1)
    cond = jnp.concatenate([c_ctx[None, :], c, jnp.zeros((8 - 1 - DEC_BATCH, D_MODEL), F32)], axis=0)
    mod = _modulation(cond, w_ada[0], b_ada[0]).reshape(8, 6, D_MODEL)

    head_id = jnp.arange(WIDTH) // HEAD_DIM
    ones = (head_id[:, None] == head_id[None, :]).astype(BF16)
    q, k_ctx, v_ctx, k_lat, v_lat, shared, g, bonus, perdir, new_k, new_v = _inproj(
        xp, xs, mod, row(norm1[0]), w_in[0].astype(BF16), w_ts[0], w0[0], w2[0], a0[0], a2[0], g2[0],
        row(k_k[0]), row(k_a[0]), row(r_k[0]), ones)

    att_ctx = _ctx_attention(q, k_ctx, v_ctx)
    att_lat = _lat_attention(q, k_lat, v_lat, cache_k[:, 0].reshape(DEC_BATCH, PAST_LEN, WIDTH),
                             cache_v[:, 0].reshape(DEC_BATCH, PAST_LEN, WIDTH), _na_bias_table(rpb[0]))
    s_lat = jnp.transpose(state_rwkv[:, 0], (0, 1, 3, 2, 4)).reshape(DEC_BATCH, N_DIRS, HEAD_DIM, WIDTH)
    s0 = jnp.concatenate([jnp.zeros((BATCH, N_DIRS, HEAD_DIM, WIDTH), F32), s_lat], axis=0)
    (y_fc, s_fc, y_fl, _, y_bc, s_bc, y_bl, _,
     w_out_bf, w1_bf, w3_bf, w2_bf) = _scan(shared, perdir, s0, (w_out[0], w_ffn1[0], w_ffn3[0], w_ffn2[0]))
    s_fin = jnp.stack([s_fc, s_bc], axis=1)

    x1, h2 = _post((y_fc, y_fl, y_bc, y_bl), bonus, g, att_ctx, att_lat, xp, xs, mod, row(ln_x_g[0]), row(ln_x_b[0]), ones,
                   w_out_bf, row(norm2[0]))
    yp, ys = _ffn(x1, h2, mod, w1_bf, w3_bf, w_ffn_conv[0], w2_bf, row(norm_f))

    new_s = jnp.transpose(s_fin.reshape(BATCH, N_DIRS, HEAD_DIM, N_HEADS, HEAD_DIM),
                          (0, 1, 3, 2, 4)).reshape(BATCH, 1, N_DIRS, N_HEADS, HEAD_DIM, HEAD_DIM)
    return (yp.reshape(BATCH, SEQ, D_MODEL), ys.reshape(DEC_BATCH, DEC_SEQ, D_MODEL), new_k, new_v, new_s)
```

```python
import functools

import jax
import jax.numpy as jnp
from jax import lax
from jax.experimental import pallas as pl
from jax.experimental.pallas import tpu as pltpu

F32 = jnp.float32
BF16 = jnp.bfloat16
HIGHEST = lax.Precision.HIGHEST

D_MODEL = 1024
BATCH = 16
SEQ = 256
DEC_BATCH = 2
DEC_SEQ = 2048
PAST_LEN = 512
GRID_W = 64
HEAD_DIM = 64
N_HEADS = 8
WIDTH = N_HEADS * HEAD_DIM
PAIR_W = 2 * HEAD_DIM
NA_ROWS = 8
NA_COLS = 16
N_DIRS = 2
LORA = 64
GATE_LORA = 128
D_FF = 2816
EPS = 1e-6
GN_EPS = 64e-5
ATT_SCALE = HEAD_DIM ** -0.5
RWKV_COLS = 3 * WIDTH + N_DIRS * 2 * LORA + GATE_LORA
MASK_VALUE = -1e30

N_CTX_TOK = BATCH * SEQ
N_LAT_TOK = DEC_BATCH * DEC_SEQ
N_TOK = N_CTX_TOK + N_LAT_TOK
TOK_BLOCK = 256
N_BLOCKS = N_TOK // TOK_BLOCK
BIG_BLOCK = 512
PAD_TOK = TOK_BLOCK
N_SEQS = BATCH + DEC_BATCH
CHUNK = 64
N_CHUNKS = N_TOK // CHUNK
CTX_CHUNKS = N_CTX_TOK // CHUNK
CHUNKS_PER_CTX_SEQ = SEQ // CHUNK
CHUNKS_PER_LAT_SEQ = DEC_SEQ // CHUNK
GRID_ROWS = DEC_SEQ // GRID_W
LAT_ROWS = 2
FF_TILE = D_FF // 2
GROUP = 4
GROUP_W = GROUP * HEAD_DIM
VMEM_LIMIT = 56 * 1024 * 1024


def _params(n_axes, limit=VMEM_LIMIT):
    return pltpu.CompilerParams(dimension_semantics=("arbitrary",) * n_axes, vmem_limit_bytes=limit)


def _const_spec(shape):
    zeros = (0,) * len(shape)
    return pl.BlockSpec(shape, lambda *_: zeros, pipeline_mode=pl.Buffered(1))


def _tok_spec(cols, blk=TOK_BLOCK):
    return pl.BlockSpec((blk, cols), lambda i: (i, 0))


def _ctx_tok_spec(cols, blk=TOK_BLOCK):
    return pl.BlockSpec((blk, cols), lambda i: (jnp.minimum(i, N_CTX_TOK // blk - 1), 0))


def _lat_tok_spec(cols, blk=TOK_BLOCK):
    return pl.BlockSpec((blk, cols), lambda i: (jnp.maximum(i - N_CTX_TOK // blk, 0), 0))


def _ctx_or_lat(i, ctx_ref, lat_ref, blk=TOK_BLOCK):
    return jnp.where(i < N_CTX_TOK // blk, ctx_ref[...], lat_ref[...])


def _mod_spec(blk=TOK_BLOCK):
    def row(i):
        return jnp.where(i < N_CTX_TOK // blk, 0, 1 + (i - N_CTX_TOK // blk) // (DEC_SEQ // blk))
    return pl.BlockSpec((1, 6, D_MODEL), lambda i: (row(i), 0, 0))


def _seq_neighbours(i, blk=TOK_BLOCK):
    per_seq = DEC_SEQ // blk
    j = (i - N_CTX_TOK // blk) % per_seq
    lat = i >= N_CTX_TOK // blk
    return lat & (j != 0), lat & (j != per_seq - 1)


def _silu(x):
    return x * jax.nn.sigmoid(x)


def _softplus(x):
    return jnp.maximum(x, 0.0) + jnp.log(1.0 + jnp.exp(-jnp.abs(x)))


def _rmsnorm(x, g):
    return x * lax.rsqrt(jnp.mean(x * x, axis=-1, keepdims=True) + EPS) * g


def _dot(a, b, precision=None):
    return jnp.dot(a, b, precision=precision, preferred_element_type=F32)


def _dot_nt(a, b, precision=None):
    return lax.dot_general(a, b, (((1,), (1,)), ((), ())), precision=precision, preferred_element_type=F32)


def _split(x):
    hi = x.astype(BF16)
    return hi, (x - hi.astype(F32)).astype(BF16)


def _mm3(dot, a, b):
    m = a[0].shape[0]
    hi = dot(jnp.concatenate([a[0], a[1]], axis=0), b[0])
    return hi[:m] + hi[m:] + dot(a[0], b[1])


def _mm(a, b):
    return _mm3(_dot, a, b)


def _segsum(x, ones_bf):
    m = x.shape[0]
    hi, lo = _split(x)
    s = _dot(jnp.concatenate([hi, lo], axis=0), ones_bf)
    return s[:m] + s[m:]


def _shifted(a, prev_row, next_row):
    t = a.shape[0]
    rid = lax.broadcasted_iota(jnp.int32, (t, 1), 0)
    prev = jnp.where(rid == 0, prev_row, pltpu.roll(a, 1, axis=0))
    nxt = jnp.where(rid == t - 1, next_row, pltpu.roll(a, t - 1, axis=0))
    return prev, nxt


def _mod_kernel(cond_ref, w_ref, b_ref, o_ref):
    o_ref[...] = _dot(_silu(cond_ref[...]), w_ref[...], HIGHEST) + b_ref[...]


def _modulation(cond, w_ada, b_ada):
    n = 6 * D_MODEL
    return pl.pallas_call(
        _mod_kernel,
        grid=(6,),
        in_specs=[pl.BlockSpec((8, D_MODEL), lambda j: (0, 0)),
                  pl.BlockSpec((D_MODEL, D_MODEL), lambda j: (0, j)),
                  pl.BlockSpec((1, D_MODEL), lambda j: (0, j))],
        out_specs=pl.BlockSpec((8, D_MODEL), lambda j: (0, j)),
        out_shape=jax.ShapeDtypeStruct((8, n), F32),
        compiler_params=_params(1),
        name="mod",
    )(cond, w_ada, b_ada.reshape(1, n))


def _inproj_kernel(xp_ref, xs_ref, xprev_ref, xnext_ref, mod_ref, n1_ref, w_ref, *rest):
    rwkv_refs, (q_ref, k_ref, v_ref), rwkv_outs, (nk_ref, nv_ref), z_bufs = (
        rest[:10], rest[10:13], rest[13:17], rest[17:19], rest[19:])
    i = pl.program_id(0)
    blk = jnp.minimum(i, N_BLOCKS - 1)

    @pl.when(i == 0)
    def _():
        z_bufs[1][...] = jnp.zeros_like(z_bufs[1])

    def modulated(x):
        h = _rmsnorm(x, n1_ref[...]) * (1.0 + mod_ref[0, 1:2, :]) + mod_ref[0, 0:1, :]
        return h.astype(BF16)

    def step(z_store, z_load):
        h = modulated(_ctx_or_lat(blk, xp_ref, xs_ref))
        edge = modulated(jnp.concatenate([xprev_ref[...], xnext_ref[...]], axis=0))
        lhs = jnp.concatenate([h, edge], axis=0)
        kv = []

        def project(c0, c1):
            z_store[:, c0:c1] = _dot(lhs, w_ref[:, 3 * WIDTH + c0:3 * WIDTH + c1])

        def portion_q():
            project(3 * WIDTH, RWKV_COLS)
            project(WIDTH, 2 * WIDTH)
            q_ref[...] = _dot(h, w_ref[:, 0:WIDTH])

        def portion_k():
            project(0, WIDTH)
            kv.append(_dot(h, w_ref[:, WIDTH:2 * WIDTH]))

        def portion_v():
            project(2 * WIDTH, 3 * WIDTH)
            kv.append(_dot(h, w_ref[:, 2 * WIDTH:3 * WIDTH]))

        has_prev, has_next = _seq_neighbours(i - 1)

        def zcols(c0, c1):
            return (z_load[0:TOK_BLOCK, c0:c1],
                    jnp.where(has_prev, z_load[TOK_BLOCK + 7:TOK_BLOCK + 8, c0:c1], 0.0),
                    jnp.where(has_next, z_load[TOK_BLOCK + 8:TOK_BLOCK + 9, c0:c1], 0.0))

        _rwkv_tokens(zcols, (portion_q, portion_k, portion_v), *rwkv_refs, *rwkv_outs)
        k, v = kv
        k_ref[...] = k
        v_ref[...] = v

        @pl.when(i < N_CTX_TOK // TOK_BLOCK)
        def _():
            nk_ref[0, 0] = k.reshape(SEQ, N_HEADS, HEAD_DIM)
            nv_ref[0, 0] = v.reshape(SEQ, N_HEADS, HEAD_DIM)

    @pl.when(i % 2 == 0)
    def _():
        step(z_bufs[0], z_bufs[1])

    @pl.when(i % 2 == 1)
    def _():
        step(z_bufs[1], z_bufs[0])


def _inproj(xp, xs, mod, norm1, w_in_bf, w_ts, w0, w2, a0, a2, g2, k_k, k_a, r_k, ones):
    tok = jax.ShapeDtypeStruct((N_TOK, WIDTH), F32)
    padded = lambda cols: jax.ShapeDtypeStruct((N_TOK + PAD_TOK, cols), F32)
    lat0 = N_CTX_TOK // TOK_BLOCK
    last = N_BLOCKS - 1
    rows8 = TOK_BLOCK // 8
    n8 = N_LAT_TOK // 8
    clamped = lambda cols: pl.BlockSpec((TOK_BLOCK, cols), lambda i: (jnp.minimum(i, last), 0))
    x_ctx = pl.BlockSpec((TOK_BLOCK, D_MODEL), lambda i: (jnp.minimum(i, lat0 - 1), 0))
    x_lat = pl.BlockSpec((TOK_BLOCK, D_MODEL), lambda i: (jnp.clip(i - lat0, 0, last - lat0), 0))
    prev8 = pl.BlockSpec((8, D_MODEL), lambda i: (jnp.clip((jnp.minimum(i, last) - lat0) * rows8 - 1, 0, n8 - 1), 0))
    next8 = pl.BlockSpec((8, D_MODEL), lambda i: (jnp.clip((jnp.minimum(i, last) - lat0 + 1) * rows8, 0, n8 - 1), 0))
    mod_rows = pl.BlockSpec((1, 6, D_MODEL), lambda i: (
        jnp.where(i < lat0, 0, 1 + (jnp.minimum(i, last) - lat0) // (DEC_SEQ // TOK_BLOCK)), 0, 0))
    cache = pl.BlockSpec((1, 1, SEQ, N_HEADS, HEAD_DIM), lambda i: (jnp.minimum(i, BATCH - 1), 0, 0, 0, 0))
    z_buf = pltpu.VMEM((TOK_BLOCK + 16, RWKV_COLS), F32)
    return pl.pallas_call(
        _inproj_kernel,
        grid=(N_BLOCKS + 1,),
        in_specs=[x_ctx, x_lat, prev8, next8, mod_rows,
                  _const_spec((1, D_MODEL)),
                  _const_spec((D_MODEL, 3 * WIDTH + RWKV_COLS)),
                  _const_spec((3, RWKV_COLS)),
                  _const_spec((N_DIRS, WIDTH)), _const_spec((N_DIRS, LORA, WIDTH)),
                  _const_spec((N_DIRS, WIDTH)), _const_spec((N_DIRS, LORA, WIDTH)),
                  _const_spec((GATE_LORA, WIDTH)),
                  _const_spec((1, WIDTH)), _const_spec((1, WIDTH)), _const_spec((1, WIDTH)),
                  _const_spec((WIDTH, WIDTH))],
        out_specs=[clamped(WIDTH), clamped(WIDTH), clamped(WIDTH),
                   _tok_spec(3 * WIDTH), _tok_spec(WIDTH), _tok_spec(WIDTH),
                   pl.BlockSpec((N_DIRS, TOK_BLOCK, 3 * WIDTH), lambda i: (0, i, 0)),
                   cache, cache],
        out_shape=[tok, tok, tok,
                   padded(3 * WIDTH), padded(WIDTH), padded(WIDTH),
                   jax.ShapeDtypeStruct((N_DIRS, N_TOK + PAD_TOK, 3 * WIDTH), F32),
                   jax.ShapeDtypeStruct((BATCH, 1, SEQ, N_HEADS, HEAD_DIM), F32),
                   jax.ShapeDtypeStruct((BATCH, 1, SEQ, N_HEADS, HEAD_DIM), F32)],
        scratch_shapes=[z_buf, z_buf],
        compiler_params=_params(1),
        name="inproj",
    )(xp, xs, xs, xs, mod, norm1, w_in_bf, w_ts, w0, w2, a0, a2, g2, k_k, k_a, r_k, ones)


def _pair_queries(q):
    lo_half = lax.broadcasted_iota(jnp.int32, q.shape, 1) < HEAD_DIM
    return jnp.concatenate([jnp.where(lo_half, q, 0.0), jnp.where(lo_half, 0.0, q)], axis=0).astype(BF16)


def _pair_outputs(o):
    t = o.shape[0] // 2
    lo_half = lax.broadcasted_iota(jnp.int32, (t, PAIR_W), 1) < HEAD_DIM
    return jnp.where(lo_half, o[:t], o[t:])


def _ctx_attn_kernel(q_ref, k_ref, v_ref, o_ref):
    pairs = [slice(p * PAIR_W, (p + 1) * PAIR_W) for p in range(N_HEADS // 2)]
    scores = [_dot_nt(_pair_queries(q_ref[:, ps]), k_ref[:, ps].astype(BF16)) * ATT_SCALE for ps in pairs]
    probs = []
    for s in scores:
        e = jnp.exp(s - jnp.max(s, axis=-1, keepdims=True))
        probs.append((e.astype(BF16), jnp.sum(e, axis=-1, keepdims=True)))
    for ps, (e, l) in zip(pairs, probs):
        o_ref[:, ps] = _pair_outputs(_dot(e, v_ref[:, ps].astype(BF16)) / l).astype(BF16)


def _ctx_attention(q, k, v):
    spec = pl.BlockSpec((SEQ, WIDTH), lambda b: (b, 0))
    return pl.pallas_call(
        _ctx_attn_kernel,
        grid=(BATCH,),
        in_specs=[spec, spec, spec],
        out_specs=spec,
        out_shape=jax.ShapeDtypeStruct((N_CTX_TOK, WIDTH), BF16),
        compiler_params=_params(1),
        name="ctxattn",
    )(q, k, v)


def _lat_attn_kernel(q_ref, k_ref, v_ref, ck_ref, cv_ref, bias_ref, o_ref, kbf, vbf, ckbf, cvbf):
    i = pl.program_id(1)

    @pl.when(i == 0)
    def _():
        kbf[...] = k_ref[...].astype(BF16)
        vbf[...] = v_ref[...].astype(BF16)
        ckbf[...] = ck_ref[0].astype(BF16)
        cvbf[...] = cv_ref[0].astype(BF16)

    win = NA_ROWS * GRID_W
    rows = []
    for rr in range(LAT_ROWS):
        gr = i * LAT_ROWS + rr
        first_row = jnp.clip(gr - NA_ROWS // 2, 0, GRID_ROWS - NA_ROWS)
        start = pl.multiple_of(first_row * GRID_W, GRID_W)
        off0 = first_row - gr + NA_ROWS - 1
        rows.append((slice(rr * GRID_W, (rr + 1) * GRID_W), start, off0))
    pairs = [slice(p * PAIR_W, (p + 1) * PAIR_W) for p in range(N_HEADS // 2)]
    scores = []
    for p, ps in enumerate(pairs):
        lhs = [_pair_queries(q_ref[rs, ps]) for rs, _, _ in rows]
        s_loc = []
        for (_, start, off0), lhs_r in zip(rows, lhs):
            bias = jnp.concatenate(
                [jnp.concatenate([bias_ref[2 * p + hh, off0 + 2 * j] for j in range(NA_ROWS // 2)], axis=-1)
                 for hh in range(2)], axis=0)
            s_loc.append(_dot_nt(lhs_r, kbf[pl.ds(start, win), ps]) * ATT_SCALE + bias)
        s_ctx = _dot_nt(jnp.concatenate(lhs, axis=0), ckbf[:, ps]) * ATT_SCALE
        scores.append((jnp.concatenate(s_loc, axis=0), s_ctx))
    probs = []
    for s_loc, s_ctx in scores:
        m = jnp.maximum(jnp.max(s_loc, axis=-1, keepdims=True), jnp.max(s_ctx, axis=-1, keepdims=True))
        e_loc = jnp.exp(s_loc - m)
        e_ctx = jnp.exp(s_ctx - m)
        l = jnp.sum(e_loc, axis=-1, keepdims=True) + jnp.sum(e_ctx, axis=-1, keepdims=True)
        probs.append((e_loc.astype(BF16), e_ctx.astype(BF16), l))
    for ps, (e_loc, e_ctx, l) in zip(pairs, probs):
        o_ctx = _dot(e_ctx, cvbf[:, ps])
        for rr, (rs, start, _) in enumerate(rows):
            both = slice(rr * 2 * GRID_W, (rr + 1) * 2 * GRID_W)
            o = _dot(e_loc[both], vbf[pl.ds(start, win), ps]) + o_ctx[both]
            o_ref[rs, ps] = _pair_outputs(o / l[both]).astype(BF16)


def _na_bias_table(rpb):
    n = rpb.shape[-1]
    w = GRID_W
    pair = jnp.concatenate([rpb[:, :-1], rpb[:, 1:]], axis=-1).astype(F32)
    x = jnp.arange(2 * n)
    y = jnp.arange(2 * w)
    cq = jnp.arange(w)
    ck = y % w
    sel = ((x // n)[:, None, None] == (y // w)[None, None, :]) & (
        (x % n)[:, None, None] == ck[None, None, :] - cq[None, :, None] + NA_COLS - 1)
    t = jnp.einsum("hdx,xqy->hdqy", pair, sel.astype(F32), precision=HIGHEST)
    cs = jnp.clip(cq - NA_COLS // 2, 0, w - NA_COLS)[:, None]
    return jnp.where((ck[None, :] >= cs) & (ck[None, :] < cs + NA_COLS), t, MASK_VALUE)


def _lat_attention(q, k, v, cache_k, cache_v, bias):
    blk = LAT_ROWS * GRID_W
    steps = GRID_ROWS // LAT_ROWS
    lat0 = N_CTX_TOK // blk
    seq = pl.BlockSpec((DEC_SEQ, WIDTH), lambda b, i: (N_CTX_TOK // DEC_SEQ + b, 0))
    ctx = pl.BlockSpec((1, PAST_LEN, WIDTH), lambda b, i: (b, 0, 0))
    return pl.pallas_call(
        _lat_attn_kernel,
        grid=(DEC_BATCH, steps),
        in_specs=[pl.BlockSpec((blk, WIDTH), lambda b, i: (lat0 + b * steps + i, 0)),
                  seq, seq, ctx, ctx,
                  _const_spec((N_HEADS, 2 * NA_ROWS - 2, GRID_W, PAIR_W))],
        out_specs=pl.BlockSpec((blk, WIDTH), lambda b, i: (b * steps + i, 0)),
        out_shape=jax.ShapeDtypeStruct((N_LAT_TOK, WIDTH), BF16),
        scratch_shapes=[pltpu.VMEM((DEC_SEQ, WIDTH), BF16), pltpu.VMEM((DEC_SEQ, WIDTH), BF16),
                        pltpu.VMEM((PAST_LEN, WIDTH), BF16), pltpu.VMEM((PAST_LEN, WIDTH), BF16)],
        compiler_params=_params(2),
        name="latattn",
    )(q, k, v, cache_k, cache_v, bias)


def _rwkv_tokens(zcols, fillers, wts_ref, w0_ref, w2_ref, a0_ref, a2_ref, g2_ref, kk_ref, ka_ref, rk_ref, ones_ref,
                 shared_out, g_out, bonus_out, dir_out):
    def conv(c0, c1):
        zc, prow, nrow = zcols(c0, c1)
        zprev, znext = _shifted(zc, prow, nrow)
        return zprev * wts_ref[0:1, c0:c1] + zc * wts_ref[1:2, c0:c1] + znext * wts_ref[2:3, c0:c1]

    fill = iter(fillers)
    dirs = [slice(d * LORA, (d + 1) * LORA) for d in range(N_DIRS)]
    ones = ones_ref[...]
    o = 3 * WIDTH
    lora_in = conv(o, RWKV_COLS)
    kr = conv(WIDTH, 2 * WIDTH)
    tw = jnp.tanh(lora_in[:, 0:N_DIRS * LORA])
    xa = lora_in[:, N_DIRS * LORA:2 * N_DIRS * LORA]
    xg = lora_in[:, 2 * N_DIRS * LORA:]
    kkraw = kr * kk_ref[...]
    next(fill)()
    kk = kkraw * lax.rsqrt(_segsum(kkraw * kkraw, ones) + 1e-12)
    bf = lambda t: t.astype(BF16)
    lora_w = [_dot(bf(tw[:, ls]), bf(w2_ref[d])) for d, ls in enumerate(dirs)]
    r = conv(0, WIDTH)
    next(fill)()
    lora_a = [_dot(bf(xa[:, ls]), bf(a2_ref[d])) for d, ls in enumerate(dirs)]
    g_out[...] = _dot(bf(jax.nn.sigmoid(xg)), bf(g2_ref[...]))
    v = conv(2 * WIDTH, 3 * WIDTH)
    next(fill)()
    kd_sum = jnp.zeros_like(kr)
    for d in range(N_DIRS):
        w_log = -_softplus(-(w0_ref[d:d + 1, :] + lora_w[d])) - 0.5
        a = jax.nn.sigmoid(a0_ref[d:d + 1, :] + lora_a[d])
        kd = kr * (1.0 + (a - 1.0) * ka_ref[...])
        dir_out[d, :, 0:WIDTH] = -jnp.exp(w_log)
        dir_out[d, :, WIDTH:2 * WIDTH] = kk * a
        dir_out[d, :, 2 * WIDTH:] = kd
        kd_sum = kd_sum + kd
    shared_out[:, 0:WIDTH] = r
    shared_out[:, WIDTH:2 * WIDTH] = kk
    shared_out[:, 2 * WIDTH:] = v
    bonus_out[...] = _dot(bf(r * kd_sum * rk_ref[...]), ones) * v


HALF = CHUNK // 2


def _blockdiag(x, block=HEAD_DIM):
    p = x.astype(BF16)
    blk = lax.broadcasted_iota(jnp.int32, p.shape, 1) // block
    return jnp.concatenate([jnp.where(blk == q, p, jnp.zeros_like(p)) for q in range(GROUP_W // block)], axis=0)


def _hp(a, b, block=HEAD_DIM):
    return _dot(a.astype(BF16), _blockdiag(b, block))


def _hp_nt(a, b):
    return _dot_nt(a.astype(BF16), _blockdiag(b))


def _head_transpose(x):
    xt = jnp.transpose(x)
    return jnp.concatenate([xt[h * HEAD_DIM:(h + 1) * HEAD_DIM] for h in range(GROUP)], axis=1)


def _scan_prepare(d, rows, shared_ref, dir_ref, y_ref, s_ref):
    row = lax.broadcasted_iota(jnp.int32, (CHUNK, GROUP_W), 0)
    lane = lax.broadcasted_iota(jnp.int32, (CHUNK, GROUP_W), 1)
    col = lane % CHUNK
    strict = col > row if d else col < row
    incl = col >= row if d else col <= row
    row_h = lax.broadcasted_iota(jnp.int32, (HALF, GROUP_W), 0)
    lane_h = lax.broadcasted_iota(jnp.int32, (HALF, GROUP_W), 1)
    lo_s = lane_h % CHUNK < HALF
    eye_half = (lane_h % HALF == row_h).astype(F32)

    lw = dir_ref[0, rows, 0:WIDTH]
    mask = jnp.where(incl[:, :CHUNK], 1.0, 0.0).astype(BF16)
    lw_hi, lw_mid = _split(lw)
    lw_lo = (lw - lw_hi.astype(F32) - lw_mid.astype(F32)).astype(BF16)
    cum = _dot(mask, lw_hi) + _dot(mask, lw_mid) + _dot(mask, lw_lo)
    tot = jnp.sum(lw, axis=0, keepdims=True)
    p_in = jnp.exp(cum)
    p_neg = jnp.exp(-cum)
    p_rem = jnp.exp(tot - cum)
    kt = shared_ref[rows, WIDTH:2 * WIDTH] * jnp.exp(cum - lw)
    rt = shared_ref[rows, 0:WIDTH] * p_in
    b = dir_ref[0, rows, WIDTH:2 * WIDTH]
    kd = dir_ref[0, rows, 2 * WIDTH:]
    bt = b * p_neg
    kdt = kd * p_neg
    bh = b * p_rem
    kh = kd * p_rem
    p_end = jnp.exp(tot)
    v = shared_ref[rows, 2 * WIDTH:]

    chains = []
    for gi in range(N_HEADS // GROUP):
        sl = slice(gi * GROUP_W, (gi + 1) * GROUP_W)
        chains.append(dict(d=d, rows=rows, sl=sl, strict=strict, incl=incl, lo_s=lo_s, eye_half=eye_half, y_ref=y_ref, s_ref=s_ref,
                           kt=kt[:, sl], rt=rt[:, sl], v=v[:, sl], bt=bt[:, sl], kdt=kdt[:, sl],
                           kh=kh[:, sl], bh=bh[:, sl], p_end=p_end[:, sl]))
    return chains


def _scan_scores(chains):
    for c in chains:
        lhs = jnp.concatenate([c["kt"], c["rt"]], axis=0)
        g_b = _hp_nt(lhs, c["bt"])
        g_k = _hp_nt(lhs, c["kdt"])
        c["a_k"] = jnp.where(c["strict"], g_k[:CHUNK], 0.0)
        c["a_rb"] = jnp.where(c["incl"], g_b[CHUNK:], 0.0)
        c["a_rk"] = jnp.where(c["incl"], g_k[CHUNK:], 0.0)
        n = jnp.where(c["strict"], -g_b[:CHUNK], 0.0)
        c["n"] = n
        c["pw"] = jnp.where(c["lo_s"], n[:HALF], n[HALF:])
        c["tp"] = c["eye_half"] + c["pw"]
    return chains


def _scan_local(chains):
    zeros = jnp.zeros((HALF, GROUP_W), F32)
    for c in chains:
        c["pw"] = _hp(c["pw"], c["pw"], HALF)
    for _ in range(3):
        for c in chains:
            both = _hp(jnp.concatenate([c["tp"], c["pw"]], axis=0), c["pw"], HALF)
            c["tp"] = c["tp"] + both[:HALF]
            c["pw"] = both[HALF:]
    for c in chains:
        c["tp"] = c["tp"] + _hp(c["tp"], c["pw"], HALF)
    for c in chains:
        c["t1"] = jnp.where(c["lo_s"], c["tp"], 0.0)
        c["t2"] = jnp.where(c["lo_s"], 0.0, c["tp"])
        if c["d"] == 0:
            off, inner = jnp.where(c["lo_s"], c["n"][HALF:], 0.0), jnp.concatenate([c["t1"], zeros], axis=0)
        else:
            off, inner = jnp.where(c["lo_s"], 0.0, c["n"][:HALF]), jnp.concatenate([zeros, c["t2"]], axis=0)
        c["off"] = _hp(off, inner)
    for c in chains:
        if c["d"] == 0:
            x = _hp(c["t2"], jnp.concatenate([zeros, c["off"]], axis=0))
            c["inv"] = jnp.concatenate([c["t1"], jnp.where(c["lo_s"], x, c["tp"])], axis=0)
        else:
            x = _hp(c["t1"], jnp.concatenate([c["off"], zeros], axis=0))
            c["inv"] = jnp.concatenate([jnp.where(c["lo_s"], c["tp"], x), c["t2"]], axis=0)
    for c in chains:
        c["av"] = _hp(jnp.concatenate([c["a_k"], c["a_rk"]], axis=0), c["v"])
        c["vt"] = _head_transpose(c["v"])
    return chains


def _scan_carry(chains):
    for c in chains:
        c["s"] = c["s_ref"][0, :, c["sl"]]
        c["z"] = _hp_nt(jnp.concatenate([c["rt"], c["kt"]], axis=0), c["s"])
    for c in chains:
        c["u"] = _hp(c["inv"], c["z"][CHUNK:] + c["av"][:CHUNK])
    for c in chains:
        c["y_ref"][c["rows"], c["sl"]] = c["z"][:CHUNK] + c["av"][CHUNK:] - _hp(c["a_rb"], c["u"])
    for c in chains:
        upd = _hp(c["vt"], c["kh"]) - _hp(_head_transpose(c["u"]), c["bh"])
        c["s_ref"][0, :, c["sl"]] = c["s"] * c["p_end"] + upd


REGION_CHUNKS = CTX_CHUNKS
STEP_CHUNKS = 2
STEP_ROWS = STEP_CHUNKS * CHUNK
SCAN_STEPS = REGION_CHUNKS // STEP_CHUNKS
SCAN_STREAMS = tuple((d, base // STEP_CHUNKS, per_seq // STEP_CHUNKS) for d in range(N_DIRS)
                     for base, per_seq in ((0, CHUNKS_PER_CTX_SEQ), (CTX_CHUNKS, CHUNKS_PER_LAT_SEQ)))
STREAM_INS = 3


def _stream_local_block(stream, j):
    return SCAN_STEPS - 1 - j if stream[0] else j


def _scan_kernel(n_cast, *refs):
    n_stream_in = STREAM_INS * len(SCAN_STREAMS)
    n_in = n_stream_in + n_cast
    j = pl.program_id(0)
    for w_ref, o_ref in zip(refs[n_stream_in:n_in], refs[len(refs) - n_cast:]):
        o_ref[...] = w_ref[...].astype(BF16)
    per_stream = []
    for s, stream in enumerate(SCAN_STREAMS):
        d, _, per_seq = stream
        ins = refs[STREAM_INS * s:STREAM_INS * (s + 1)]
        y_ref, s_ref = refs[n_in + 2 * s:n_in + 2 * s + 2]
        s0_ref = ins[-1]
        local = _stream_local_block(stream, j) % per_seq

        @pl.when(local == (per_seq - 1 if d else 0))
        def _():
            s_ref[...] = s0_ref[0]

        per_stream.append((d,) + ins[:-1] + (y_ref, s_ref))

    chains_by_chunk = []
    for k in range(STEP_CHUNKS):
        chains = []
        for d, shared_ref, dir_ref, y_ref, s_ref in per_stream:
            first = (STEP_CHUNKS - 1 - k if d else k) * CHUNK
            chains += _scan_prepare(d, slice(first, first + CHUNK), shared_ref, dir_ref, y_ref, s_ref)
        chains_by_chunk.append(chains)
    _scan_local(_scan_scores([c for chains in chains_by_chunk for c in chains]))
    for chains in chains_by_chunk:
        _scan_carry(chains)


def _scan(shared, perdir, s0, weights):
    in_specs, out_specs, out_shape, operands = [], [], [], []
    cast_in, cast_out, cast_shape = [], [], []
    for w in weights:
        rows = w.shape[0] // SCAN_STEPS
        if rows % 16 or rows * SCAN_STEPS != w.shape[0]:
            rows = 128
        assert w.shape[0] % rows == 0 and w.shape[0] // rows <= SCAN_STEPS
        spec = pl.BlockSpec((rows, w.shape[1]), lambda j, n=w.shape[0] // rows: (jnp.minimum(j, n - 1), 0))
        cast_in.append(spec)
        cast_out.append(spec)
        cast_shape.append(jax.ShapeDtypeStruct(w.shape, BF16))
    for stream in SCAN_STREAMS:
        d, base, per_seq = stream
        seq0 = 0 if base == 0 else BATCH
        n_seq = SCAN_STEPS // per_seq
        loc = lambda j, stream=stream: _stream_local_block(stream, j)
        base += PAD_TOK // STEP_ROWS
        tok = pl.BlockSpec((STEP_ROWS, 3 * WIDTH), lambda j, loc=loc, base=base: (base + loc(j), 0))
        dtok = pl.BlockSpec((1, STEP_ROWS, 3 * WIDTH), lambda j, loc=loc, base=base, d=d: (d, base + loc(j), 0))
        st_in = pl.BlockSpec((1, 1, HEAD_DIM, WIDTH),
                             lambda j, loc=loc, seq0=seq0, per_seq=per_seq, d=d: (seq0 + loc(j) // per_seq, d, 0, 0))
        in_specs += [tok, dtok, st_in]
        operands += [shared, perdir, s0]
        out_specs += [pl.BlockSpec((STEP_ROWS, WIDTH), lambda j, loc=loc: (loc(j), 0)),
                      pl.BlockSpec((1, HEAD_DIM, WIDTH), lambda j, loc=loc, per_seq=per_seq: (loc(j) // per_seq, 0, 0))]
        out_shape += [jax.ShapeDtypeStruct((REGION_CHUNKS * CHUNK, WIDTH), F32),
                      jax.ShapeDtypeStruct((n_seq, HEAD_DIM, WIDTH), F32)]
    return pl.pallas_call(
        functools.partial(_scan_kernel, len(weights)),
        grid=(SCAN_STEPS,),
        in_specs=in_specs + cast_in,
        out_specs=out_specs + cast_out,
        out_shape=out_shape + cast_shape,
        compiler_params=_params(1),
        name="scan",
    )(*operands, *weights)


def _post_kernel(yfc_ref, yfl_ref, ybc_ref, ybl_ref, bonus_lo_ref, bonus_hi_ref, g_lo_ref, g_hi_ref,
                 attc_ref, attl_ref, xp_ref, xs_ref, mod_ref,
                 lng_ref, lnb_ref, ones_ref, wout_ref, n2_ref, x1_ref, h2_ref):
    i = pl.program_id(0)
    bonus = jnp.concatenate([bonus_lo_ref[...], bonus_hi_ref[...]], axis=0)
    gate = jnp.concatenate([g_lo_ref[...], g_hi_ref[...]], axis=0)
    def pick(ctx_ref, lat_ref):
        return _ctx_or_lat(i, ctx_ref, lat_ref, BIG_BLOCK)

    ones = ones_ref[...]
    inv_n = 1.0 / HEAD_DIM
    y = pick(yfc_ref, yfl_ref) + pick(ybc_ref, ybl_ref)
    yc = y - _segsum(y, ones) * inv_n
    var = _segsum(yc * yc, ones) * inv_n
    yn = yc * lax.rsqrt(var + GN_EPS) * lng_ref[...] + lnb_ref[...]
    r_out = ((yn + bonus) * gate).astype(BF16)
    o = _dot(pick(attc_ref, attl_ref), wout_ref[0:WIDTH, :]) + _dot(r_out, wout_ref[WIDTH:, :])
    x1 = pick(xp_ref, xs_ref) + mod_ref[0, 2:3, :] * o
    x1_ref[...] = x1
    h2 = _rmsnorm(x1, n2_ref[...]) * (1.0 + mod_ref[0, 4:5, :]) + mod_ref[0, 3:4, :]
    h2_ref[...] = h2.astype(BF16)


def _post(ys, bonus, g, att_ctx, att_lat, xp, xs, mod, ln_g, ln_b, ones, w_out_bf, norm2):
    blk = BIG_BLOCK
    tok = lambda cols: _tok_spec(cols, blk)
    ctx_lat = [_ctx_tok_spec(WIDTH, blk), _lat_tok_spec(WIDTH, blk)]
    halves = [pl.BlockSpec((PAD_TOK, WIDTH), lambda i, h=h: (i * (blk // PAD_TOK) + 1 + h, 0)) for h in range(2)]
    assert blk == 2 * PAD_TOK
    return pl.pallas_call(
        _post_kernel,
        grid=(N_TOK // blk,),
        in_specs=ctx_lat + ctx_lat + halves + halves
        + ctx_lat + [_ctx_tok_spec(D_MODEL, blk), _lat_tok_spec(D_MODEL, blk),
                  _mod_spec(blk),
                  _const_spec((1, WIDTH)), _const_spec((1, WIDTH)), _const_spec((WIDTH, WIDTH)),
                  _const_spec((2 * WIDTH, D_MODEL)), _const_spec((1, D_MODEL))],
        out_specs=[tok(D_MODEL), tok(D_MODEL)],
        out_shape=[jax.ShapeDtypeStruct((N_TOK, D_MODEL), F32),
                   jax.ShapeDtypeStruct((N_TOK, D_MODEL), BF16)],
        compiler_params=_params(1),
        name="post",
    )(*ys, bonus, bonus, g, g, att_ctx, att_lat, xp, xs, mod, ln_g, ln_b, ones, w_out_bf, norm2)


def _ffn_kernel(x1_ref, h2_ref, hp_ref, hn_ref, mod_ref, w1_ref, w3_ref, wc_ref, w2_ref, nf_ref, yp_ref, ys_ref):
    i = pl.program_id(0)
    blk = BIG_BLOCK
    is_ctx = i < N_CTX_TOK // blk
    has_prev, has_next = _seq_neighbours(i, blk)
    lhs = jnp.concatenate([h2_ref[...], hp_ref[...], hn_ref[...]], axis=0)
    h2 = lhs[:blk]
    rid = lax.broadcasted_iota(jnp.int32, (blk, 1), 0)
    inner = jnp.where(is_ctx, SEQ, -1)
    seq_start = rid == inner
    seq_end = rid == inner - 1
    acc = jnp.zeros((blk, D_MODEL), F32)
    for f in range(D_FF // FF_TILE):
        fs = slice(f * FF_TILE, (f + 1) * FF_TILE)
        a_all = _dot(lhs, w1_ref[:, fs])
        a = a_all[:blk]
        prow = jnp.where(has_prev, a_all[blk + 15:blk + 16, :], 0.0)
        nrow = jnp.where(has_next, a_all[blk + 16:blk + 17, :], 0.0)
        aprev, anext = _shifted(a, prow, nrow)
        aprev = jnp.where(seq_start, 0.0, aprev)
        anext = jnp.where(seq_end, 0.0, anext)
        cv = aprev * wc_ref[0:1, fs] + a * wc_ref[1:2, fs] + anext * wc_ref[2:3, fs]
        act = _silu(cv) * _dot(h2, w3_ref[:, fs])
        acc = acc + _dot(act.astype(BF16), w2_ref[fs, :])
    x2 = x1_ref[...] + mod_ref[0, 5:6, :] * acc
    y = _rmsnorm(x2, nf_ref[...])

    @pl.when(is_ctx)
    def _():
        yp_ref[...] = y

    @pl.when(jnp.logical_not(is_ctx))
    def _():
        ys_ref[...] = y


def _ffn(x1, h2, mod, w1_bf, w3_bf, wc, w2_bf, norm_f):
    blk = BIG_BLOCK
    rows16 = blk // 16
    return pl.pallas_call(
        _ffn_kernel,
        grid=(N_TOK // blk,),
        in_specs=[_tok_spec(D_MODEL, blk), _tok_spec(D_MODEL, blk),
                  pl.BlockSpec((16, D_MODEL), lambda i: (jnp.maximum(i * rows16 - 1, 0), 0)),
                  pl.BlockSpec((16, D_MODEL), lambda i: (jnp.minimum((i + 1) * rows16, N_TOK // 16 - 1), 0)),
                  _mod_spec(blk),
                  _const_spec((D_MODEL, D_FF)), _const_spec((D_MODEL, D_FF)), _const_spec((3, D_FF)),
                  _const_spec((D_FF, D_MODEL)), _const_spec((1, D_MODEL))],
        out_specs=[_ctx_tok_spec(D_MODEL, blk), _lat_tok_spec(D_MODEL, blk)],
        out_shape=[jax.ShapeDtypeStruct((N_CTX_TOK, D_MODEL), F32),
                   jax.ShapeDtypeStruct((N_LAT_TOK, D_MODEL), F32)],
        compiler_params=_params(1),
        name="ffn",
    )(x1, h2, h2, h2, mod, w1_bf, w3_bf, wc, w2_bf, norm_f)


def kernel(x_prompt, x_sample, cache_k, cache_v, state_rwkv, c, c_ctx, w_ada, b_ada, norm1, norm2,
           w_in, w_ts, w0, w2, a0, a2, g2, k_k, k_a, r_k, ln_x_g, ln_x_b, rpb, w_out,
           w_ffn1, w_ffn3, w_ffn_conv, w_ffn2, norm_f):
    xp = x_prompt.reshape(N_CTX_TOK, D_MODEL)
    xs = x_sample.reshape(N_LAT_TOK, D_MODEL)
    row = lambda t: t.reshape(1, -1)
    cond = jnp.concatenate([c_ctx[None, :], c, jnp.zeros((8 - 1 - DEC_BATCH, D_MODEL), F32)], axis=0)
    mod = _modulation(cond, w_ada[0], b_ada[0]).reshape(8, 6, D_MODEL)

    head_id = jnp.arange(WIDTH) // HEAD_DIM
    ones = (head_id[:, None] == head_id[None, :]).astype(BF16)
    q, k, v, shared, g, bonus, perdir, new_k, new_v = _inproj(
        xp, xs, mod, row(norm1[0]), w_in[0].astype(BF16), w_ts[0], w0[0], w2[0], a0[0], a2[0], g2[0],
        row(k_k[0]), row(k_a[0]), row(r_k[0]), ones)

    att_ctx = _ctx_attention(q, k, v)
    att_lat = _lat_attention(q, k, v, cache_k[:, 0].reshape(DEC_BATCH, PAST_LEN, WIDTH),
                             cache_v[:, 0].reshape(DEC_BATCH, PAST_LEN, WIDTH), _na_bias_table(rpb[0]))
    s_lat = jnp.transpose(state_rwkv[:, 0], (0, 1, 3, 2, 4)).reshape(DEC_BATCH, N_DIRS, HEAD_DIM, WIDTH)
    s0 = jnp.concatenate([jnp.zeros((BATCH, N_DIRS, HEAD_DIM, WIDTH), F32), s_lat], axis=0)
    (y_fc, s_fc, y_fl, _, y_bc, s_bc, y_bl, _,
     w_out_bf, w1_bf, w3_bf, w2_bf) = _scan(shared, perdir, s0, (w_out[0], w_ffn1[0], w_ffn3[0], w_ffn2[0]))
    s_fin = jnp.stack([s_fc, s_bc], axis=1)

    x1, h2 = _post((y_fc, y_fl, y_bc, y_bl), bonus, g, att_ctx, att_lat, xp, xs, mod, row(ln_x_g[0]), row(ln_x_b[0]), ones,
                   w_out_bf, row(norm2[0]))
    yp, ys = _ffn(x1, h2, mod, w1_bf, w3_bf, w_ffn_conv[0], w2_bf, row(norm_f))

    new_s = jnp.transpose(s_fin.reshape(BATCH, N_DIRS, HEAD_DIM, N_HEADS, HEAD_DIM),
                          (0, 1, 3, 2, 4)).reshape(BATCH, 1, N_DIRS, N_HEADS, HEAD_DIM, HEAD_DIM)
    return (yp.reshape(BATCH, SEQ, D_MODEL), ys.reshape(DEC_BATCH, DEC_SEQ, D_MODEL), new_k, new_v, new_s)
```

```python
import functools

import jax
import jax.numpy as jnp
from jax import lax
from jax.experimental import pallas as pl
from jax.experimental.pallas import tpu as pltpu

F32 = jnp.float32
BF16 = jnp.bfloat16
HIGHEST = lax.Precision.HIGHEST

D_MODEL = 1024
BATCH = 16
SEQ = 256
DEC_BATCH = 2
DEC_SEQ = 2048
PAST_LEN = 512
GRID_W = 64
HEAD_DIM = 64
N_HEADS = 8
WIDTH = N_HEADS * HEAD_DIM
PAIR_W = 2 * HEAD_DIM
NA_ROWS = 8
NA_COLS = 16
N_DIRS = 2
LORA = 64
GATE_LORA = 128
D_FF = 2816
EPS = 1e-6
GN_EPS = 64e-5
ATT_SCALE = HEAD_DIM ** -0.5
RWKV_COLS = 3 * WIDTH + N_DIRS * 2 * LORA + GATE_LORA
MASK_VALUE = -1e30

N_CTX_TOK = BATCH * SEQ
N_LAT_TOK = DEC_BATCH * DEC_SEQ
N_TOK = N_CTX_TOK + N_LAT_TOK
TOK_BLOCK = 256
N_BLOCKS = N_TOK // TOK_BLOCK
BIG_BLOCK = 512
PAD_TOK = TOK_BLOCK
N_SEQS = BATCH + DEC_BATCH
CHUNK = 64
N_CHUNKS = N_TOK // CHUNK
CTX_CHUNKS = N_CTX_TOK // CHUNK
CHUNKS_PER_CTX_SEQ = SEQ // CHUNK
CHUNKS_PER_LAT_SEQ = DEC_SEQ // CHUNK
GRID_ROWS = DEC_SEQ // GRID_W
LAT_ROWS = 2
FF_TILE = D_FF // 2
GROUP = 4
GROUP_W = GROUP * HEAD_DIM
VMEM_LIMIT = 56 * 1024 * 1024


def _params(n_axes, limit=VMEM_LIMIT):
    return pltpu.CompilerParams(dimension_semantics=("arbitrary",) * n_axes, vmem_limit_bytes=limit)


def _const_spec(shape):
    zeros = (0,) * len(shape)
    return pl.BlockSpec(shape, lambda *_: zeros, pipeline_mode=pl.Buffered(1))


def _tok_spec(cols, blk=TOK_BLOCK):
    return pl.BlockSpec((blk, cols), lambda i: (i, 0))


def _ctx_tok_spec(cols, blk=TOK_BLOCK):
    return pl.BlockSpec((blk, cols), lambda i: (jnp.minimum(i, N_CTX_TOK // blk - 1), 0))


def _lat_tok_spec(cols, blk=TOK_BLOCK):
    return pl.BlockSpec((blk, cols), lambda i: (jnp.maximum(i - N_CTX_TOK // blk, 0), 0))


def _ctx_or_lat(i, ctx_ref, lat_ref, blk=TOK_BLOCK):
    return jnp.where(i < N_CTX_TOK // blk, ctx_ref[...], lat_ref[...])


def _mod_spec(blk=TOK_BLOCK):
    def row(i):
        return jnp.where(i < N_CTX_TOK // blk, 0, 1 + (i - N_CTX_TOK // blk) // (DEC_SEQ // blk))
    return pl.BlockSpec((1, 6, D_MODEL), lambda i: (row(i), 0, 0))


def _seq_neighbours(i, blk=TOK_BLOCK):
    per_seq = DEC_SEQ // blk
    j = (i - N_CTX_TOK // blk) % per_seq
    lat = i >= N_CTX_TOK // blk
    return lat & (j != 0), lat & (j != per_seq - 1)


def _silu(x):
    return x * jax.nn.sigmoid(x)


def _softplus(x):
    return jnp.maximum(x, 0.0) + jnp.log(1.0 + jnp.exp(-jnp.abs(x)))


def _rmsnorm(x, g):
    return x * lax.rsqrt(jnp.mean(x * x, axis=-1, keepdims=True) + EPS) * g


def _dot(a, b, precision=None):
    return jnp.dot(a, b, precision=precision, preferred_element_type=F32)


def _dot_nt(a, b, precision=None):
    return lax.dot_general(a, b, (((1,), (1,)), ((), ())), precision=precision, preferred_element_type=F32)


def _split(x):
    hi = x.astype(BF16)
    return hi, (x - hi.astype(F32)).astype(BF16)


def _segsum(x, ones_bf):
    return _dot(x.astype(BF16), ones_bf)


def _shifted(a, prev_row, next_row):
    t = a.shape[0]
    rid = lax.broadcasted_iota(jnp.int32, (t, 1), 0)
    prev = jnp.where(rid == 0, prev_row, pltpu.roll(a, 1, axis=0))
    nxt = jnp.where(rid == t - 1, next_row, pltpu.roll(a, t - 1, axis=0))
    return prev, nxt


def _mod_kernel(cond_ref, w_ref, b_ref, o_ref):
    o_ref[...] = _dot(_silu(cond_ref[...]).astype(BF16), w_ref[...].astype(BF16)) + b_ref[...]


def _modulation(cond, w_ada, b_ada):
    n = 6 * D_MODEL
    return pl.pallas_call(
        _mod_kernel,
        grid=(6,),
        in_specs=[pl.BlockSpec((8, D_MODEL), lambda j: (0, 0)),
                  pl.BlockSpec((D_MODEL, D_MODEL), lambda j: (0, j)),
                  pl.BlockSpec((1, D_MODEL), lambda j: (0, j))],
        out_specs=pl.BlockSpec((8, D_MODEL), lambda j: (0, j)),
        out_shape=jax.ShapeDtypeStruct((8, n), F32),
        compiler_params=_params(1),
        name="mod",
    )(cond, w_ada, b_ada.reshape(1, n))


def _inproj_kernel(xp_ref, xs_ref, xprev_ref, xnext_ref, mod_ref, n1_ref, w_ref, *rest):
    rwkv_refs, (q_ref, k_ref, v_ref), rwkv_outs, (nk_ref, nv_ref), z_bufs = (
        rest[:10], rest[10:13], rest[13:17], rest[17:19], rest[19:])
    i = pl.program_id(0)
    blk = jnp.minimum(i, N_BLOCKS - 1)

    @pl.when(i == 0)
    def _():
        z_bufs[1][...] = jnp.zeros_like(z_bufs[1])

    def modulated(x):
        h = _rmsnorm(x, n1_ref[...]) * (1.0 + mod_ref[0, 1:2, :]) + mod_ref[0, 0:1, :]
        return h.astype(BF16)

    def step(z_store, z_load):
        h = modulated(_ctx_or_lat(blk, xp_ref, xs_ref))
        edge = modulated(jnp.concatenate([xprev_ref[...], xnext_ref[...]], axis=0))
        lhs = jnp.concatenate([h, edge], axis=0)
        kv = []

        def project(c0, c1):
            z_store[:, c0:c1] = _dot(lhs, w_ref[:, 3 * WIDTH + c0:3 * WIDTH + c1])

        def portion_q():
            project(3 * WIDTH, RWKV_COLS)
            project(WIDTH, 2 * WIDTH)
            q_ref[...] = _dot(h, w_ref[:, 0:WIDTH])

        def portion_k():
            project(0, WIDTH)
            kv.append(_dot(h, w_ref[:, WIDTH:2 * WIDTH]))

        def portion_v():
            project(2 * WIDTH, 3 * WIDTH)
            kv.append(_dot(h, w_ref[:, 2 * WIDTH:3 * WIDTH]))

        has_prev, has_next = _seq_neighbours(i - 1)

        def zcols(c0, c1):
            return (z_load[0:TOK_BLOCK, c0:c1],
                    jnp.where(has_prev, z_load[TOK_BLOCK + 7:TOK_BLOCK + 8, c0:c1], 0.0),
                    jnp.where(has_next, z_load[TOK_BLOCK + 8:TOK_BLOCK + 9, c0:c1], 0.0))

        _rwkv_tokens(zcols, (portion_q, portion_k, portion_v), *rwkv_refs, *rwkv_outs)
        k, v = kv
        k_ref[...] = k
        v_ref[...] = v

        @pl.when(i < N_CTX_TOK // TOK_BLOCK)
        def _():
            nk_ref[0, 0] = k.reshape(SEQ, N_HEADS, HEAD_DIM)
            nv_ref[0, 0] = v.reshape(SEQ, N_HEADS, HEAD_DIM)

    @pl.when(i % 2 == 0)
    def _():
        step(z_bufs[0], z_bufs[1])

    @pl.when(i % 2 == 1)
    def _():
        step(z_bufs[1], z_bufs[0])


def _inproj(xp, xs, mod, norm1, w_in_bf, w_ts, w0, w2, a0, a2, g2, k_k, k_a, r_k, ones):
    tok = jax.ShapeDtypeStruct((N_TOK, WIDTH), F32)
    padded = lambda cols: jax.ShapeDtypeStruct((N_TOK + PAD_TOK, cols), F32)
    lat0 = N_CTX_TOK // TOK_BLOCK
    last = N_BLOCKS - 1
    rows8 = TOK_BLOCK // 8
    n8 = N_LAT_TOK // 8
    clamped = lambda cols: pl.BlockSpec((TOK_BLOCK, cols), lambda i: (jnp.minimum(i, last), 0))
    x_ctx = pl.BlockSpec((TOK_BLOCK, D_MODEL), lambda i: (jnp.minimum(i, lat0 - 1), 0))
    x_lat = pl.BlockSpec((TOK_BLOCK, D_MODEL), lambda i: (jnp.clip(i - lat0, 0, last - lat0), 0))
    prev8 = pl.BlockSpec((8, D_MODEL), lambda i: (jnp.clip((jnp.minimum(i, last) - lat0) * rows8 - 1, 0, n8 - 1), 0))
    next8 = pl.BlockSpec((8, D_MODEL), lambda i: (jnp.clip((jnp.minimum(i, last) - lat0 + 1) * rows8, 0, n8 - 1), 0))
    mod_rows = pl.BlockSpec((1, 6, D_MODEL), lambda i: (
        jnp.where(i < lat0, 0, 1 + (jnp.minimum(i, last) - lat0) // (DEC_SEQ // TOK_BLOCK)), 0, 0))
    cache = pl.BlockSpec((1, 1, SEQ, N_HEADS, HEAD_DIM), lambda i: (jnp.minimum(i, BATCH - 1), 0, 0, 0, 0))
    z_buf = pltpu.VMEM((TOK_BLOCK + 16, RWKV_COLS), F32)
    return pl.pallas_call(
        _inproj_kernel,
        grid=(N_BLOCKS + 1,),
        in_specs=[x_ctx, x_lat, prev8, next8, mod_rows,
                  _const_spec((1, D_MODEL)),
                  _const_spec((D_MODEL, 3 * WIDTH + RWKV_COLS)),
                  _const_spec((3, RWKV_COLS)),
                  _const_spec((N_DIRS, WIDTH)), _const_spec((N_DIRS, LORA, WIDTH)),
                  _const_spec((N_DIRS, WIDTH)), _const_spec((N_DIRS, LORA, WIDTH)),
                  _const_spec((GATE_LORA, WIDTH)),
                  _const_spec((1, WIDTH)), _const_spec((1, WIDTH)), _const_spec((1, WIDTH)),
                  _const_spec((WIDTH, WIDTH))],
        out_specs=[clamped(WIDTH), clamped(WIDTH), clamped(WIDTH),
                   _tok_spec(3 * WIDTH), _tok_spec(WIDTH), _tok_spec(WIDTH),
                   pl.BlockSpec((N_DIRS, TOK_BLOCK, 3 * WIDTH), lambda i: (0, i, 0)),
                   cache, cache],
        out_shape=[tok, tok, tok,
                   padded(3 * WIDTH), padded(WIDTH), padded(WIDTH),
                   jax.ShapeDtypeStruct((N_DIRS, N_TOK + PAD_TOK, 3 * WIDTH), F32),
                   jax.ShapeDtypeStruct((BATCH, 1, SEQ, N_HEADS, HEAD_DIM), F32),
                   jax.ShapeDtypeStruct((BATCH, 1, SEQ, N_HEADS, HEAD_DIM), F32)],
        scratch_shapes=[z_buf, z_buf],
        compiler_params=_params(1),
        name="inproj",
    )(xp, xs, xs, xs, mod, norm1, w_in_bf, w_ts, w0, w2, a0, a2, g2, k_k, k_a, r_k, ones)


def _pair_queries(q):
    lo_half = lax.broadcasted_iota(jnp.int32, q.shape, 1) < HEAD_DIM
    return jnp.concatenate([jnp.where(lo_half, q, 0.0), jnp.where(lo_half, 0.0, q)], axis=0).astype(BF16)


def _pair_outputs(o):
    t = o.shape[0] // 2
    lo_half = lax.broadcasted_iota(jnp.int32, (t, PAIR_W), 1) < HEAD_DIM
    return jnp.where(lo_half, o[:t], o[t:])


def _ctx_attn_kernel(q_ref, k_ref, v_ref, o_ref):
    pairs = [slice(p * PAIR_W, (p + 1) * PAIR_W) for p in range(N_HEADS // 2)]
    scores = [_dot_nt(_pair_queries(q_ref[:, ps]), k_ref[:, ps].astype(BF16)) * ATT_SCALE for ps in pairs]
    probs = []
    for s in scores:
        e = jnp.exp(s - jnp.max(s, axis=-1, keepdims=True))
        probs.append((e.astype(BF16), jnp.sum(e, axis=-1, keepdims=True)))
    for ps, (e, l) in zip(pairs, probs):
        o_ref[:, ps] = _pair_outputs(_dot(e, v_ref[:, ps].astype(BF16)) / l).astype(BF16)


def _ctx_attention(q, k, v):
    spec = pl.BlockSpec((SEQ, WIDTH), lambda b: (b, 0))
    return pl.pallas_call(
        _ctx_attn_kernel,
        grid=(BATCH,),
        in_specs=[spec, spec, spec],
        out_specs=spec,
        out_shape=jax.ShapeDtypeStruct((N_CTX_TOK, WIDTH), BF16),
        compiler_params=_params(1),
        name="ctxattn",
    )(q, k, v)


def _lat_attn_kernel(q_ref, k_ref, v_ref, ck_ref, cv_ref, bias_ref, o_ref, kbf, vbf, ckbf, cvbf):
    i = pl.program_id(1)

    @pl.when(i == 0)
    def _():
        kbf[...] = k_ref[...].astype(BF16)
        vbf[...] = v_ref[...].astype(BF16)
        ckbf[...] = ck_ref[0].astype(BF16)
        cvbf[...] = cv_ref[0].astype(BF16)

    win = NA_ROWS * GRID_W
    rows = []
    for rr in range(LAT_ROWS):
        gr = i * LAT_ROWS + rr
        first_row = jnp.clip(gr - NA_ROWS // 2, 0, GRID_ROWS - NA_ROWS)
        start = pl.multiple_of(first_row * GRID_W, GRID_W)
        off0 = first_row - gr + NA_ROWS - 1
        rows.append((slice(rr * GRID_W, (rr + 1) * GRID_W), start, off0))
    pairs = [slice(p * PAIR_W, (p + 1) * PAIR_W) for p in range(N_HEADS // 2)]
    scores = []
    for p, ps in enumerate(pairs):
        lhs = [_pair_queries(q_ref[rs, ps]) for rs, _, _ in rows]
        s_loc = []
        for (_, start, off0), lhs_r in zip(rows, lhs):
            bias = jnp.concatenate(
                [jnp.concatenate([bias_ref[2 * p + hh, off0 + 2 * j] for j in range(NA_ROWS // 2)], axis=-1)
                 for hh in range(2)], axis=0)
            s_loc.append(_dot_nt(lhs_r, kbf[pl.ds(start, win), ps]) * ATT_SCALE + bias)
        s_ctx = _dot_nt(jnp.concatenate(lhs, axis=0), ckbf[:, ps]) * ATT_SCALE
        scores.append((jnp.concatenate(s_loc, axis=0), s_ctx))
    probs = []
    for s_loc, s_ctx in scores:
        m = jnp.maximum(jnp.max(s_loc, axis=-1, keepdims=True), jnp.max(s_ctx, axis=-1, keepdims=True))
        e_loc = jnp.exp(s_loc - m)
        e_ctx = jnp.exp(s_ctx - m)
        l = jnp.sum(e_loc, axis=-1, keepdims=True) + jnp.sum(e_ctx, axis=-1, keepdims=True)
        probs.append((e_loc.astype(BF16), e_ctx.astype(BF16), l))
    for ps, (e_loc, e_ctx, l) in zip(pairs, probs):
        o_ctx = _dot(e_ctx, cvbf[:, ps])
        for rr, (rs, start, _) in enumerate(rows):
            both = slice(rr * 2 * GRID_W, (rr + 1) * 2 * GRID_W)
            o = _dot(e_loc[both], vbf[pl.ds(start, win), ps]) + o_ctx[both]
            o_ref[rs, ps] = _pair_outputs(o / l[both]).astype(BF16)


def _na_bias_table(rpb):
    n = rpb.shape[-1]
    w = GRID_W
    pair = jnp.concatenate([rpb[:, :-1], rpb[:, 1:]], axis=-1).astype(F32)
    x = jnp.arange(2 * n)
    y = jnp.arange(2 * w)
    cq = jnp.arange(w)
    ck = y % w
    sel = ((x // n)[:, None, None] == (y // w)[None, None, :]) & (
        (x % n)[:, None, None] == ck[None, None, :] - cq[None, :, None] + NA_COLS - 1)
    t = jnp.einsum("hdx,xqy->hdqy", pair, sel.astype(F32), precision=HIGHEST)
    cs = jnp.clip(cq - NA_COLS // 2, 0, w - NA_COLS)[:, None]
    return jnp.where((ck[None, :] >= cs) & (ck[None, :] < cs + NA_COLS), t, MASK_VALUE)


def _lat_attention(q, k, v, cache_k, cache_v, bias):
    blk = LAT_ROWS * GRID_W
    steps = GRID_ROWS // LAT_ROWS
    lat0 = N_CTX_TOK // blk
    seq = pl.BlockSpec((DEC_SEQ, WIDTH), lambda b, i: (N_CTX_TOK // DEC_SEQ + b, 0))
    ctx = pl.BlockSpec((1, PAST_LEN, WIDTH), lambda b, i: (b, 0, 0))
    return pl.pallas_call(
        _lat_attn_kernel,
        grid=(DEC_BATCH, steps),
        in_specs=[pl.BlockSpec((blk, WIDTH), lambda b, i: (lat0 + b * steps + i, 0)),
                  seq, seq, ctx, ctx,
                  _const_spec((N_HEADS, 2 * NA_ROWS - 2, GRID_W, PAIR_W))],
        out_specs=pl.BlockSpec((blk, WIDTH), lambda b, i: (b * steps + i, 0)),
        out_shape=jax.ShapeDtypeStruct((N_LAT_TOK, WIDTH), BF16),
        scratch_shapes=[pltpu.VMEM((DEC_SEQ, WIDTH), BF16), pltpu.VMEM((DEC_SEQ, WIDTH), BF16),
                        pltpu.VMEM((PAST_LEN, WIDTH), BF16), pltpu.VMEM((PAST_LEN, WIDTH), BF16)],
        compiler_params=_params(2),
        name="latattn",
    )(q, k, v, cache_k, cache_v, bias)


def _rwkv_tokens(zcols, fillers, wts_ref, w0_ref, w2_ref, a0_ref, a2_ref, g2_ref, kk_ref, ka_ref, rk_ref, ones_ref,
                 shared_out, g_out, bonus_out, dir_out):
    def conv(c0, c1):
        zc, prow, nrow = zcols(c0, c1)
        zprev, znext = _shifted(zc, prow, nrow)
        return zprev * wts_ref[0:1, c0:c1] + zc * wts_ref[1:2, c0:c1] + znext * wts_ref[2:3, c0:c1]

    fill = iter(fillers)
    dirs = [slice(d * LORA, (d + 1) * LORA) for d in range(N_DIRS)]
    ones = ones_ref[...]
    o = 3 * WIDTH
    lora_in = conv(o, RWKV_COLS)
    kr = conv(WIDTH, 2 * WIDTH)
    tw = jnp.tanh(lora_in[:, 0:N_DIRS * LORA])
    xa = lora_in[:, N_DIRS * LORA:2 * N_DIRS * LORA]
    xg = lora_in[:, 2 * N_DIRS * LORA:]
    kkraw = kr * kk_ref[...]
    next(fill)()
    kk = kkraw * lax.rsqrt(_segsum(kkraw * kkraw, ones) + 1e-12)
    bf = lambda t: t.astype(BF16)
    lora_w = [_dot(bf(tw[:, ls]), bf(w2_ref[d])) for d, ls in enumerate(dirs)]
    r = conv(0, WIDTH)
    next(fill)()
    lora_a = [_dot(bf(xa[:, ls]), bf(a2_ref[d])) for d, ls in enumerate(dirs)]
    g_out[...] = _dot(bf(jax.nn.sigmoid(xg)), bf(g2_ref[...]))
    v = conv(2 * WIDTH, 3 * WIDTH)
    next(fill)()
    kd_sum = jnp.zeros_like(kr)
    for d in range(N_DIRS):
        w_log = -_softplus(-(w0_ref[d:d + 1, :] + lora_w[d])) - 0.5
        a = jax.nn.sigmoid(a0_ref[d:d + 1, :] + lora_a[d])
        kd = kr * (1.0 + (a - 1.0) * ka_ref[...])
        dir_out[d, :, 0:WIDTH] = -jnp.exp(w_log)
        dir_out[d, :, WIDTH:2 * WIDTH] = kk * a
        dir_out[d, :, 2 * WIDTH:] = kd
        kd_sum = kd_sum + kd
    shared_out[:, 0:WIDTH] = r
    shared_out[:, WIDTH:2 * WIDTH] = kk
    shared_out[:, 2 * WIDTH:] = v
    bonus_out[...] = _segsum(r * kd_sum * rk_ref[...], ones) * v


HALF = CHUNK // 2


def _blockdiag(x, block=HEAD_DIM):
    p = x.astype(BF16)
    blk = lax.broadcasted_iota(jnp.int32, p.shape, 1) // block
    return jnp.concatenate([jnp.where(blk == q, p, jnp.zeros_like(p)) for q in range(GROUP_W // block)], axis=0)


def _hp(a, b, block=HEAD_DIM):
    return _dot(a.astype(BF16), _blockdiag(b, block))


def _hp_nt(a, b):
    return _dot_nt(a.astype(BF16), _blockdiag(b))


def _head_transpose(x):
    xt = jnp.transpose(x)
    return jnp.concatenate([xt[h * HEAD_DIM:(h + 1) * HEAD_DIM] for h in range(GROUP)], axis=1)


def _scan_prepare(d, rows, shared_ref, dir_ref, y_ref, s_ref):
    row = lax.broadcasted_iota(jnp.int32, (CHUNK, GROUP_W), 0)
    lane = lax.broadcasted_iota(jnp.int32, (CHUNK, GROUP_W), 1)
    col = lane % CHUNK
    strict = col > row if d else col < row
    incl = col >= row if d else col <= row
    row_h = lax.broadcasted_iota(jnp.int32, (HALF, GROUP_W), 0)
    lane_h = lax.broadcasted_iota(jnp.int32, (HALF, GROUP_W), 1)
    lo_s = lane_h % CHUNK < HALF
    eye_half = (lane_h % HALF == row_h).astype(F32)

    lw = dir_ref[0, rows, 0:WIDTH]
    mask = jnp.where(incl[:, :CHUNK], 1.0, 0.0).astype(BF16)
    lw_hi, lw_mid = _split(lw)
    lw_lo = (lw - lw_hi.astype(F32) - lw_mid.astype(F32)).astype(BF16)
    cum = _dot(mask, lw_hi) + _dot(mask, lw_mid) + _dot(mask, lw_lo)
    tot = jnp.sum(lw, axis=0, keepdims=True)
    p_in = jnp.exp(cum)
    p_neg = jnp.exp(-cum)
    p_rem = jnp.exp(tot - cum)
    kt = shared_ref[rows, WIDTH:2 * WIDTH] * jnp.exp(cum - lw)
    rt = shared_ref[rows, 0:WIDTH] * p_in
    b = dir_ref[0, rows, WIDTH:2 * WIDTH]
    kd = dir_ref[0, rows, 2 * WIDTH:]
    bt = b * p_neg
    kdt = kd * p_neg
    bh = b * p_rem
    kh = kd * p_rem
    p_end = jnp.exp(tot)
    v = shared_ref[rows, 2 * WIDTH:]

    chains = []
    for gi in range(N_HEADS // GROUP):
        sl = slice(gi * GROUP_W, (gi + 1) * GROUP_W)
        chains.append(dict(d=d, rows=rows, sl=sl, strict=strict, incl=incl, lo_s=lo_s, eye_half=eye_half, y_ref=y_ref, s_ref=s_ref,
                           kt=kt[:, sl], rt=rt[:, sl], v=v[:, sl], bt=bt[:, sl], kdt=kdt[:, sl],
                           kh=kh[:, sl], bh=bh[:, sl], p_end=p_end[:, sl]))
    return chains


def _scan_scores(chains):
    for c in chains:
        lhs = jnp.concatenate([c["kt"], c["rt"]], axis=0)
        g_b = _hp_nt(lhs, c["bt"])
        g_k = _hp_nt(lhs, c["kdt"])
        c["a_k"] = jnp.where(c["strict"], g_k[:CHUNK], 0.0)
        c["a_rb"] = jnp.where(c["incl"], g_b[CHUNK:], 0.0)
        c["a_rk"] = jnp.where(c["incl"], g_k[CHUNK:], 0.0)
        n = jnp.where(c["strict"], -g_b[:CHUNK], 0.0)
        c["n"] = n
        c["pw"] = jnp.where(c["lo_s"], n[:HALF], n[HALF:])
        c["tp"] = c["eye_half"] + c["pw"]
    return chains


def _scan_local(chains):
    zeros = jnp.zeros((HALF, GROUP_W), F32)
    for c in chains:
        c["pw"] = _hp(c["pw"], c["pw"], HALF)
    for _ in range(3):
        for c in chains:
            both = _hp(jnp.concatenate([c["tp"], c["pw"]], axis=0), c["pw"], HALF)
            c["tp"] = c["tp"] + both[:HALF]
            c["pw"] = both[HALF:]
    for c in chains:
        c["tp"] = c["tp"] + _hp(c["tp"], c["pw"], HALF)
    for c in chains:
        c["t1"] = jnp.where(c["lo_s"], c["tp"], 0.0)
        c["t2"] = jnp.where(c["lo_s"], 0.0, c["tp"])
        if c["d"] == 0:
            off, inner = jnp.where(c["lo_s"], c["n"][HALF:], 0.0), jnp.concatenate([c["t1"], zeros], axis=0)
        else:
            off, inner = jnp.where(c["lo_s"], 0.0, c["n"][:HALF]), jnp.concatenate([zeros, c["t2"]], axis=0)
        c["off"] = _hp(off, inner)
    for c in chains:
        if c["d"] == 0:
            x = _hp(c["t2"], jnp.concatenate([zeros, c["off"]], axis=0))
            c["inv"] = jnp.concatenate([c["t1"], jnp.where(c["lo_s"], x, c["tp"])], axis=0)
        else:
            x = _hp(c["t1"], jnp.concatenate([c["off"], zeros], axis=0))
            c["inv"] = jnp.concatenate([jnp.where(c["lo_s"], c["tp"], x), c["t2"]], axis=0)
    for c in chains:
        c["av"] = _hp(jnp.concatenate([c["a_k"], c["a_rk"]], axis=0), c["v"])
        c["vt"] = _head_transpose(c["v"])
    return chains


def _scan_carry(chains):
    for c in chains:
        c["s"] = c["s_ref"][0, :, c["sl"]]
        c["z"] = _hp_nt(jnp.concatenate([c["rt"], c["kt"]], axis=0), c["s"])
    for c in chains:
        c["u"] = _hp(c["inv"], c["z"][CHUNK:] + c["av"][:CHUNK])
    for c in chains:
        c["y_ref"][c["rows"], c["sl"]] = c["z"][:CHUNK] + c["av"][CHUNK:] - _hp(c["a_rb"], c["u"])
    for c in chains:
        upd = _hp(c["vt"], c["kh"]) - _hp(_head_transpose(c["u"]), c["bh"])
        c["s_ref"][0, :, c["sl"]] = c["s"] * c["p_end"] + upd


REGION_CHUNKS = CTX_CHUNKS
STEP_CHUNKS = 2
STEP_ROWS = STEP_CHUNKS * CHUNK
SCAN_STEPS = REGION_CHUNKS // STEP_CHUNKS
SCAN_STREAMS = tuple((d, base // STEP_CHUNKS, per_seq // STEP_CHUNKS) for d in range(N_DIRS)
                     for base, per_seq in ((0, CHUNKS_PER_CTX_SEQ), (CTX_CHUNKS, CHUNKS_PER_LAT_SEQ)))
STREAM_INS = 3


def _stream_local_block(stream, j):
    return SCAN_STEPS - 1 - j if stream[0] else j


def _scan_kernel(n_cast, *refs):
    n_stream_in = STREAM_INS * len(SCAN_STREAMS)
    n_in = n_stream_in + n_cast
    j = pl.program_id(0)
    for w_ref, o_ref in zip(refs[n_stream_in:n_in], refs[len(refs) - n_cast:]):
        o_ref[...] = w_ref[...].astype(BF16)
    per_stream = []
    for s, stream in enumerate(SCAN_STREAMS):
        d, _, per_seq = stream
        ins = refs[STREAM_INS * s:STREAM_INS * (s + 1)]
        y_ref, s_ref = refs[n_in + 2 * s:n_in + 2 * s + 2]
        s0_ref = ins[-1]
        local = _stream_local_block(stream, j) % per_seq

        @pl.when(local == (per_seq - 1 if d else 0))
        def _():
            s_ref[...] = s0_ref[0]

        per_stream.append((d,) + ins[:-1] + (y_ref, s_ref))

    chains_by_chunk = []
    for k in range(STEP_CHUNKS):
        chains = []
        for d, shared_ref, dir_ref, y_ref, s_ref in per_stream:
            first = (STEP_CHUNKS - 1 - k if d else k) * CHUNK
            chains += _scan_prepare(d, slice(first, first + CHUNK), shared_ref, dir_ref, y_ref, s_ref)
        chains_by_chunk.append(_scan_local(_scan_scores(chains)))
    for chains in chains_by_chunk:
        _scan_carry(chains)


def _scan(shared, perdir, s0, weights):
    in_specs, out_specs, out_shape, operands = [], [], [], []
    cast_in, cast_out, cast_shape = [], [], []
    for w in weights:
        rows = w.shape[0] // SCAN_STEPS
        if rows % 16 or rows * SCAN_STEPS != w.shape[0]:
            rows = 128
        assert w.shape[0] % rows == 0 and w.shape[0] // rows <= SCAN_STEPS
        spec = pl.BlockSpec((rows, w.shape[1]), lambda j, n=w.shape[0] // rows: (jnp.minimum(j, n - 1), 0))
        cast_in.append(spec)
        cast_out.append(spec)
        cast_shape.append(jax.ShapeDtypeStruct(w.shape, BF16))
    for stream in SCAN_STREAMS:
        d, base, per_seq = stream
        seq0 = 0 if base == 0 else BATCH
        n_seq = SCAN_STEPS // per_seq
        loc = lambda j, stream=stream: _stream_local_block(stream, j)
        base += PAD_TOK // STEP_ROWS
        tok = pl.BlockSpec((STEP_ROWS, 3 * WIDTH), lambda j, loc=loc, base=base: (base + loc(j), 0))
        dtok = pl.BlockSpec((1, STEP_ROWS, 3 * WIDTH), lambda j, loc=loc, base=base, d=d: (d, base + loc(j), 0))
        st_in = pl.BlockSpec((1, 1, HEAD_DIM, WIDTH),
                             lambda j, loc=loc, seq0=seq0, per_seq=per_seq, d=d: (seq0 + loc(j) // per_seq, d, 0, 0))
        in_specs += [tok, dtok, st_in]
        operands += [shared, perdir, s0]
        out_specs += [pl.BlockSpec((STEP_ROWS, WIDTH), lambda j, loc=loc: (loc(j), 0)),
                      pl.BlockSpec((1, HEAD_DIM, WIDTH), lambda j, loc=loc, per_seq=per_seq: (loc(j) // per_seq, 0, 0))]
        out_shape += [jax.ShapeDtypeStruct((REGION_CHUNKS * CHUNK, WIDTH), F32),
                      jax.ShapeDtypeStruct((n_seq, HEAD_DIM, WIDTH), F32)]
    return pl.pallas_call(
        functools.partial(_scan_kernel, len(weights)),
        grid=(SCAN_STEPS,),
        in_specs=in_specs + cast_in,
        out_specs=out_specs + cast_out,
        out_shape=out_shape + cast_shape,
        compiler_params=_params(1),
        name="scan",
    )(*operands, *weights)


def _post_kernel(yfc_ref, yfl_ref, ybc_ref, ybl_ref, bonus_lo_ref, bonus_hi_ref, g_lo_ref, g_hi_ref,
                 attc_ref, attl_ref, xp_ref, xs_ref, mod_ref,
                 lng_ref, lnb_ref, ones_ref, wout_ref, n2_ref, x1_ref, h2_ref):
    i = pl.program_id(0)
    bonus = jnp.concatenate([bonus_lo_ref[...], bonus_hi_ref[...]], axis=0)
    gate = jnp.concatenate([g_lo_ref[...], g_hi_ref[...]], axis=0)
    def pick(ctx_ref, lat_ref):
        return _ctx_or_lat(i, ctx_ref, lat_ref, BIG_BLOCK)

    ones = ones_ref[...]
    inv_n = 1.0 / HEAD_DIM
    y = pick(yfc_ref, yfl_ref) + pick(ybc_ref, ybl_ref)
    yc = y - _segsum(y, ones) * inv_n
    var = _segsum(yc * yc, ones) * inv_n
    yn = yc * lax.rsqrt(var + GN_EPS) * lng_ref[...] + lnb_ref[...]
    r_out = ((yn + bonus) * gate).astype(BF16)
    o = _dot(pick(attc_ref, attl_ref), wout_ref[0:WIDTH, :]) + _dot(r_out, wout_ref[WIDTH:, :])
    x1 = pick(xp_ref, xs_ref) + mod_ref[0, 2:3, :] * o
    x1_ref[...] = x1
    h2 = _rmsnorm(x1, n2_ref[...]) * (1.0 + mod_ref[0, 4:5, :]) + mod_ref[0, 3:4, :]
    h2_ref[...] = h2.astype(BF16)


def _post(ys, bonus, g, att_ctx, att_lat, xp, xs, mod, ln_g, ln_b, ones, w_out_bf, norm2):
    blk = BIG_BLOCK
    tok = lambda cols: _tok_spec(cols, blk)
    ctx_lat = [_ctx_tok_spec(WIDTH, blk), _lat_tok_spec(WIDTH, blk)]
    halves = [pl.BlockSpec((PAD_TOK, WIDTH), lambda i, h=h: (i * (blk // PAD_TOK) + 1 + h, 0)) for h in range(2)]
    assert blk == 2 * PAD_TOK
    return pl.pallas_call(
        _post_kernel,
        grid=(N_TOK // blk,),
        in_specs=ctx_lat + ctx_lat + halves + halves
        + ctx_lat + [_ctx_tok_spec(D_MODEL, blk), _lat_tok_spec(D_MODEL, blk),
                  _mod_spec(blk),
                  _const_spec((1, WIDTH)), _const_spec((1, WIDTH)), _const_spec((WIDTH, WIDTH)),
                  _const_spec((2 * WIDTH, D_MODEL)), _const_spec((1, D_MODEL))],
        out_specs=[tok(D_MODEL), tok(D_MODEL)],
        out_shape=[jax.ShapeDtypeStruct((N_TOK, D_MODEL), F32),
                   jax.ShapeDtypeStruct((N_TOK, D_MODEL), BF16)],
        compiler_params=_params(1),
        name="post",
    )(*ys, bonus, bonus, g, g, att_ctx, att_lat, xp, xs, mod, ln_g, ln_b, ones, w_out_bf, norm2)


def _ffn_kernel(x1_ref, h2_ref, hp_ref, hn_ref, mod_ref, w1_ref, w3_ref, wc_ref, w2_ref, nf_ref, yp_ref, ys_ref):
    i = pl.program_id(0)
    blk = BIG_BLOCK
    is_ctx = i < N_CTX_TOK // blk
    has_prev, has_next = _seq_neighbours(i, blk)
    lhs = jnp.concatenate([h2_ref[...], hp_ref[...], hn_ref[...]], axis=0)
    h2 = lhs[:blk]
    rid = lax.broadcasted_iota(jnp.int32, (blk, 1), 0)
    inner = jnp.where(is_ctx, SEQ, -1)
    seq_start = rid == inner
    seq_end = rid == inner - 1
    acc = jnp.zeros((blk, D_MODEL), F32)
    for f in range(D_FF // FF_TILE):
        fs = slice(f * FF_TILE, (f + 1) * FF_TILE)
        a_all = _dot(lhs, w1_ref[:, fs])
        a = a_all[:blk]
        prow = jnp.where(has_prev, a_all[blk + 15:blk + 16, :], 0.0)
        nrow = jnp.where(has_next, a_all[blk + 16:blk + 17, :], 0.0)
        aprev, anext = _shifted(a, prow, nrow)
        aprev = jnp.where(seq_start, 0.0, aprev)
        anext = jnp.where(seq_end, 0.0, anext)
        cv = aprev * wc_ref[0:1, fs] + a * wc_ref[1:2, fs] + anext * wc_ref[2:3, fs]
        act = _silu(cv) * _dot(h2, w3_ref[:, fs])
        acc = acc + _dot(act.astype(BF16), w2_ref[fs, :])
    x2 = x1_ref[...] + mod_ref[0, 5:6, :] * acc
    y = _rmsnorm(x2, nf_ref[...])

    @pl.when(is_ctx)
    def _():
        yp_ref[...] = y

    @pl.when(jnp.logical_not(is_ctx))
    def _():
        ys_ref[...] = y


def _ffn(x1, h2, mod, w1_bf, w3_bf, wc, w2_bf, norm_f):
    blk = BIG_BLOCK
    rows16 = blk // 16
    return pl.pallas_call(
        _ffn_kernel,
        grid=(N_TOK // blk,),
        in_specs=[_tok_spec(D_MODEL, blk), _tok_spec(D_MODEL, blk),
                  pl.BlockSpec((16, D_MODEL), lambda i: (jnp.maximum(i * rows16 - 1, 0), 0)),
                  pl.BlockSpec((16, D_MODEL), lambda i: (jnp.minimum((i + 1) * rows16, N_TOK // 16 - 1), 0)),
                  _mod_spec(blk),
                  _const_spec((D_MODEL, D_FF)), _const_spec((D_MODEL, D_FF)), _const_spec((3, D_FF)),
                  _const_spec((D_FF, D_MODEL)), _const_spec((1, D_MODEL))],
        out_specs=[_ctx_tok_spec(D_MODEL, blk), _lat_tok_spec(D_MODEL, blk)],
        out_shape=[jax.ShapeDtypeStruct((N_CTX_TOK, D_MODEL), F32),
                   jax.ShapeDtypeStruct((N_LAT_TOK, D_MODEL), F32)],
        compiler_params=_params(1),
        name="ffn",
    )(x1, h2, h2, h2, mod, w1_bf, w3_bf, wc, w2_bf, norm_f)


def kernel(x_prompt, x_sample, cache_k, cache_v, state_rwkv, c, c_ctx, w_ada, b_ada, norm1, norm2,
           w_in, w_ts, w0, w2, a0, a2, g2, k_k, k_a, r_k, ln_x_g, ln_x_b, rpb, w_out,
           w_ffn1, w_ffn3, w_ffn_conv, w_ffn2, norm_f):
    xp = x_prompt.reshape(N_CTX_TOK, D_MODEL)
    xs = x_sample.reshape(N_LAT_TOK, D_MODEL)
    row = lambda t: t.reshape(1, -1)
    cond = jnp.concatenate([c_ctx[None, :], c, jnp.zeros((8 - 1 - DEC_BATCH, D_MODEL), F32)], axis=0)
    mod = _modulation(cond, w_ada[0], b_ada[0]).reshape(8, 6, D_MODEL)

    head_id = jnp.arange(WIDTH) // HEAD_DIM
    ones = (head_id[:, None] == head_id[None, :]).astype(BF16)
    q, k, v, shared, g, bonus, perdir, new_k, new_v = _inproj(
        xp, xs, mod, row(norm1[0]), w_in[0].astype(BF16), w_ts[0], w0[0], w2[0], a0[0], a2[0], g2[0],
        row(k_k[0]), row(k_a[0]), row(r_k[0]), ones)

    att_ctx = _ctx_attention(q, k, v)
    att_lat = _lat_attention(q, k, v, cache_k[:, 0].reshape(DEC_BATCH, PAST_LEN, WIDTH),
                             cache_v[:, 0].reshape(DEC_BATCH, PAST_LEN, WIDTH), _na_bias_table(rpb[0]))
    s_lat = jnp.transpose(state_rwkv[:, 0], (0, 1, 3, 2, 4)).reshape(DEC_BATCH, N_DIRS, HEAD_DIM, WIDTH)
    s0 = jnp.concatenate([jnp.zeros((BATCH, N_DIRS, HEAD_DIM, WIDTH), F32), s_lat], axis=0)
    (y_fc, s_fc, y_fl, _, y_bc, s_bc, y_bl, _,
     w_out_bf, w1_bf, w3_bf, w2_bf) = _scan(shared, perdir, s0, (w_out[0], w_ffn1[0], w_ffn3[0], w_ffn2[0]))
    s_fin = jnp.stack([s_fc, s_bc], axis=1)

    x1, h2 = _post((y_fc, y_fl, y_bc, y_bl), bonus, g, att_ctx, att_lat, xp, xs, mod, row(ln_x_g[0]), row(ln_x_b[0]), ones,
                   w_out_bf, row(norm2[0]))
    yp, ys = _ffn(x1, h2, mod, w1_bf, w3_bf, w_ffn_conv[0], w2_bf, row(norm_f))

    new_s = jnp.transpose(s_fin.reshape(BATCH, N_DIRS, HEAD_DIM, N_HEADS, HEAD_DIM),
                          (0, 1, 3, 2, 4)).reshape(BATCH, 1, N_DIRS, N_HEADS, HEAD_DIM, HEAD_DIM)
    return (yp.reshape(BATCH, SEQ, D_MODEL), ys.reshape(DEC_BATCH, DEC_SEQ, D_MODEL), new_k, new_v, new_s)
```

```python
import functools

import jax
import jax.numpy as jnp
from jax import lax
from jax.experimental import pallas as pl
from jax.experimental.pallas import tpu as pltpu

F32 = jnp.float32
BF16 = jnp.bfloat16
HIGHEST = lax.Precision.HIGHEST

D_MODEL = 1024
BATCH = 16
SEQ = 256
DEC_BATCH = 2
DEC_SEQ = 2048
PAST_LEN = 512
GRID_W = 64
HEAD_DIM = 64
N_HEADS = 8
WIDTH = N_HEADS * HEAD_DIM
PAIR_W = 2 * HEAD_DIM
NA_ROWS = 8
NA_COLS = 16
N_DIRS = 2
LORA = 64
GATE_LORA = 128
D_FF = 2816
EPS = 1e-6
GN_EPS = 64e-5
ATT_SCALE = HEAD_DIM ** -0.5
RWKV_COLS = 3 * WIDTH + N_DIRS * 2 * LORA + GATE_LORA
MASK_VALUE = -1e30

N_CTX_TOK = BATCH * SEQ
N_LAT_TOK = DEC_BATCH * DEC_SEQ
N_TOK = N_CTX_TOK + N_LAT_TOK
TOK_BLOCK = 256
N_BLOCKS = N_TOK // TOK_BLOCK
BIG_BLOCK = 512
PAD_TOK = TOK_BLOCK
N_SEQS = BATCH + DEC_BATCH
CHUNK = 64
N_CHUNKS = N_TOK // CHUNK
CTX_CHUNKS = N_CTX_TOK // CHUNK
CHUNKS_PER_CTX_SEQ = SEQ // CHUNK
CHUNKS_PER_LAT_SEQ = DEC_SEQ // CHUNK
GRID_ROWS = DEC_SEQ // GRID_W
LAT_ROWS = 4
MXU_WIDTH = 256
FF_SPLIT = (D_FF // MXU_WIDTH + 1) // 2 * MXU_WIDTH
GROUP = 4
GROUP_W = GROUP * HEAD_DIM
VMEM_LIMIT = 56 * 1024 * 1024


def _params(n_axes, limit=VMEM_LIMIT):
    return pltpu.CompilerParams(dimension_semantics=("arbitrary",) * n_axes, vmem_limit_bytes=limit)


def _const_spec(shape):
    zeros = (0,) * len(shape)
    return pl.BlockSpec(shape, lambda *_: zeros, pipeline_mode=pl.Buffered(1))


def _tok_spec(cols, blk=TOK_BLOCK):
    return pl.BlockSpec((blk, cols), lambda i: (i, 0))


def _ctx_tok_spec(cols, blk=TOK_BLOCK):
    return pl.BlockSpec((blk, cols), lambda i: (jnp.minimum(i, N_CTX_TOK // blk - 1), 0))


def _lat_tok_spec(cols, blk=TOK_BLOCK):
    return pl.BlockSpec((blk, cols), lambda i: (jnp.maximum(i - N_CTX_TOK // blk, 0), 0))


def _ctx_or_lat(i, ctx_ref, lat_ref, blk=TOK_BLOCK):
    return jnp.where(i < N_CTX_TOK // blk, ctx_ref[...], lat_ref[...])


def _mod_spec(blk=TOK_BLOCK):
    def row(i):
        return jnp.where(i < N_CTX_TOK // blk, 0, 1 + (i - N_CTX_TOK // blk) // (DEC_SEQ // blk))
    return pl.BlockSpec((1, 6, D_MODEL), lambda i: (row(i), 0, 0))


def _seq_neighbours(i, blk=TOK_BLOCK):
    per_seq = DEC_SEQ // blk
    j = (i - N_CTX_TOK // blk) % per_seq
    lat = i >= N_CTX_TOK // blk
    return lat & (j != 0), lat & (j != per_seq - 1)


def _silu(x):
    return x * jax.nn.sigmoid(x)


def _softplus(x):
    return jnp.maximum(x, 0.0) + jnp.log(1.0 + jnp.exp(-jnp.abs(x)))


def _rmsnorm(x, g):
    return x * lax.rsqrt(jnp.mean(x * x, axis=-1, keepdims=True) + EPS) * g


def _dot(a, b, precision=None):
    return jnp.dot(a, b, precision=precision, preferred_element_type=F32)


def _dot_nt(a, b, precision=None):
    return lax.dot_general(a, b, (((1,), (1,)), ((), ())), precision=precision, preferred_element_type=F32)


def _split(x):
    hi = x.astype(BF16)
    return hi, (x - hi.astype(F32)).astype(BF16)


def _segsum(x, ones_bf):
    return _dot(x.astype(BF16), ones_bf)


def _shifted(a, prev_row, next_row):
    t = a.shape[0]
    rid = lax.broadcasted_iota(jnp.int32, (t, 1), 0)
    prev = jnp.where(rid == 0, prev_row, pltpu.roll(a, 1, axis=0))
    nxt = jnp.where(rid == t - 1, next_row, pltpu.roll(a, t - 1, axis=0))
    return prev, nxt


def _mod_kernel(cond_ref, w_ref, b_ref, o_ref):
    o_ref[...] = _dot(_silu(cond_ref[...]).astype(BF16), w_ref[...].astype(BF16)) + b_ref[...]


def _modulation(cond, w_ada, b_ada):
    n = 6 * D_MODEL
    return pl.pallas_call(
        _mod_kernel,
        grid=(6,),
        in_specs=[pl.BlockSpec((8, D_MODEL), lambda j: (0, 0)),
                  pl.BlockSpec((D_MODEL, D_MODEL), lambda j: (0, j)),
                  pl.BlockSpec((1, D_MODEL), lambda j: (0, j))],
        out_specs=pl.BlockSpec((8, D_MODEL), lambda j: (0, j)),
        out_shape=jax.ShapeDtypeStruct((8, n), F32),
        compiler_params=_params(1),
        name="mod",
    )(cond, w_ada, b_ada.reshape(1, n))


def _inproj_kernel(xp_ref, xs_ref, xprev_ref, xnext_ref, mod_ref, n1_ref, w_ref, *rest):
    rwkv_refs, (q_ref, k_ref, v_ref), rwkv_outs, (nk_ref, nv_ref), z_bufs = (
        rest[:10], rest[10:13], rest[13:17], rest[17:19], rest[19:])
    i = pl.program_id(0)
    blk = jnp.minimum(i, N_BLOCKS - 1)

    @pl.when(i == 0)
    def _():
        z_bufs[1][...] = jnp.zeros_like(z_bufs[1])

    def modulated(x):
        h = _rmsnorm(x, n1_ref[...]) * (1.0 + mod_ref[0, 1:2, :]) + mod_ref[0, 0:1, :]
        return h.astype(BF16)

    def step(z_store, z_load):
        h = modulated(_ctx_or_lat(blk, xp_ref, xs_ref))
        edge = modulated(jnp.concatenate([xprev_ref[...], xnext_ref[...]], axis=0))
        lhs = jnp.concatenate([h, edge], axis=0)
        kv = []

        def project(c0, c1):
            z_store[:, c0:c1] = _dot(lhs, w_ref[:, 3 * WIDTH + c0:3 * WIDTH + c1])

        def portion_q():
            project(3 * WIDTH, RWKV_COLS)
            project(WIDTH, 2 * WIDTH)
            q_ref[...] = _dot(h, w_ref[:, 0:WIDTH])

        def portion_k():
            project(0, WIDTH)
            kv.append(_dot(h, w_ref[:, WIDTH:2 * WIDTH]))

        def portion_v():
            project(2 * WIDTH, 3 * WIDTH)
            kv.append(_dot(h, w_ref[:, 2 * WIDTH:3 * WIDTH]))

        has_prev, has_next = _seq_neighbours(i - 1)

        def zcols(c0, c1):
            return (z_load[0:TOK_BLOCK, c0:c1],
                    jnp.where(has_prev, z_load[TOK_BLOCK + 7:TOK_BLOCK + 8, c0:c1], 0.0),
                    jnp.where(has_next, z_load[TOK_BLOCK + 8:TOK_BLOCK + 9, c0:c1], 0.0))

        _rwkv_tokens(zcols, (portion_q, portion_k, portion_v), *rwkv_refs, *rwkv_outs)
        k, v = kv
        k_ref[...] = k
        v_ref[...] = v

        @pl.when(i < N_CTX_TOK // TOK_BLOCK)
        def _():
            nk_ref[0, 0] = k.reshape(SEQ, N_HEADS, HEAD_DIM)
            nv_ref[0, 0] = v.reshape(SEQ, N_HEADS, HEAD_DIM)

    @pl.when(i % 2 == 0)
    def _():
        step(z_bufs[0], z_bufs[1])

    @pl.when(i % 2 == 1)
    def _():
        step(z_bufs[1], z_bufs[0])


def _inproj(xp, xs, mod, norm1, w_in_bf, w_ts, w0, w2, a0, a2, g2, k_k, k_a, r_k, ones):
    tok = jax.ShapeDtypeStruct((N_TOK, WIDTH), F32)
    padded = lambda cols: jax.ShapeDtypeStruct((N_TOK + PAD_TOK, cols), F32)
    lat0 = N_CTX_TOK // TOK_BLOCK
    last = N_BLOCKS - 1
    rows8 = TOK_BLOCK // 8
    n8 = N_LAT_TOK // 8
    clamped = lambda cols: pl.BlockSpec((TOK_BLOCK, cols), lambda i: (jnp.minimum(i, last), 0))
    x_ctx = pl.BlockSpec((TOK_BLOCK, D_MODEL), lambda i: (jnp.minimum(i, lat0 - 1), 0))
    x_lat = pl.BlockSpec((TOK_BLOCK, D_MODEL), lambda i: (jnp.clip(i - lat0, 0, last - lat0), 0))
    prev8 = pl.BlockSpec((8, D_MODEL), lambda i: (jnp.clip((jnp.minimum(i, last) - lat0) * rows8 - 1, 0, n8 - 1), 0))
    next8 = pl.BlockSpec((8, D_MODEL), lambda i: (jnp.clip((jnp.minimum(i, last) - lat0 + 1) * rows8, 0, n8 - 1), 0))
    mod_rows = pl.BlockSpec((1, 6, D_MODEL), lambda i: (
        jnp.where(i < lat0, 0, 1 + (jnp.minimum(i, last) - lat0) // (DEC_SEQ // TOK_BLOCK)), 0, 0))
    cache = pl.BlockSpec((1, 1, SEQ, N_HEADS, HEAD_DIM), lambda i: (jnp.minimum(i, BATCH - 1), 0, 0, 0, 0))
    z_buf = pltpu.VMEM((TOK_BLOCK + 16, RWKV_COLS), F32)
    return pl.pallas_call(
        _inproj_kernel,
        grid=(N_BLOCKS + 1,),
        in_specs=[x_ctx, x_lat, prev8, next8, mod_rows,
                  _const_spec((1, D_MODEL)),
                  _const_spec((D_MODEL, 3 * WIDTH + RWKV_COLS)),
                  _const_spec((3, RWKV_COLS)),
                  _const_spec((N_DIRS, WIDTH)), _const_spec((N_DIRS, LORA, WIDTH)),
                  _const_spec((N_DIRS, WIDTH)), _const_spec((N_DIRS, LORA, WIDTH)),
                  _const_spec((GATE_LORA, WIDTH)),
                  _const_spec((1, WIDTH)), _const_spec((1, WIDTH)), _const_spec((1, WIDTH)),
                  _const_spec((WIDTH, WIDTH))],
        out_specs=[clamped(WIDTH), clamped(WIDTH), clamped(WIDTH),
                   _tok_spec(3 * WIDTH), _tok_spec(WIDTH), _tok_spec(WIDTH),
                   pl.BlockSpec((N_DIRS, TOK_BLOCK, 3 * WIDTH), lambda i: (0, i, 0)),
                   cache, cache],
        out_shape=[tok, tok, tok,
                   padded(3 * WIDTH), padded(WIDTH), padded(WIDTH),
                   jax.ShapeDtypeStruct((N_DIRS, N_TOK + PAD_TOK, 3 * WIDTH), F32),
                   jax.ShapeDtypeStruct((BATCH, 1, SEQ, N_HEADS, HEAD_DIM), F32),
                   jax.ShapeDtypeStruct((BATCH, 1, SEQ, N_HEADS, HEAD_DIM), F32)],
        scratch_shapes=[z_buf, z_buf],
        compiler_params=_params(1),
        name="inproj",
    )(xp, xs, xs, xs, mod, norm1, w_in_bf, w_ts, w0, w2, a0, a2, g2, k_k, k_a, r_k, ones)


def _pair_queries(q):
    lo_half = lax.broadcasted_iota(jnp.int32, q.shape, 1) < HEAD_DIM
    return jnp.concatenate([jnp.where(lo_half, q, 0.0), jnp.where(lo_half, 0.0, q)], axis=0).astype(BF16)


def _pair_outputs(o):
    t = o.shape[0] // 2
    lo_half = lax.broadcasted_iota(jnp.int32, (t, PAIR_W), 1) < HEAD_DIM
    return jnp.where(lo_half, o[:t], o[t:])


def _ctx_attn_kernel(q_ref, k_ref, v_ref, o_ref):
    pairs = [slice(p * PAIR_W, (p + 1) * PAIR_W) for p in range(N_HEADS // 2)]
    scores = [_dot_nt(_pair_queries(q_ref[:, ps]), k_ref[:, ps].astype(BF16)) * ATT_SCALE for ps in pairs]
    probs = []
    for s in scores:
        e = jnp.exp(s - jnp.max(s, axis=-1, keepdims=True))
        probs.append((e.astype(BF16), jnp.sum(e, axis=-1, keepdims=True)))
    for ps, (e, l) in zip(pairs, probs):
        o_ref[:, ps] = _pair_outputs(_dot(e, v_ref[:, ps].astype(BF16)) / l).astype(BF16)


def _ctx_attention(q, k, v):
    spec = pl.BlockSpec((SEQ, WIDTH), lambda b: (b, 0))
    return pl.pallas_call(
        _ctx_attn_kernel,
        grid=(BATCH,),
        in_specs=[spec, spec, spec],
        out_specs=spec,
        out_shape=jax.ShapeDtypeStruct((N_CTX_TOK, WIDTH), BF16),
        compiler_params=_params(1),
        name="ctxattn",
    )(q, k, v)


def _lat_attn_kernel(q_ref, k_ref, v_ref, ck_ref, cv_ref, bias_ref, o_ref, kbf, vbf, ckbf, cvbf):
    i = pl.program_id(1)

    @pl.when(i == 0)
    def _():
        kbf[...] = k_ref[...].astype(BF16)
        vbf[...] = v_ref[...].astype(BF16)
        ckbf[...] = ck_ref[0].astype(BF16)
        cvbf[...] = cv_ref[0].astype(BF16)

    win = NA_ROWS * GRID_W
    rows = []
    for rr in range(LAT_ROWS):
        gr = i * LAT_ROWS + rr
        first_row = jnp.clip(gr - NA_ROWS // 2, 0, GRID_ROWS - NA_ROWS)
        start = pl.multiple_of(first_row * GRID_W, GRID_W)
        off0 = first_row - gr + NA_ROWS - 1
        rows.append((slice(rr * GRID_W, (rr + 1) * GRID_W), start, off0))
    pairs = [slice(p * PAIR_W, (p + 1) * PAIR_W) for p in range(N_HEADS // 2)]
    scores = []
    for p, ps in enumerate(pairs):
        lhs = [_pair_queries(q_ref[rs, ps]) for rs, _, _ in rows]
        s_loc = []
        for (_, start, off0), lhs_r in zip(rows, lhs):
            bias = jnp.concatenate(
                [jnp.concatenate([bias_ref[2 * p + hh, off0 + 2 * j] for j in range(NA_ROWS // 2)], axis=-1)
                 for hh in range(2)], axis=0)
            s_loc.append(_dot_nt(lhs_r, kbf[pl.ds(start, win), ps]) * ATT_SCALE + bias)
        s_ctx = _dot_nt(jnp.concatenate(lhs, axis=0), ckbf[:, ps]) * ATT_SCALE
        scores.append((jnp.concatenate(s_loc, axis=0), s_ctx))
    probs = []
    for s_loc, s_ctx in scores:
        m = jnp.maximum(jnp.max(s_loc, axis=-1, keepdims=True), jnp.max(s_ctx, axis=-1, keepdims=True))
        e_loc = jnp.exp(s_loc - m)
        e_ctx = jnp.exp(s_ctx - m)
        l = jnp.sum(e_loc, axis=-1, keepdims=True) + jnp.sum(e_ctx, axis=-1, keepdims=True)
        probs.append((e_loc.astype(BF16), e_ctx.astype(BF16), l))
    for ps, (e_loc, e_ctx, l) in zip(pairs, probs):
        o_ctx = _dot(e_ctx, cvbf[:, ps])
        for rr, (rs, start, _) in enumerate(rows):
            both = slice(rr * 2 * GRID_W, (rr + 1) * 2 * GRID_W)
            o = _dot(e_loc[both], vbf[pl.ds(start, win), ps]) + o_ctx[both]
            o_ref[rs, ps] = _pair_outputs(o / l[both]).astype(BF16)


def _na_bias_table(rpb):
    n = rpb.shape[-1]
    w = GRID_W
    pair = jnp.concatenate([rpb[:, :-1], rpb[:, 1:]], axis=-1).astype(F32)
    x = jnp.arange(2 * n)
    y = jnp.arange(2 * w)
    cq = jnp.arange(w)
    ck = y % w
    sel = ((x // n)[:, None, None] == (y // w)[None, None, :]) & (
        (x % n)[:, None, None] == ck[None, None, :] - cq[None, :, None] + NA_COLS - 1)
    t = jnp.einsum("hdx,xqy->hdqy", pair, sel.astype(F32), precision=HIGHEST)
    cs = jnp.clip(cq - NA_COLS // 2, 0, w - NA_COLS)[:, None]
    return jnp.where((ck[None, :] >= cs) & (ck[None, :] < cs + NA_COLS), t, MASK_VALUE)


def _lat_attention(q, k, v, cache_k, cache_v, bias):
    blk = LAT_ROWS * GRID_W
    steps = GRID_ROWS // LAT_ROWS
    lat0 = N_CTX_TOK // blk
    seq = pl.BlockSpec((DEC_SEQ, WIDTH), lambda b, i: (N_CTX_TOK // DEC_SEQ + b, 0))
    ctx = pl.BlockSpec((1, PAST_LEN, WIDTH), lambda b, i: (b, 0, 0))
    return pl.pallas_call(
        _lat_attn_kernel,
        grid=(DEC_BATCH, steps),
        in_specs=[pl.BlockSpec((blk, WIDTH), lambda b, i: (lat0 + b * steps + i, 0)),
                  seq, seq, ctx, ctx,
                  _const_spec((N_HEADS, 2 * NA_ROWS - 2, GRID_W, PAIR_W))],
        out_specs=pl.BlockSpec((blk, WIDTH), lambda b, i: (b * steps + i, 0)),
        out_shape=jax.ShapeDtypeStruct((N_LAT_TOK, WIDTH), BF16),
        scratch_shapes=[pltpu.VMEM((DEC_SEQ, WIDTH), BF16), pltpu.VMEM((DEC_SEQ, WIDTH), BF16),
                        pltpu.VMEM((PAST_LEN, WIDTH), BF16), pltpu.VMEM((PAST_LEN, WIDTH), BF16)],
        compiler_params=_params(2),
        name="latattn",
    )(q, k, v, cache_k, cache_v, bias)


def _rwkv_tokens(zcols, fillers, wts_ref, w0_ref, w2_ref, a0_ref, a2_ref, g2_ref, kk_ref, ka_ref, rk_ref, ones_ref,
                 shared_out, g_out, bonus_out, dir_out):
    def conv(c0, c1):
        zc, prow, nrow = zcols(c0, c1)
        zprev, znext = _shifted(zc, prow, nrow)
        return zprev * wts_ref[0:1, c0:c1] + zc * wts_ref[1:2, c0:c1] + znext * wts_ref[2:3, c0:c1]

    fill = iter(fillers)
    dirs = [slice(d * LORA, (d + 1) * LORA) for d in range(N_DIRS)]
    ones = ones_ref[...]
    o = 3 * WIDTH
    lora_in = conv(o, RWKV_COLS)
    kr = conv(WIDTH, 2 * WIDTH)
    tw = jnp.tanh(lora_in[:, 0:N_DIRS * LORA])
    xa = lora_in[:, N_DIRS * LORA:2 * N_DIRS * LORA]
    xg = lora_in[:, 2 * N_DIRS * LORA:]
    kkraw = kr * kk_ref[...]
    next(fill)()
    kk = kkraw * lax.rsqrt(_segsum(kkraw * kkraw, ones) + 1e-12)
    bf = lambda t: t.astype(BF16)
    lora_w = [_dot(bf(tw[:, ls]), bf(w2_ref[d])) for d, ls in enumerate(dirs)]
    r = conv(0, WIDTH)
    next(fill)()
    lora_a = [_dot(bf(xa[:, ls]), bf(a2_ref[d])) for d, ls in enumerate(dirs)]
    g_out[...] = _dot(bf(jax.nn.sigmoid(xg)), bf(g2_ref[...]))
    v = conv(2 * WIDTH, 3 * WIDTH)
    next(fill)()
    kd_sum = jnp.zeros_like(kr)
    for d in range(N_DIRS):
        w_log = -_softplus(-(w0_ref[d:d + 1, :] + lora_w[d])) - 0.5
        a = jax.nn.sigmoid(a0_ref[d:d + 1, :] + lora_a[d])
        kd = kr * (1.0 + (a - 1.0) * ka_ref[...])
        dir_out[d, :, 0:WIDTH] = -jnp.exp(w_log)
        dir_out[d, :, WIDTH:2 * WIDTH] = kk * a
        dir_out[d, :, 2 * WIDTH:] = kd
        kd_sum = kd_sum + kd
    shared_out[:, 0:WIDTH] = r
    shared_out[:, WIDTH:2 * WIDTH] = kk
    shared_out[:, 2 * WIDTH:] = v
    bonus_out[...] = _segsum(r * kd_sum * rk_ref[...], ones) * v


HALF = CHUNK // 2


def _blockdiag(x, block=HEAD_DIM):
    p = x.astype(BF16)
    blk = lax.broadcasted_iota(jnp.int32, p.shape, 1) // block
    return jnp.concatenate([jnp.where(blk == q, p, jnp.zeros_like(p)) for q in range(GROUP_W // block)], axis=0)


def _hp(a, b, block=HEAD_DIM):
    return _dot(a.astype(BF16), _blockdiag(b, block))


def _hp_nt(a, b):
    return _dot_nt(a.astype(BF16), _blockdiag(b))


def _head_transpose(x):
    xt = jnp.transpose(x)
    return jnp.concatenate([xt[h * HEAD_DIM:(h + 1) * HEAD_DIM] for h in range(GROUP)], axis=1)


def _scan_prepare(d, rows, shared_ref, dir_ref, y_ref, s_ref):
    row = lax.broadcasted_iota(jnp.int32, (CHUNK, GROUP_W), 0)
    lane = lax.broadcasted_iota(jnp.int32, (CHUNK, GROUP_W), 1)
    col = lane % CHUNK
    strict = col > row if d else col < row
    incl = col >= row if d else col <= row
    row_h = lax.broadcasted_iota(jnp.int32, (HALF, GROUP_W), 0)
    lane_h = lax.broadcasted_iota(jnp.int32, (HALF, GROUP_W), 1)
    lo_s = lane_h % CHUNK < HALF
    eye_half = (lane_h % HALF == row_h).astype(F32)

    lw = dir_ref[0, rows, 0:WIDTH]
    mask = jnp.where(incl[:, :CHUNK], 1.0, 0.0).astype(BF16)
    lw_hi, lw_mid = _split(lw)
    lw_lo = (lw - lw_hi.astype(F32) - lw_mid.astype(F32)).astype(BF16)
    cum = _dot(mask, lw_hi) + _dot(mask, lw_mid) + _dot(mask, lw_lo)
    tot = jnp.sum(lw, axis=0, keepdims=True)
    p_in = jnp.exp(cum)
    p_neg = jnp.exp(-cum)
    p_rem = jnp.exp(tot - cum)
    kt = shared_ref[rows, WIDTH:2 * WIDTH] * jnp.exp(cum - lw)
    rt = shared_ref[rows, 0:WIDTH] * p_in
    b = dir_ref[0, rows, WIDTH:2 * WIDTH]
    kd = dir_ref[0, rows, 2 * WIDTH:]
    bt = b * p_neg
    kdt = kd * p_neg
    bh = b * p_rem
    kh = kd * p_rem
    p_end = jnp.exp(tot)
    v = shared_ref[rows, 2 * WIDTH:]

    chains = []
    for gi in range(N_HEADS // GROUP):
        sl = slice(gi * GROUP_W, (gi + 1) * GROUP_W)
        chains.append(dict(d=d, rows=rows, sl=sl, strict=strict, incl=incl, lo_s=lo_s, eye_half=eye_half, y_ref=y_ref, s_ref=s_ref,
                           kt=kt[:, sl], rt=rt[:, sl], v=v[:, sl], bt=bt[:, sl], kdt=kdt[:, sl],
                           kh=kh[:, sl], bh=bh[:, sl], p_end=p_end[:, sl]))
    return chains


def _scan_scores(chains):
    for c in chains:
        lhs = jnp.concatenate([c["kt"], c["rt"]], axis=0)
        g_b = _hp_nt(lhs, c["bt"])
        g_k = _hp_nt(lhs, c["kdt"])
        c["a_k"] = jnp.where(c["strict"], g_k[:CHUNK], 0.0)
        c["a_rb"] = jnp.where(c["incl"], g_b[CHUNK:], 0.0)
        c["a_rk"] = jnp.where(c["incl"], g_k[CHUNK:], 0.0)
        n = jnp.where(c["strict"], -g_b[:CHUNK], 0.0)
        c["n"] = n
        c["pw"] = jnp.where(c["lo_s"], n[:HALF], n[HALF:])
        c["tp"] = c["eye_half"] + c["pw"]
    return chains


def _scan_local(chains):
    zeros = jnp.zeros((HALF, GROUP_W), F32)
    for c in chains:
        c["pw"] = _hp(c["pw"], c["pw"], HALF)
    for _ in range(3):
        for c in chains:
            both = _hp(jnp.concatenate([c["tp"], c["pw"]], axis=0), c["pw"], HALF)
            c["tp"] = c["tp"] + both[:HALF]
            c["pw"] = both[HALF:]
    for c in chains:
        c["tp"] = c["tp"] + _hp(c["tp"], c["pw"], HALF)
    for c in chains:
        c["t1"] = jnp.where(c["lo_s"], c["tp"], 0.0)
        c["t2"] = jnp.where(c["lo_s"], 0.0, c["tp"])
        if c["d"] == 0:
            off, inner = jnp.where(c["lo_s"], c["n"][HALF:], 0.0), jnp.concatenate([c["t1"], zeros], axis=0)
        else:
            off, inner = jnp.where(c["lo_s"], 0.0, c["n"][:HALF]), jnp.concatenate([zeros, c["t2"]], axis=0)
        c["off"] = _hp(off, inner)
    for c in chains:
        if c["d"] == 0:
            x = _hp(c["t2"], jnp.concatenate([zeros, c["off"]], axis=0))
            c["inv"] = jnp.concatenate([c["t1"], jnp.where(c["lo_s"], x, c["tp"])], axis=0)
        else:
            x = _hp(c["t1"], jnp.concatenate([c["off"], zeros], axis=0))
            c["inv"] = jnp.concatenate([jnp.where(c["lo_s"], c["tp"], x), c["t2"]], axis=0)
    for c in chains:
        c["av"] = _hp(jnp.concatenate([c["a_k"], c["a_rk"]], axis=0), c["v"])
        c["vt"] = _head_transpose(c["v"])
    return chains


def _scan_carry(chains):
    for c in chains:
        c["s"] = c["s_ref"][0, :, c["sl"]]
        c["z"] = _hp_nt(jnp.concatenate([c["rt"], c["kt"]], axis=0), c["s"])
    for c in chains:
        c["u"] = _hp(c["inv"], c["z"][CHUNK:] + c["av"][:CHUNK])
    for c in chains:
        c["y_ref"][c["rows"], c["sl"]] = c["z"][:CHUNK] + c["av"][CHUNK:] - _hp(c["a_rb"], c["u"])
    for c in chains:
        upd = _hp(c["vt"], c["kh"]) - _hp(_head_transpose(c["u"]), c["bh"])
        c["s_ref"][0, :, c["sl"]] = c["s"] * c["p_end"] + upd


REGION_CHUNKS = CTX_CHUNKS
STEP_CHUNKS = 2
STEP_ROWS = STEP_CHUNKS * CHUNK
SCAN_STEPS = REGION_CHUNKS // STEP_CHUNKS
SCAN_STREAMS = tuple((d, base // STEP_CHUNKS, per_seq // STEP_CHUNKS) for d in range(N_DIRS)
                     for base, per_seq in ((0, CHUNKS_PER_CTX_SEQ), (CTX_CHUNKS, CHUNKS_PER_LAT_SEQ)))
STREAM_INS = 3


def _stream_local_block(stream, j):
    return SCAN_STEPS - 1 - j if stream[0] else j


def _scan_kernel(n_cast, *refs):
    n_stream_in = STREAM_INS * len(SCAN_STREAMS)
    n_in = n_stream_in + n_cast
    j = pl.program_id(0)
    for w_ref, o_ref in zip(refs[n_stream_in:n_in], refs[len(refs) - n_cast:]):
        o_ref[...] = w_ref[...].astype(BF16)
    per_stream = []
    for s, stream in enumerate(SCAN_STREAMS):
        d, _, per_seq = stream
        ins = refs[STREAM_INS * s:STREAM_INS * (s + 1)]
        y_ref, s_ref = refs[n_in + 2 * s:n_in + 2 * s + 2]
        s0_ref = ins[-1]
        local = _stream_local_block(stream, j) % per_seq

        @pl.when(local == (per_seq - 1 if d else 0))
        def _():
            s_ref[...] = s0_ref[0]

        per_stream.append((d,) + ins[:-1] + (y_ref, s_ref))

    chains_by_chunk = []
    for k in range(STEP_CHUNKS):
        chains = []
        for d, shared_ref, dir_ref, y_ref, s_ref in per_stream:
            first = (STEP_CHUNKS - 1 - k if d else k) * CHUNK
            chains += _scan_prepare(d, slice(first, first + CHUNK), shared_ref, dir_ref, y_ref, s_ref)
        chains_by_chunk.append(_scan_local(_scan_scores(chains)))
    for chains in chains_by_chunk:
        _scan_carry(chains)


def _scan(shared, perdir, s0, weights):
    in_specs, out_specs, out_shape, operands = [], [], [], []
    cast_in, cast_out, cast_shape = [], [], []
    for w in weights:
        rows = w.shape[0] // SCAN_STEPS
        if rows % 16 or rows * SCAN_STEPS != w.shape[0]:
            rows = 128
        assert w.shape[0] % rows == 0 and w.shape[0] // rows <= SCAN_STEPS
        spec = pl.BlockSpec((rows, w.shape[1]), lambda j, n=w.shape[0] // rows: (jnp.minimum(j, n - 1), 0))
        cast_in.append(spec)
        cast_out.append(spec)
        cast_shape.append(jax.ShapeDtypeStruct(w.shape, BF16))
    for stream in SCAN_STREAMS:
        d, base, per_seq = stream
        seq0 = 0 if base == 0 else BATCH
        n_seq = SCAN_STEPS // per_seq
        loc = lambda j, stream=stream: _stream_local_block(stream, j)
        base += PAD_TOK // STEP_ROWS
        tok = pl.BlockSpec((STEP_ROWS, 3 * WIDTH), lambda j, loc=loc, base=base: (base + loc(j), 0))
        dtok = pl.BlockSpec((1, STEP_ROWS, 3 * WIDTH), lambda j, loc=loc, base=base, d=d: (d, base + loc(j), 0))
        st_in = pl.BlockSpec((1, 1, HEAD_DIM, WIDTH),
                             lambda j, loc=loc, seq0=seq0, per_seq=per_seq, d=d: (seq0 + loc(j) // per_seq, d, 0, 0))
        in_specs += [tok, dtok, st_in]
        operands += [shared, perdir, s0]
        out_specs += [pl.BlockSpec((STEP_ROWS, WIDTH), lambda j, loc=loc: (loc(j), 0)),
                      pl.BlockSpec((1, HEAD_DIM, WIDTH), lambda j, loc=loc, per_seq=per_seq: (loc(j) // per_seq, 0, 0))]
        out_shape += [jax.ShapeDtypeStruct((REGION_CHUNKS * CHUNK, WIDTH), F32),
                      jax.ShapeDtypeStruct((n_seq, HEAD_DIM, WIDTH), F32)]
    return pl.pallas_call(
        functools.partial(_scan_kernel, len(weights)),
        grid=(SCAN_STEPS,),
        in_specs=in_specs + cast_in,
        out_specs=out_specs + cast_out,
        out_shape=out_shape + cast_shape,
        compiler_params=_params(1),
        name="scan",
    )(*operands, *weights)


def _post_kernel(yfc_ref, yfl_ref, ybc_ref, ybl_ref, bonus_lo_ref, bonus_hi_ref, g_lo_ref, g_hi_ref,
                 attc_ref, attl_ref, xp_ref, xs_ref, mod_ref,
                 lng_ref, lnb_ref, ones_ref, wout_ref, n2_ref, x1_ref, h2_ref):
    i = pl.program_id(0)
    bonus = jnp.concatenate([bonus_lo_ref[...], bonus_hi_ref[...]], axis=0)
    gate = jnp.concatenate([g_lo_ref[...], g_hi_ref[...]], axis=0)
    def pick(ctx_ref, lat_ref):
        return _ctx_or_lat(i, ctx_ref, lat_ref, BIG_BLOCK)

    ones = ones_ref[...]
    inv_n = 1.0 / HEAD_DIM
    y = pick(yfc_ref, yfl_ref) + pick(ybc_ref, ybl_ref)
    yc = y - _segsum(y, ones) * inv_n
    var = _segsum(yc * yc, ones) * inv_n
    yn = yc * lax.rsqrt(var + GN_EPS) * lng_ref[...] + lnb_ref[...]
    r_out = ((yn + bonus) * gate).astype(BF16)
    o = _dot(pick(attc_ref, attl_ref), wout_ref[0:WIDTH, :]) + _dot(r_out, wout_ref[WIDTH:, :])
    x1 = pick(xp_ref, xs_ref) + mod_ref[0, 2:3, :] * o
    x1_ref[...] = x1
    h2 = _rmsnorm(x1, n2_ref[...]) * (1.0 + mod_ref[0, 4:5, :]) + mod_ref[0, 3:4, :]
    h2_ref[...] = h2.astype(BF16)


def _post(ys, bonus, g, att_ctx, att_lat, xp, xs, mod, ln_g, ln_b, ones, w_out_bf, norm2):
    blk = BIG_BLOCK
    tok = lambda cols: _tok_spec(cols, blk)
    ctx_lat = [_ctx_tok_spec(WIDTH, blk), _lat_tok_spec(WIDTH, blk)]
    halves = [pl.BlockSpec((PAD_TOK, WIDTH), lambda i, h=h: (i * (blk // PAD_TOK) + 1 + h, 0)) for h in range(2)]
    assert blk == 2 * PAD_TOK
    return pl.pallas_call(
        _post_kernel,
        grid=(N_TOK // blk,),
        in_specs=ctx_lat + ctx_lat + halves + halves
        + ctx_lat + [_ctx_tok_spec(D_MODEL, blk), _lat_tok_spec(D_MODEL, blk),
                  _mod_spec(blk),
                  _const_spec((1, WIDTH)), _const_spec((1, WIDTH)), _const_spec((WIDTH, WIDTH)),
                  _const_spec((2 * WIDTH, D_MODEL)), _const_spec((1, D_MODEL))],
        out_specs=[tok(D_MODEL), tok(D_MODEL)],
        out_shape=[jax.ShapeDtypeStruct((N_TOK, D_MODEL), F32),
                   jax.ShapeDtypeStruct((N_TOK, D_MODEL), BF16)],
        compiler_params=_params(1),
        name="post",
    )(*ys, bonus, bonus, g, g, att_ctx, att_lat, xp, xs, mod, ln_g, ln_b, ones, w_out_bf, norm2)


def _ffn_kernel(x1_ref, h2_ref, hp_ref, hn_ref, mod_ref, w1_ref, w3_ref, wc_ref, w2_ref, nf_ref, yp_ref, ys_ref):
    i = pl.program_id(0)
    blk = BIG_BLOCK
    is_ctx = i < N_CTX_TOK // blk
    has_prev, has_next = _seq_neighbours(i, blk)
    lhs = jnp.concatenate([h2_ref[...], hp_ref[...], hn_ref[...]], axis=0)
    h2 = lhs[:blk]
    rid = lax.broadcasted_iota(jnp.int32, (blk, 1), 0)
    inner = jnp.where(is_ctx, SEQ, -1)
    seq_start = rid == inner
    seq_end = rid == inner - 1
    acc = jnp.zeros((blk, D_MODEL), F32)
    for fs in (slice(0, FF_SPLIT), slice(FF_SPLIT, D_FF)):
        a_all = _dot(lhs, w1_ref[:, fs])
        a = a_all[:blk]
        prow = jnp.where(has_prev, a_all[blk + 15:blk + 16, :], 0.0)
        nrow = jnp.where(has_next, a_all[blk + 16:blk + 17, :], 0.0)
        aprev, anext = _shifted(a, prow, nrow)
        aprev = jnp.where(seq_start, 0.0, aprev)
        anext = jnp.where(seq_end, 0.0, anext)
        cv = aprev * wc_ref[0:1, fs] + a * wc_ref[1:2, fs] + anext * wc_ref[2:3, fs]
        act = _silu(cv) * _dot(h2, w3_ref[:, fs])
        acc = acc + _dot(act.astype(BF16), w2_ref[fs, :])
    x2 = x1_ref[...] + mod_ref[0, 5:6, :] * acc
    y = _rmsnorm(x2, nf_ref[...])

    @pl.when(is_ctx)
    def _():
        yp_ref[...] = y

    @pl.when(jnp.logical_not(is_ctx))
    def _():
        ys_ref[...] = y


def _ffn(x1, h2, mod, w1_bf, w3_bf, wc, w2_bf, norm_f):
    blk = BIG_BLOCK
    rows16 = blk // 16
    return pl.pallas_call(
        _ffn_kernel,
        grid=(N_TOK // blk,),
        in_specs=[_tok_spec(D_MODEL, blk), _tok_spec(D_MODEL, blk),
                  pl.BlockSpec((16, D_MODEL), lambda i: (jnp.maximum(i * rows16 - 1, 0), 0)),
                  pl.BlockSpec((16, D_MODEL), lambda i: (jnp.minimum((i + 1) * rows16, N_TOK // 16 - 1), 0)),
                  _mod_spec(blk),
                  _const_spec((D_MODEL, D_FF)), _const_spec((D_MODEL, D_FF)), _const_spec((3, D_FF)),
                  _const_spec((D_FF, D_MODEL)), _const_spec((1, D_MODEL))],
        out_specs=[_ctx_tok_spec(D_MODEL, blk), _lat_tok_spec(D_MODEL, blk)],
        out_shape=[jax.ShapeDtypeStruct((N_CTX_TOK, D_MODEL), F32),
                   jax.ShapeDtypeStruct((N_LAT_TOK, D_MODEL), F32)],
        compiler_params=_params(1),
        name="ffn",
    )(x1, h2, h2, h2, mod, w1_bf, w3_bf, wc, w2_bf, norm_f)


def kernel(x_prompt, x_sample, cache_k, cache_v, state_rwkv, c, c_ctx, w_ada, b_ada, norm1, norm2,
           w_in, w_ts, w0, w2, a0, a2, g2, k_k, k_a, r_k, ln_x_g, ln_x_b, rpb, w_out,
           w_ffn1, w_ffn3, w_ffn_conv, w_ffn2, norm_f):
    xp = x_prompt.reshape(N_CTX_TOK, D_MODEL)
    xs = x_sample.reshape(N_LAT_TOK, D_MODEL)
    row = lambda t: t.reshape(1, -1)
    cond = jnp.concatenate([c_ctx[None, :], c, jnp.zeros((8 - 1 - DEC_BATCH, D_MODEL), F32)], axis=0)
    mod = _modulation(cond, w_ada[0], b_ada[0]).reshape(8, 6, D_MODEL)

    head_id = jnp.arange(WIDTH) // HEAD_DIM
    ones = (head_id[:, None] == head_id[None, :]).astype(BF16)
    q, k, v, shared, g, bonus, perdir, new_k, new_v = _inproj(
        xp, xs, mod, row(norm1[0]), w_in[0].astype(BF16), w_ts[0], w0[0], w2[0], a0[0], a2[0], g2[0],
        row(k_k[0]), row(k_a[0]), row(r_k[0]), ones)

    att_ctx = _ctx_attention(q, k, v)
    att_lat = _lat_attention(q, k, v, cache_k[:, 0].reshape(DEC_BATCH, PAST_LEN, WIDTH),
                             cache_v[:, 0].reshape(DEC_BATCH, PAST_LEN, WIDTH), _na_bias_table(rpb[0]))
    s_lat = jnp.transpose(state_rwkv[:, 0], (0, 1, 3, 2, 4)).reshape(DEC_BATCH, N_DIRS, HEAD_DIM, WIDTH)
    s0 = jnp.concatenate([jnp.zeros((BATCH, N_DIRS, HEAD_DIM, WIDTH), F32), s_lat], axis=0)
    (y_fc, s_fc, y_fl, _, y_bc, s_bc, y_bl, _,
     w_out_bf, w1_bf, w3_bf, w2_bf) = _scan(shared, perdir, s0, (w_out[0], w_ffn1[0], w_ffn3[0], w_ffn2[0]))
    s_fin = jnp.stack([s_fc, s_bc], axis=1)

    x1, h2 = _post((y_fc, y_fl, y_bc, y_bl), bonus, g, att_ctx, att_lat, xp, xs, mod, row(ln_x_g[0]), row(ln_x_b[0]), ones,
                   w_out_bf, row(norm2[0]))
    yp, ys = _ffn(x1, h2, mod, w1_bf, w3_bf, w_ffn_conv[0], w2_bf, row(norm_f))

    new_s = jnp.transpose(s_fin.reshape(BATCH, N_DIRS, HEAD_DIM, N_HEADS, HEAD_DIM),
                          (0, 1, 3, 2, 4)).reshape(BATCH, 1, N_DIRS, N_HEADS, HEAD_DIM, HEAD_DIM)
    return (yp.reshape(BATCH, SEQ, D_MODEL), ys.reshape(DEC_BATCH, DEC_SEQ, D_MODEL), new_k, new_v, new_s)
```

```python
import functools

import jax
import jax.numpy as jnp
from jax import lax
from jax.experimental import pallas as pl
from jax.experimental.pallas import tpu as pltpu

F32 = jnp.float32
BF16 = jnp.bfloat16
HIGHEST = lax.Precision.HIGHEST

D_MODEL = 1024
BATCH = 16
SEQ = 256
DEC_BATCH = 2
DEC_SEQ = 2048
PAST_LEN = 512
GRID_W = 64
HEAD_DIM = 64
N_HEADS = 8
WIDTH = N_HEADS * HEAD_DIM
PAIR_W = 2 * HEAD_DIM
NA_ROWS = 8
NA_COLS = 16
N_DIRS = 2
LORA = 64
GATE_LORA = 128
D_FF = 2816
EPS = 1e-6
GN_EPS = 64e-5
ATT_SCALE = HEAD_DIM ** -0.5
RWKV_COLS = 3 * WIDTH + N_DIRS * 2 * LORA + GATE_LORA
MASK_VALUE = -1e30

N_CTX_TOK = BATCH * SEQ
N_LAT_TOK = DEC_BATCH * DEC_SEQ
N_TOK = N_CTX_TOK + N_LAT_TOK
TOK_BLOCK = 256
N_BLOCKS = N_TOK // TOK_BLOCK
BIG_BLOCK = 512
PAD_TOK = TOK_BLOCK
N_SEQS = BATCH + DEC_BATCH
CHUNK = 64
N_CHUNKS = N_TOK // CHUNK
CTX_CHUNKS = N_CTX_TOK // CHUNK
CHUNKS_PER_CTX_SEQ = SEQ // CHUNK
CHUNKS_PER_LAT_SEQ = DEC_SEQ // CHUNK
GRID_ROWS = DEC_SEQ // GRID_W
LAT_ROWS = 4
MXU_WIDTH = 256
FF_SPLIT = (D_FF // MXU_WIDTH + 1) // 2 * MXU_WIDTH
GROUP = 4
GROUP_W = GROUP * HEAD_DIM
VMEM_LIMIT = 56 * 1024 * 1024


def _params(n_axes, limit=VMEM_LIMIT):
    return pltpu.CompilerParams(dimension_semantics=("arbitrary",) * n_axes, vmem_limit_bytes=limit)


def _const_spec(shape):
    zeros = (0,) * len(shape)
    return pl.BlockSpec(shape, lambda *_: zeros, pipeline_mode=pl.Buffered(1))


def _tok_spec(cols, blk=TOK_BLOCK):
    return pl.BlockSpec((blk, cols), lambda i: (i, 0))


def _ctx_tok_spec(cols, blk=TOK_BLOCK):
    return pl.BlockSpec((blk, cols), lambda i: (jnp.minimum(i, N_CTX_TOK // blk - 1), 0))


def _lat_tok_spec(cols, blk=TOK_BLOCK):
    return pl.BlockSpec((blk, cols), lambda i: (jnp.maximum(i - N_CTX_TOK // blk, 0), 0))


def _ctx_or_lat(i, ctx_ref, lat_ref, blk=TOK_BLOCK):
    return jnp.where(i < N_CTX_TOK // blk, ctx_ref[...], lat_ref[...])


def _mod_spec(blk=TOK_BLOCK):
    def row(i):
        return jnp.where(i < N_CTX_TOK // blk, 0, 1 + (i - N_CTX_TOK // blk) // (DEC_SEQ // blk))
    return pl.BlockSpec((1, 6, D_MODEL), lambda i: (row(i), 0, 0))


def _seq_neighbours(i, blk=TOK_BLOCK):
    per_seq = DEC_SEQ // blk
    j = (i - N_CTX_TOK // blk) % per_seq
    lat = i >= N_CTX_TOK // blk
    return lat & (j != 0), lat & (j != per_seq - 1)


def _silu(x):
    return x * jax.nn.sigmoid(x)


def _softplus(x):
    return jnp.maximum(x, 0.0) + jnp.log(1.0 + jnp.exp(-jnp.abs(x)))


def _rmsnorm(x, g):
    return x * lax.rsqrt(jnp.mean(x * x, axis=-1, keepdims=True) + EPS) * g


def _dot(a, b, precision=None):
    return jnp.dot(a, b, precision=precision, preferred_element_type=F32)


def _dot_nt(a, b, precision=None):
    return lax.dot_general(a, b, (((1,), (1,)), ((), ())), precision=precision, preferred_element_type=F32)


def _split(x):
    hi = x.astype(BF16)
    return hi, (x - hi.astype(F32)).astype(BF16)


def _segsum(x, ones_bf):
    return _dot(x.astype(BF16), ones_bf)


def _shifted(a, prev_row, next_row):
    t = a.shape[0]
    rid = lax.broadcasted_iota(jnp.int32, (t, 1), 0)
    prev = jnp.where(rid == 0, prev_row, pltpu.roll(a, 1, axis=0))
    nxt = jnp.where(rid == t - 1, next_row, pltpu.roll(a, t - 1, axis=0))
    return prev, nxt


def _mod_kernel(cond_ref, w_ref, b_ref, o_ref):
    o_ref[...] = _dot(_silu(cond_ref[...]).astype(BF16), w_ref[...].astype(BF16)) + b_ref[...]


def _modulation(cond, w_ada, b_ada):
    n = 6 * D_MODEL
    return pl.pallas_call(
        _mod_kernel,
        grid=(6,),
        in_specs=[pl.BlockSpec((8, D_MODEL), lambda j: (0, 0)),
                  pl.BlockSpec((D_MODEL, D_MODEL), lambda j: (0, j)),
                  pl.BlockSpec((1, D_MODEL), lambda j: (0, j))],
        out_specs=pl.BlockSpec((8, D_MODEL), lambda j: (0, j)),
        out_shape=jax.ShapeDtypeStruct((8, n), F32),
        compiler_params=_params(1),
        name="mod",
    )(cond, w_ada, b_ada.reshape(1, n))


def _inproj_kernel(xp_ref, xs_ref, xprev_ref, xnext_ref, mod_ref, n1_ref, w_ref, *rest):
    rwkv_refs, (q_ref, k_ref, v_ref), rwkv_outs, (nk_ref, nv_ref), z_bufs = (
        rest[:10], rest[10:13], rest[13:17], rest[17:19], rest[19:])
    i = pl.program_id(0)
    blk = jnp.minimum(i, N_BLOCKS - 1)

    @pl.when(i == 0)
    def _():
        z_bufs[1][...] = jnp.zeros_like(z_bufs[1])

    def modulated(x):
        h = _rmsnorm(x, n1_ref[...]) * (1.0 + mod_ref[0, 1:2, :]) + mod_ref[0, 0:1, :]
        return h.astype(BF16)

    def step(z_store, z_load):
        h = modulated(_ctx_or_lat(blk, xp_ref, xs_ref))
        edge = modulated(jnp.concatenate([xprev_ref[...], xnext_ref[...]], axis=0))
        lhs = jnp.concatenate([h, edge], axis=0)
        kv = []

        def project(c0, c1):
            z_store[:, c0:c1] = _dot(lhs, w_ref[:, 3 * WIDTH + c0:3 * WIDTH + c1])

        def portion_q():
            project(3 * WIDTH, RWKV_COLS)
            project(WIDTH, 2 * WIDTH)
            q_ref[...] = _dot(h, w_ref[:, 0:WIDTH])

        def portion_k():
            project(0, WIDTH)
            kv.append(_dot(h, w_ref[:, WIDTH:2 * WIDTH]))

        def portion_v():
            project(2 * WIDTH, 3 * WIDTH)
            kv.append(_dot(h, w_ref[:, 2 * WIDTH:3 * WIDTH]))

        has_prev, has_next = _seq_neighbours(i - 1)

        def zcols(c0, c1):
            return (z_load[0:TOK_BLOCK, c0:c1],
                    jnp.where(has_prev, z_load[TOK_BLOCK + 7:TOK_BLOCK + 8, c0:c1], 0.0),
                    jnp.where(has_next, z_load[TOK_BLOCK + 8:TOK_BLOCK + 9, c0:c1], 0.0))

        _rwkv_tokens(zcols, (portion_q, portion_k, portion_v), *rwkv_refs, *rwkv_outs)
        k, v = kv
        k_ref[...] = k
        v_ref[...] = v

        @pl.when(i < N_CTX_TOK // TOK_BLOCK)
        def _():
            nk_ref[0, 0] = k.reshape(SEQ, N_HEADS, HEAD_DIM)
            nv_ref[0, 0] = v.reshape(SEQ, N_HEADS, HEAD_DIM)

    @pl.when(i % 2 == 0)
    def _():
        step(z_bufs[0], z_bufs[1])

    @pl.when(i % 2 == 1)
    def _():
        step(z_bufs[1], z_bufs[0])


def _inproj(xp, xs, mod, norm1, w_in_bf, w_ts, w0, w2, a0, a2, g2, k_k, k_a, r_k, ones):
    tok = jax.ShapeDtypeStruct((N_TOK, WIDTH), F32)
    padded = lambda cols: jax.ShapeDtypeStruct((N_TOK + PAD_TOK, cols), F32)
    lat0 = N_CTX_TOK // TOK_BLOCK
    last = N_BLOCKS - 1
    rows8 = TOK_BLOCK // 8
    n8 = N_LAT_TOK // 8
    clamped = lambda cols: pl.BlockSpec((TOK_BLOCK, cols), lambda i: (jnp.minimum(i, last), 0))
    x_ctx = pl.BlockSpec((TOK_BLOCK, D_MODEL), lambda i: (jnp.minimum(i, lat0 - 1), 0))
    x_lat = pl.BlockSpec((TOK_BLOCK, D_MODEL), lambda i: (jnp.clip(i - lat0, 0, last - lat0), 0))
    prev8 = pl.BlockSpec((8, D_MODEL), lambda i: (jnp.clip((jnp.minimum(i, last) - lat0) * rows8 - 1, 0, n8 - 1), 0))
    next8 = pl.BlockSpec((8, D_MODEL), lambda i: (jnp.clip((jnp.minimum(i, last) - lat0 + 1) * rows8, 0, n8 - 1), 0))
    mod_rows = pl.BlockSpec((1, 6, D_MODEL), lambda i: (
        jnp.where(i < lat0, 0, 1 + (jnp.minimum(i, last) - lat0) // (DEC_SEQ // TOK_BLOCK)), 0, 0))
    cache = pl.BlockSpec((1, 1, SEQ, N_HEADS, HEAD_DIM), lambda i: (jnp.minimum(i, BATCH - 1), 0, 0, 0, 0))
    z_buf = pltpu.VMEM((TOK_BLOCK + 16, RWKV_COLS), F32)
    return pl.pallas_call(
        _inproj_kernel,
        grid=(N_BLOCKS + 1,),
        in_specs=[x_ctx, x_lat, prev8, next8, mod_rows,
                  _const_spec((1, D_MODEL)),
                  _const_spec((D_MODEL, 3 * WIDTH + RWKV_COLS)),
                  _const_spec((3, RWKV_COLS)),
                  _const_spec((N_DIRS, WIDTH)), _const_spec((N_DIRS, LORA, WIDTH)),
                  _const_spec((N_DIRS, WIDTH)), _const_spec((N_DIRS, LORA, WIDTH)),
                  _const_spec((GATE_LORA, WIDTH)),
                  _const_spec((1, WIDTH)), _const_spec((1, WIDTH)), _const_spec((1, WIDTH)),
                  _const_spec((WIDTH, WIDTH))],
        out_specs=[clamped(WIDTH), clamped(WIDTH), clamped(WIDTH),
                   _tok_spec(3 * WIDTH), _tok_spec(WIDTH), _tok_spec(WIDTH),
                   pl.BlockSpec((N_DIRS, TOK_BLOCK, 3 * WIDTH), lambda i: (0, i, 0)),
                   cache, cache],
        out_shape=[tok, tok, tok,
                   padded(3 * WIDTH), padded(WIDTH), padded(WIDTH),
                   jax.ShapeDtypeStruct((N_DIRS, N_TOK + PAD_TOK, 3 * WIDTH), F32),
                   jax.ShapeDtypeStruct((BATCH, 1, SEQ, N_HEADS, HEAD_DIM), F32),
                   jax.ShapeDtypeStruct((BATCH, 1, SEQ, N_HEADS, HEAD_DIM), F32)],
        scratch_shapes=[z_buf, z_buf],
        compiler_params=_params(1),
        name="inproj",
    )(xp, xs, xs, xs, mod, norm1, w_in_bf, w_ts, w0, w2, a0, a2, g2, k_k, k_a, r_k, ones)


def _pair_queries(q):
    lo_half = lax.broadcasted_iota(jnp.int32, q.shape, 1) < HEAD_DIM
    return jnp.concatenate([jnp.where(lo_half, q, 0.0), jnp.where(lo_half, 0.0, q)], axis=0).astype(BF16)


def _pair_outputs(o):
    t = o.shape[0] // 2
    lo_half = lax.broadcasted_iota(jnp.int32, (t, PAIR_W), 1) < HEAD_DIM
    return jnp.where(lo_half, o[:t], o[t:])


def _ctx_attn_kernel(q_ref, k_ref, v_ref, o_ref):
    pairs = [slice(p * PAIR_W, (p + 1) * PAIR_W) for p in range(N_HEADS // 2)]
    scores = [_dot_nt(_pair_queries(q_ref[:, ps]), k_ref[:, ps].astype(BF16)) * ATT_SCALE for ps in pairs]
    probs = []
    for s in scores:
        e = jnp.exp(s - jnp.max(s, axis=-1, keepdims=True))
        probs.append((e.astype(BF16), jnp.sum(e, axis=-1, keepdims=True)))
    for ps, (e, l) in zip(pairs, probs):
        o_ref[:, ps] = _pair_outputs(_dot(e, v_ref[:, ps].astype(BF16)) / l).astype(BF16)


def _ctx_attention(q, k, v):
    spec = pl.BlockSpec((SEQ, WIDTH), lambda b: (b, 0))
    return pl.pallas_call(
        _ctx_attn_kernel,
        grid=(BATCH,),
        in_specs=[spec, spec, spec],
        out_specs=spec,
        out_shape=jax.ShapeDtypeStruct((N_CTX_TOK, WIDTH), BF16),
        compiler_params=_params(1),
        name="ctxattn",
    )(q, k, v)


def _lat_attn_kernel(q_ref, k_ref, v_ref, ck_ref, cv_ref, bias_ref, o_ref, kbf, vbf, ckbf, cvbf):
    i = pl.program_id(1)

    @pl.when(i == 0)
    def _():
        kbf[...] = k_ref[...].astype(BF16)
        vbf[...] = v_ref[...].astype(BF16)
        ckbf[...] = ck_ref[0].astype(BF16)
        cvbf[...] = cv_ref[0].astype(BF16)

    win = NA_ROWS * GRID_W
    rows = []
    for rr in range(LAT_ROWS):
        gr = i * LAT_ROWS + rr
        first_row = jnp.clip(gr - NA_ROWS // 2, 0, GRID_ROWS - NA_ROWS)
        start = pl.multiple_of(first_row * GRID_W, GRID_W)
        off0 = first_row - gr + NA_ROWS - 1
        rows.append((slice(rr * GRID_W, (rr + 1) * GRID_W), start, off0))
    pairs = [slice(p * PAIR_W, (p + 1) * PAIR_W) for p in range(N_HEADS // 2)]
    scores = []
    for p, ps in enumerate(pairs):
        lhs = [_pair_queries(q_ref[rs, ps]) for rs, _, _ in rows]
        s_loc = []
        for (_, start, off0), lhs_r in zip(rows, lhs):
            bias = jnp.concatenate(
                [jnp.concatenate([bias_ref[2 * p + hh, off0 + 2 * j] for j in range(NA_ROWS // 2)], axis=-1)
                 for hh in range(2)], axis=0)
            s_loc.append(_dot_nt(lhs_r, kbf[pl.ds(start, win), ps]) * ATT_SCALE + bias)
        s_ctx = _dot_nt(jnp.concatenate(lhs, axis=0), ckbf[:, ps]) * ATT_SCALE
        scores.append((jnp.concatenate(s_loc, axis=0), s_ctx))
    probs = []
    for s_loc, s_ctx in scores:
        m = jnp.maximum(jnp.max(s_loc, axis=-1, keepdims=True), jnp.max(s_ctx, axis=-1, keepdims=True))
        e_loc = jnp.exp(s_loc - m)
        e_ctx = jnp.exp(s_ctx - m)
        l = jnp.sum(e_loc, axis=-1, keepdims=True) + jnp.sum(e_ctx, axis=-1, keepdims=True)
        probs.append((e_loc.astype(BF16), e_ctx.astype(BF16), l))
    for ps, (e_loc, e_ctx, l) in zip(pairs, probs):
        o_ctx = _dot(e_ctx, cvbf[:, ps])
        for rr, (rs, start, _) in enumerate(rows):
            both = slice(rr * 2 * GRID_W, (rr + 1) * 2 * GRID_W)
            o = _dot(e_loc[both], vbf[pl.ds(start, win), ps]) + o_ctx[both]
            o_ref[rs, ps] = _pair_outputs(o / l[both]).astype(BF16)


def _na_bias_table(rpb):
    n = rpb.shape[-1]
    w = GRID_W
    pair = jnp.concatenate([rpb[:, :-1], rpb[:, 1:]], axis=-1).astype(F32)
    x = jnp.arange(2 * n)
    y = jnp.arange(2 * w)
    cq = jnp.arange(w)
    ck = y % w
    sel = ((x // n)[:, None, None] == (y // w)[None, None, :]) & (
        (x % n)[:, None, None] == ck[None, None, :] - cq[None, :, None] + NA_COLS - 1)
    t = jnp.einsum("hdx,xqy->hdqy", pair, sel.astype(F32), precision=HIGHEST)
    cs = jnp.clip(cq - NA_COLS // 2, 0, w - NA_COLS)[:, None]
    return jnp.where((ck[None, :] >= cs) & (ck[None, :] < cs + NA_COLS), t, MASK_VALUE)


def _lat_attention(q, k, v, cache_k, cache_v, bias):
    blk = LAT_ROWS * GRID_W
    steps = GRID_ROWS // LAT_ROWS
    lat0 = N_CTX_TOK // blk
    seq = pl.BlockSpec((DEC_SEQ, WIDTH), lambda b, i: (N_CTX_TOK // DEC_SEQ + b, 0))
    ctx = pl.BlockSpec((1, PAST_LEN, WIDTH), lambda b, i: (b, 0, 0))
    return pl.pallas_call(
        _lat_attn_kernel,
        grid=(DEC_BATCH, steps),
        in_specs=[pl.BlockSpec((blk, WIDTH), lambda b, i: (lat0 + b * steps + i, 0)),
                  seq, seq, ctx, ctx,
                  _const_spec((N_HEADS, 2 * NA_ROWS - 2, GRID_W, PAIR_W))],
        out_specs=pl.BlockSpec((blk, WIDTH), lambda b, i: (b * steps + i, 0)),
        out_shape=jax.ShapeDtypeStruct((N_LAT_TOK, WIDTH), BF16),
        scratch_shapes=[pltpu.VMEM((DEC_SEQ, WIDTH), BF16), pltpu.VMEM((DEC_SEQ, WIDTH), BF16),
                        pltpu.VMEM((PAST_LEN, WIDTH), BF16), pltpu.VMEM((PAST_LEN, WIDTH), BF16)],
        compiler_params=_params(2),
        name="latattn",
    )(q, k, v, cache_k, cache_v, bias)


def _rwkv_tokens(zcols, fillers, wts_ref, w0_ref, w2_ref, a0_ref, a2_ref, g2_ref, kk_ref, ka_ref, rk_ref, ones_ref,
                 shared_out, g_out, bonus_out, dir_out):
    def conv(c0, c1):
        zc, prow, nrow = zcols(c0, c1)
        zprev, znext = _shifted(zc, prow, nrow)
        return zprev * wts_ref[0:1, c0:c1] + zc * wts_ref[1:2, c0:c1] + znext * wts_ref[2:3, c0:c1]

    fill = iter(fillers)
    dirs = [slice(d * LORA, (d + 1) * LORA) for d in range(N_DIRS)]
    ones = ones_ref[...]
    o = 3 * WIDTH
    lora_in = conv(o, RWKV_COLS)
    kr = conv(WIDTH, 2 * WIDTH)
    tw = jnp.tanh(lora_in[:, 0:N_DIRS * LORA])
    xa = lora_in[:, N_DIRS * LORA:2 * N_DIRS * LORA]
    xg = lora_in[:, 2 * N_DIRS * LORA:]
    kkraw = kr * kk_ref[...]
    next(fill)()
    kk = kkraw * lax.rsqrt(_segsum(kkraw * kkraw, ones) + 1e-12)
    bf = lambda t: t.astype(BF16)
    lora_w = [_dot(bf(tw[:, ls]), bf(w2_ref[d])) for d, ls in enumerate(dirs)]
    r = conv(0, WIDTH)
    next(fill)()
    lora_a = [_dot(bf(xa[:, ls]), bf(a2_ref[d])) for d, ls in enumerate(dirs)]
    g_out[...] = _dot(bf(jax.nn.sigmoid(xg)), bf(g2_ref[...]))
    v = conv(2 * WIDTH, 3 * WIDTH)
    next(fill)()
    kd_sum = jnp.zeros_like(kr)
    for d in range(N_DIRS):
        w_log = -_softplus(-(w0_ref[d:d + 1, :] + lora_w[d])) - 0.5
        a = jax.nn.sigmoid(a0_ref[d:d + 1, :] + lora_a[d])
        kd = kr * (1.0 + (a - 1.0) * ka_ref[...])
        dir_out[d, :, 0:WIDTH] = -jnp.exp(w_log)
        dir_out[d, :, WIDTH:2 * WIDTH] = kk * a
        dir_out[d, :, 2 * WIDTH:] = kd
        kd_sum = kd_sum + kd
    shared_out[:, 0:WIDTH] = r
    shared_out[:, WIDTH:2 * WIDTH] = kk
    shared_out[:, 2 * WIDTH:] = v
    bonus_out[...] = _segsum(r * kd_sum * rk_ref[...], ones) * v


HALF = CHUNK // 2


def _blockdiag(x, block=HEAD_DIM):
    p = x.astype(BF16)
    blk = lax.broadcasted_iota(jnp.int32, p.shape, 1) // block
    return jnp.concatenate([jnp.where(blk == q, p, jnp.zeros_like(p)) for q in range(GROUP_W // block)], axis=0)


def _hp(a, b, block=HEAD_DIM):
    return _dot(a.astype(BF16), _blockdiag(b, block))


def _hp_nt(a, b):
    return _dot_nt(a.astype(BF16), _blockdiag(b))


def _head_transpose(x):
    xt = jnp.transpose(x)
    return jnp.concatenate([xt[h * HEAD_DIM:(h + 1) * HEAD_DIM] for h in range(GROUP)], axis=1)


def _scan_prepare(d, rows, shared_ref, dir_ref, y_ref, s_ref):
    row = lax.broadcasted_iota(jnp.int32, (CHUNK, GROUP_W), 0)
    lane = lax.broadcasted_iota(jnp.int32, (CHUNK, GROUP_W), 1)
    col = lane % CHUNK
    strict = col > row if d else col < row
    incl = col >= row if d else col <= row
    row_h = lax.broadcasted_iota(jnp.int32, (HALF, GROUP_W), 0)
    lane_h = lax.broadcasted_iota(jnp.int32, (HALF, GROUP_W), 1)
    lo_s = lane_h % CHUNK < HALF
    eye_half = (lane_h % HALF == row_h).astype(F32)

    lw = dir_ref[0, rows, 0:WIDTH]
    mask = jnp.where(incl[:, :CHUNK], 1.0, 0.0).astype(BF16)
    lw_hi, lw_mid = _split(lw)
    lw_lo = (lw - lw_hi.astype(F32) - lw_mid.astype(F32)).astype(BF16)
    cum = _dot(mask, lw_hi) + _dot(mask, lw_mid) + _dot(mask, lw_lo)
    tot = jnp.sum(lw, axis=0, keepdims=True)
    p_in = jnp.exp(cum)
    p_neg = jnp.exp(-cum)
    p_rem = jnp.exp(tot - cum)
    kt = shared_ref[rows, WIDTH:2 * WIDTH] * jnp.exp(cum - lw)
    rt = shared_ref[rows, 0:WIDTH] * p_in
    b = dir_ref[0, rows, WIDTH:2 * WIDTH]
    kd = dir_ref[0, rows, 2 * WIDTH:]
    bt = b * p_neg
    kdt = kd * p_neg
    bh = b * p_rem
    kh = kd * p_rem
    p_end = jnp.exp(tot)
    v = shared_ref[rows, 2 * WIDTH:]

    chains = []
    for gi in range(N_HEADS // GROUP):
        sl = slice(gi * GROUP_W, (gi + 1) * GROUP_W)
        chains.append(dict(d=d, rows=rows, sl=sl, strict=strict, incl=incl, lo_s=lo_s, eye_half=eye_half, y_ref=y_ref, s_ref=s_ref,
                           kt=kt[:, sl], rt=rt[:, sl], v=v[:, sl], bt=bt[:, sl], kdt=kdt[:, sl],
                           kh=kh[:, sl], bh=bh[:, sl], p_end=p_end[:, sl]))
    return chains


def _scan_scores(chains):
    for c in chains:
        lhs = jnp.concatenate([c["kt"], c["rt"]], axis=0)
        g_b = _hp_nt(lhs, c["bt"])
        g_k = _hp_nt(lhs, c["kdt"])
        c["a_k"] = jnp.where(c["strict"], g_k[:CHUNK], 0.0)
        c["a_rb"] = jnp.where(c["incl"], g_b[CHUNK:], 0.0)
        c["a_rk"] = jnp.where(c["incl"], g_k[CHUNK:], 0.0)
        n = jnp.where(c["strict"], -g_b[:CHUNK], 0.0)
        c["n"] = n
        c["pw"] = jnp.where(c["lo_s"], n[:HALF], n[HALF:])
        c["tp"] = c["eye_half"] + c["pw"]
    return chains


def _scan_local(chains):
    zeros = jnp.zeros((HALF, GROUP_W), F32)
    for c in chains:
        c["pw"] = _hp(c["pw"], c["pw"], HALF)
    for _ in range(3):
        for c in chains:
            both = _hp(jnp.concatenate([c["tp"], c["pw"]], axis=0), c["pw"], HALF)
            c["tp"] = c["tp"] + both[:HALF]
            c["pw"] = both[HALF:]
    for c in chains:
        c["tp"] = c["tp"] + _hp(c["tp"], c["pw"], HALF)
    for c in chains:
        c["t1"] = jnp.where(c["lo_s"], c["tp"], 0.0)
        c["t2"] = jnp.where(c["lo_s"], 0.0, c["tp"])
        if c["d"] == 0:
            off, inner = jnp.where(c["lo_s"], c["n"][HALF:], 0.0), jnp.concatenate([c["t1"], zeros], axis=0)
        else:
            off, inner = jnp.where(c["lo_s"], 0.0, c["n"][:HALF]), jnp.concatenate([zeros, c["t2"]], axis=0)
        c["off"] = _hp(off, inner)
    for c in chains:
        if c["d"] == 0:
            x = _hp(c["t2"], jnp.concatenate([zeros, c["off"]], axis=0))
            c["inv"] = jnp.concatenate([c["t1"], jnp.where(c["lo_s"], x, c["tp"])], axis=0)
        else:
            x = _hp(c["t1"], jnp.concatenate([c["off"], zeros], axis=0))
            c["inv"] = jnp.concatenate([jnp.where(c["lo_s"], c["tp"], x), c["t2"]], axis=0)
    for c in chains:
        c["av"] = _hp(jnp.concatenate([c["a_k"], c["a_rk"]], axis=0), c["v"])
        c["vt"] = _head_transpose(c["v"])
    return chains


def _scan_carry(chains):
    for c in chains:
        c["s"] = c["s_ref"][:, c["sl"]]
        c["z"] = _hp_nt(jnp.concatenate([c["rt"], c["kt"]], axis=0), c["s"])
    for c in chains:
        c["u"] = _hp(c["inv"], c["z"][CHUNK:] + c["av"][:CHUNK])
    for c in chains:
        c["y_ref"][c["rows"], c["sl"]] = c["z"][:CHUNK] + c["av"][CHUNK:] - _hp(c["a_rb"], c["u"])
    for c in chains:
        upd = _hp(c["vt"], c["kh"]) - _hp(_head_transpose(c["u"]), c["bh"])
        c["s_ref"][:, c["sl"]] = c["s"] * c["p_end"] + upd


REGION_CHUNKS = CTX_CHUNKS
STEP_CHUNKS = 2
STEP_ROWS = STEP_CHUNKS * CHUNK
SCAN_STEPS = REGION_CHUNKS // STEP_CHUNKS
SCAN_STREAMS = tuple((d, base // STEP_CHUNKS, per_seq // STEP_CHUNKS) for d in range(N_DIRS)
                     for base, per_seq in ((0, CHUNKS_PER_CTX_SEQ), (CTX_CHUNKS, CHUNKS_PER_LAT_SEQ)))
STREAM_INS = 3


def _stream_local_block(stream, j):
    return SCAN_STEPS - 1 - j if stream[0] else j


def _scan_kernel(n_cast, *refs):
    n_streams = len(SCAN_STREAMS)
    n_stream_in = STREAM_INS * n_streams
    n_in = n_stream_in + n_cast
    n_out = 2 * n_streams + n_cast
    carries = refs[n_in + n_out:]
    j = pl.program_id(0)
    for w_ref, o_ref in zip(refs[n_stream_in:n_in], refs[n_in + 2 * n_streams:n_in + n_out]):
        o_ref[...] = w_ref[...].astype(BF16)
    per_stream = []
    for s, stream in enumerate(SCAN_STREAMS):
        d, _, per_seq = stream
        ins = refs[STREAM_INS * s:STREAM_INS * (s + 1)]
        y_ref = refs[n_in + 2 * s]
        s0_ref = ins[-1]
        local = _stream_local_block(stream, j) % per_seq

        @pl.when(local == (per_seq - 1 if d else 0))
        def _():
            carries[s][...] = s0_ref[0, 0]

        per_stream.append((d,) + ins[:-1] + (y_ref, carries[s]))

    chains_by_chunk = []
    for k in range(STEP_CHUNKS):
        chains = []
        for d, shared_ref, dir_ref, y_ref, s_ref in per_stream:
            first = (STEP_CHUNKS - 1 - k if d else k) * CHUNK
            chains += _scan_prepare(d, slice(first, first + CHUNK), shared_ref, dir_ref, y_ref, s_ref)
        chains_by_chunk.append(_scan_local(_scan_scores(chains)))
    for chains in chains_by_chunk:
        _scan_carry(chains)

    for s, stream in enumerate(SCAN_STREAMS):
        d, _, per_seq = stream
        out_ref = refs[n_in + 2 * s + 1]
        local = _stream_local_block(stream, j) % per_seq

        @pl.when(local == (0 if d else per_seq - 1))
        def _():
            for h in range(N_HEADS):
                out_ref[0, h] = carries[s][:, h * HEAD_DIM:(h + 1) * HEAD_DIM]


def _scan(shared, perdir, s0, weights):
    in_specs, out_specs, out_shape, operands = [], [], [], []
    cast_in, cast_out, cast_shape = [], [], []
    for w in weights:
        rows = w.shape[0] // SCAN_STEPS
        if rows % 16 or rows * SCAN_STEPS != w.shape[0]:
            rows = 128
        assert w.shape[0] % rows == 0 and w.shape[0] // rows <= SCAN_STEPS
        spec = pl.BlockSpec((rows, w.shape[1]), lambda j, n=w.shape[0] // rows: (jnp.minimum(j, n - 1), 0))
        cast_in.append(spec)
        cast_out.append(spec)
        cast_shape.append(jax.ShapeDtypeStruct(w.shape, BF16))
    for stream in SCAN_STREAMS:
        d, base, per_seq = stream
        seq0 = 0 if base == 0 else BATCH
        n_seq = SCAN_STEPS // per_seq
        loc = lambda j, stream=stream: _stream_local_block(stream, j)
        base += PAD_TOK // STEP_ROWS
        tok = pl.BlockSpec((STEP_ROWS, 3 * WIDTH), lambda j, loc=loc, base=base: (base + loc(j), 0))
        dtok = pl.BlockSpec((1, STEP_ROWS, 3 * WIDTH), lambda j, loc=loc, base=base, d=d: (d, base + loc(j), 0))
        st_in = pl.BlockSpec((1, 1, HEAD_DIM, WIDTH),
                             lambda j, loc=loc, seq0=seq0, per_seq=per_seq, d=d: (seq0 + loc(j) // per_seq, d, 0, 0))
        in_specs += [tok, dtok, st_in]
        operands += [shared, perdir, s0]
        out_specs += [pl.BlockSpec((STEP_ROWS, WIDTH), lambda j, loc=loc: (loc(j), 0)),
                      pl.BlockSpec((1, N_HEADS, HEAD_DIM, HEAD_DIM),
                                   lambda j, loc=loc, per_seq=per_seq: (loc(j) // per_seq, 0, 0, 0))]
        out_shape += [jax.ShapeDtypeStruct((REGION_CHUNKS * CHUNK, WIDTH), F32),
                      jax.ShapeDtypeStruct((n_seq, N_HEADS, HEAD_DIM, HEAD_DIM), F32)]
    return pl.pallas_call(
        functools.partial(_scan_kernel, len(weights)),
        grid=(SCAN_STEPS,),
        in_specs=in_specs + cast_in,
        out_specs=out_specs + cast_out,
        out_shape=out_shape + cast_shape,
        scratch_shapes=[pltpu.VMEM((HEAD_DIM, WIDTH), F32)] * len(SCAN_STREAMS),
        compiler_params=_params(1),
        name="scan",
    )(*operands, *weights)


def _post_kernel(yfc_ref, yfl_ref, ybc_ref, ybl_ref, bonus_lo_ref, bonus_hi_ref, g_lo_ref, g_hi_ref,
                 attc_ref, attl_ref, xp_ref, xs_ref, mod_ref,
                 lng_ref, lnb_ref, ones_ref, wout_ref, n2_ref, x1_ref, h2_ref):
    i = pl.program_id(0)
    bonus = jnp.concatenate([bonus_lo_ref[...], bonus_hi_ref[...]], axis=0)
    gate = jnp.concatenate([g_lo_ref[...], g_hi_ref[...]], axis=0)
    def pick(ctx_ref, lat_ref):
        return _ctx_or_lat(i, ctx_ref, lat_ref, BIG_BLOCK)

    ones = ones_ref[...]
    inv_n = 1.0 / HEAD_DIM
    y = pick(yfc_ref, yfl_ref) + pick(ybc_ref, ybl_ref)
    yc = y - _segsum(y, ones) * inv_n
    var = _segsum(yc * yc, ones) * inv_n
    yn = yc * lax.rsqrt(var + GN_EPS) * lng_ref[...] + lnb_ref[...]
    r_out = ((yn + bonus) * gate).astype(BF16)
    o = _dot(pick(attc_ref, attl_ref), wout_ref[0:WIDTH, :]) + _dot(r_out, wout_ref[WIDTH:, :])
    x1 = pick(xp_ref, xs_ref) + mod_ref[0, 2:3, :] * o
    x1_ref[...] = x1
    h2 = _rmsnorm(x1, n2_ref[...]) * (1.0 + mod_ref[0, 4:5, :]) + mod_ref[0, 3:4, :]
    h2_ref[...] = h2.astype(BF16)


def _post(ys, bonus, g, att_ctx, att_lat, xp, xs, mod, ln_g, ln_b, ones, w_out_bf, norm2):
    blk = BIG_BLOCK
    tok = lambda cols: _tok_spec(cols, blk)
    ctx_lat = [_ctx_tok_spec(WIDTH, blk), _lat_tok_spec(WIDTH, blk)]
    halves = [pl.BlockSpec((PAD_TOK, WIDTH), lambda i, h=h: (i * (blk // PAD_TOK) + 1 + h, 0)) for h in range(2)]
    assert blk == 2 * PAD_TOK
    return pl.pallas_call(
        _post_kernel,
        grid=(N_TOK // blk,),
        in_specs=ctx_lat + ctx_lat + halves + halves
        + ctx_lat + [_ctx_tok_spec(D_MODEL, blk), _lat_tok_spec(D_MODEL, blk),
                  _mod_spec(blk),
                  _const_spec((1, WIDTH)), _const_spec((1, WIDTH)), _const_spec((WIDTH, WIDTH)),
                  _const_spec((2 * WIDTH, D_MODEL)), _const_spec((1, D_MODEL))],
        out_specs=[tok(D_MODEL), tok(D_MODEL)],
        out_shape=[jax.ShapeDtypeStruct((N_TOK, D_MODEL), F32),
                   jax.ShapeDtypeStruct((N_TOK, D_MODEL), BF16)],
        compiler_params=_params(1),
        name="post",
    )(*ys, bonus, bonus, g, g, att_ctx, att_lat, xp, xs, mod, ln_g, ln_b, ones, w_out_bf, norm2)


def _ffn_kernel(x1_ref, h2_ref, hp_ref, hn_ref, mod_ref, w1_ref, w3_ref, wc_ref, w2_ref, nf_ref, yp_ref, ys_ref):
    i = pl.program_id(0)
    blk = BIG_BLOCK
    is_ctx = i < N_CTX_TOK // blk
    has_prev, has_next = _seq_neighbours(i, blk)
    lhs = jnp.concatenate([h2_ref[...], hp_ref[...], hn_ref[...]], axis=0)
    h2 = lhs[:blk]
    rid = lax.broadcasted_iota(jnp.int32, (blk, 1), 0)
    inner = jnp.where(is_ctx, SEQ, -1)
    seq_start = rid == inner
    seq_end = rid == inner - 1
    acc = jnp.zeros((blk, D_MODEL), F32)
    for fs in (slice(0, FF_SPLIT), slice(FF_SPLIT, D_FF)):
        a_all = _dot(lhs, w1_ref[:, fs])
        a = a_all[:blk]
        prow = jnp.where(has_prev, a_all[blk + 15:blk + 16, :], 0.0)
        nrow = jnp.where(has_next, a_all[blk + 16:blk + 17, :], 0.0)
        aprev, anext = _shifted(a, prow, nrow)
        aprev = jnp.where(seq_start, 0.0, aprev)
        anext = jnp.where(seq_end, 0.0, anext)
        cv = aprev * wc_ref[0:1, fs] + a * wc_ref[1:2, fs] + anext * wc_ref[2:3, fs]
        act = _silu(cv) * _dot(h2, w3_ref[:, fs])
        acc = acc + _dot(act.astype(BF16), w2_ref[fs, :])
    x2 = x1_ref[...] + mod_ref[0, 5:6, :] * acc
    y = _rmsnorm(x2, nf_ref[...])

    @pl.when(is_ctx)
    def _():
        yp_ref[...] = y

    @pl.when(jnp.logical_not(is_ctx))
    def _():
        ys_ref[...] = y


def _ffn(x1, h2, mod, w1_bf, w3_bf, wc, w2_bf, norm_f):
    blk = BIG_BLOCK
    rows16 = blk // 16
    return pl.pallas_call(
        _ffn_kernel,
        grid=(N_TOK // blk,),
        in_specs=[_tok_spec(D_MODEL, blk), _tok_spec(D_MODEL, blk),
                  pl.BlockSpec((16, D_MODEL), lambda i: (jnp.maximum(i * rows16 - 1, 0), 0)),
                  pl.BlockSpec((16, D_MODEL), lambda i: (jnp.minimum((i + 1) * rows16, N_TOK // 16 - 1), 0)),
                  _mod_spec(blk),
                  _const_spec((D_MODEL, D_FF)), _const_spec((D_MODEL, D_FF)), _const_spec((3, D_FF)),
                  _const_spec((D_FF, D_MODEL)), _const_spec((1, D_MODEL))],
        out_specs=[_ctx_tok_spec(D_MODEL, blk), _lat_tok_spec(D_MODEL, blk)],
        out_shape=[jax.ShapeDtypeStruct((N_CTX_TOK, D_MODEL), F32),
                   jax.ShapeDtypeStruct((N_LAT_TOK, D_MODEL), F32)],
        compiler_params=_params(1),
        name="ffn",
    )(x1, h2, h2, h2, mod, w1_bf, w3_bf, wc, w2_bf, norm_f)


def kernel(x_prompt, x_sample, cache_k, cache_v, state_rwkv, c, c_ctx, w_ada, b_ada, norm1, norm2,
           w_in, w_ts, w0, w2, a0, a2, g2, k_k, k_a, r_k, ln_x_g, ln_x_b, rpb, w_out,
           w_ffn1, w_ffn3, w_ffn_conv, w_ffn2, norm_f):
    xp = x_prompt.reshape(N_CTX_TOK, D_MODEL)
    xs = x_sample.reshape(N_LAT_TOK, D_MODEL)
    row = lambda t: t.reshape(1, -1)
    cond = jnp.concatenate([c_ctx[None, :], c, jnp.zeros((8 - 1 - DEC_BATCH, D_MODEL), F32)], axis=0)
    mod = _modulation(cond, w_ada[0], b_ada[0]).reshape(8, 6, D_MODEL)

    head_id = jnp.arange(WIDTH) // HEAD_DIM
    ones = (head_id[:, None] == head_id[None, :]).astype(BF16)
    q, k, v, shared, g, bonus, perdir, new_k, new_v = _inproj(
        xp, xs, mod, row(norm1[0]), w_in[0].astype(BF16), w_ts[0], w0[0], w2[0], a0[0], a2[0], g2[0],
        row(k_k[0]), row(k_a[0]), row(r_k[0]), ones)

    att_ctx = _ctx_attention(q, k, v)
    att_lat = _lat_attention(q, k, v, cache_k[:, 0].reshape(DEC_BATCH, PAST_LEN, WIDTH),
                             cache_v[:, 0].reshape(DEC_BATCH, PAST_LEN, WIDTH), _na_bias_table(rpb[0]))
    s_lat = jnp.transpose(state_rwkv[:, 0], (0, 1, 3, 2, 4)).reshape(DEC_BATCH, N_DIRS, HEAD_DIM, WIDTH)
    s0 = jnp.concatenate([jnp.zeros((BATCH, N_DIRS, HEAD_DIM, WIDTH), F32), s_lat], axis=0)
    (y_fc, s_fc, y_fl, _, y_bc, s_bc, y_bl, _,
     w_out_bf, w1_bf, w3_bf, w2_bf) = _scan(shared, perdir, s0, (w_out[0], w_ffn1[0], w_ffn3[0], w_ffn2[0]))
    new_s = jnp.stack([s_fc, s_bc], axis=1)[:, None]

    x1, h2 = _post((y_fc, y_fl, y_bc, y_bl), bonus, g, att_ctx, att_lat, xp, xs, mod, row(ln_x_g[0]), row(ln_x_b[0]), ones,
                   w_out_bf, row(norm2[0]))
    yp, ys = _ffn(x1, h2, mod, w1_bf, w3_bf, w_ffn_conv[0], w2_bf, row(norm_f))

    return (yp.reshape(BATCH, SEQ, D_MODEL), ys.reshape(DEC_BATCH, DEC_SEQ, D_MODEL), new_k, new_v, new_s)
```

```python
import functools

import jax
import jax.numpy as jnp
from jax import lax
from jax.experimental import pallas as pl
from jax.experimental.pallas import tpu as pltpu

F32 = jnp.float32
BF16 = jnp.bfloat16
HIGHEST = lax.Precision.HIGHEST

D_MODEL = 1024
BATCH = 16
SEQ = 256
DEC_BATCH = 2
DEC_SEQ = 2048
PAST_LEN = 512
GRID_W = 64
HEAD_DIM = 64
N_HEADS = 8
WIDTH = N_HEADS * HEAD_DIM
PAIR_W = 2 * HEAD_DIM
NA_ROWS = 8
NA_COLS = 16
N_DIRS = 2
LORA = 64
GATE_LORA = 128
D_FF = 2816
EPS = 1e-6
GN_EPS = 64e-5
ATT_SCALE = HEAD_DIM ** -0.5
RWKV_COLS = 3 * WIDTH + N_DIRS * 2 * LORA + GATE_LORA
MASK_VALUE = -1e30

N_CTX_TOK = BATCH * SEQ
N_LAT_TOK = DEC_BATCH * DEC_SEQ
N_TOK = N_CTX_TOK + N_LAT_TOK
TOK_BLOCK = 256
N_BLOCKS = N_TOK // TOK_BLOCK
BIG_BLOCK = 512
PAD_TOK = TOK_BLOCK
N_SEQS = BATCH + DEC_BATCH
CHUNK = 64
N_CHUNKS = N_TOK // CHUNK
CTX_CHUNKS = N_CTX_TOK // CHUNK
CHUNKS_PER_CTX_SEQ = SEQ // CHUNK
CHUNKS_PER_LAT_SEQ = DEC_SEQ // CHUNK
GRID_ROWS = DEC_SEQ // GRID_W
LAT_ROWS = 4
MXU_WIDTH = 256
FF_SPLIT = (D_FF // MXU_WIDTH + 1) // 2 * MXU_WIDTH
GROUP = 4
GROUP_W = GROUP * HEAD_DIM
VMEM_LIMIT = 56 * 1024 * 1024


def _params(n_axes, limit=VMEM_LIMIT):
    return pltpu.CompilerParams(dimension_semantics=("arbitrary",) * n_axes, vmem_limit_bytes=limit)


def _const_spec(shape):
    zeros = (0,) * len(shape)
    return pl.BlockSpec(shape, lambda *_: zeros, pipeline_mode=pl.Buffered(1))


def _tok_spec(cols, blk=TOK_BLOCK):
    return pl.BlockSpec((blk, cols), lambda i: (i, 0))


def _ctx_tok_spec(cols, blk=TOK_BLOCK):
    return pl.BlockSpec((blk, cols), lambda i: (jnp.minimum(i, N_CTX_TOK // blk - 1), 0))


def _lat_tok_spec(cols, blk=TOK_BLOCK):
    return pl.BlockSpec((blk, cols), lambda i: (jnp.maximum(i - N_CTX_TOK // blk, 0), 0))


def _ctx_or_lat(i, ctx_ref, lat_ref, blk=TOK_BLOCK):
    return jnp.where(i < N_CTX_TOK // blk, ctx_ref[...], lat_ref[...])


def _mod_spec(blk=TOK_BLOCK):
    def row(i):
        return jnp.where(i < N_CTX_TOK // blk, 0, 1 + (i - N_CTX_TOK // blk) // (DEC_SEQ // blk))
    return pl.BlockSpec((1, 6, D_MODEL), lambda i: (row(i), 0, 0))


def _seq_neighbours(i, blk=TOK_BLOCK):
    per_seq = DEC_SEQ // blk
    j = (i - N_CTX_TOK // blk) % per_seq
    lat = i >= N_CTX_TOK // blk
    return lat & (j != 0), lat & (j != per_seq - 1)


def _silu(x):
    return x * jax.nn.sigmoid(x)


def _softplus(x):
    return jnp.maximum(x, 0.0) + jnp.log(1.0 + jnp.exp(-jnp.abs(x)))


def _rmsnorm(x, g):
    return x * lax.rsqrt(jnp.mean(x * x, axis=-1, keepdims=True) + EPS) * g


def _dot(a, b, precision=None):
    return jnp.dot(a, b, precision=precision, preferred_element_type=F32)


def _dot_nt(a, b, precision=None):
    return lax.dot_general(a, b, (((1,), (1,)), ((), ())), precision=precision, preferred_element_type=F32)


def _split(x):
    hi = x.astype(BF16)
    return hi, (x - hi.astype(F32)).astype(BF16)


def _segsum(x, ones_bf):
    return _dot(x.astype(BF16), ones_bf)


def _shifted(a, prev_row, next_row):
    t = a.shape[0]
    rid = lax.broadcasted_iota(jnp.int32, (t, 1), 0)
    prev = jnp.where(rid == 0, prev_row, pltpu.roll(a, 1, axis=0))
    nxt = jnp.where(rid == t - 1, next_row, pltpu.roll(a, t - 1, axis=0))
    return prev, nxt


MOD_STEPS = 8


def _mod_kernel(cond_ref, w_ref, b_ref, win_ref, o_ref, win_bf_ref):
    o_ref[...] = _dot(_silu(cond_ref[...]).astype(BF16), w_ref[...].astype(BF16)) + b_ref[...]
    win_bf_ref[...] = win_ref[...].astype(BF16)


def _modulation(cond, w_ada, b_ada, w_in):
    n = 6 * D_MODEL
    cols = n // MOD_STEPS
    rows = D_MODEL // MOD_STEPS
    w_in_spec = pl.BlockSpec((rows, w_in.shape[1]), lambda j: (j, 0))
    return pl.pallas_call(
        _mod_kernel,
        grid=(MOD_STEPS,),
        in_specs=[pl.BlockSpec((8, D_MODEL), lambda j: (0, 0)),
                  pl.BlockSpec((D_MODEL, cols), lambda j: (0, j)),
                  pl.BlockSpec((1, cols), lambda j: (0, j)),
                  w_in_spec],
        out_specs=[pl.BlockSpec((8, cols), lambda j: (0, j)), w_in_spec],
        out_shape=[jax.ShapeDtypeStruct((8, n), F32), jax.ShapeDtypeStruct(w_in.shape, BF16)],
        compiler_params=_params(1),
        name="mod",
    )(cond, w_ada, b_ada.reshape(1, n), w_in)


def _inproj_kernel(xp_ref, xs_ref, xprev_ref, xnext_ref, mod_ref, n1_ref, w_ref, *rest):
    rwkv_refs, (q_ref, k_ref, v_ref), rwkv_outs, (nk_ref, nv_ref), z_bufs = (
        rest[:10], rest[10:13], rest[13:17], rest[17:19], rest[19:])
    i = pl.program_id(0)
    blk = jnp.minimum(i, N_BLOCKS - 1)

    @pl.when(i == 0)
    def _():
        z_bufs[1][...] = jnp.zeros_like(z_bufs[1])

    def modulated(x):
        h = _rmsnorm(x, n1_ref[...]) * (1.0 + mod_ref[0, 1:2, :]) + mod_ref[0, 0:1, :]
        return h.astype(BF16)

    def step(z_store, z_load):
        h = modulated(_ctx_or_lat(blk, xp_ref, xs_ref))
        edge = modulated(jnp.concatenate([xprev_ref[...], xnext_ref[...]], axis=0))
        lhs = jnp.concatenate([h, edge], axis=0)
        kv = []

        def project(c0, c1):
            z_store[:, c0:c1] = _dot(lhs, w_ref[:, 3 * WIDTH + c0:3 * WIDTH + c1])

        def portion_q():
            project(3 * WIDTH, RWKV_COLS)
            project(WIDTH, 2 * WIDTH)
            q_ref[...] = _dot(h, w_ref[:, 0:WIDTH])

        def portion_k():
            project(0, WIDTH)
            kv.append(_dot(h, w_ref[:, WIDTH:2 * WIDTH]))

        def portion_v():
            project(2 * WIDTH, 3 * WIDTH)
            kv.append(_dot(h, w_ref[:, 2 * WIDTH:3 * WIDTH]))

        has_prev, has_next = _seq_neighbours(i - 1)

        def zcols(c0, c1):
            return (z_load[0:TOK_BLOCK, c0:c1],
                    jnp.where(has_prev, z_load[TOK_BLOCK + 7:TOK_BLOCK + 8, c0:c1], 0.0),
                    jnp.where(has_next, z_load[TOK_BLOCK + 8:TOK_BLOCK + 9, c0:c1], 0.0))

        _rwkv_tokens(zcols, (portion_q, portion_k, portion_v), *rwkv_refs, *rwkv_outs)
        k, v = kv
        k_ref[...] = k
        v_ref[...] = v

        @pl.when(i < N_CTX_TOK // TOK_BLOCK)
        def _():
            nk_ref[0, 0] = k.reshape(SEQ, N_HEADS, HEAD_DIM)
            nv_ref[0, 0] = v.reshape(SEQ, N_HEADS, HEAD_DIM)

    @pl.when(i % 2 == 0)
    def _():
        step(z_bufs[0], z_bufs[1])

    @pl.when(i % 2 == 1)
    def _():
        step(z_bufs[1], z_bufs[0])


def _inproj(xp, xs, mod, norm1, w_in_bf, w_ts, w0, w2, a0, a2, g2, k_k, k_a, r_k, ones):
    tok = jax.ShapeDtypeStruct((N_TOK, WIDTH), F32)
    padded = lambda cols: jax.ShapeDtypeStruct((N_TOK + PAD_TOK, cols), F32)
    lat0 = N_CTX_TOK // TOK_BLOCK
    last = N_BLOCKS - 1
    rows8 = TOK_BLOCK // 8
    n8 = N_LAT_TOK // 8
    clamped = lambda cols: pl.BlockSpec((TOK_BLOCK, cols), lambda i: (jnp.minimum(i, last), 0))
    x_ctx = pl.BlockSpec((TOK_BLOCK, D_MODEL), lambda i: (jnp.minimum(i, lat0 - 1), 0))
    x_lat = pl.BlockSpec((TOK_BLOCK, D_MODEL), lambda i: (jnp.clip(i - lat0, 0, last - lat0), 0))
    prev8 = pl.BlockSpec((8, D_MODEL), lambda i: (jnp.clip((jnp.minimum(i, last) - lat0) * rows8 - 1, 0, n8 - 1), 0))
    next8 = pl.BlockSpec((8, D_MODEL), lambda i: (jnp.clip((jnp.minimum(i, last) - lat0 + 1) * rows8, 0, n8 - 1), 0))
    mod_rows = pl.BlockSpec((1, 6, D_MODEL), lambda i: (
        jnp.where(i < lat0, 0, 1 + (jnp.minimum(i, last) - lat0) // (DEC_SEQ // TOK_BLOCK)), 0, 0))
    cache = pl.BlockSpec((1, 1, SEQ, N_HEADS, HEAD_DIM), lambda i: (jnp.minimum(i, BATCH - 1), 0, 0, 0, 0))
    z_buf = pltpu.VMEM((TOK_BLOCK + 16, RWKV_COLS), F32)
    return pl.pallas_call(
        _inproj_kernel,
        grid=(N_BLOCKS + 1,),
        in_specs=[x_ctx, x_lat, prev8, next8, mod_rows,
                  _const_spec((1, D_MODEL)),
                  _const_spec((D_MODEL, 3 * WIDTH + RWKV_COLS)),
                  _const_spec((3, RWKV_COLS)),
                  _const_spec((N_DIRS, WIDTH)), _const_spec((N_DIRS, LORA, WIDTH)),
                  _const_spec((N_DIRS, WIDTH)), _const_spec((N_DIRS, LORA, WIDTH)),
                  _const_spec((GATE_LORA, WIDTH)),
                  _const_spec((1, WIDTH)), _const_spec((1, WIDTH)), _const_spec((1, WIDTH)),
                  _const_spec((WIDTH, WIDTH))],
        out_specs=[clamped(WIDTH), clamped(WIDTH), clamped(WIDTH),
                   _tok_spec(3 * WIDTH), _tok_spec(WIDTH), _tok_spec(WIDTH),
                   pl.BlockSpec((N_DIRS, TOK_BLOCK, 3 * WIDTH), lambda i: (0, i, 0)),
                   cache, cache],
        out_shape=[tok, tok, tok,
                   padded(3 * WIDTH), padded(WIDTH), padded(WIDTH),
                   jax.ShapeDtypeStruct((N_DIRS, N_TOK + PAD_TOK, 3 * WIDTH), F32),
                   jax.ShapeDtypeStruct((BATCH, 1, SEQ, N_HEADS, HEAD_DIM), F32),
                   jax.ShapeDtypeStruct((BATCH, 1, SEQ, N_HEADS, HEAD_DIM), F32)],
        scratch_shapes=[z_buf, z_buf],
        compiler_params=_params(1),
        name="inproj",
    )(xp, xs, xs, xs, mod, norm1, w_in_bf, w_ts, w0, w2, a0, a2, g2, k_k, k_a, r_k, ones)


def _pair_queries(q):
    lo_half = lax.broadcasted_iota(jnp.int32, q.shape, 1) < HEAD_DIM
    return jnp.concatenate([jnp.where(lo_half, q, 0.0), jnp.where(lo_half, 0.0, q)], axis=0).astype(BF16)


def _pair_outputs(o):
    t = o.shape[0] // 2
    lo_half = lax.broadcasted_iota(jnp.int32, (t, PAIR_W), 1) < HEAD_DIM
    return jnp.where(lo_half, o[:t], o[t:])


CTX_SEQS_PER_STEP = 2


def _ctx_attn_kernel(q_ref, k_ref, v_ref, o_ref):
    units = [(slice(s * SEQ, (s + 1) * SEQ), slice(p * PAIR_W, (p + 1) * PAIR_W))
             for s in range(CTX_SEQS_PER_STEP) for p in range(N_HEADS // 2)]
    scores = [_dot_nt(_pair_queries(q_ref[rs, ps]), k_ref[rs, ps].astype(BF16)) * ATT_SCALE for rs, ps in units]
    probs = []
    for s in scores:
        e = jnp.exp(s - jnp.max(s, axis=-1, keepdims=True))
        probs.append((e.astype(BF16), jnp.sum(e, axis=-1, keepdims=True)))
    for (rs, ps), (e, l) in zip(units, probs):
        o_ref[rs, ps] = _pair_outputs(_dot(e, v_ref[rs, ps].astype(BF16)) / l).astype(BF16)


def _ctx_attention(q, k, v):
    spec = pl.BlockSpec((CTX_SEQS_PER_STEP * SEQ, WIDTH), lambda b: (b, 0))
    return pl.pallas_call(
        _ctx_attn_kernel,
        grid=(BATCH // CTX_SEQS_PER_STEP,),
        in_specs=[spec, spec, spec],
        out_specs=spec,
        out_shape=jax.ShapeDtypeStruct((N_CTX_TOK, WIDTH), BF16),
        compiler_params=_params(1),
        name="ctxattn",
    )(q, k, v)


def _lat_attn_kernel(q_ref, k_ref, v_ref, ck_ref, cv_ref, bias_ref, o_ref, kbf, vbf, ckbf, cvbf):
    i = pl.program_id(1)

    @pl.when(i == 0)
    def _():
        kbf[...] = k_ref[...].astype(BF16)
        vbf[...] = v_ref[...].astype(BF16)
        ckbf[...] = ck_ref[0].astype(BF16)
        cvbf[...] = cv_ref[0].astype(BF16)

    win = NA_ROWS * GRID_W
    rows = []
    for rr in range(LAT_ROWS):
        gr = i * LAT_ROWS + rr
        first_row = jnp.clip(gr - NA_ROWS // 2, 0, GRID_ROWS - NA_ROWS)
        start = pl.multiple_of(first_row * GRID_W, GRID_W)
        off0 = first_row - gr + NA_ROWS - 1
        rows.append((slice(rr * GRID_W, (rr + 1) * GRID_W), start, off0))
    pairs = [slice(p * PAIR_W, (p + 1) * PAIR_W) for p in range(N_HEADS // 2)]
    scores = []
    for p, ps in enumerate(pairs):
        lhs = [_pair_queries(q_ref[rs, ps]) for rs, _, _ in rows]
        s_loc = []
        for (_, start, off0), lhs_r in zip(rows, lhs):
            bias = jnp.concatenate(
                [jnp.concatenate([bias_ref[2 * p + hh, off0 + 2 * j] for j in range(NA_ROWS // 2)], axis=-1)
                 for hh in range(2)], axis=0)
            s_loc.append(_dot_nt(lhs_r, kbf[pl.ds(start, win), ps]) * ATT_SCALE + bias)
        s_ctx = _dot_nt(jnp.concatenate(lhs, axis=0), ckbf[:, ps]) * ATT_SCALE
        scores.append((jnp.concatenate(s_loc, axis=0), s_ctx))
    probs = []
    for s_loc, s_ctx in scores:
        m = jnp.maximum(jnp.max(s_loc, axis=-1, keepdims=True), jnp.max(s_ctx, axis=-1, keepdims=True))
        e_loc = jnp.exp(s_loc - m)
        e_ctx = jnp.exp(s_ctx - m)
        l = jnp.sum(e_loc, axis=-1, keepdims=True) + jnp.sum(e_ctx, axis=-1, keepdims=True)
        probs.append((e_loc.astype(BF16), e_ctx.astype(BF16), l))
    for ps, (e_loc, e_ctx, l) in zip(pairs, probs):
        o_ctx = _dot(e_ctx, cvbf[:, ps])
        for rr, (rs, start, _) in enumerate(rows):
            both = slice(rr * 2 * GRID_W, (rr + 1) * 2 * GRID_W)
            o = _dot(e_loc[both], vbf[pl.ds(start, win), ps]) + o_ctx[both]
            o_ref[rs, ps] = _pair_outputs(o / l[both]).astype(BF16)


def _na_bias_table(rpb):
    n = rpb.shape[-1]
    w = GRID_W
    pair = jnp.concatenate([rpb[:, :-1], rpb[:, 1:]], axis=-1).astype(F32)
    x = jnp.arange(2 * n)
    y = jnp.arange(2 * w)
    cq = jnp.arange(w)
    ck = y % w
    sel = ((x // n)[:, None, None] == (y // w)[None, None, :]) & (
        (x % n)[:, None, None] == ck[None, None, :] - cq[None, :, None] + NA_COLS - 1)
    t = jnp.einsum("hdx,xqy->hdqy", pair, sel.astype(F32), precision=HIGHEST)
    cs = jnp.clip(cq - NA_COLS // 2, 0, w - NA_COLS)[:, None]
    return jnp.where((ck[None, :] >= cs) & (ck[None, :] < cs + NA_COLS), t, MASK_VALUE)


def _lat_attention(q, k, v, cache_k, cache_v, bias):
    blk = LAT_ROWS * GRID_W
    steps = GRID_ROWS // LAT_ROWS
    lat0 = N_CTX_TOK // blk
    seq = pl.BlockSpec((DEC_SEQ, WIDTH), lambda b, i: (N_CTX_TOK // DEC_SEQ + b, 0))
    ctx = pl.BlockSpec((1, PAST_LEN, WIDTH), lambda b, i: (b, 0, 0))
    return pl.pallas_call(
        _lat_attn_kernel,
        grid=(DEC_BATCH, steps),
        in_specs=[pl.BlockSpec((blk, WIDTH), lambda b, i: (lat0 + b * steps + i, 0)),
                  seq, seq, ctx, ctx,
                  _const_spec((N_HEADS, 2 * NA_ROWS - 2, GRID_W, PAIR_W))],
        out_specs=pl.BlockSpec((blk, WIDTH), lambda b, i: (b * steps + i, 0)),
        out_shape=jax.ShapeDtypeStruct((N_LAT_TOK, WIDTH), BF16),
        scratch_shapes=[pltpu.VMEM((DEC_SEQ, WIDTH), BF16), pltpu.VMEM((DEC_SEQ, WIDTH), BF16),
                        pltpu.VMEM((PAST_LEN, WIDTH), BF16), pltpu.VMEM((PAST_LEN, WIDTH), BF16)],
        compiler_params=_params(2),
        name="latattn",
    )(q, k, v, cache_k, cache_v, bias)


def _rwkv_tokens(zcols, fillers, wts_ref, w0_ref, w2_ref, a0_ref, a2_ref, g2_ref, kk_ref, ka_ref, rk_ref, ones_ref,
                 shared_out, g_out, bonus_out, dir_out):
    def conv(c0, c1):
        zc, prow, nrow = zcols(c0, c1)
        zprev, znext = _shifted(zc, prow, nrow)
        return zprev * wts_ref[0:1, c0:c1] + zc * wts_ref[1:2, c0:c1] + znext * wts_ref[2:3, c0:c1]

    fill = iter(fillers)
    dirs = [slice(d * LORA, (d + 1) * LORA) for d in range(N_DIRS)]
    ones = ones_ref[...]
    o = 3 * WIDTH
    lora_in = conv(o, RWKV_COLS)
    kr = conv(WIDTH, 2 * WIDTH)
    tw = jnp.tanh(lora_in[:, 0:N_DIRS * LORA])
    xa = lora_in[:, N_DIRS * LORA:2 * N_DIRS * LORA]
    xg = lora_in[:, 2 * N_DIRS * LORA:]
    kkraw = kr * kk_ref[...]
    next(fill)()
    kk = kkraw * lax.rsqrt(_segsum(kkraw * kkraw, ones) + 1e-12)
    bf = lambda t: t.astype(BF16)
    lora_w = [_dot(bf(tw[:, ls]), bf(w2_ref[d])) for d, ls in enumerate(dirs)]
    r = conv(0, WIDTH)
    next(fill)()
    lora_a = [_dot(bf(xa[:, ls]), bf(a2_ref[d])) for d, ls in enumerate(dirs)]
    g_out[...] = _dot(bf(jax.nn.sigmoid(xg)), bf(g2_ref[...]))
    v = conv(2 * WIDTH, 3 * WIDTH)
    next(fill)()
    kd_sum = jnp.zeros_like(kr)
    for d in range(N_DIRS):
        w_log = -_softplus(-(w0_ref[d:d + 1, :] + lora_w[d])) - 0.5
        a = jax.nn.sigmoid(a0_ref[d:d + 1, :] + lora_a[d])
        kd = kr * (1.0 + (a - 1.0) * ka_ref[...])
        dir_out[d, :, 0:WIDTH] = -jnp.exp(w_log)
        dir_out[d, :, WIDTH:2 * WIDTH] = kk * a
        dir_out[d, :, 2 * WIDTH:] = kd
        kd_sum = kd_sum + kd
    shared_out[:, 0:WIDTH] = r
    shared_out[:, WIDTH:2 * WIDTH] = kk
    shared_out[:, 2 * WIDTH:] = v
    bonus_out[...] = _segsum(r * kd_sum * rk_ref[...], ones) * v


HALF = CHUNK // 2


def _blockdiag(x, block=HEAD_DIM):
    p = x.astype(BF16)
    blk = lax.broadcasted_iota(jnp.int32, p.shape, 1) // block
    return jnp.concatenate([jnp.where(blk == q, p, jnp.zeros_like(p)) for q in range(GROUP_W // block)], axis=0)


def _hp(a, b, block=HEAD_DIM):
    return _dot(a.astype(BF16), _blockdiag(b, block))


def _hp_nt(a, b):
    return _dot_nt(a.astype(BF16), _blockdiag(b))


def _head_transpose(x):
    xt = jnp.transpose(x)
    return jnp.concatenate([xt[h * HEAD_DIM:(h + 1) * HEAD_DIM] for h in range(GROUP)], axis=1)


def _scan_prepare(d, rows, shared_ref, dir_ref, y_ref, s_ref):
    row = lax.broadcasted_iota(jnp.int32, (CHUNK, GROUP_W), 0)
    lane = lax.broadcasted_iota(jnp.int32, (CHUNK, GROUP_W), 1)
    col = lane % CHUNK
    strict = col > row if d else col < row
    incl = col >= row if d else col <= row
    row_h = lax.broadcasted_iota(jnp.int32, (HALF, GROUP_W), 0)
    lane_h = lax.broadcasted_iota(jnp.int32, (HALF, GROUP_W), 1)
    lo_s = lane_h % CHUNK < HALF
    eye_half = (lane_h % HALF == row_h).astype(F32)

    lw = dir_ref[0, rows, 0:WIDTH]
    mask = jnp.where(incl[:, :CHUNK], 1.0, 0.0).astype(BF16)
    lw_hi, lw_mid = _split(lw)
    lw_lo = (lw - lw_hi.astype(F32) - lw_mid.astype(F32)).astype(BF16)
    cum = _dot(mask, lw_hi) + _dot(mask, lw_mid) + _dot(mask, lw_lo)
    tot = jnp.sum(lw, axis=0, keepdims=True)
    p_in = jnp.exp(cum)
    p_neg = jnp.exp(-cum)
    p_rem = jnp.exp(tot - cum)
    kt = shared_ref[rows, WIDTH:2 * WIDTH] * jnp.exp(cum - lw)
    rt = shared_ref[rows, 0:WIDTH] * p_in
    b = dir_ref[0, rows, WIDTH:2 * WIDTH]
    kd = dir_ref[0, rows, 2 * WIDTH:]
    bt = b * p_neg
    kdt = kd * p_neg
    bh = b * p_rem
    kh = kd * p_rem
    p_end = jnp.exp(tot)
    v = shared_ref[rows, 2 * WIDTH:]

    chains = []
    for gi in range(N_HEADS // GROUP):
        sl = slice(gi * GROUP_W, (gi + 1) * GROUP_W)
        chains.append(dict(d=d, rows=rows, sl=sl, strict=strict, incl=incl, lo_s=lo_s, eye_half=eye_half, y_ref=y_ref, s_ref=s_ref,
                           kt=kt[:, sl], rt=rt[:, sl], v=v[:, sl], bt=bt[:, sl], kdt=kdt[:, sl],
                           kh=kh[:, sl], bh=bh[:, sl], p_end=p_end[:, sl]))
    return chains


def _scan_scores(chains):
    for c in chains:
        lhs = jnp.concatenate([c["kt"], c["rt"]], axis=0)
        g_b = _hp_nt(lhs, c["bt"])
        g_k = _hp_nt(lhs, c["kdt"])
        c["a_k"] = jnp.where(c["strict"], g_k[:CHUNK], 0.0)
        c["a_rb"] = jnp.where(c["incl"], g_b[CHUNK:], 0.0)
        c["a_rk"] = jnp.where(c["incl"], g_k[CHUNK:], 0.0)
        n = jnp.where(c["strict"], -g_b[:CHUNK], 0.0)
        c["n"] = n
        c["pw"] = jnp.where(c["lo_s"], n[:HALF], n[HALF:])
        c["tp"] = c["eye_half"] + c["pw"]
    return chains


def _scan_local(chains):
    zeros = jnp.zeros((HALF, GROUP_W), F32)
    for c in chains:
        c["pw"] = _hp(c["pw"], c["pw"], HALF)
    for _ in range(3):
        for c in chains:
            both = _hp(jnp.concatenate([c["tp"], c["pw"]], axis=0), c["pw"], HALF)
            c["tp"] = c["tp"] + both[:HALF]
            c["pw"] = both[HALF:]
    for c in chains:
        c["tp"] = c["tp"] + _hp(c["tp"], c["pw"], HALF)
    for c in chains:
        c["t1"] = jnp.where(c["lo_s"], c["tp"], 0.0)
        c["t2"] = jnp.where(c["lo_s"], 0.0, c["tp"])
        if c["d"] == 0:
            off, inner = jnp.where(c["lo_s"], c["n"][HALF:], 0.0), jnp.concatenate([c["t1"], zeros], axis=0)
        else:
            off, inner = jnp.where(c["lo_s"], 0.0, c["n"][:HALF]), jnp.concatenate([zeros, c["t2"]], axis=0)
        c["off"] = _hp(off, inner)
    for c in chains:
        if c["d"] == 0:
            x = _hp(c["t2"], jnp.concatenate([zeros, c["off"]], axis=0))
            c["inv"] = jnp.concatenate([c["t1"], jnp.where(c["lo_s"], x, c["tp"])], axis=0)
        else:
            x = _hp(c["t1"], jnp.concatenate([c["off"], zeros], axis=0))
            c["inv"] = jnp.concatenate([jnp.where(c["lo_s"], c["tp"], x), c["t2"]], axis=0)
    for c in chains:
        c["av"] = _hp(jnp.concatenate([c["a_k"], c["a_rk"]], axis=0), c["v"])
        c["vt"] = _head_transpose(c["v"])
    return chains


def _scan_carry(chains):
    for c in chains:
        c["s"] = c["s_ref"][:, c["sl"]]
        c["z"] = _hp_nt(jnp.concatenate([c["rt"], c["kt"]], axis=0), c["s"])
    for c in chains:
        c["u"] = _hp(c["inv"], c["z"][CHUNK:] + c["av"][:CHUNK])
    for c in chains:
        c["y_ref"][c["rows"], c["sl"]] = c["z"][:CHUNK] + c["av"][CHUNK:] - _hp(c["a_rb"], c["u"])
    for c in chains:
        upd = _hp(c["vt"], c["kh"]) - _hp(_head_transpose(c["u"]), c["bh"])
        c["s_ref"][:, c["sl"]] = c["s"] * c["p_end"] + upd


REGION_CHUNKS = CTX_CHUNKS
STEP_CHUNKS = 2
STEP_ROWS = STEP_CHUNKS * CHUNK
SCAN_STEPS = REGION_CHUNKS // STEP_CHUNKS
SCAN_STREAMS = tuple((d, base // STEP_CHUNKS, per_seq // STEP_CHUNKS) for d in range(N_DIRS)
                     for base, per_seq in ((0, CHUNKS_PER_CTX_SEQ), (CTX_CHUNKS, CHUNKS_PER_LAT_SEQ)))
STREAM_INS = 3


def _stream_local_block(stream, j):
    return SCAN_STEPS - 1 - j if stream[0] else j


def _scan_kernel(n_cast, *refs):
    n_streams = len(SCAN_STREAMS)
    n_stream_in = STREAM_INS * n_streams
    n_in = n_stream_in + n_cast
    n_out = 2 * n_streams + n_cast
    carries = refs[n_in + n_out:]
    j = pl.program_id(0)
    for w_ref, o_ref in zip(refs[n_stream_in:n_in], refs[n_in + 2 * n_streams:n_in + n_out]):
        o_ref[...] = w_ref[...].astype(BF16)
    per_stream = []
    for s, stream in enumerate(SCAN_STREAMS):
        d, _, per_seq = stream
        ins = refs[STREAM_INS * s:STREAM_INS * (s + 1)]
        y_ref = refs[n_in + 2 * s]
        s0_ref = ins[-1]
        local = _stream_local_block(stream, j) % per_seq

        @pl.when(local == (per_seq - 1 if d else 0))
        def _():
            carries[s][...] = s0_ref[0, 0]

        per_stream.append((d,) + ins[:-1] + (y_ref, carries[s]))

    chains_by_chunk = []
    for k in range(STEP_CHUNKS):
        chains = []
        for d, shared_ref, dir_ref, y_ref, s_ref in per_stream:
            first = (STEP_CHUNKS - 1 - k if d else k) * CHUNK
            chains += _scan_prepare(d, slice(first, first + CHUNK), shared_ref, dir_ref, y_ref, s_ref)
        chains_by_chunk.append(_scan_local(_scan_scores(chains)))
    for chains in chains_by_chunk:
        _scan_carry(chains)

    for s, stream in enumerate(SCAN_STREAMS):
        d, _, per_seq = stream
        out_ref = refs[n_in + 2 * s + 1]
        local = _stream_local_block(stream, j) % per_seq

        @pl.when(local == (0 if d else per_seq - 1))
        def _():
            for h in range(N_HEADS):
                out_ref[0, h] = carries[s][:, h * HEAD_DIM:(h + 1) * HEAD_DIM]


def _scan(shared, perdir, s0, weights):
    in_specs, out_specs, out_shape, operands = [], [], [], []
    cast_in, cast_out, cast_shape = [], [], []
    for w in weights:
        rows = w.shape[0] // SCAN_STEPS
        if rows % 16 or rows * SCAN_STEPS != w.shape[0]:
            rows = 128
        assert w.shape[0] % rows == 0 and w.shape[0] // rows <= SCAN_STEPS
        spec = pl.BlockSpec((rows, w.shape[1]), lambda j, n=w.shape[0] // rows: (jnp.minimum(j, n - 1), 0))
        cast_in.append(spec)
        cast_out.append(spec)
        cast_shape.append(jax.ShapeDtypeStruct(w.shape, BF16))
    for stream in SCAN_STREAMS:
        d, base, per_seq = stream
        seq0 = 0 if base == 0 else BATCH
        n_seq = SCAN_STEPS // per_seq
        loc = lambda j, stream=stream: _stream_local_block(stream, j)
        base += PAD_TOK // STEP_ROWS
        tok = pl.BlockSpec((STEP_ROWS, 3 * WIDTH), lambda j, loc=loc, base=base: (base + loc(j), 0))
        dtok = pl.BlockSpec((1, STEP_ROWS, 3 * WIDTH), lambda j, loc=loc, base=base, d=d: (d, base + loc(j), 0))
        st_in = pl.BlockSpec((1, 1, HEAD_DIM, WIDTH),
                             lambda j, loc=loc, seq0=seq0, per_seq=per_seq, d=d: (seq0 + loc(j) // per_seq, d, 0, 0))
        in_specs += [tok, dtok, st_in]
        operands += [shared, perdir, s0]
        out_specs += [pl.BlockSpec((STEP_ROWS, WIDTH), lambda j, loc=loc: (loc(j), 0)),
                      pl.BlockSpec((1, N_HEADS, HEAD_DIM, HEAD_DIM),
                                   lambda j, loc=loc, per_seq=per_seq: (loc(j) // per_seq, 0, 0, 0))]
        out_shape += [jax.ShapeDtypeStruct((REGION_CHUNKS * CHUNK, WIDTH), F32),
                      jax.ShapeDtypeStruct((n_seq, N_HEADS, HEAD_DIM, HEAD_DIM), F32)]
    return pl.pallas_call(
        functools.partial(_scan_kernel, len(weights)),
        grid=(SCAN_STEPS,),
        in_specs=in_specs + cast_in,
        out_specs=out_specs + cast_out,
        out_shape=out_shape + cast_shape,
        scratch_shapes=[pltpu.VMEM((HEAD_DIM, WIDTH), F32)] * len(SCAN_STREAMS),
        compiler_params=_params(1),
        name="scan",
    )(*operands, *weights)


def _post_kernel(yfc_ref, yfl_ref, ybc_ref, ybl_ref, bonus_lo_ref, bonus_hi_ref, g_lo_ref, g_hi_ref,
                 attc_ref, attl_ref, xp_ref, xs_ref, mod_ref,
                 lng_ref, lnb_ref, ones_ref, wout_ref, n2_ref, x1_ref, h2_ref):
    i = pl.program_id(0)
    bonus = jnp.concatenate([bonus_lo_ref[...], bonus_hi_ref[...]], axis=0)
    gate = jnp.concatenate([g_lo_ref[...], g_hi_ref[...]], axis=0)
    def pick(ctx_ref, lat_ref):
        return _ctx_or_lat(i, ctx_ref, lat_ref, BIG_BLOCK)

    ones = ones_ref[...]
    inv_n = 1.0 / HEAD_DIM
    y = pick(yfc_ref, yfl_ref) + pick(ybc_ref, ybl_ref)
    yc = y - _segsum(y, ones) * inv_n
    var = _segsum(yc * yc, ones) * inv_n
    yn = yc * lax.rsqrt(var + GN_EPS) * lng_ref[...] + lnb_ref[...]
    r_out = ((yn + bonus) * gate).astype(BF16)
    o = _dot(pick(attc_ref, attl_ref), wout_ref[0:WIDTH, :]) + _dot(r_out, wout_ref[WIDTH:, :])
    x1 = pick(xp_ref, xs_ref) + mod_ref[0, 2:3, :] * o
    x1_ref[...] = x1
    h2 = _rmsnorm(x1, n2_ref[...]) * (1.0 + mod_ref[0, 4:5, :]) + mod_ref[0, 3:4, :]
    h2_ref[...] = h2.astype(BF16)


def _post(ys, bonus, g, att_ctx, att_lat, xp, xs, mod, ln_g, ln_b, ones, w_out_bf, norm2):
    blk = BIG_BLOCK
    tok = lambda cols: _tok_spec(cols, blk)
    ctx_lat = [_ctx_tok_spec(WIDTH, blk), _lat_tok_spec(WIDTH, blk)]
    halves = [pl.BlockSpec((PAD_TOK, WIDTH), lambda i, h=h: (i * (blk // PAD_TOK) + 1 + h, 0)) for h in range(2)]
    assert blk == 2 * PAD_TOK
    return pl.pallas_call(
        _post_kernel,
        grid=(N_TOK // blk,),
        in_specs=ctx_lat + ctx_lat + halves + halves
        + ctx_lat + [_ctx_tok_spec(D_MODEL, blk), _lat_tok_spec(D_MODEL, blk),
                  _mod_spec(blk),
                  _const_spec((1, WIDTH)), _const_spec((1, WIDTH)), _const_spec((WIDTH, WIDTH)),
                  _const_spec((2 * WIDTH, D_MODEL)), _const_spec((1, D_MODEL))],
        out_specs=[tok(D_MODEL), tok(D_MODEL)],
        out_shape=[jax.ShapeDtypeStruct((N_TOK, D_MODEL), F32),
                   jax.ShapeDtypeStruct((N_TOK, D_MODEL), BF16)],
        compiler_params=_params(1),
        name="post",
    )(*ys, bonus, bonus, g, g, att_ctx, att_lat, xp, xs, mod, ln_g, ln_b, ones, w_out_bf, norm2)


def _ffn_kernel(x1_ref, h2_ref, hp_ref, hn_ref, mod_ref, w1_ref, w3_ref, wc_ref, w2_ref, nf_ref, yp_ref, ys_ref):
    i = pl.program_id(0)
    blk = BIG_BLOCK
    is_ctx = i < N_CTX_TOK // blk
    has_prev, has_next = _seq_neighbours(i, blk)
    lhs = jnp.concatenate([h2_ref[...], hp_ref[...], hn_ref[...]], axis=0)
    h2 = lhs[:blk]
    rid = lax.broadcasted_iota(jnp.int32, (blk, 1), 0)
    inner = jnp.where(is_ctx, SEQ, -1)
    seq_start = rid == inner
    seq_end = rid == inner - 1
    acc = jnp.zeros((blk, D_MODEL), F32)
    for fs in (slice(0, FF_SPLIT), slice(FF_SPLIT, D_FF)):
        a_all = _dot(lhs, w1_ref[:, fs])
        a = a_all[:blk]
        prow = jnp.where(has_prev, a_all[blk + 15:blk + 16, :], 0.0)
        nrow = jnp.where(has_next, a_all[blk + 16:blk + 17, :], 0.0)
        aprev, anext = _shifted(a, prow, nrow)
        aprev = jnp.where(seq_start, 0.0, aprev)
        anext = jnp.where(seq_end, 0.0, anext)
        cv = aprev * wc_ref[0:1, fs] + a * wc_ref[1:2, fs] + anext * wc_ref[2:3, fs]
        act = _silu(cv) * _dot(h2, w3_ref[:, fs])
        acc = acc + _dot(act.astype(BF16), w2_ref[fs, :])
    x2 = x1_ref[...] + mod_ref[0, 5:6, :] * acc
    y = _rmsnorm(x2, nf_ref[...])

    @pl.when(is_ctx)
    def _():
        yp_ref[...] = y

    @pl.when(jnp.logical_not(is_ctx))
    def _():
        ys_ref[...] = y


def _ffn(x1, h2, mod, w1_bf, w3_bf, wc, w2_bf, norm_f):
    blk = BIG_BLOCK
    rows16 = blk // 16
    return pl.pallas_call(
        _ffn_kernel,
        grid=(N_TOK // blk,),
        in_specs=[_tok_spec(D_MODEL, blk), _tok_spec(D_MODEL, blk),
                  pl.BlockSpec((16, D_MODEL), lambda i: (jnp.maximum(i * rows16 - 1, 0), 0)),
                  pl.BlockSpec((16, D_MODEL), lambda i: (jnp.minimum((i + 1) * rows16, N_TOK // 16 - 1), 0)),
                  _mod_spec(blk),
                  _const_spec((D_MODEL, D_FF)), _const_spec((D_MODEL, D_FF)), _const_spec((3, D_FF)),
                  _const_spec((D_FF, D_MODEL)), _const_spec((1, D_MODEL))],
        out_specs=[_ctx_tok_spec(D_MODEL, blk), _lat_tok_spec(D_MODEL, blk)],
        out_shape=[jax.ShapeDtypeStruct((N_CTX_TOK, D_MODEL), F32),
                   jax.ShapeDtypeStruct((N_LAT_TOK, D_MODEL), F32)],
        compiler_params=_params(1),
        name="ffn",
    )(x1, h2, h2, h2, mod, w1_bf, w3_bf, wc, w2_bf, norm_f)


def kernel(x_prompt, x_sample, cache_k, cache_v, state_rwkv, c, c_ctx, w_ada, b_ada, norm1, norm2,
           w_in, w_ts, w0, w2, a0, a2, g2, k_k, k_a, r_k, ln_x_g, ln_x_b, rpb, w_out,
           w_ffn1, w_ffn3, w_ffn_conv, w_ffn2, norm_f):
    xp = x_prompt.reshape(N_CTX_TOK, D_MODEL)
    xs = x_sample.reshape(N_LAT_TOK, D_MODEL)
    row = lambda t: t.reshape(1, -1)
    cond = jnp.concatenate([c_ctx[None, :], c, jnp.zeros((8 - 1 - DEC_BATCH, D_MODEL), F32)], axis=0)
    mod, w_in_bf = _modulation(cond, w_ada[0], b_ada[0], w_in[0])
    mod = mod.reshape(8, 6, D_MODEL)

    head_id = jnp.arange(WIDTH) // HEAD_DIM
    ones = (head_id[:, None] == head_id[None, :]).astype(BF16)
    q, k, v, shared, g, bonus, perdir, new_k, new_v = _inproj(
        xp, xs, mod, row(norm1[0]), w_in_bf, w_ts[0], w0[0], w2[0], a0[0], a2[0], g2[0],
        row(k_k[0]), row(k_a[0]), row(r_k[0]), ones)

    att_ctx = _ctx_attention(q, k, v)
    att_lat = _lat_attention(q, k, v, cache_k[:, 0].reshape(DEC_BATCH, PAST_LEN, WIDTH),
                             cache_v[:, 0].reshape(DEC_BATCH, PAST_LEN, WIDTH), _na_bias_table(rpb[0]))
    s_lat = jnp.transpose(state_rwkv[:, 0], (0, 1, 3, 2, 4)).reshape(DEC_BATCH, N_DIRS, HEAD_DIM, WIDTH)
    s0 = jnp.concatenate([jnp.zeros((BATCH, N_DIRS, HEAD_DIM, WIDTH), F32), s_lat], axis=0)
    (y_fc, s_fc, y_fl, _, y_bc, s_bc, y_bl, _,
     w_out_bf, w1_bf, w3_bf, w2_bf) = _scan(shared, perdir, s0, (w_out[0], w_ffn1[0], w_ffn3[0], w_ffn2[0]))
    new_s = jnp.stack([s_fc, s_bc], axis=1)[:, None]

    x1, h2 = _post((y_fc, y_fl, y_bc, y_bl), bonus, g, att_ctx, att_lat, xp, xs, mod, row(ln_x_g[0]), row(ln_x_b[0]), ones,
                   w_out_bf, row(norm2[0]))
    yp, ys = _ffn(x1, h2, mod, w1_bf, w3_bf, w_ffn_conv[0], w2_bf, row(norm_f))

    return (yp.reshape(BATCH, SEQ, D_MODEL), ys.reshape(DEC_BATCH, DEC_SEQ, D_MODEL), new_k, new_v, new_s)
```

```python
import functools

import jax
import jax.numpy as jnp
from jax import lax
from jax.experimental import pallas as pl
from jax.experimental.pallas import tpu as pltpu

F32 = jnp.float32
BF16 = jnp.bfloat16
HIGHEST = lax.Precision.HIGHEST

D_MODEL = 1024
BATCH = 16
SEQ = 256
DEC_BATCH = 2
DEC_SEQ = 2048
PAST_LEN = 512
GRID_W = 64
HEAD_DIM = 64
N_HEADS = 8
WIDTH = N_HEADS * HEAD_DIM
PAIR_W = 2 * HEAD_DIM
NA_ROWS = 8
NA_COLS = 16
N_DIRS = 2
LORA = 64
GATE_LORA = 128
D_FF = 2816
EPS = 1e-6
GN_EPS = 64e-5
ATT_SCALE = HEAD_DIM ** -0.5
RWKV_COLS = 3 * WIDTH + N_DIRS * 2 * LORA + GATE_LORA
MASK_VALUE = -1e30

N_CTX_TOK = BATCH * SEQ
N_LAT_TOK = DEC_BATCH * DEC_SEQ
N_TOK = N_CTX_TOK + N_LAT_TOK
TOK_BLOCK = 256
N_BLOCKS = N_TOK // TOK_BLOCK
BIG_BLOCK = 512
PAD_TOK = TOK_BLOCK
N_SEQS = BATCH + DEC_BATCH
CHUNK = 64
N_CHUNKS = N_TOK // CHUNK
CTX_CHUNKS = N_CTX_TOK // CHUNK
CHUNKS_PER_CTX_SEQ = SEQ // CHUNK
CHUNKS_PER_LAT_SEQ = DEC_SEQ // CHUNK
GRID_ROWS = DEC_SEQ // GRID_W
LAT_ROWS = 4
MXU_WIDTH = 256
FF_SPLIT = (D_FF // MXU_WIDTH + 1) // 2 * MXU_WIDTH
GROUP = 4
GROUP_W = GROUP * HEAD_DIM
VMEM_LIMIT = 56 * 1024 * 1024


def _params(n_axes, limit=VMEM_LIMIT):
    return pltpu.CompilerParams(dimension_semantics=("arbitrary",) * n_axes, vmem_limit_bytes=limit)


def _const_spec(shape):
    zeros = (0,) * len(shape)
    return pl.BlockSpec(shape, lambda *_: zeros, pipeline_mode=pl.Buffered(1))


def _tok_spec(cols, blk=TOK_BLOCK):
    return pl.BlockSpec((blk, cols), lambda i: (i, 0))


def _ctx_tok_spec(cols, blk=TOK_BLOCK):
    return pl.BlockSpec((blk, cols), lambda i: (jnp.minimum(i, N_CTX_TOK // blk - 1), 0))


def _lat_tok_spec(cols, blk=TOK_BLOCK):
    return pl.BlockSpec((blk, cols), lambda i: (jnp.maximum(i - N_CTX_TOK // blk, 0), 0))


def _ctx_or_lat(i, ctx_ref, lat_ref, blk=TOK_BLOCK):
    return jnp.where(i < N_CTX_TOK // blk, ctx_ref[...], lat_ref[...])


def _mod_spec(blk=TOK_BLOCK):
    def row(i):
        return jnp.where(i < N_CTX_TOK // blk, 0, 1 + (i - N_CTX_TOK // blk) // (DEC_SEQ // blk))
    return pl.BlockSpec((1, 6, D_MODEL), lambda i: (row(i), 0, 0))


def _seq_neighbours(i, blk=TOK_BLOCK):
    per_seq = DEC_SEQ // blk
    j = (i - N_CTX_TOK // blk) % per_seq
    lat = i >= N_CTX_TOK // blk
    return lat & (j != 0), lat & (j != per_seq - 1)


def _silu(x):
    return x * jax.nn.sigmoid(x)


def _softplus(x):
    return jnp.maximum(x, 0.0) + jnp.log(1.0 + jnp.exp(-jnp.abs(x)))


def _rmsnorm(x, g):
    return x * lax.rsqrt(jnp.mean(x * x, axis=-1, keepdims=True) + EPS) * g


def _dot(a, b, precision=None):
    return jnp.dot(a, b, precision=precision, preferred_element_type=F32)


def _dot_nt(a, b, precision=None):
    return lax.dot_general(a, b, (((1,), (1,)), ((), ())), precision=precision, preferred_element_type=F32)


def _split(x):
    hi = x.astype(BF16)
    return hi, (x - hi.astype(F32)).astype(BF16)


def _segsum(x, ones_bf):
    return _dot(x.astype(BF16), ones_bf)


def _shifted(a, prev_row, next_row):
    t = a.shape[0]
    rid = lax.broadcasted_iota(jnp.int32, (t, 1), 0)
    prev = jnp.where(rid == 0, prev_row, pltpu.roll(a, 1, axis=0))
    nxt = jnp.where(rid == t - 1, next_row, pltpu.roll(a, t - 1, axis=0))
    return prev, nxt


MOD_STEPS = 8


def _mod_kernel(cond_ref, w_ref, b_ref, win_ref, o_ref, win_bf_ref):
    o_ref[...] = _dot(_silu(cond_ref[...]).astype(BF16), w_ref[...].astype(BF16)) + b_ref[...]
    win_bf_ref[...] = win_ref[...].astype(BF16)


def _modulation(cond, w_ada, b_ada, w_in):
    n = 6 * D_MODEL
    cols = n // MOD_STEPS
    rows = D_MODEL // MOD_STEPS
    w_in_spec = pl.BlockSpec((rows, w_in.shape[1]), lambda j: (j, 0))
    return pl.pallas_call(
        _mod_kernel,
        grid=(MOD_STEPS,),
        in_specs=[pl.BlockSpec((8, D_MODEL), lambda j: (0, 0)),
                  pl.BlockSpec((D_MODEL, cols), lambda j: (0, j)),
                  pl.BlockSpec((1, cols), lambda j: (0, j)),
                  w_in_spec],
        out_specs=[pl.BlockSpec((8, cols), lambda j: (0, j)), w_in_spec],
        out_shape=[jax.ShapeDtypeStruct((8, n), F32), jax.ShapeDtypeStruct(w_in.shape, BF16)],
        compiler_params=_params(1),
        name="mod",
    )(cond, w_ada, b_ada.reshape(1, n), w_in)


def _inproj_kernel(xp_ref, xs_ref, xprev_ref, xnext_ref, mod_ref, n1_ref, w_ref, *rest):
    rwkv_refs, (q_ref, k_ref, v_ref), rwkv_outs, (nk_ref, nv_ref), z_bufs = (
        rest[:10], rest[10:13], rest[13:17], rest[17:19], rest[19:])
    i = pl.program_id(0)
    blk = jnp.minimum(i, N_BLOCKS - 1)

    @pl.when(i == 0)
    def _():
        z_bufs[1][...] = jnp.zeros_like(z_bufs[1])

    def modulated(x):
        h = _rmsnorm(x, n1_ref[...]) * (1.0 + mod_ref[0, 1:2, :]) + mod_ref[0, 0:1, :]
        return h.astype(BF16)

    def step(z_store, z_load):
        h = modulated(_ctx_or_lat(blk, xp_ref, xs_ref))
        edge = modulated(jnp.concatenate([xprev_ref[...], xnext_ref[...]], axis=0))
        lhs = jnp.concatenate([h, edge], axis=0)
        kv = []

        def project(c0, c1):
            z_store[:, c0:c1] = _dot(lhs, w_ref[:, 3 * WIDTH + c0:3 * WIDTH + c1])

        def portion_q():
            project(3 * WIDTH, RWKV_COLS)
            project(WIDTH, 2 * WIDTH)
            q_ref[...] = _dot(h, w_ref[:, 0:WIDTH])

        def portion_k():
            project(0, WIDTH)
            kv.append(_dot(h, w_ref[:, WIDTH:2 * WIDTH]))

        def portion_v():
            project(2 * WIDTH, 3 * WIDTH)
            kv.append(_dot(h, w_ref[:, 2 * WIDTH:3 * WIDTH]))

        has_prev, has_next = _seq_neighbours(i - 1)

        def zcols(c0, c1):
            return (z_load[0:TOK_BLOCK, c0:c1],
                    jnp.where(has_prev, z_load[TOK_BLOCK + 7:TOK_BLOCK + 8, c0:c1], 0.0),
                    jnp.where(has_next, z_load[TOK_BLOCK + 8:TOK_BLOCK + 9, c0:c1], 0.0))

        _rwkv_tokens(zcols, (portion_q, portion_k, portion_v), *rwkv_refs, *rwkv_outs)
        k, v = kv
        k_ref[...] = k
        v_ref[...] = v

        @pl.when(i < N_CTX_TOK // TOK_BLOCK)
        def _():
            nk_ref[0, 0] = k.reshape(SEQ, N_HEADS, HEAD_DIM)
            nv_ref[0, 0] = v.reshape(SEQ, N_HEADS, HEAD_DIM)

    @pl.when(i % 2 == 0)
    def _():
        step(z_bufs[0], z_bufs[1])

    @pl.when(i % 2 == 1)
    def _():
        step(z_bufs[1], z_bufs[0])


def _inproj(xp, xs, mod, norm1, w_in_bf, w_ts, w0, w2, a0, a2, g2, k_k, k_a, r_k, ones):
    tok = jax.ShapeDtypeStruct((N_TOK, WIDTH), F32)
    padded = lambda cols, dtype=F32: jax.ShapeDtypeStruct((N_TOK + PAD_TOK, cols), dtype)
    lat0 = N_CTX_TOK // TOK_BLOCK
    last = N_BLOCKS - 1
    rows8 = TOK_BLOCK // 8
    n8 = N_LAT_TOK // 8
    clamped = lambda cols: pl.BlockSpec((TOK_BLOCK, cols), lambda i: (jnp.minimum(i, last), 0))
    x_ctx = pl.BlockSpec((TOK_BLOCK, D_MODEL), lambda i: (jnp.minimum(i, lat0 - 1), 0))
    x_lat = pl.BlockSpec((TOK_BLOCK, D_MODEL), lambda i: (jnp.clip(i - lat0, 0, last - lat0), 0))
    prev8 = pl.BlockSpec((8, D_MODEL), lambda i: (jnp.clip((jnp.minimum(i, last) - lat0) * rows8 - 1, 0, n8 - 1), 0))
    next8 = pl.BlockSpec((8, D_MODEL), lambda i: (jnp.clip((jnp.minimum(i, last) - lat0 + 1) * rows8, 0, n8 - 1), 0))
    mod_rows = pl.BlockSpec((1, 6, D_MODEL), lambda i: (
        jnp.where(i < lat0, 0, 1 + (jnp.minimum(i, last) - lat0) // (DEC_SEQ // TOK_BLOCK)), 0, 0))
    cache = pl.BlockSpec((1, 1, SEQ, N_HEADS, HEAD_DIM), lambda i: (jnp.minimum(i, BATCH - 1), 0, 0, 0, 0))
    z_buf = pltpu.VMEM((TOK_BLOCK + 16, RWKV_COLS), F32)
    return pl.pallas_call(
        _inproj_kernel,
        grid=(N_BLOCKS + 1,),
        in_specs=[x_ctx, x_lat, prev8, next8, mod_rows,
                  _const_spec((1, D_MODEL)),
                  _const_spec((D_MODEL, 3 * WIDTH + RWKV_COLS)),
                  _const_spec((3, RWKV_COLS)),
                  _const_spec((N_DIRS, WIDTH)), _const_spec((N_DIRS, LORA, WIDTH)),
                  _const_spec((N_DIRS, WIDTH)), _const_spec((N_DIRS, LORA, WIDTH)),
                  _const_spec((GATE_LORA, WIDTH)),
                  _const_spec((1, WIDTH)), _const_spec((1, WIDTH)), _const_spec((1, WIDTH)),
                  _const_spec((WIDTH, WIDTH))],
        out_specs=[clamped(WIDTH), clamped(WIDTH), clamped(WIDTH),
                   _tok_spec(3 * WIDTH), _tok_spec(WIDTH), _tok_spec(WIDTH),
                   pl.BlockSpec((N_DIRS, TOK_BLOCK, 3 * WIDTH), lambda i: (0, i, 0)),
                   cache, cache],
        out_shape=[tok, tok, tok,
                   padded(3 * WIDTH), padded(WIDTH, BF16), padded(WIDTH, BF16),
                   jax.ShapeDtypeStruct((N_DIRS, N_TOK + PAD_TOK, 3 * WIDTH), F32),
                   jax.ShapeDtypeStruct((BATCH, 1, SEQ, N_HEADS, HEAD_DIM), F32),
                   jax.ShapeDtypeStruct((BATCH, 1, SEQ, N_HEADS, HEAD_DIM), F32)],
        scratch_shapes=[z_buf, z_buf],
        compiler_params=_params(1),
        name="inproj",
    )(xp, xs, xs, xs, mod, norm1, w_in_bf, w_ts, w0, w2, a0, a2, g2, k_k, k_a, r_k, ones)


def _pair_queries(q):
    lo_half = lax.broadcasted_iota(jnp.int32, q.shape, 1) < HEAD_DIM
    return jnp.concatenate([jnp.where(lo_half, q, 0.0), jnp.where(lo_half, 0.0, q)], axis=0).astype(BF16)


def _pair_outputs(o):
    t = o.shape[0] // 2
    lo_half = lax.broadcasted_iota(jnp.int32, (t, PAIR_W), 1) < HEAD_DIM
    return jnp.where(lo_half, o[:t], o[t:])


CTX_SEQS_PER_STEP = 2


def _ctx_attn_kernel(q_ref, k_ref, v_ref, o_ref):
    units = [(slice(s * SEQ, (s + 1) * SEQ), slice(p * PAIR_W, (p + 1) * PAIR_W))
             for s in range(CTX_SEQS_PER_STEP) for p in range(N_HEADS // 2)]
    scores = [_dot_nt(_pair_queries(q_ref[rs, ps]), k_ref[rs, ps].astype(BF16)) * ATT_SCALE for rs, ps in units]
    probs = []
    for s in scores:
        e = jnp.exp(s - jnp.max(s, axis=-1, keepdims=True))
        probs.append((e.astype(BF16), jnp.sum(e, axis=-1, keepdims=True)))
    for (rs, ps), (e, l) in zip(units, probs):
        o_ref[rs, ps] = _pair_outputs(_dot(e, v_ref[rs, ps].astype(BF16)) / l).astype(BF16)


def _ctx_attention(q, k, v):
    spec = pl.BlockSpec((CTX_SEQS_PER_STEP * SEQ, WIDTH), lambda b: (b, 0))
    return pl.pallas_call(
        _ctx_attn_kernel,
        grid=(BATCH // CTX_SEQS_PER_STEP,),
        in_specs=[spec, spec, spec],
        out_specs=spec,
        out_shape=jax.ShapeDtypeStruct((N_CTX_TOK, WIDTH), BF16),
        compiler_params=_params(1),
        name="ctxattn",
    )(q, k, v)


def _lat_attn_kernel(q_ref, k_ref, v_ref, ck_ref, cv_ref, bias_ref, o_ref, kbf, vbf, ckbf, cvbf):
    i = pl.program_id(1)

    @pl.when(i == 0)
    def _():
        kbf[...] = k_ref[...].astype(BF16)
        vbf[...] = v_ref[...].astype(BF16)
        ckbf[...] = ck_ref[0].astype(BF16)
        cvbf[...] = cv_ref[0].astype(BF16)

    win = NA_ROWS * GRID_W
    rows = []
    for rr in range(LAT_ROWS):
        gr = i * LAT_ROWS + rr
        first_row = jnp.clip(gr - NA_ROWS // 2, 0, GRID_ROWS - NA_ROWS)
        start = pl.multiple_of(first_row * GRID_W, GRID_W)
        off0 = first_row - gr + NA_ROWS - 1
        rows.append((slice(rr * GRID_W, (rr + 1) * GRID_W), start, off0))
    pairs = [slice(p * PAIR_W, (p + 1) * PAIR_W) for p in range(N_HEADS // 2)]
    scores = []
    for p, ps in enumerate(pairs):
        lhs = [_pair_queries(q_ref[rs, ps]) for rs, _, _ in rows]
        s_loc = []
        for (_, start, off0), lhs_r in zip(rows, lhs):
            bias = jnp.concatenate(
                [jnp.concatenate([bias_ref[2 * p + hh, off0 + 2 * j] for j in range(NA_ROWS // 2)], axis=-1)
                 for hh in range(2)], axis=0)
            s_loc.append(_dot_nt(lhs_r, kbf[pl.ds(start, win), ps]) * ATT_SCALE + bias)
        s_ctx = _dot_nt(jnp.concatenate(lhs, axis=0), ckbf[:, ps]) * ATT_SCALE
        scores.append((jnp.concatenate(s_loc, axis=0), s_ctx))
    probs = []
    for s_loc, s_ctx in scores:
        m = jnp.maximum(jnp.max(s_loc, axis=-1, keepdims=True), jnp.max(s_ctx, axis=-1, keepdims=True))
        e_loc = jnp.exp(s_loc - m)
        e_ctx = jnp.exp(s_ctx - m)
        l = jnp.sum(e_loc, axis=-1, keepdims=True) + jnp.sum(e_ctx, axis=-1, keepdims=True)
        probs.append((e_loc.astype(BF16), e_ctx.astype(BF16), l))
    for ps, (e_loc, e_ctx, l) in zip(pairs, probs):
        o_ctx = _dot(e_ctx, cvbf[:, ps])
        for rr, (rs, start, _) in enumerate(rows):
            both = slice(rr * 2 * GRID_W, (rr + 1) * 2 * GRID_W)
            o = _dot(e_loc[both], vbf[pl.ds(start, win), ps]) + o_ctx[both]
            o_ref[rs, ps] = _pair_outputs(o / l[both]).astype(BF16)


def _na_bias_table(rpb):
    n = rpb.shape[-1]
    w = GRID_W
    pair = jnp.concatenate([rpb[:, :-1], rpb[:, 1:]], axis=-1).astype(F32)
    x = jnp.arange(2 * n)
    y = jnp.arange(2 * w)
    cq = jnp.arange(w)
    ck = y % w
    sel = ((x // n)[:, None, None] == (y // w)[None, None, :]) & (
        (x % n)[:, None, None] == ck[None, None, :] - cq[None, :, None] + NA_COLS - 1)
    t = jnp.einsum("hdx,xqy->hdqy", pair, sel.astype(F32), precision=HIGHEST)
    cs = jnp.clip(cq - NA_COLS // 2, 0, w - NA_COLS)[:, None]
    return jnp.where((ck[None, :] >= cs) & (ck[None, :] < cs + NA_COLS), t, MASK_VALUE)


def _lat_attention(q, k, v, cache_k, cache_v, bias):
    blk = LAT_ROWS * GRID_W
    steps = GRID_ROWS // LAT_ROWS
    lat0 = N_CTX_TOK // blk
    seq = pl.BlockSpec((DEC_SEQ, WIDTH), lambda b, i: (N_CTX_TOK // DEC_SEQ + b, 0))
    ctx = pl.BlockSpec((1, PAST_LEN, WIDTH), lambda b, i: (b, 0, 0))
    return pl.pallas_call(
        _lat_attn_kernel,
        grid=(DEC_BATCH, steps),
        in_specs=[pl.BlockSpec((blk, WIDTH), lambda b, i: (lat0 + b * steps + i, 0)),
                  seq, seq, ctx, ctx,
                  _const_spec((N_HEADS, 2 * NA_ROWS - 2, GRID_W, PAIR_W))],
        out_specs=pl.BlockSpec((blk, WIDTH), lambda b, i: (b * steps + i, 0)),
        out_shape=jax.ShapeDtypeStruct((N_LAT_TOK, WIDTH), BF16),
        scratch_shapes=[pltpu.VMEM((DEC_SEQ, WIDTH), BF16), pltpu.VMEM((DEC_SEQ, WIDTH), BF16),
                        pltpu.VMEM((PAST_LEN, WIDTH), BF16), pltpu.VMEM((PAST_LEN, WIDTH), BF16)],
        compiler_params=_params(2),
        name="latattn",
    )(q, k, v, cache_k, cache_v, bias)


def _rwkv_tokens(zcols, fillers, wts_ref, w0_ref, w2_ref, a0_ref, a2_ref, g2_ref, kk_ref, ka_ref, rk_ref, ones_ref,
                 shared_out, g_out, bonus_out, dir_out):
    def conv(c0, c1):
        zc, prow, nrow = zcols(c0, c1)
        zprev, znext = _shifted(zc, prow, nrow)
        return zprev * wts_ref[0:1, c0:c1] + zc * wts_ref[1:2, c0:c1] + znext * wts_ref[2:3, c0:c1]

    fill = iter(fillers)
    dirs = [slice(d * LORA, (d + 1) * LORA) for d in range(N_DIRS)]
    ones = ones_ref[...]
    o = 3 * WIDTH
    lora_in = conv(o, RWKV_COLS)
    kr = conv(WIDTH, 2 * WIDTH)
    tw = jnp.tanh(lora_in[:, 0:N_DIRS * LORA])
    xa = lora_in[:, N_DIRS * LORA:2 * N_DIRS * LORA]
    xg = lora_in[:, 2 * N_DIRS * LORA:]
    kkraw = kr * kk_ref[...]
    next(fill)()
    kk = kkraw * lax.rsqrt(_segsum(kkraw * kkraw, ones) + 1e-12)
    bf = lambda t: t.astype(BF16)
    lora_w = [_dot(bf(tw[:, ls]), bf(w2_ref[d])) for d, ls in enumerate(dirs)]
    r = conv(0, WIDTH)
    next(fill)()
    lora_a = [_dot(bf(xa[:, ls]), bf(a2_ref[d])) for d, ls in enumerate(dirs)]
    g_out[...] = _dot(bf(jax.nn.sigmoid(xg)), bf(g2_ref[...])).astype(g_out.dtype)
    v = conv(2 * WIDTH, 3 * WIDTH)
    next(fill)()
    kd_sum = jnp.zeros_like(kr)
    for d in range(N_DIRS):
        w_log = -_softplus(-(w0_ref[d:d + 1, :] + lora_w[d])) - 0.5
        a = jax.nn.sigmoid(a0_ref[d:d + 1, :] + lora_a[d])
        kd = kr * (1.0 + (a - 1.0) * ka_ref[...])
        dir_out[d, :, 0:WIDTH] = -jnp.exp(w_log)
        dir_out[d, :, WIDTH:2 * WIDTH] = kk * a
        dir_out[d, :, 2 * WIDTH:] = kd
        kd_sum = kd_sum + kd
    shared_out[:, 0:WIDTH] = r
    shared_out[:, WIDTH:2 * WIDTH] = kk
    shared_out[:, 2 * WIDTH:] = v
    bonus_out[...] = (_segsum(r * kd_sum * rk_ref[...], ones) * v).astype(bonus_out.dtype)


HALF = CHUNK // 2


def _blockdiag(x, block=HEAD_DIM):
    p = x.astype(BF16)
    blk = lax.broadcasted_iota(jnp.int32, p.shape, 1) // block
    return jnp.concatenate([jnp.where(blk == q, p, jnp.zeros_like(p)) for q in range(GROUP_W // block)], axis=0)


def _hp(a, b, block=HEAD_DIM):
    return _dot(a.astype(BF16), _blockdiag(b, block))


def _hp_nt(a, b):
    return _dot_nt(a.astype(BF16), _blockdiag(b))


def _head_transpose(x):
    xt = jnp.transpose(x)
    return jnp.concatenate([xt[h * HEAD_DIM:(h + 1) * HEAD_DIM] for h in range(GROUP)], axis=1)


def _scan_prepare(d, rows, shared_ref, dir_ref, y_ref, s_ref):
    row = lax.broadcasted_iota(jnp.int32, (CHUNK, GROUP_W), 0)
    lane = lax.broadcasted_iota(jnp.int32, (CHUNK, GROUP_W), 1)
    col = lane % CHUNK
    strict = col > row if d else col < row
    incl = col >= row if d else col <= row
    row_h = lax.broadcasted_iota(jnp.int32, (HALF, GROUP_W), 0)
    lane_h = lax.broadcasted_iota(jnp.int32, (HALF, GROUP_W), 1)
    lo_s = lane_h % CHUNK < HALF
    eye_half = (lane_h % HALF == row_h).astype(F32)

    lw = dir_ref[0, rows, 0:WIDTH]
    mask = jnp.where(incl[:, :CHUNK], 1.0, 0.0).astype(BF16)
    lw_hi, lw_mid = _split(lw)
    lw_lo = (lw - lw_hi.astype(F32) - lw_mid.astype(F32)).astype(BF16)
    cum = _dot(mask, lw_hi) + _dot(mask, lw_mid) + _dot(mask, lw_lo)
    tot = jnp.sum(lw, axis=0, keepdims=True)
    p_in = jnp.exp(cum)
    p_neg = jnp.exp(-cum)
    p_rem = jnp.exp(tot - cum)
    kt = shared_ref[rows, WIDTH:2 * WIDTH] * jnp.exp(cum - lw)
    rt = shared_ref[rows, 0:WIDTH] * p_in
    b = dir_ref[0, rows, WIDTH:2 * WIDTH]
    kd = dir_ref[0, rows, 2 * WIDTH:]
    bt = b * p_neg
    kdt = kd * p_neg
    bh = b * p_rem
    kh = kd * p_rem
    p_end = jnp.exp(tot)
    v = shared_ref[rows, 2 * WIDTH:]

    chains = []
    for gi in range(N_HEADS // GROUP):
        sl = slice(gi * GROUP_W, (gi + 1) * GROUP_W)
        chains.append(dict(d=d, rows=rows, sl=sl, strict=strict, incl=incl, lo_s=lo_s, eye_half=eye_half, y_ref=y_ref, s_ref=s_ref,
                           kt=kt[:, sl], rt=rt[:, sl], v=v[:, sl], bt=bt[:, sl], kdt=kdt[:, sl],
                           kh=kh[:, sl], bh=bh[:, sl], p_end=p_end[:, sl]))
    return chains


def _scan_scores(chains):
    for c in chains:
        lhs = jnp.concatenate([c["kt"], c["rt"]], axis=0)
        g_b = _hp_nt(lhs, c["bt"])
        g_k = _hp_nt(lhs, c["kdt"])
        c["a_k"] = jnp.where(c["strict"], g_k[:CHUNK], 0.0)
        c["a_rb"] = jnp.where(c["incl"], g_b[CHUNK:], 0.0)
        c["a_rk"] = jnp.where(c["incl"], g_k[CHUNK:], 0.0)
        n = jnp.where(c["strict"], -g_b[:CHUNK], 0.0)
        c["n"] = n
        c["pw"] = jnp.where(c["lo_s"], n[:HALF], n[HALF:])
        c["tp"] = c["eye_half"] + c["pw"]
    return chains


def _scan_local(chains):
    zeros = jnp.zeros((HALF, GROUP_W), F32)
    for c in chains:
        c["pw"] = _hp(c["pw"], c["pw"], HALF)
    for _ in range(3):
        for c in chains:
            both = _hp(jnp.concatenate([c["tp"], c["pw"]], axis=0), c["pw"], HALF)
            c["tp"] = c["tp"] + both[:HALF]
            c["pw"] = both[HALF:]
    for c in chains:
        c["tp"] = c["tp"] + _hp(c["tp"], c["pw"], HALF)
    for c in chains:
        c["t1"] = jnp.where(c["lo_s"], c["tp"], 0.0)
        c["t2"] = jnp.where(c["lo_s"], 0.0, c["tp"])
        if c["d"] == 0:
            off, inner = jnp.where(c["lo_s"], c["n"][HALF:], 0.0), jnp.concatenate([c["t1"], zeros], axis=0)
        else:
            off, inner = jnp.where(c["lo_s"], 0.0, c["n"][:HALF]), jnp.concatenate([zeros, c["t2"]], axis=0)
        c["off"] = _hp(off, inner)
    for c in chains:
        if c["d"] == 0:
            x = _hp(c["t2"], jnp.concatenate([zeros, c["off"]], axis=0))
            c["inv"] = jnp.concatenate([c["t1"], jnp.where(c["lo_s"], x, c["tp"])], axis=0)
        else:
            x = _hp(c["t1"], jnp.concatenate([c["off"], zeros], axis=0))
            c["inv"] = jnp.concatenate([jnp.where(c["lo_s"], c["tp"], x), c["t2"]], axis=0)
    for c in chains:
        c["av"] = _hp(jnp.concatenate([c["a_k"], c["a_rk"]], axis=0), c["v"])
        c["vt"] = _head_transpose(c["v"])
    return chains


def _scan_carry(chains):
    for c in chains:
        c["s"] = c["s_ref"][:, c["sl"]]
        c["z"] = _hp_nt(jnp.concatenate([c["rt"], c["kt"]], axis=0), c["s"])
    for c in chains:
        c["u"] = _hp(c["inv"], c["z"][CHUNK:] + c["av"][:CHUNK])
    for c in chains:
        y = c["z"][:CHUNK] + c["av"][CHUNK:] - _hp(c["a_rb"], c["u"])
        c["y_ref"][c["rows"], c["sl"]] = y.astype(BF16)
    for c in chains:
        upd = _hp(c["vt"], c["kh"]) - _hp(_head_transpose(c["u"]), c["bh"])
        c["s_ref"][:, c["sl"]] = c["s"] * c["p_end"] + upd


REGION_CHUNKS = CTX_CHUNKS
STEP_CHUNKS = 2
STEP_ROWS = STEP_CHUNKS * CHUNK
SCAN_STEPS = REGION_CHUNKS // STEP_CHUNKS
SCAN_STREAMS = tuple((d, base // STEP_CHUNKS, per_seq // STEP_CHUNKS) for d in range(N_DIRS)
                     for base, per_seq in ((0, CHUNKS_PER_CTX_SEQ), (CTX_CHUNKS, CHUNKS_PER_LAT_SEQ)))
STREAM_INS = 3


def _stream_local_block(stream, j):
    return SCAN_STEPS - 1 - j if stream[0] else j


def _scan_kernel(n_cast, *refs):
    n_streams = len(SCAN_STREAMS)
    n_stream_in = STREAM_INS * n_streams
    n_in = n_stream_in + n_cast
    n_out = 2 * n_streams + n_cast
    carries = refs[n_in + n_out:]
    j = pl.program_id(0)
    for w_ref, o_ref in zip(refs[n_stream_in:n_in], refs[n_in + 2 * n_streams:n_in + n_out]):
        o_ref[...] = w_ref[...].astype(BF16)
    per_stream = []
    for s, stream in enumerate(SCAN_STREAMS):
        d, _, per_seq = stream
        ins = refs[STREAM_INS * s:STREAM_INS * (s + 1)]
        y_ref = refs[n_in + 2 * s]
        s0_ref = ins[-1]
        local = _stream_local_block(stream, j) % per_seq

        @pl.when(local == (per_seq - 1 if d else 0))
        def _():
            carries[s][...] = s0_ref[0, 0]

        per_stream.append((d,) + ins[:-1] + (y_ref, carries[s]))

    chains_by_chunk = []
    for k in range(STEP_CHUNKS):
        chains = []
        for d, shared_ref, dir_ref, y_ref, s_ref in per_stream:
            first = (STEP_CHUNKS - 1 - k if d else k) * CHUNK
            chains += _scan_prepare(d, slice(first, first + CHUNK), shared_ref, dir_ref, y_ref, s_ref)
        chains_by_chunk.append(_scan_local(_scan_scores(chains)))
    for chains in chains_by_chunk:
        _scan_carry(chains)

    for s, stream in enumerate(SCAN_STREAMS):
        d, _, per_seq = stream
        out_ref = refs[n_in + 2 * s + 1]
        local = _stream_local_block(stream, j) % per_seq

        @pl.when(local == (0 if d else per_seq - 1))
        def _():
            for h in range(N_HEADS):
                out_ref[0, h] = carries[s][:, h * HEAD_DIM:(h + 1) * HEAD_DIM]


def _scan(shared, perdir, s0, weights):
    in_specs, out_specs, out_shape, operands = [], [], [], []
    cast_in, cast_out, cast_shape = [], [], []
    for w in weights:
        rows = w.shape[0] // SCAN_STEPS
        if rows % 16 or rows * SCAN_STEPS != w.shape[0]:
            rows = 128
        assert w.shape[0] % rows == 0 and w.shape[0] // rows <= SCAN_STEPS
        spec = pl.BlockSpec((rows, w.shape[1]), lambda j, n=w.shape[0] // rows: (jnp.minimum(j, n - 1), 0))
        cast_in.append(spec)
        cast_out.append(spec)
        cast_shape.append(jax.ShapeDtypeStruct(w.shape, BF16))
    for stream in SCAN_STREAMS:
        d, base, per_seq = stream
        seq0 = 0 if base == 0 else BATCH
        n_seq = SCAN_STEPS // per_seq
        loc = lambda j, stream=stream: _stream_local_block(stream, j)
        base += PAD_TOK // STEP_ROWS
        tok = pl.BlockSpec((STEP_ROWS, 3 * WIDTH), lambda j, loc=loc, base=base: (base + loc(j), 0))
        dtok = pl.BlockSpec((1, STEP_ROWS, 3 * WIDTH), lambda j, loc=loc, base=base, d=d: (d, base + loc(j), 0))
        st_in = pl.BlockSpec((1, 1, HEAD_DIM, WIDTH),
                             lambda j, loc=loc, seq0=seq0, per_seq=per_seq, d=d: (seq0 + loc(j) // per_seq, d, 0, 0))
        in_specs += [tok, dtok, st_in]
        operands += [shared, perdir, s0]
        out_specs += [pl.BlockSpec((STEP_ROWS, WIDTH), lambda j, loc=loc: (loc(j), 0)),
                      pl.BlockSpec((1, N_HEADS, HEAD_DIM, HEAD_DIM),
                                   lambda j, loc=loc, per_seq=per_seq: (loc(j) // per_seq, 0, 0, 0))]
        out_shape += [jax.ShapeDtypeStruct((REGION_CHUNKS * CHUNK, WIDTH), BF16),
                      jax.ShapeDtypeStruct((n_seq, N_HEADS, HEAD_DIM, HEAD_DIM), F32)]
    return pl.pallas_call(
        functools.partial(_scan_kernel, len(weights)),
        grid=(SCAN_STEPS,),
        in_specs=in_specs + cast_in,
        out_specs=out_specs + cast_out,
        out_shape=out_shape + cast_shape,
        scratch_shapes=[pltpu.VMEM((HEAD_DIM, WIDTH), F32)] * len(SCAN_STREAMS),
        compiler_params=_params(1),
        name="scan",
    )(*operands, *weights)


def _post_kernel(yfc_ref, yfl_ref, ybc_ref, ybl_ref, bonus_lo_ref, bonus_hi_ref, g_lo_ref, g_hi_ref,
                 attc_ref, attl_ref, xp_ref, xs_ref, mod_ref,
                 lng_ref, lnb_ref, ones_ref, wout_ref, n2_ref, x1_ref, h2_ref):
    i = pl.program_id(0)
    bonus = jnp.concatenate([bonus_lo_ref[...], bonus_hi_ref[...]], axis=0)
    gate = jnp.concatenate([g_lo_ref[...], g_hi_ref[...]], axis=0)
    def pick(ctx_ref, lat_ref):
        return _ctx_or_lat(i, ctx_ref, lat_ref, BIG_BLOCK)

    ones = ones_ref[...]
    inv_n = 1.0 / HEAD_DIM
    y = pick(yfc_ref, yfl_ref).astype(F32) + pick(ybc_ref, ybl_ref).astype(F32)
    yc = y - _segsum(y, ones) * inv_n
    var = _segsum(yc * yc, ones) * inv_n
    yn = yc * lax.rsqrt(var + GN_EPS) * lng_ref[...] + lnb_ref[...]
    r_out = ((yn + bonus) * gate).astype(BF16)
    o = _dot(pick(attc_ref, attl_ref), wout_ref[0:WIDTH, :]) + _dot(r_out, wout_ref[WIDTH:, :])
    x1 = pick(xp_ref, xs_ref) + mod_ref[0, 2:3, :] * o
    x1_ref[...] = x1
    h2 = _rmsnorm(x1, n2_ref[...]) * (1.0 + mod_ref[0, 4:5, :]) + mod_ref[0, 3:4, :]
    h2_ref[...] = h2.astype(BF16)


def _post(ys, bonus, g, att_ctx, att_lat, xp, xs, mod, ln_g, ln_b, ones, w_out_bf, norm2):
    blk = BIG_BLOCK
    tok = lambda cols: _tok_spec(cols, blk)
    ctx_lat = [_ctx_tok_spec(WIDTH, blk), _lat_tok_spec(WIDTH, blk)]
    halves = [pl.BlockSpec((PAD_TOK, WIDTH), lambda i, h=h: (i * (blk // PAD_TOK) + 1 + h, 0)) for h in range(2)]
    assert blk == 2 * PAD_TOK
    return pl.pallas_call(
        _post_kernel,
        grid=(N_TOK // blk,),
        in_specs=ctx_lat + ctx_lat + halves + halves
        + ctx_lat + [_ctx_tok_spec(D_MODEL, blk), _lat_tok_spec(D_MODEL, blk),
                  _mod_spec(blk),
                  _const_spec((1, WIDTH)), _const_spec((1, WIDTH)), _const_spec((WIDTH, WIDTH)),
                  _const_spec((2 * WIDTH, D_MODEL)), _const_spec((1, D_MODEL))],
        out_specs=[tok(D_MODEL), tok(D_MODEL)],
        out_shape=[jax.ShapeDtypeStruct((N_TOK, D_MODEL), F32),
                   jax.ShapeDtypeStruct((N_TOK, D_MODEL), BF16)],
        compiler_params=_params(1),
        name="post",
    )(*ys, bonus, bonus, g, g, att_ctx, att_lat, xp, xs, mod, ln_g, ln_b, ones, w_out_bf, norm2)


def _ffn_kernel(x1_ref, h2_ref, hp_ref, hn_ref, mod_ref, w1_ref, w3_ref, wc_ref, w2_ref, nf_ref, yp_ref, ys_ref):
    i = pl.program_id(0)
    blk = BIG_BLOCK
    is_ctx = i < N_CTX_TOK // blk
    has_prev, has_next = _seq_neighbours(i, blk)
    lhs = jnp.concatenate([h2_ref[...], hp_ref[...], hn_ref[...]], axis=0)
    h2 = lhs[:blk]
    rid = lax.broadcasted_iota(jnp.int32, (blk, 1), 0)
    inner = jnp.where(is_ctx, SEQ, -1)
    seq_start = rid == inner
    seq_end = rid == inner - 1
    acc = jnp.zeros((blk, D_MODEL), F32)
    for fs in (slice(0, FF_SPLIT), slice(FF_SPLIT, D_FF)):
        a_all = _dot(lhs, w1_ref[:, fs])
        a = a_all[:blk]
        prow = jnp.where(has_prev, a_all[blk + 15:blk + 16, :], 0.0)
        nrow = jnp.where(has_next, a_all[blk + 16:blk + 17, :], 0.0)
        aprev, anext = _shifted(a, prow, nrow)
        aprev = jnp.where(seq_start, 0.0, aprev)
        anext = jnp.where(seq_end, 0.0, anext)
        cv = aprev * wc_ref[0:1, fs] + a * wc_ref[1:2, fs] + anext * wc_ref[2:3, fs]
        act = _silu(cv) * _dot(h2, w3_ref[:, fs])
        acc = acc + _dot(act.astype(BF16), w2_ref[fs, :])
    x2 = x1_ref[...] + mod_ref[0, 5:6, :] * acc
    y = _rmsnorm(x2, nf_ref[...])

    @pl.when(is_ctx)
    def _():
        yp_ref[...] = y

    @pl.when(jnp.logical_not(is_ctx))
    def _():
        ys_ref[...] = y


def _ffn(x1, h2, mod, w1_bf, w3_bf, wc, w2_bf, norm_f):
    blk = BIG_BLOCK
    rows16 = blk // 16
    return pl.pallas_call(
        _ffn_kernel,
        grid=(N_TOK // blk,),
        in_specs=[_tok_spec(D_MODEL, blk), _tok_spec(D_MODEL, blk),
                  pl.BlockSpec((16, D_MODEL), lambda i: (jnp.maximum(i * rows16 - 1, 0), 0)),
                  pl.BlockSpec((16, D_MODEL), lambda i: (jnp.minimum((i + 1) * rows16, N_TOK // 16 - 1), 0)),
                  _mod_spec(blk),
                  _const_spec((D_MODEL, D_FF)), _const_spec((D_MODEL, D_FF)), _const_spec((3, D_FF)),
                  _const_spec((D_FF, D_MODEL)), _const_spec((1, D_MODEL))],
        out_specs=[_ctx_tok_spec(D_MODEL, blk), _lat_tok_spec(D_MODEL, blk)],
        out_shape=[jax.ShapeDtypeStruct((N_CTX_TOK, D_MODEL), F32),
                   jax.ShapeDtypeStruct((N_LAT_TOK, D_MODEL), F32)],
        compiler_params=_params(1),
        name="ffn",
    )(x1, h2, h2, h2, mod, w1_bf, w3_bf, wc, w2_bf, norm_f)


def kernel(x_prompt, x_sample, cache_k, cache_v, state_rwkv, c, c_ctx, w_ada, b_ada, norm1, norm2,
           w_in, w_ts, w0, w2, a0, a2, g2, k_k, k_a, r_k, ln_x_g, ln_x_b, rpb, w_out,
           w_ffn1, w_ffn3, w_ffn_conv, w_ffn2, norm_f):
    xp = x_prompt.reshape(N_CTX_TOK, D_MODEL)
    xs = x_sample.reshape(N_LAT_TOK, D_MODEL)
    row = lambda t: t.reshape(1, -1)
    cond = jnp.concatenate([c_ctx[None, :], c, jnp.zeros((8 - 1 - DEC_BATCH, D_MODEL), F32)], axis=0)
    mod, w_in_bf = _modulation(cond, w_ada[0], b_ada[0], w_in[0])
    mod = mod.reshape(8, 6, D_MODEL)

    head_id = jnp.arange(WIDTH) // HEAD_DIM
    ones = (head_id[:, None] == head_id[None, :]).astype(BF16)
    q, k, v, shared, g, bonus, perdir, new_k, new_v = _inproj(
        xp, xs, mod, row(norm1[0]), w_in_bf, w_ts[0], w0[0], w2[0], a0[0], a2[0], g2[0],
        row(k_k[0]), row(k_a[0]), row(r_k[0]), ones)

    att_ctx = _ctx_attention(q, k, v)
    att_lat = _lat_attention(q, k, v, cache_k[:, 0].reshape(DEC_BATCH, PAST_LEN, WIDTH),
                             cache_v[:, 0].reshape(DEC_BATCH, PAST_LEN, WIDTH), _na_bias_table(rpb[0]))
    s_lat = jnp.transpose(state_rwkv[:, 0], (0, 1, 3, 2, 4)).reshape(DEC_BATCH, N_DIRS, HEAD_DIM, WIDTH)
    s0 = jnp.concatenate([jnp.zeros((BATCH, N_DIRS, HEAD_DIM, WIDTH), F32), s_lat], axis=0)
    (y_fc, s_fc, y_fl, _, y_bc, s_bc, y_bl, _,
     w_out_bf, w1_bf, w3_bf, w2_bf) = _scan(shared, perdir, s0, (w_out[0], w_ffn1[0], w_ffn3[0], w_ffn2[0]))
    new_s = jnp.stack([s_fc, s_bc], axis=1)[:, None]

    x1, h2 = _post((y_fc, y_fl, y_bc, y_bl), bonus, g, att_ctx, att_lat, xp, xs, mod, row(ln_x_g[0]), row(ln_x_b[0]), ones,
                   w_out_bf, row(norm2[0]))
    yp, ys = _ffn(x1, h2, mod, w1_bf, w3_bf, w_ffn_conv[0], w2_bf, row(norm_f))

    return (yp.reshape(BATCH, SEQ, D_MODEL), ys.reshape(DEC_BATCH, DEC_SEQ, D_MODEL), new_k, new_v, new_s)
```

```python
import functools

import jax
import jax.numpy as jnp
from jax import lax
from jax.experimental import pallas as pl
from jax.experimental.pallas import tpu as pltpu

F32 = jnp.float32
BF16 = jnp.bfloat16
HIGHEST = lax.Precision.HIGHEST

D_MODEL = 1024
BATCH = 16
SEQ = 256
DEC_BATCH = 2
DEC_SEQ = 2048
PAST_LEN = 512
GRID_W = 64
HEAD_DIM = 64
N_HEADS = 8
WIDTH = N_HEADS * HEAD_DIM
PAIR_W = 2 * HEAD_DIM
NA_ROWS = 8
NA_COLS = 16
N_DIRS = 2
LORA = 64
GATE_LORA = 128
D_FF = 2816
EPS = 1e-6
GN_EPS = 64e-5
ATT_SCALE = HEAD_DIM ** -0.5
RWKV_COLS = 3 * WIDTH + N_DIRS * 2 * LORA + GATE_LORA
MASK_VALUE = -1e30

N_CTX_TOK = BATCH * SEQ
N_LAT_TOK = DEC_BATCH * DEC_SEQ
N_TOK = N_CTX_TOK + N_LAT_TOK
TOK_BLOCK = 256
N_BLOCKS = N_TOK // TOK_BLOCK
BIG_BLOCK = 512
PAD_TOK = TOK_BLOCK
N_SEQS = BATCH + DEC_BATCH
CHUNK = 64
N_CHUNKS = N_TOK // CHUNK
CTX_CHUNKS = N_CTX_TOK // CHUNK
CHUNKS_PER_CTX_SEQ = SEQ // CHUNK
CHUNKS_PER_LAT_SEQ = DEC_SEQ // CHUNK
GRID_ROWS = DEC_SEQ // GRID_W
LAT_ROWS = 4
MXU_WIDTH = 256
FF_SPLIT = (D_FF // MXU_WIDTH + 1) // 2 * MXU_WIDTH
GROUP = 4
GROUP_W = GROUP * HEAD_DIM
VMEM_LIMIT = 56 * 1024 * 1024


def _params(n_axes, limit=VMEM_LIMIT):
    return pltpu.CompilerParams(dimension_semantics=("arbitrary",) * n_axes, vmem_limit_bytes=limit)


def _const_spec(shape):
    zeros = (0,) * len(shape)
    return pl.BlockSpec(shape, lambda *_: zeros, pipeline_mode=pl.Buffered(1))


def _tok_spec(cols, blk=TOK_BLOCK):
    return pl.BlockSpec((blk, cols), lambda i: (i, 0))


def _ctx_tok_spec(cols, blk=TOK_BLOCK):
    return pl.BlockSpec((blk, cols), lambda i: (jnp.minimum(i, N_CTX_TOK // blk - 1), 0))


def _lat_tok_spec(cols, blk=TOK_BLOCK):
    return pl.BlockSpec((blk, cols), lambda i: (jnp.maximum(i - N_CTX_TOK // blk, 0), 0))


def _ctx_or_lat(i, ctx_ref, lat_ref, blk=TOK_BLOCK):
    return jnp.where(i < N_CTX_TOK // blk, ctx_ref[...], lat_ref[...])


def _mod_spec(blk=TOK_BLOCK):
    def row(i):
        return jnp.where(i < N_CTX_TOK // blk, 0, 1 + (i - N_CTX_TOK // blk) // (DEC_SEQ // blk))
    return pl.BlockSpec((1, 6, D_MODEL), lambda i: (row(i), 0, 0))


def _seq_neighbours(i, blk=TOK_BLOCK):
    per_seq = DEC_SEQ // blk
    j = (i - N_CTX_TOK // blk) % per_seq
    lat = i >= N_CTX_TOK // blk
    return lat & (j != 0), lat & (j != per_seq - 1)


def _silu(x):
    return x * jax.nn.sigmoid(x)


def _softplus(x):
    return jnp.maximum(x, 0.0) + jnp.log(1.0 + jnp.exp(-jnp.abs(x)))


def _rmsnorm(x, g):
    return x * lax.rsqrt(jnp.mean(x * x, axis=-1, keepdims=True) + EPS) * g


def _dot(a, b, precision=None):
    return jnp.dot(a, b, precision=precision, preferred_element_type=F32)


def _dot_nt(a, b, precision=None):
    return lax.dot_general(a, b, (((1,), (1,)), ((), ())), precision=precision, preferred_element_type=F32)


def _split(x):
    hi = x.astype(BF16)
    return hi, (x - hi.astype(F32)).astype(BF16)


def _segsum(x, ones_bf):
    return _dot(x.astype(BF16), ones_bf)


def _shifted(a, prev_row, next_row):
    t = a.shape[0]
    rid = lax.broadcasted_iota(jnp.int32, (t, 1), 0)
    prev = jnp.where(rid == 0, prev_row, pltpu.roll(a, 1, axis=0))
    nxt = jnp.where(rid == t - 1, next_row, pltpu.roll(a, t - 1, axis=0))
    return prev, nxt


MOD_STEPS = 8


def _mod_kernel(cond_ref, w_ref, b_ref, win_ref, o_ref, win_bf_ref):
    o_ref[...] = _dot(_silu(cond_ref[...]).astype(BF16), w_ref[...].astype(BF16)) + b_ref[...]
    win_bf_ref[...] = win_ref[...].astype(BF16)


def _modulation(cond, w_ada, b_ada, w_in):
    n = 6 * D_MODEL
    cols = n // MOD_STEPS
    rows = D_MODEL // MOD_STEPS
    w_in_spec = pl.BlockSpec((rows, w_in.shape[1]), lambda j: (j, 0))
    return pl.pallas_call(
        _mod_kernel,
        grid=(MOD_STEPS,),
        in_specs=[pl.BlockSpec((8, D_MODEL), lambda j: (0, 0)),
                  pl.BlockSpec((D_MODEL, cols), lambda j: (0, j)),
                  pl.BlockSpec((1, cols), lambda j: (0, j)),
                  w_in_spec],
        out_specs=[pl.BlockSpec((8, cols), lambda j: (0, j)), w_in_spec],
        out_shape=[jax.ShapeDtypeStruct((8, n), F32), jax.ShapeDtypeStruct(w_in.shape, BF16)],
        compiler_params=_params(1),
        name="mod",
    )(cond, w_ada, b_ada.reshape(1, n), w_in)


def _inproj_kernel(xp_ref, xs_ref, xprev_ref, xnext_ref, mod_ref, n1_ref, w_ref, *rest):
    rwkv_refs, (q_ref, k_ref, v_ref), rwkv_outs, (nk_ref, nv_ref), z_bufs = (
        rest[:10], rest[10:13], rest[13:18], rest[18:20], rest[20:])
    i = pl.program_id(0)
    blk = jnp.minimum(i, N_BLOCKS - 1)

    @pl.when(i == 0)
    def _():
        z_bufs[1][...] = jnp.zeros_like(z_bufs[1])

    def modulated(x):
        h = _rmsnorm(x, n1_ref[...]) * (1.0 + mod_ref[0, 1:2, :]) + mod_ref[0, 0:1, :]
        return h.astype(BF16)

    def step(z_store, z_load):
        h = modulated(_ctx_or_lat(blk, xp_ref, xs_ref))
        edge = modulated(jnp.concatenate([xprev_ref[...], xnext_ref[...]], axis=0))
        lhs = jnp.concatenate([h, edge], axis=0)
        kv = []

        def project(c0, c1):
            z_store[:, c0:c1] = _dot(lhs, w_ref[:, 3 * WIDTH + c0:3 * WIDTH + c1])

        def portion_q():
            project(3 * WIDTH, RWKV_COLS)
            project(WIDTH, 2 * WIDTH)
            q_ref[...] = _dot(h, w_ref[:, 0:WIDTH])

        def portion_k():
            project(0, WIDTH)
            kv.append(_dot(h, w_ref[:, WIDTH:2 * WIDTH]))

        def portion_v():
            project(2 * WIDTH, 3 * WIDTH)
            kv.append(_dot(h, w_ref[:, 2 * WIDTH:3 * WIDTH]))

        has_prev, has_next = _seq_neighbours(i - 1)

        def zcols(c0, c1):
            return (z_load[0:TOK_BLOCK, c0:c1],
                    jnp.where(has_prev, z_load[TOK_BLOCK + 7:TOK_BLOCK + 8, c0:c1], 0.0),
                    jnp.where(has_next, z_load[TOK_BLOCK + 8:TOK_BLOCK + 9, c0:c1], 0.0))

        _rwkv_tokens(zcols, (portion_q, portion_k, portion_v), *rwkv_refs, *rwkv_outs)
        k, v = kv
        k_ref[...] = k
        v_ref[...] = v

        @pl.when(i < N_CTX_TOK // TOK_BLOCK)
        def _():
            nk_ref[0, 0] = k.reshape(SEQ, N_HEADS, HEAD_DIM)
            nv_ref[0, 0] = v.reshape(SEQ, N_HEADS, HEAD_DIM)

    @pl.when(i % 2 == 0)
    def _():
        step(z_bufs[0], z_bufs[1])

    @pl.when(i % 2 == 1)
    def _():
        step(z_bufs[1], z_bufs[0])


def _inproj(xp, xs, mod, norm1, w_in_bf, w_ts, w0, w2, a0, a2, g2, k_k, k_a, r_k, ones):
    tok = jax.ShapeDtypeStruct((N_TOK, WIDTH), F32)
    padded = lambda cols, dtype=F32: jax.ShapeDtypeStruct((N_TOK + PAD_TOK, cols), dtype)
    lat0 = N_CTX_TOK // TOK_BLOCK
    last = N_BLOCKS - 1
    rows8 = TOK_BLOCK // 8
    n8 = N_LAT_TOK // 8
    clamped = lambda cols: pl.BlockSpec((TOK_BLOCK, cols), lambda i: (jnp.minimum(i, last), 0))
    x_ctx = pl.BlockSpec((TOK_BLOCK, D_MODEL), lambda i: (jnp.minimum(i, lat0 - 1), 0))
    x_lat = pl.BlockSpec((TOK_BLOCK, D_MODEL), lambda i: (jnp.clip(i - lat0, 0, last - lat0), 0))
    prev8 = pl.BlockSpec((8, D_MODEL), lambda i: (jnp.clip((jnp.minimum(i, last) - lat0) * rows8 - 1, 0, n8 - 1), 0))
    next8 = pl.BlockSpec((8, D_MODEL), lambda i: (jnp.clip((jnp.minimum(i, last) - lat0 + 1) * rows8, 0, n8 - 1), 0))
    mod_rows = pl.BlockSpec((1, 6, D_MODEL), lambda i: (
        jnp.where(i < lat0, 0, 1 + (jnp.minimum(i, last) - lat0) // (DEC_SEQ // TOK_BLOCK)), 0, 0))
    cache = pl.BlockSpec((1, 1, SEQ, N_HEADS, HEAD_DIM), lambda i: (jnp.minimum(i, BATCH - 1), 0, 0, 0, 0))
    z_buf = pltpu.VMEM((TOK_BLOCK + 16, RWKV_COLS), F32)
    return pl.pallas_call(
        _inproj_kernel,
        grid=(N_BLOCKS + 1,),
        in_specs=[x_ctx, x_lat, prev8, next8, mod_rows,
                  _const_spec((1, D_MODEL)),
                  _const_spec((D_MODEL, 3 * WIDTH + RWKV_COLS)),
                  _const_spec((3, RWKV_COLS)),
                  _const_spec((N_DIRS, WIDTH)), _const_spec((N_DIRS, LORA, WIDTH)),
                  _const_spec((N_DIRS, WIDTH)), _const_spec((N_DIRS, LORA, WIDTH)),
                  _const_spec((GATE_LORA, WIDTH)),
                  _const_spec((1, WIDTH)), _const_spec((1, WIDTH)), _const_spec((1, WIDTH)),
                  _const_spec((WIDTH, WIDTH))],
        out_specs=[clamped(WIDTH), clamped(WIDTH), clamped(WIDTH),
                   _tok_spec(3 * WIDTH), _tok_spec(WIDTH), _tok_spec(WIDTH),
                   pl.BlockSpec((N_DIRS, TOK_BLOCK, WIDTH), lambda i: (0, i, 0)),
                   pl.BlockSpec((N_DIRS, TOK_BLOCK, 2 * WIDTH), lambda i: (0, i, 0)),
                   cache, cache],
        out_shape=[tok, tok, tok,
                   padded(3 * WIDTH, BF16), padded(WIDTH, BF16), padded(WIDTH, BF16),
                   jax.ShapeDtypeStruct((N_DIRS, N_TOK + PAD_TOK, WIDTH), F32),
                   jax.ShapeDtypeStruct((N_DIRS, N_TOK + PAD_TOK, 2 * WIDTH), BF16),
                   jax.ShapeDtypeStruct((BATCH, 1, SEQ, N_HEADS, HEAD_DIM), F32),
                   jax.ShapeDtypeStruct((BATCH, 1, SEQ, N_HEADS, HEAD_DIM), F32)],
        scratch_shapes=[z_buf, z_buf],
        compiler_params=_params(1),
        name="inproj",
    )(xp, xs, xs, xs, mod, norm1, w_in_bf, w_ts, w0, w2, a0, a2, g2, k_k, k_a, r_k, ones)


def _pair_queries(q):
    lo_half = lax.broadcasted_iota(jnp.int32, q.shape, 1) < HEAD_DIM
    return jnp.concatenate([jnp.where(lo_half, q, 0.0), jnp.where(lo_half, 0.0, q)], axis=0).astype(BF16)


def _pair_outputs(o):
    t = o.shape[0] // 2
    lo_half = lax.broadcasted_iota(jnp.int32, (t, PAIR_W), 1) < HEAD_DIM
    return jnp.where(lo_half, o[:t], o[t:])


CTX_SEQS_PER_STEP = 2


def _ctx_attn_kernel(q_ref, k_ref, v_ref, o_ref):
    units = [(slice(s * SEQ, (s + 1) * SEQ), slice(p * PAIR_W, (p + 1) * PAIR_W))
             for s in range(CTX_SEQS_PER_STEP) for p in range(N_HEADS // 2)]
    scores = [_dot_nt(_pair_queries(q_ref[rs, ps]), k_ref[rs, ps].astype(BF16)) * ATT_SCALE for rs, ps in units]
    probs = []
    for s in scores:
        e = jnp.exp(s - jnp.max(s, axis=-1, keepdims=True))
        probs.append((e.astype(BF16), jnp.sum(e, axis=-1, keepdims=True)))
    for (rs, ps), (e, l) in zip(units, probs):
        o_ref[rs, ps] = _pair_outputs(_dot(e, v_ref[rs, ps].astype(BF16)) / l).astype(BF16)


def _ctx_attention(q, k, v):
    spec = pl.BlockSpec((CTX_SEQS_PER_STEP * SEQ, WIDTH), lambda b: (b, 0))
    return pl.pallas_call(
        _ctx_attn_kernel,
        grid=(BATCH // CTX_SEQS_PER_STEP,),
        in_specs=[spec, spec, spec],
        out_specs=spec,
        out_shape=jax.ShapeDtypeStruct((N_CTX_TOK, WIDTH), BF16),
        compiler_params=_params(1),
        name="ctxattn",
    )(q, k, v)


def _lat_attn_kernel(q_ref, k_ref, v_ref, ck_ref, cv_ref, bias_ref, o_ref, kbf, vbf, ckbf, cvbf):
    i = pl.program_id(1)

    @pl.when(i == 0)
    def _():
        kbf[...] = k_ref[...].astype(BF16)
        vbf[...] = v_ref[...].astype(BF16)
        ckbf[...] = ck_ref[0].astype(BF16)
        cvbf[...] = cv_ref[0].astype(BF16)

    win = NA_ROWS * GRID_W
    rows = []
    for rr in range(LAT_ROWS):
        gr = i * LAT_ROWS + rr
        first_row = jnp.clip(gr - NA_ROWS // 2, 0, GRID_ROWS - NA_ROWS)
        start = pl.multiple_of(first_row * GRID_W, GRID_W)
        off0 = first_row - gr + NA_ROWS - 1
        rows.append((slice(rr * GRID_W, (rr + 1) * GRID_W), start, off0))
    pairs = [slice(p * PAIR_W, (p + 1) * PAIR_W) for p in range(N_HEADS // 2)]
    scores = []
    for p, ps in enumerate(pairs):
        lhs = [_pair_queries(q_ref[rs, ps]) for rs, _, _ in rows]
        s_loc = []
        for (_, start, off0), lhs_r in zip(rows, lhs):
            bias = jnp.concatenate(
                [jnp.concatenate([bias_ref[2 * p + hh, off0 + 2 * j] for j in range(NA_ROWS // 2)], axis=-1)
                 for hh in range(2)], axis=0)
            s_loc.append(_dot_nt(lhs_r, kbf[pl.ds(start, win), ps]) * ATT_SCALE + bias)
        s_ctx = _dot_nt(jnp.concatenate(lhs, axis=0), ckbf[:, ps]) * ATT_SCALE
        scores.append((jnp.concatenate(s_loc, axis=0), s_ctx))
    probs = []
    for s_loc, s_ctx in scores:
        m = jnp.maximum(jnp.max(s_loc, axis=-1, keepdims=True), jnp.max(s_ctx, axis=-1, keepdims=True))
        e_loc = jnp.exp(s_loc - m)
        e_ctx = jnp.exp(s_ctx - m)
        l = jnp.sum(e_loc, axis=-1, keepdims=True) + jnp.sum(e_ctx, axis=-1, keepdims=True)
        probs.append((e_loc.astype(BF16), e_ctx.astype(BF16), l))
    for ps, (e_loc, e_ctx, l) in zip(pairs, probs):
        o_ctx = _dot(e_ctx, cvbf[:, ps])
        for rr, (rs, start, _) in enumerate(rows):
            both = slice(rr * 2 * GRID_W, (rr + 1) * 2 * GRID_W)
            o = _dot(e_loc[both], vbf[pl.ds(start, win), ps]) + o_ctx[both]
            o_ref[rs, ps] = _pair_outputs(o / l[both]).astype(BF16)


def _na_bias_table(rpb):
    n = rpb.shape[-1]
    w = GRID_W
    pair = jnp.concatenate([rpb[:, :-1], rpb[:, 1:]], axis=-1).astype(F32)
    x = jnp.arange(2 * n)
    y = jnp.arange(2 * w)
    cq = jnp.arange(w)
    ck = y % w
    sel = ((x // n)[:, None, None] == (y // w)[None, None, :]) & (
        (x % n)[:, None, None] == ck[None, None, :] - cq[None, :, None] + NA_COLS - 1)
    t = jnp.einsum("hdx,xqy->hdqy", pair, sel.astype(F32), precision=HIGHEST)
    cs = jnp.clip(cq - NA_COLS // 2, 0, w - NA_COLS)[:, None]
    return jnp.where((ck[None, :] >= cs) & (ck[None, :] < cs + NA_COLS), t, MASK_VALUE)


def _lat_attention(q, k, v, cache_k, cache_v, bias):
    blk = LAT_ROWS * GRID_W
    steps = GRID_ROWS // LAT_ROWS
    lat0 = N_CTX_TOK // blk
    seq = pl.BlockSpec((DEC_SEQ, WIDTH), lambda b, i: (N_CTX_TOK // DEC_SEQ + b, 0))
    ctx = pl.BlockSpec((1, PAST_LEN, WIDTH), lambda b, i: (b, 0, 0))
    return pl.pallas_call(
        _lat_attn_kernel,
        grid=(DEC_BATCH, steps),
        in_specs=[pl.BlockSpec((blk, WIDTH), lambda b, i: (lat0 + b * steps + i, 0)),
                  seq, seq, ctx, ctx,
                  _const_spec((N_HEADS, 2 * NA_ROWS - 2, GRID_W, PAIR_W))],
        out_specs=pl.BlockSpec((blk, WIDTH), lambda b, i: (b * steps + i, 0)),
        out_shape=jax.ShapeDtypeStruct((N_LAT_TOK, WIDTH), BF16),
        scratch_shapes=[pltpu.VMEM((DEC_SEQ, WIDTH), BF16), pltpu.VMEM((DEC_SEQ, WIDTH), BF16),
                        pltpu.VMEM((PAST_LEN, WIDTH), BF16), pltpu.VMEM((PAST_LEN, WIDTH), BF16)],
        compiler_params=_params(2),
        name="latattn",
    )(q, k, v, cache_k, cache_v, bias)


def _rwkv_tokens(zcols, fillers, wts_ref, w0_ref, w2_ref, a0_ref, a2_ref, g2_ref, kk_ref, ka_ref, rk_ref, ones_ref,
                 shared_out, g_out, bonus_out, lw_out, bk_out):
    def conv(c0, c1):
        zc, prow, nrow = zcols(c0, c1)
        zprev, znext = _shifted(zc, prow, nrow)
        return zprev * wts_ref[0:1, c0:c1] + zc * wts_ref[1:2, c0:c1] + znext * wts_ref[2:3, c0:c1]

    fill = iter(fillers)
    dirs = [slice(d * LORA, (d + 1) * LORA) for d in range(N_DIRS)]
    ones = ones_ref[...]
    o = 3 * WIDTH
    lora_in = conv(o, RWKV_COLS)
    kr = conv(WIDTH, 2 * WIDTH)
    tw = jnp.tanh(lora_in[:, 0:N_DIRS * LORA])
    xa = lora_in[:, N_DIRS * LORA:2 * N_DIRS * LORA]
    xg = lora_in[:, 2 * N_DIRS * LORA:]
    kkraw = kr * kk_ref[...]
    next(fill)()
    kk = kkraw * lax.rsqrt(_segsum(kkraw * kkraw, ones) + 1e-12)
    bf = lambda t: t.astype(BF16)
    lora_w = [_dot(bf(tw[:, ls]), bf(w2_ref[d])) for d, ls in enumerate(dirs)]
    r = conv(0, WIDTH)
    next(fill)()
    lora_a = [_dot(bf(xa[:, ls]), bf(a2_ref[d])) for d, ls in enumerate(dirs)]
    g_out[...] = _dot(bf(jax.nn.sigmoid(xg)), bf(g2_ref[...])).astype(g_out.dtype)
    v = conv(2 * WIDTH, 3 * WIDTH)
    next(fill)()
    kd_sum = jnp.zeros_like(kr)
    for d in range(N_DIRS):
        w_log = -_softplus(-(w0_ref[d:d + 1, :] + lora_w[d])) - 0.5
        a = jax.nn.sigmoid(a0_ref[d:d + 1, :] + lora_a[d])
        kd = kr * (1.0 + (a - 1.0) * ka_ref[...])
        lw_out[d] = -jnp.exp(w_log)
        bk_out[d, :, 0:WIDTH] = bf(kk * a)
        bk_out[d, :, WIDTH:] = bf(kd)
        kd_sum = kd_sum + kd
    shared_out[:, 0:WIDTH] = bf(r)
    shared_out[:, WIDTH:2 * WIDTH] = bf(kk)
    shared_out[:, 2 * WIDTH:] = bf(v)
    bonus_out[...] = (_segsum(r * kd_sum * rk_ref[...], ones) * v).astype(bonus_out.dtype)


HALF = CHUNK // 2


def _blockdiag(x, block=HEAD_DIM):
    p = x.astype(BF16)
    blk = lax.broadcasted_iota(jnp.int32, p.shape, 1) // block
    return jnp.concatenate([jnp.where(blk == q, p, jnp.zeros_like(p)) for q in range(GROUP_W // block)], axis=0)


def _hp(a, b, block=HEAD_DIM):
    return _dot(a.astype(BF16), _blockdiag(b, block))


def _hp_nt(a, b):
    return _dot_nt(a.astype(BF16), _blockdiag(b))


def _head_transpose(x):
    xt = jnp.transpose(x)
    return jnp.concatenate([xt[h * HEAD_DIM:(h + 1) * HEAD_DIM] for h in range(GROUP)], axis=1)


def _scan_prepare(d, rows, shared_ref, lw_ref, bk_ref, y_ref, s_ref):
    row = lax.broadcasted_iota(jnp.int32, (CHUNK, GROUP_W), 0)
    lane = lax.broadcasted_iota(jnp.int32, (CHUNK, GROUP_W), 1)
    col = lane % CHUNK
    strict = col > row if d else col < row
    incl = col >= row if d else col <= row
    row_h = lax.broadcasted_iota(jnp.int32, (HALF, GROUP_W), 0)
    lane_h = lax.broadcasted_iota(jnp.int32, (HALF, GROUP_W), 1)
    lo_s = lane_h % CHUNK < HALF
    eye_half = (lane_h % HALF == row_h).astype(F32)

    lw = lw_ref[0, rows, :]
    mask = jnp.where(incl[:, :CHUNK], 1.0, 0.0).astype(BF16)
    lw_hi, lw_mid = _split(lw)
    lw_lo = (lw - lw_hi.astype(F32) - lw_mid.astype(F32)).astype(BF16)
    cum = _dot(mask, lw_hi) + _dot(mask, lw_mid) + _dot(mask, lw_lo)
    tot = jnp.sum(lw, axis=0, keepdims=True)
    p_in = jnp.exp(cum)
    p_neg = jnp.exp(-cum)
    p_rem = jnp.exp(tot - cum)
    kt = shared_ref[rows, WIDTH:2 * WIDTH].astype(F32) * jnp.exp(cum - lw)
    rt = shared_ref[rows, 0:WIDTH].astype(F32) * p_in
    b = bk_ref[0, rows, 0:WIDTH].astype(F32)
    kd = bk_ref[0, rows, WIDTH:].astype(F32)
    bt = b * p_neg
    kdt = kd * p_neg
    bh = b * p_rem
    kh = kd * p_rem
    p_end = jnp.exp(tot)
    v = shared_ref[rows, 2 * WIDTH:].astype(F32)

    chains = []
    for gi in range(N_HEADS // GROUP):
        sl = slice(gi * GROUP_W, (gi + 1) * GROUP_W)
        chains.append(dict(d=d, rows=rows, sl=sl, strict=strict, incl=incl, lo_s=lo_s, eye_half=eye_half, y_ref=y_ref, s_ref=s_ref,
                           kt=kt[:, sl], rt=rt[:, sl], v=v[:, sl], bt=bt[:, sl], kdt=kdt[:, sl],
                           kh=kh[:, sl], bh=bh[:, sl], p_end=p_end[:, sl]))
    return chains


def _scan_scores(chains):
    for c in chains:
        lhs = jnp.concatenate([c["kt"], c["rt"]], axis=0)
        g_b = _hp_nt(lhs, c["bt"])
        g_k = _hp_nt(lhs, c["kdt"])
        c["a_k"] = jnp.where(c["strict"], g_k[:CHUNK], 0.0)
        c["a_rb"] = jnp.where(c["incl"], g_b[CHUNK:], 0.0)
        c["a_rk"] = jnp.where(c["incl"], g_k[CHUNK:], 0.0)
        n = jnp.where(c["strict"], -g_b[:CHUNK], 0.0)
        c["n"] = n
        c["pw"] = jnp.where(c["lo_s"], n[:HALF], n[HALF:])
        c["tp"] = c["eye_half"] + c["pw"]
    return chains


def _scan_local(chains):
    zeros = jnp.zeros((HALF, GROUP_W), F32)
    for c in chains:
        c["pw"] = _hp(c["pw"], c["pw"], HALF)
    for _ in range(3):
        for c in chains:
            both = _hp(jnp.concatenate([c["tp"], c["pw"]], axis=0), c["pw"], HALF)
            c["tp"] = c["tp"] + both[:HALF]
            c["pw"] = both[HALF:]
    for c in chains:
        c["tp"] = c["tp"] + _hp(c["tp"], c["pw"], HALF)
    for c in chains:
        c["t1"] = jnp.where(c["lo_s"], c["tp"], 0.0)
        c["t2"] = jnp.where(c["lo_s"], 0.0, c["tp"])
        if c["d"] == 0:
            off, inner = jnp.where(c["lo_s"], c["n"][HALF:], 0.0), jnp.concatenate([c["t1"], zeros], axis=0)
        else:
            off, inner = jnp.where(c["lo_s"], 0.0, c["n"][:HALF]), jnp.concatenate([zeros, c["t2"]], axis=0)
        c["off"] = _hp(off, inner)
    for c in chains:
        if c["d"] == 0:
            x = _hp(c["t2"], jnp.concatenate([zeros, c["off"]], axis=0))
            c["inv"] = jnp.concatenate([c["t1"], jnp.where(c["lo_s"], x, c["tp"])], axis=0)
        else:
            x = _hp(c["t1"], jnp.concatenate([c["off"], zeros], axis=0))
            c["inv"] = jnp.concatenate([jnp.where(c["lo_s"], c["tp"], x), c["t2"]], axis=0)
    for c in chains:
        c["av"] = _hp(jnp.concatenate([c["a_k"], c["a_rk"]], axis=0), c["v"])
        c["vt"] = _head_transpose(c["v"])
    return chains


def _scan_carry(chains):
    for c in chains:
        c["s"] = c["s_ref"][:, c["sl"]]
        c["z"] = _hp_nt(jnp.concatenate([c["rt"], c["kt"]], axis=0), c["s"])
    for c in chains:
        c["u"] = _hp(c["inv"], c["z"][CHUNK:] + c["av"][:CHUNK])
    for c in chains:
        y = c["z"][:CHUNK] + c["av"][CHUNK:] - _hp(c["a_rb"], c["u"])
        c["y_ref"][c["rows"], c["sl"]] = y.astype(BF16)
    for c in chains:
        upd = _hp(c["vt"], c["kh"]) - _hp(_head_transpose(c["u"]), c["bh"])
        c["s_ref"][:, c["sl"]] = c["s"] * c["p_end"] + upd


REGION_CHUNKS = CTX_CHUNKS
STEP_CHUNKS = 2
STEP_ROWS = STEP_CHUNKS * CHUNK
SCAN_STEPS = REGION_CHUNKS // STEP_CHUNKS
SCAN_STREAMS = tuple((d, base // STEP_CHUNKS, per_seq // STEP_CHUNKS) for d in range(N_DIRS)
                     for base, per_seq in ((0, CHUNKS_PER_CTX_SEQ), (CTX_CHUNKS, CHUNKS_PER_LAT_SEQ)))
STREAM_INS = 4


def _stream_local_block(stream, j):
    return SCAN_STEPS - 1 - j if stream[0] else j


def _scan_kernel(n_cast, *refs):
    n_streams = len(SCAN_STREAMS)
    n_stream_in = STREAM_INS * n_streams
    n_in = n_stream_in + n_cast
    n_out = 2 * n_streams + n_cast
    carries = refs[n_in + n_out:]
    j = pl.program_id(0)
    for w_ref, o_ref in zip(refs[n_stream_in:n_in], refs[n_in + 2 * n_streams:n_in + n_out]):
        o_ref[...] = w_ref[...].astype(BF16)
    per_stream = []
    for s, stream in enumerate(SCAN_STREAMS):
        d, _, per_seq = stream
        ins = refs[STREAM_INS * s:STREAM_INS * (s + 1)]
        y_ref = refs[n_in + 2 * s]
        s0_ref = ins[-1]
        local = _stream_local_block(stream, j) % per_seq

        @pl.when(local == (per_seq - 1 if d else 0))
        def _():
            carries[s][...] = s0_ref[0, 0]

        per_stream.append((d,) + ins[:-1] + (y_ref, carries[s]))

    chains_by_chunk = []
    for k in range(STEP_CHUNKS):
        chains = []
        for d, *stream_refs in per_stream:
            first = (STEP_CHUNKS - 1 - k if d else k) * CHUNK
            chains += _scan_prepare(d, slice(first, first + CHUNK), *stream_refs)
        chains_by_chunk.append(_scan_local(_scan_scores(chains)))
    for chains in chains_by_chunk:
        _scan_carry(chains)

    for s, stream in enumerate(SCAN_STREAMS):
        d, _, per_seq = stream
        out_ref = refs[n_in + 2 * s + 1]
        local = _stream_local_block(stream, j) % per_seq

        @pl.when(local == (0 if d else per_seq - 1))
        def _():
            for h in range(N_HEADS):
                out_ref[0, h] = carries[s][:, h * HEAD_DIM:(h + 1) * HEAD_DIM]


def _scan(shared, logdecay, bk, s0, weights):
    in_specs, out_specs, out_shape, operands = [], [], [], []
    cast_in, cast_out, cast_shape = [], [], []
    for w in weights:
        rows = w.shape[0] // SCAN_STEPS
        if rows % 16 or rows * SCAN_STEPS != w.shape[0]:
            rows = 128
        assert w.shape[0] % rows == 0 and w.shape[0] // rows <= SCAN_STEPS
        spec = pl.BlockSpec((rows, w.shape[1]), lambda j, n=w.shape[0] // rows: (jnp.minimum(j, n - 1), 0))
        cast_in.append(spec)
        cast_out.append(spec)
        cast_shape.append(jax.ShapeDtypeStruct(w.shape, BF16))
    for stream in SCAN_STREAMS:
        d, base, per_seq = stream
        seq0 = 0 if base == 0 else BATCH
        n_seq = SCAN_STEPS // per_seq
        loc = lambda j, stream=stream: _stream_local_block(stream, j)
        base += PAD_TOK // STEP_ROWS
        tok = pl.BlockSpec((STEP_ROWS, 3 * WIDTH), lambda j, loc=loc, base=base: (base + loc(j), 0))
        dtok = lambda cols: pl.BlockSpec((1, STEP_ROWS, cols),
                                         lambda j, loc=loc, base=base, d=d: (d, base + loc(j), 0))
        st_in = pl.BlockSpec((1, 1, HEAD_DIM, WIDTH),
                             lambda j, loc=loc, seq0=seq0, per_seq=per_seq, d=d: (seq0 + loc(j) // per_seq, d, 0, 0))
        in_specs += [tok, dtok(WIDTH), dtok(2 * WIDTH), st_in]
        operands += [shared, logdecay, bk, s0]
        out_specs += [pl.BlockSpec((STEP_ROWS, WIDTH), lambda j, loc=loc: (loc(j), 0)),
                      pl.BlockSpec((1, N_HEADS, HEAD_DIM, HEAD_DIM),
                                   lambda j, loc=loc, per_seq=per_seq: (loc(j) // per_seq, 0, 0, 0))]
        out_shape += [jax.ShapeDtypeStruct((REGION_CHUNKS * CHUNK, WIDTH), BF16),
                      jax.ShapeDtypeStruct((n_seq, N_HEADS, HEAD_DIM, HEAD_DIM), F32)]
    return pl.pallas_call(
        functools.partial(_scan_kernel, len(weights)),
        grid=(SCAN_STEPS,),
        in_specs=in_specs + cast_in,
        out_specs=out_specs + cast_out,
        out_shape=out_shape + cast_shape,
        scratch_shapes=[pltpu.VMEM((HEAD_DIM, WIDTH), F32)] * len(SCAN_STREAMS),
        compiler_params=_params(1),
        name="scan",
    )(*operands, *weights)


def _post_kernel(yfc_ref, yfl_ref, ybc_ref, ybl_ref, bonus_lo_ref, bonus_hi_ref, g_lo_ref, g_hi_ref,
                 attc_ref, attl_ref, xp_ref, xs_ref, mod_ref,
                 lng_ref, lnb_ref, ones_ref, wout_ref, n2_ref, x1_ref, h2_ref):
    i = pl.program_id(0)
    bonus = jnp.concatenate([bonus_lo_ref[...], bonus_hi_ref[...]], axis=0)
    gate = jnp.concatenate([g_lo_ref[...], g_hi_ref[...]], axis=0)
    def pick(ctx_ref, lat_ref):
        return _ctx_or_lat(i, ctx_ref, lat_ref, BIG_BLOCK)

    ones = ones_ref[...]
    inv_n = 1.0 / HEAD_DIM
    y = pick(yfc_ref, yfl_ref).astype(F32) + pick(ybc_ref, ybl_ref).astype(F32)
    yc = y - _segsum(y, ones) * inv_n
    var = _segsum(yc * yc, ones) * inv_n
    yn = yc * lax.rsqrt(var + GN_EPS) * lng_ref[...] + lnb_ref[...]
    r_out = ((yn + bonus) * gate).astype(BF16)
    o = _dot(pick(attc_ref, attl_ref), wout_ref[0:WIDTH, :]) + _dot(r_out, wout_ref[WIDTH:, :])
    x1 = pick(xp_ref, xs_ref) + mod_ref[0, 2:3, :] * o
    x1_ref[...] = x1
    h2 = _rmsnorm(x1, n2_ref[...]) * (1.0 + mod_ref[0, 4:5, :]) + mod_ref[0, 3:4, :]
    h2_ref[...] = h2.astype(BF16)


def _post(ys, bonus, g, att_ctx, att_lat, xp, xs, mod, ln_g, ln_b, ones, w_out_bf, norm2):
    blk = BIG_BLOCK
    tok = lambda cols: _tok_spec(cols, blk)
    ctx_lat = [_ctx_tok_spec(WIDTH, blk), _lat_tok_spec(WIDTH, blk)]
    halves = [pl.BlockSpec((PAD_TOK, WIDTH), lambda i, h=h: (i * (blk // PAD_TOK) + 1 + h, 0)) for h in range(2)]
    assert blk == 2 * PAD_TOK
    return pl.pallas_call(
        _post_kernel,
        grid=(N_TOK // blk,),
        in_specs=ctx_lat + ctx_lat + halves + halves
        + ctx_lat + [_ctx_tok_spec(D_MODEL, blk), _lat_tok_spec(D_MODEL, blk),
                  _mod_spec(blk),
                  _const_spec((1, WIDTH)), _const_spec((1, WIDTH)), _const_spec((WIDTH, WIDTH)),
                  _const_spec((2 * WIDTH, D_MODEL)), _const_spec((1, D_MODEL))],
        out_specs=[tok(D_MODEL), tok(D_MODEL)],
        out_shape=[jax.ShapeDtypeStruct((N_TOK, D_MODEL), F32),
                   jax.ShapeDtypeStruct((N_TOK, D_MODEL), BF16)],
        compiler_params=_params(1),
        name="post",
    )(*ys, bonus, bonus, g, g, att_ctx, att_lat, xp, xs, mod, ln_g, ln_b, ones, w_out_bf, norm2)


def _ffn_kernel(x1_ref, h2_ref, hp_ref, hn_ref, mod_ref, w1_ref, w3_ref, wc_ref, w2_ref, nf_ref, yp_ref, ys_ref):
    i = pl.program_id(0)
    blk = BIG_BLOCK
    is_ctx = i < N_CTX_TOK // blk
    has_prev, has_next = _seq_neighbours(i, blk)
    lhs = jnp.concatenate([h2_ref[...], hp_ref[...], hn_ref[...]], axis=0)
    h2 = lhs[:blk]
    rid = lax.broadcasted_iota(jnp.int32, (blk, 1), 0)
    inner = jnp.where(is_ctx, SEQ, -1)
    seq_start = rid == inner
    seq_end = rid == inner - 1
    acc = jnp.zeros((blk, D_MODEL), F32)
    for fs in (slice(0, FF_SPLIT), slice(FF_SPLIT, D_FF)):
        a_all = _dot(lhs, w1_ref[:, fs])
        a = a_all[:blk]
        prow = jnp.where(has_prev, a_all[blk + 15:blk + 16, :], 0.0)
        nrow = jnp.where(has_next, a_all[blk + 16:blk + 17, :], 0.0)
        aprev, anext = _shifted(a, prow, nrow)
        aprev = jnp.where(seq_start, 0.0, aprev)
        anext = jnp.where(seq_end, 0.0, anext)
        cv = aprev * wc_ref[0:1, fs] + a * wc_ref[1:2, fs] + anext * wc_ref[2:3, fs]
        act = _silu(cv) * _dot(h2, w3_ref[:, fs])
        acc = acc + _dot(act.astype(BF16), w2_ref[fs, :])
    x2 = x1_ref[...] + mod_ref[0, 5:6, :] * acc
    y = _rmsnorm(x2, nf_ref[...])

    @pl.when(is_ctx)
    def _():
        yp_ref[...] = y

    @pl.when(jnp.logical_not(is_ctx))
    def _():
        ys_ref[...] = y


def _ffn(x1, h2, mod, w1_bf, w3_bf, wc, w2_bf, norm_f):
    blk = BIG_BLOCK
    rows16 = blk // 16
    return pl.pallas_call(
        _ffn_kernel,
        grid=(N_TOK // blk,),
        in_specs=[_tok_spec(D_MODEL, blk), _tok_spec(D_MODEL, blk),
                  pl.BlockSpec((16, D_MODEL), lambda i: (jnp.maximum(i * rows16 - 1, 0), 0)),
                  pl.BlockSpec((16, D_MODEL), lambda i: (jnp.minimum((i + 1) * rows16, N_TOK // 16 - 1), 0)),
                  _mod_spec(blk),
                  _const_spec((D_MODEL, D_FF)), _const_spec((D_MODEL, D_FF)), _const_spec((3, D_FF)),
                  _const_spec((D_FF, D_MODEL)), _const_spec((1, D_MODEL))],
        out_specs=[_ctx_tok_spec(D_MODEL, blk), _lat_tok_spec(D_MODEL, blk)],
        out_shape=[jax.ShapeDtypeStruct((N_CTX_TOK, D_MODEL), F32),
                   jax.ShapeDtypeStruct((N_LAT_TOK, D_MODEL), F32)],
        compiler_params=_params(1),
        name="ffn",
    )(x1, h2, h2, h2, mod, w1_bf, w3_bf, wc, w2_bf, norm_f)


def kernel(x_prompt, x_sample, cache_k, cache_v, state_rwkv, c, c_ctx, w_ada, b_ada, norm1, norm2,
           w_in, w_ts, w0, w2, a0, a2, g2, k_k, k_a, r_k, ln_x_g, ln_x_b, rpb, w_out,
           w_ffn1, w_ffn3, w_ffn_conv, w_ffn2, norm_f):
    xp = x_prompt.reshape(N_CTX_TOK, D_MODEL)
    xs = x_sample.reshape(N_LAT_TOK, D_MODEL)
    row = lambda t: t.reshape(1, -1)
    cond = jnp.concatenate([c_ctx[None, :], c, jnp.zeros((8 - 1 - DEC_BATCH, D_MODEL), F32)], axis=0)
    mod, w_in_bf = _modulation(cond, w_ada[0], b_ada[0], w_in[0])
    mod = mod.reshape(8, 6, D_MODEL)

    head_id = jnp.arange(WIDTH) // HEAD_DIM
    ones = (head_id[:, None] == head_id[None, :]).astype(BF16)
    q, k, v, shared, g, bonus, logdecay, bk, new_k, new_v = _inproj(
        xp, xs, mod, row(norm1[0]), w_in_bf, w_ts[0], w0[0], w2[0], a0[0], a2[0], g2[0],
        row(k_k[0]), row(k_a[0]), row(r_k[0]), ones)

    att_ctx = _ctx_attention(q, k, v)
    att_lat = _lat_attention(q, k, v, cache_k[:, 0].reshape(DEC_BATCH, PAST_LEN, WIDTH),
                             cache_v[:, 0].reshape(DEC_BATCH, PAST_LEN, WIDTH), _na_bias_table(rpb[0]))
    s_lat = jnp.transpose(state_rwkv[:, 0], (0, 1, 3, 2, 4)).reshape(DEC_BATCH, N_DIRS, HEAD_DIM, WIDTH)
    s0 = jnp.concatenate([jnp.zeros((BATCH, N_DIRS, HEAD_DIM, WIDTH), F32), s_lat], axis=0)
    (y_fc, s_fc, y_fl, _, y_bc, s_bc, y_bl, _,
     w_out_bf, w1_bf, w3_bf, w2_bf) = _scan(shared, logdecay, bk, s0, (w_out[0], w_ffn1[0], w_ffn3[0], w_ffn2[0]))
    new_s = jnp.stack([s_fc, s_bc], axis=1)[:, None]

    x1, h2 = _post((y_fc, y_fl, y_bc, y_bl), bonus, g, att_ctx, att_lat, xp, xs, mod, row(ln_x_g[0]), row(ln_x_b[0]), ones,
                   w_out_bf, row(norm2[0]))
    yp, ys = _ffn(x1, h2, mod, w1_bf, w3_bf, w_ffn_conv[0], w2_bf, row(norm_f))

    return (yp.reshape(BATCH, SEQ, D_MODEL), ys.reshape(DEC_BATCH, DEC_SEQ, D_MODEL), new_k, new_v, new_s)
```
